```python
import math
import jax, jax.numpy as jnp
from jax import lax
import numpy as np

D_MODEL = 1024
BATCH = 1
SEQ = 16384
DEPTH = 4

D_MIX = D_MODEL
S5_WIDTH = D_MIX // 4
S5_GROUP = 16
S5_GROUPS = S5_WIDTH // S5_GROUP
S5_STATE = 64
CONV_WIDTH = D_MIX // 4
CONV_K = 3
HG_WIDTH = D_MIX // 2
HG_HEAD_DIM = 128
HG_HEADS = HG_WIDTH // HG_HEAD_DIM
HG_CHUNK = 64
IN_SPLITS = (S5_WIDTH, CONV_WIDTH, CONV_WIDTH, CONV_WIDTH, HG_WIDTH, HG_WIDTH, HG_WIDTH, HG_WIDTH)
D_IN = S5_WIDTH + 3 * CONV_WIDTH + 4 * HG_WIDTH
D_FF = (7 * D_MODEL) // 2
N_EXPERTS = 8
TOP_K = 2
MOE_BLOCK = 128
N_DENSE = (DEPTH + 1) // 2
N_MOE = DEPTH // 2
NORM_EPS = 1e-6
DT_MIN = 1e-3
DT_MAX = 1e-1
LB_FLOOR = 1e-30

kernel_name = "hybrid_s5_conv_hgrn2_moe"

F32 = jnp.float32


def rmsnorm(x, g):
    x32 = x.astype(F32)
    y = x32 * lax.rsqrt(jnp.mean(x32 * x32, axis=-1, keepdims=True) + NORM_EPS)
    return (y * g.astype(F32)).astype(x.dtype)


def split_cols(h):
    parts, off = [], 0
    for w in IN_SPLITS:
        parts.append(h[..., off:off + w])
        off += w
    return parts


def _diag_scan_op(e1, e2):
    a1, b1 = e1
    a2, b2 = e2
    return a1 * a2, a2 * b1 + b2


def s5_mixer(u, lam_re, lam_im, log_dt, b_re, b_im, c_re, c_im, d, w_glu):
    bsz, L, _ = u.shape
    ug = u.astype(F32).reshape(bsz, L, S5_GROUPS, S5_GROUP)
    lam = lax.complex(lam_re.astype(F32), lam_im.astype(F32))
    dt = jnp.exp(log_dt.astype(F32))[:, None]
    lam_bar = jnp.exp(lam * dt)
    b = lax.complex(b_re.astype(F32), b_im.astype(F32))
    b_bar = ((lam_bar - 1.0) / lam)[..., None] * b
    bu = jnp.einsum('gpc,blgc->blgp', b_bar, ug.astype(jnp.complex64))
    a = jnp.broadcast_to(lam_bar, bu.shape)
    _, states = lax.associative_scan(_diag_scan_op, (a, bu), axis=1)
    c = lax.complex(c_re.astype(F32), c_im.astype(F32))
    y = jnp.real(jnp.einsum('gcp,blgp->blgc', c, states)) + d.astype(F32).reshape(S5_GROUPS, S5_GROUP) * ug
    y = jax.nn.gelu(y.reshape(bsz, L, S5_WIDTH))
    return y * jax.nn.sigmoid(y @ w_glu.astype(F32))


def conv_mixer(gate_b, gate_c, v, conv_w):
    z = gate_c * v
    L = z.shape[1]
    zp = jnp.pad(z, ((0, 0), (CONV_K - 1, 0), (0, 0)))
    y = zp[:, 0:L] * conv_w[0]
    for k in range(1, CONV_K):
        y = y + zp[:, k:k + L] * conv_w[k]
    return gate_b * y


def hgrn2_mixer(q, f_pre, i_val, gate, lb, norm_g):
    bsz, L, _ = q.shape
    n_chunks = L // HG_CHUNK

    def heads(t):
        return t.astype(F32).reshape(bsz, n_chunks, HG_CHUNK, HG_HEADS, HG_HEAD_DIM).transpose(1, 0, 3, 2, 4)

    lbh = jnp.clip(lb.astype(F32), 0.0, 1.0 - 1e-6).reshape(HG_HEADS, 1, HG_HEAD_DIM)
    fp = heads(f_pre)
    log_f = jnp.logaddexp(jnp.log(jnp.maximum(lbh, LB_FLOOR)), jnp.log1p(-lbh) + jax.nn.log_sigmoid(fp))
    k = (1.0 - lbh) * jax.nn.sigmoid(-fp)
    qh = jax.nn.silu(heads(q))
    vh = heads(i_val)
    mask = jnp.tril(jnp.ones((HG_CHUNK, HG_CHUNK), dtype=bool))[:, :, None]

    def step(state, inp):
        qc, kc, vc, lfc = inp
        b = jnp.cumsum(lfc, axis=2)
        o_inter = jnp.einsum('bhtk,bhkv->bhtv', qc * jnp.exp(b), state)
        diff = b[:, :, :, None, :] - b[:, :, None, :, :]
        decay = jnp.where(mask, jnp.exp(jnp.where(mask, diff, 0.0)), 0.0)
        scores = jnp.einsum('bhtk,bhtsk,bhsk->bhts', qc, decay, kc)
        o = o_inter + jnp.einsum('bhts,bhsv->bhtv', scores, vc)
        b_last = b[:, :, -1:, :]
        state = jnp.exp(b_last[:, :, 0, :])[..., None] * state + jnp.einsum('bhsk,bhsv->bhkv', kc * jnp.exp(b_last - b), vc)
        return state, o

    s0 = jnp.zeros((bsz, HG_HEADS, HG_HEAD_DIM, HG_HEAD_DIM), F32)
    _, o = lax.scan(step, s0, (qh, k, vh, log_f))
    o = o.transpose(1, 0, 3, 2, 4).reshape(bsz, L, HG_HEADS, HG_HEAD_DIM)
    o = rmsnorm(o, norm_g.reshape(HG_HEADS, HG_HEAD_DIM)).reshape(bsz, L, HG_WIDTH)
    return o * jax.nn.silu(gate.astype(F32))


def swiglu(h, w1, w3, w2):
    return (jax.nn.silu(h @ w1) * (h @ w3)) @ w2


def moe_swiglu(h, router, w1, w3, w2):
    T = h.shape[0]
    logits = h.astype(F32) @ router.astype(F32)
    top_val, top_idx = lax.top_k(logits, TOP_K)
    gates = jax.nn.softmax(top_val, axis=-1)
    flat_e = top_idx.reshape(-1)
    flat_tok = jnp.arange(T * TOP_K, dtype=jnp.int32) // TOP_K
    flat_g = gates.reshape(-1)
    order = jnp.argsort(flat_e)
    se, stok, sg = flat_e[order], flat_tok[order], flat_g[order]
    counts = jnp.zeros((N_EXPERTS,), jnp.int32).at[flat_e].add(1)
    padded = (counts + MOE_BLOCK - 1) // MOE_BLOCK * MOE_BLOCK
    start = jnp.cumsum(counts) - counts
    pend = jnp.cumsum(padded)
    pstart = pend - padded
    dest = pstart[se] + jnp.arange(T * TOP_K, dtype=jnp.int32) - start[se]
    n_blk = -(-(T * TOP_K) // MOE_BLOCK) + N_EXPERTS
    slot_tok = jnp.full((n_blk * MOE_BLOCK,), T, jnp.int32).at[dest].set(stok)
    slot_g = jnp.zeros((n_blk * MOE_BLOCK,), F32).at[dest].set(sg)
    blk_e = jnp.clip(jnp.searchsorted(pend, jnp.arange(n_blk, dtype=jnp.int32) * MOE_BLOCK, side='right'), 0, N_EXPERTS - 1)
    xs = jnp.take(h, slot_tok, axis=0, mode='fill', fill_value=0).reshape(n_blk, MOE_BLOCK, -1)

    def run_block(args):
        xb, e = args
        return swiglu(xb, w1[e], w3[e], w2[e])

    yb = lax.map(run_block, (xs, blk_e)).reshape(n_blk * MOE_BLOCK, -1)
    yb = yb * slot_g[:, None].astype(yb.dtype)
    return jnp.zeros_like(h).at[slot_tok].add(yb, mode='drop')


def setup_inputs(seed: int = 0) -> dict:
    key = jax.random.key(seed)
    ks = jax.random.split(key, 32)

    def nrm(k, shape, scale):
        return jax.random.normal(k, shape, F32) * scale

    def gain(k, shape):
        return 1.0 + 0.02 * jax.random.normal(k, shape, F32)

    return {
        "x": nrm(ks[0], (BATCH, SEQ, D_MODEL), 1.0),
        "attn_norm": gain(ks[1], (DEPTH, D_MODEL)),
        "ffn_norm": gain(ks[2], (DEPTH, D_MODEL)),
        "final_norm": gain(ks[3], (D_MODEL,)),
        "w_in": nrm(ks[4], (DEPTH, D_MODEL, D_IN), D_MODEL ** -0.5),
        "w_out": nrm(ks[5], (DEPTH, D_MIX, D_MODEL), D_MIX ** -0.5),
        "s5_lambda_re": -0.5 + 0.01 * jax.random.normal(ks[6], (DEPTH, S5_GROUPS, S5_STATE), F32),
        "s5_lambda_im": jnp.pi * jnp.arange(S5_STATE, dtype=F32) + 0.01 * jax.random.normal(ks[7], (DEPTH, S5_GROUPS, S5_STATE), F32),
        "s5_log_dt": jax.random.uniform(ks[8], (DEPTH, S5_GROUPS), F32, minval=math.log(DT_MIN), maxval=math.log(DT_MAX)),
        "s5_b_re": nrm(ks[9], (DEPTH, S5_GROUPS, S5_STATE, S5_GROUP), (2 * S5_GROUP) ** -0.5),
        "s5_b_im": nrm(ks[10], (DEPTH, S5_GROUPS, S5_STATE, S5_GROUP), (2 * S5_GROUP) ** -0.5),
        "s5_c_re": nrm(ks[11], (DEPTH, S5_GROUPS, S5_GROUP, S5_STATE), S5_STATE ** -0.5),
        "s5_c_im": nrm(ks[12], (DEPTH, S5_GROUPS, S5_GROUP, S5_STATE), S5_STATE ** -0.5),
        "s5_d": nrm(ks[13], (DEPTH, S5_WIDTH), 1.0),
        "s5_glu": nrm(ks[14], (DEPTH, S5_WIDTH, S5_WIDTH), S5_WIDTH ** -0.5),
        "s5_out_norm": gain(ks[15], (DEPTH, S5_WIDTH)),
        "conv_w": nrm(ks[16], (DEPTH, CONV_K, CONV_WIDTH), CONV_K ** -0.5),
        "conv_out_norm": gain(ks[17], (DEPTH, CONV_WIDTH)),
        "hg_lower_bounds": nrm(ks[18], (DEPTH, HG_WIDTH), 0.1),
        "hg_out_norm": gain(ks[19], (DEPTH, HG_WIDTH)),
        "ffn_w1": nrm(ks[20], (N_DENSE, D_MODEL, D_FF), D_MODEL ** -0.5),
        "ffn_w3": nrm(ks[21], (N_DENSE, D_MODEL, D_FF), D_MODEL ** -0.5),
        "ffn_w2": nrm(ks[22], (N_DENSE, D_FF, D_MODEL), D_FF ** -0.5),
        "moe_router": nrm(ks[23], (N_MOE, D_MODEL, N_EXPERTS), D_MODEL ** -0.5),
        "moe_w1": nrm(ks[24], (N_MOE, N_EXPERTS, D_MODEL, D_FF), D_MODEL ** -0.5),
        "moe_w3": nrm(ks[25], (N_MOE, N_EXPERTS, D_MODEL, D_FF), D_MODEL ** -0.5),
        "moe_w2": nrm(ks[26], (N_MOE, N_EXPERTS, D_FF, D_MODEL), D_FF ** -0.5),
    }


def reference(x, attn_norm, ffn_norm, final_norm, w_in, w_out,
              s5_lambda_re, s5_lambda_im, s5_log_dt, s5_b_re, s5_b_im, s5_c_re, s5_c_im,
              s5_d, s5_glu, s5_out_norm, conv_w, conv_out_norm, hg_lower_bounds, hg_out_norm,
              ffn_w1, ffn_w3, ffn_w2, moe_router, moe_w1, moe_w3, moe_w2):
    lb_soft = jax.nn.softmax(hg_lower_bounds.astype(F32), axis=0)
    lb_all = jnp.cumsum(lb_soft, axis=0) - lb_soft[0]
    h = x
    bsz, L, _ = x.shape
    for l in range(DEPTH):
        hn = rmsnorm(h, attn_norm[l])
        u, cb, cc, cv, hq, hf, hi, hgate = split_cols(hn @ w_in[l])
        y_s5 = s5_mixer(u, s5_lambda_re[l], s5_lambda_im[l], s5_log_dt[l], s5_b_re[l], s5_b_im[l],
                        s5_c_re[l], s5_c_im[l], s5_d[l], s5_glu[l])
        y_s5 = rmsnorm(y_s5, s5_out_norm[l])
        y_conv = rmsnorm(conv_mixer(cb, cc, cv, conv_w[l]), conv_out_norm[l])
        y_hg = hgrn2_mixer(hq, hf, hi, hgate, lb_all[l], hg_out_norm[l])
        mix = jnp.concatenate([y_s5.astype(h.dtype), y_conv.astype(h.dtype), y_hg.astype(h.dtype)], axis=-1)
        h = h + mix @ w_out[l]
        hn = rmsnorm(h, ffn_norm[l])
        j = l // 2
        if l % 2 == 0:
            ff = swiglu(hn, ffn_w1[j], ffn_w3[j], ffn_w2[j])
        else:
            ff = moe_swiglu(hn.reshape(bsz * L, D_MODEL), moe_router[j], moe_w1[j], moe_w3[j], moe_w2[j]).reshape(bsz, L, D_MODEL)
        h = h + ff
    return rmsnorm(h, final_norm)
```

```python
import functools
import math

import numpy as np
import jax
import jax.numpy as jnp
from jax import lax
from jax.experimental import pallas as pl
from jax.experimental.pallas import tpu as pltpu

F32 = jnp.float32
BF16 = jnp.bfloat16
I32 = jnp.int32

NORM_EPS = 1e-6
LB_FLOOR = 1e-30
TOP_K = 2

S5_WIDTH = 256
S5_GROUP = 16
S5_STATE = 64
CONV_WIDTH = 256
CONV_K = 3
HG_WIDTH = 512
HG_HEAD_DIM = 128
HG_HEADS = HG_WIDTH // HG_HEAD_DIM
LANES = 128
COL_CONV = S5_WIDTH
COL_HG = S5_WIDTH + 3 * CONV_WIDTH

VMEM_LIMIT = 56 * 1024 * 1024
S5_CHUNK = 32
HG_CHUNK = 64
HG_LEVELS = 6
HG_TBLOCK = 256
ROW_TILE = 512
FFN_ROW_TILE = 1024
FFN_COL_TILE = 512
MOE_TILE = 512
MOE_COL_TILE = 512
TOK_WIN = 256
SLOT_WIN = TOK_WIN + LANES
SLOT_ALIGN = 16


def _cparams(*sem):
    return pltpu.CompilerParams(dimension_semantics=sem, vmem_limit_bytes=VMEM_LIMIT)


def _rms(x, g):
    ms = jnp.mean(x * x, axis=-1, keepdims=True)
    return x * lax.rsqrt(ms + NORM_EPS) * g


def _sigmoid(x):
    return 1.0 / (1.0 + jnp.exp(-x))


def _norm_inproj_kernel(h_ref, g_ref, w_ref, o_ref):
    xn = _rms(h_ref[...], g_ref[...]).astype(BF16)
    o_ref[...] = jnp.dot(xn, w_ref[...], preferred_element_type=F32)


def _norm_inproj(h, g, w):
    t, d = h.shape
    n = w.shape[1]
    tm = min(ROW_TILE, t)
    return pl.pallas_call(
        _norm_inproj_kernel,
        grid=(t // tm,),
        in_specs=[pl.BlockSpec((tm, d), lambda i: (i, 0)),
                  pl.BlockSpec((1, d), lambda i: (0, 0)),
                  pl.BlockSpec((d, n), lambda i: (0, 0))],
        out_specs=pl.BlockSpec((tm, n), lambda i: (i, 0)),
        out_shape=jax.ShapeDtypeStruct((t, n), F32),
        compiler_params=_cparams("parallel"),
        name="norm_inproj",
    )(h, g.reshape(1, d), w)


def _s5_operators(lam_re, lam_im, log_dt, b_re, b_im, c_re, c_im, n_scan):
    lc = S5_CHUNK
    hi = lax.Precision.HIGHEST
    lam = lax.complex(lam_re.astype(F32), lam_im.astype(F32))
    dt = jnp.exp(log_dt.astype(F32))[:, None]
    z = lam * dt
    lam_bar = jnp.exp(z)
    b_bar = ((lam_bar - 1.0) / lam)[..., None] * lax.complex(b_re.astype(F32), b_im.astype(F32))
    c = lax.complex(c_re.astype(F32), c_im.astype(F32))
    taus = jnp.arange(lc + 1, dtype=F32)
    pw = jnp.exp(z[None] * taus[:, None, None])
    g_, p_ = lam_re.shape

    def re_contract(cp, bb):
        return (jnp.einsum('tgcp,gpi->tgci', jnp.real(cp), jnp.real(bb), precision=hi)
                - jnp.einsum('tgcp,gpi->tgci', jnp.imag(cp), jnp.imag(bb), precision=hi))

    cp0 = c[None] * pw[:lc, :, None, :]
    kt = re_contract(cp0, b_bar)
    ktt = kt.transpose(1, 0, 3, 2)
    s_idx = np.arange(lc)[:, None]
    t_idx = np.arange(lc)[None, :]
    lag = np.clip(t_idx - s_idx, 0, lc - 1)
    causal = jnp.asarray((t_idx >= s_idx), F32)
    m_intra = ktt[:, lag] * causal[None, :, :, None, None]
    m_intra = m_intra.transpose(0, 1, 3, 2, 4).reshape(g_, lc * S5_GROUP, lc * S5_GROUP)

    ms = pw[lc - 1::-1][:lc]
    ms = ms[:, :, :, None] * b_bar[None]
    ms = ms.transpose(1, 0, 3, 2).reshape(g_, lc * S5_GROUP, p_)
    m_state = jnp.concatenate([jnp.real(ms), jnp.imag(ms)], axis=-1)

    cp1 = c[None] * pw[1:lc + 1, :, None, :]
    cp1 = cp1.transpose(1, 3, 0, 2).reshape(g_, p_, lc * S5_GROUP)
    m_carry = jnp.concatenate([jnp.real(cp1), -jnp.imag(cp1)], axis=1)

    steps = (2.0 ** jnp.arange(n_scan, dtype=F32)) * lc
    ak = jnp.exp(z[None] * steps[:, None, None])
    ar = jnp.concatenate([jnp.real(ak), jnp.real(ak)], axis=-1).transpose(1, 0, 2)
    ai = jnp.concatenate([-jnp.imag(ak), jnp.imag(ak)], axis=-1).transpose(1, 0, 2)
    kpad = max(8, -(-n_scan // 8) * 8)
    ar = jnp.pad(ar, ((0, 0), (0, kpad - n_scan), (0, 0)))
    ai = jnp.pad(ai, ((0, 0), (0, kpad - n_scan), (0, 0)))
    return m_intra.astype(BF16), m_state.astype(BF16), m_carry.astype(BF16), ar, ai


def _s5_kernel(u_ref, mi_ref, ms_ref, mc_ref, ar_ref, ai_ref, y_ref, *, n_scan):
    u = u_ref[0]
    x = jnp.dot(u, ms_ref[0], preferred_element_type=F32)
    row = lax.broadcasted_iota(I32, x.shape, 0)
    half = x.shape[1] // 2
    for k in range(n_scan):
        d = 1 << k
        s = jnp.where(row >= d, pltpu.roll(x, d, axis=0), 0.0)
        x = x + ar_ref[0, k:k + 1, :] * s + ai_ref[0, k:k + 1, :] * pltpu.roll(s, half, axis=1)
    xe = jnp.where(row >= 1, pltpu.roll(x, 1, axis=0), 0.0)
    y = jnp.dot(u, mi_ref[0], preferred_element_type=F32)
    y = y + jnp.dot(xe.astype(BF16), mc_ref[0], preferred_element_type=F32)
    y_ref[0] = y


def _s5_conv(u, ops):
    m_intra, m_state, m_carry, ar, ai = ops
    t = u.shape[0]
    lc = S5_CHUNK
    g_ = S5_WIDTH // S5_GROUP
    nch = t // lc
    n_scan = int(math.log2(nch))
    assert (1 << n_scan) == nch
    w = lc * S5_GROUP
    p2 = m_state.shape[-1]
    ug = u.astype(BF16).reshape(nch, lc, g_, S5_GROUP).transpose(2, 0, 1, 3).reshape(g_, nch, w)
    y = pl.pallas_call(
        functools.partial(_s5_kernel, n_scan=n_scan),
        grid=(g_,),
        in_specs=[pl.BlockSpec((1, nch, w), lambda g: (g, 0, 0)),
                  pl.BlockSpec((1, w, w), lambda g: (g, 0, 0)),
                  pl.BlockSpec((1, w, p2), lambda g: (g, 0, 0)),
                  pl.BlockSpec((1, p2, w), lambda g: (g, 0, 0)),
                  pl.BlockSpec((1, ar.shape[1], p2), lambda g: (g, 0, 0)),
                  pl.BlockSpec((1, ai.shape[1], p2), lambda g: (g, 0, 0))],
        out_specs=pl.BlockSpec((1, nch, w), lambda g: (g, 0, 0)),
        out_shape=jax.ShapeDtypeStruct((g_, nch, w), F32),
        compiler_params=_cparams("parallel"),
        name="s5_conv",
    )(ug, m_intra, m_state, m_carry, ar, ai)
    return y.reshape(g_, nch, lc, S5_GROUP).transpose(1, 2, 0, 3).reshape(t, S5_WIDTH)


def _hg_select_matrix():
    c = HG_CHUNK
    r = np.zeros((HG_LEVELS * c, c), np.float32)
    for lev in range(HG_LEVELS):
        m = c >> lev
        for t in range(c):
            r[lev * c + t, (t // m) * m + m // 2 - 1] = 1.0
    return r


def _hgrn_kernel(q_ref, f_ref, i_ref, gt_ref, lb_ref, lbf_ref, ng_ref, sel_ref, o_ref, st_ref):
    c = HG_CHUNK

    @pl.when(pl.program_id(1) == 0)
    def _():
        st_ref[...] = jnp.zeros_like(st_ref)

    fp = f_ref[...]
    tb = fp.shape[0]
    lb = lb_ref[0]
    en = jnp.exp(-jnp.abs(fp))
    rc = 1.0 / (1.0 + en)
    pos_f = fp >= 0.0
    sig_p = jnp.where(pos_f, rc, en * rc)
    sig_n = jnp.where(pos_f, en * rc, rc)
    lf = jnp.log(lbf_ref[0] + (1.0 - lb) * sig_p)
    kk = (1.0 - lb) * sig_n
    q = q_ref[...]
    qs = q * _sigmoid(q)
    v = i_ref[...]
    gt = gt_ref[...]
    gts = gt * _sigmoid(gt)

    row = lax.broadcasted_iota(I32, (tb, LANES), 0)
    pos = jnp.bitwise_and(row, c - 1)
    b = lf
    for k in range(HG_LEVELS):
        d = 1 << k
        b = b + jnp.where(pos >= d, pltpu.roll(b, d, axis=0), 0.0)

    prow = lax.broadcasted_iota(I32, (c, LANES), 0)
    ti = lax.broadcasted_iota(I32, (c, c), 0)
    si = lax.broadcasted_iota(I32, (c, c), 1)
    sel = sel_ref[...]
    ng = ng_ref[0]
    st = st_ref[...]
    nt = (((1,), (1,)), ((), ()))
    tn = (((0,), (0,)), ((), ()))
    for n in range(tb // c):
        sl = slice(n * c, (n + 1) * c)
        bc, qc, kc, vc = b[sl], qs[sl], kk[sl], v[sl]
        b_hi = bc.astype(BF16)
        b_lo = (bc - b_hi.astype(F32)).astype(BF16)
        beta = (jnp.dot(sel, b_hi, preferred_element_type=F32)
                + jnp.dot(sel, b_lo, preferred_element_type=F32))
        a = jnp.zeros((c, c), F32)
        for lev in range(HG_LEVELS):
            m = c >> lev
            upper = jnp.bitwise_and(prow, m - 1) >= (m // 2)
            e = bc - beta[lev * c:(lev + 1) * c]
            w = jnp.exp(jnp.where(upper, e, -e))
            qt = jnp.where(upper, qc * w, 0.0).astype(BF16)
            kt = jnp.where(upper, 0.0, kc * w).astype(BF16)
            s = lax.dot_general(qt, kt, nt, preferred_element_type=F32)
            same = jnp.right_shift(ti, HG_LEVELS - lev) == jnp.right_shift(si, HG_LEVELS - lev)
            a = a + jnp.where(same, s, 0.0)
        vb = vc.astype(BF16)
        o = jnp.dot(a.astype(BF16), vb, preferred_element_type=F32)
        o = o + jnp.sum(qc * kc, axis=-1, keepdims=True) * vc
        o = o + lax.dot_general((qc * jnp.exp(bc)).astype(BF16), st.astype(BF16), nt,
                                preferred_element_type=F32)
        bl = bc[c - 1:c, :]
        khat = (kc * jnp.exp(bl - bc)).astype(BF16)
        st = st * jnp.exp(bl) + lax.dot_general(vb, khat, tn, preferred_element_type=F32)
        o_ref[sl, :] = _rms(o, ng) * gts[sl]
    st_ref[...] = st


def _hgrn(proj, lb, norm_g):
    t = proj.shape[0]
    tb = min(HG_TBLOCK, t)
    lbh = jnp.clip(lb.astype(F32), 0.0, 1.0 - 1e-6).reshape(HG_HEADS, 1, LANES)
    lbf = jnp.maximum(lbh, LB_FLOOR)
    ng = norm_g.astype(F32).reshape(HG_HEADS, 1, LANES)
    sel = jnp.asarray(_hg_select_matrix(), BF16)
    cb = COL_HG // LANES

    def col(k):
        return pl.BlockSpec((tb, LANES), lambda h, i, k=k: (i, cb + k * HG_HEADS + h))

    vec = pl.BlockSpec((1, 1, LANES), lambda h, i: (h, 0, 0))
    return pl.pallas_call(
        _hgrn_kernel,
        grid=(HG_HEADS, t // tb),
        in_specs=[col(0), col(1), col(2), col(3), vec, vec, vec,
                  pl.BlockSpec(sel.shape, lambda h, i: (0, 0))],
        out_specs=pl.BlockSpec((tb, LANES), lambda h, i: (i, h)),
        out_shape=jax.ShapeDtypeStruct((t, HG_WIDTH), F32),
        scratch_shapes=[pltpu.VMEM((LANES, LANES), F32)],
        compiler_params=_cparams("parallel", "arbitrary"),
        name="hgrn2",
    )(proj, proj, proj, proj, lbh, lbf, ng, sel)


def _mix_out_kernel(ys_ref, u_ref, cb_ref, cc_ref, cv_ref, hg_ref, h_ref,
                    d_ref, glu_ref, sn_ref, cw_ref, cn_ref, wo_ref, o_ref, carry_ref):
    @pl.when(pl.program_id(0) == 0)
    def _():
        carry_ref[...] = jnp.zeros_like(carry_ref)

    y = ys_ref[...] + d_ref[...] * u_ref[...]
    y = jax.nn.gelu(y)
    y = y * _sigmoid(jnp.dot(y.astype(BF16), glu_ref[...], preferred_element_type=F32))
    y_s5 = _rms(y, sn_ref[...])

    z = cc_ref[...] * cv_ref[...]
    tm = z.shape[0]
    row = lax.broadcasted_iota(I32, z.shape, 0)
    p1 = carry_ref[7:8, :]
    p2 = carry_ref[6:7, :]
    z1 = jnp.where(row == 0, p1, pltpu.roll(z, 1, axis=0))
    z2 = jnp.where(row == 0, p2, jnp.where(row == 1, p1, pltpu.roll(z, 2, axis=0)))
    carry_ref[...] = z[tm - 8:tm, :]
    yc = cb_ref[...] * (z2 * cw_ref[0:1, :] + z1 * cw_ref[1:2, :] + z * cw_ref[2:3, :])
    y_cv = _rms(yc, cn_ref[...])

    acc = h_ref[...]
    acc = acc + jnp.dot(y_s5.astype(BF16), wo_ref[0:S5_WIDTH, :], preferred_element_type=F32)
    acc = acc + jnp.dot(y_cv.astype(BF16), wo_ref[S5_WIDTH:COL_HG // 2, :], preferred_element_type=F32)
    acc = acc + jnp.dot(hg_ref[...].astype(BF16), wo_ref[COL_HG // 2:, :], preferred_element_type=F32)
    o_ref[...] = acc


def _mix_out(ys5, proj, yhg, h, s5_d, s5_glu, s5_norm, conv_w, conv_norm, w_out):
    t, d = h.shape
    tm = min(ROW_TILE, t)
    cw = jnp.pad(conv_w.astype(F32), ((0, 8 - CONV_K), (0, 0)))
    wq = S5_WIDTH

    def rowblk(width, colblk):
        return pl.BlockSpec((tm, width), lambda i, c=colblk: (i, c))

    def full(shape):
        return pl.BlockSpec(shape, lambda i: (0,) * len(shape))

    return pl.pallas_call(
        _mix_out_kernel,
        grid=(t // tm,),
        in_specs=[rowblk(wq, 0), rowblk(wq, 0), rowblk(wq, 1), rowblk(wq, 2), rowblk(wq, 3),
                  rowblk(HG_WIDTH, 0), rowblk(d, 0),
                  full((1, wq)), full((wq, wq)), full((1, wq)), full((8, wq)), full((1, wq)),
                  full(w_out.shape)],
        out_specs=rowblk(d, 0),
        out_shape=jax.ShapeDtypeStruct((t, d), F32),
        scratch_shapes=[pltpu.VMEM((8, CONV_WIDTH), F32)],
        compiler_params=_cparams("arbitrary"),
        name="mix_out",
    )(ys5, proj, proj, proj, proj, yhg, h,
      s5_d.astype(F32).reshape(1, wq), s5_glu, s5_norm.astype(F32).reshape(1, wq), cw,
      conv_norm.astype(F32).reshape(1, wq), w_out)


def _ffn_kernel(h_ref, g_ref, w1_ref, w3_ref, w2_ref, o_ref, hn_ref, acc_ref):
    j = pl.program_id(1)

    @pl.when(j == 0)
    def _():
        x = h_ref[...]
        hn_ref[...] = _rms(x, g_ref[...]).astype(BF16)
        acc_ref[...] = x

    hn = hn_ref[...]
    a = jnp.dot(hn, w1_ref[...], preferred_element_type=F32)
    b = jnp.dot(hn, w3_ref[...], preferred_element_type=F32)
    gact = (a * _sigmoid(a) * b).astype(BF16)
    acc_ref[...] += jnp.dot(gact, w2_ref[...], preferred_element_type=F32)

    @pl.when(j == pl.num_programs(1) - 1)
    def _():
        o_ref[...] = acc_ref[...]


def _ffn(h, g, w1, w3, w2):
    t, d = h.shape
    f = w1.shape[1]
    tm = min(FFN_ROW_TILE, t)
    tf = FFN_COL_TILE
    return pl.pallas_call(
        _ffn_kernel,
        grid=(t // tm, f // tf),
        in_specs=[pl.BlockSpec((tm, d), lambda i, j: (i, 0)),
                  pl.BlockSpec((1, d), lambda i, j: (0, 0)),
                  pl.BlockSpec((d, tf), lambda i, j: (0, j)),
                  pl.BlockSpec((d, tf), lambda i, j: (0, j)),
                  pl.BlockSpec((tf, d), lambda i, j: (j, 0))],
        out_specs=pl.BlockSpec((tm, d), lambda i, j: (i, 0)),
        out_shape=jax.ShapeDtypeStruct((t, d), F32),
        scratch_shapes=[pltpu.VMEM((tm, d), BF16), pltpu.VMEM((tm, d), F32)],
        compiler_params=_cparams("parallel", "arbitrary"),
        name="ffn_swiglu",
    )(h, g.astype(F32).reshape(1, d), w1, w3, w2)


def _router_kernel(h_ref, g_ref, r_ref, hn_ref, info_ref, before_ref, total_ref, cnt_ref, *, n_exp):
    @pl.when(pl.program_id(0) == 0)
    def _():
        cnt_ref[...] = jnp.zeros_like(cnt_ref)

    xn = _rms(h_ref[...], g_ref[...])
    hn_ref[...] = xn.astype(BF16)
    logits = jnp.dot(xn, r_ref[...], preferred_element_type=F32, precision=lax.Precision.HIGHEST)
    tm = logits.shape[0]
    lane = lax.broadcasted_iota(I32, logits.shape, 1)
    lanef = lane.astype(F32)
    neg = jnp.float32(-jnp.inf)
    lg = jnp.where(lane < n_exp, logits, neg)
    m1 = jnp.max(lg, axis=-1, keepdims=True)
    i1 = jnp.min(jnp.where(lg == m1, lanef, float(LANES)), axis=-1, keepdims=True)
    oh1 = lanef == i1
    lg2 = jnp.where(oh1, neg, lg)
    m2 = jnp.max(lg2, axis=-1, keepdims=True)
    i2 = jnp.min(jnp.where(lg2 == m2, lanef, float(LANES)), axis=-1, keepdims=True)
    oh2 = lanef == i2
    ex = jnp.exp(m2 - m1)
    g1 = 1.0 / (1.0 + ex)
    g2 = ex * g1
    chosen = jnp.where(oh1, 1.0, jnp.where(oh2, 1.0, 0.0))
    ri = lax.broadcasted_iota(I32, (tm, tm), 0)
    ci = lax.broadcasted_iota(I32, (tm, tm), 1)
    tri = jnp.where(ri > ci, 1.0, 0.0).astype(BF16)
    before = cnt_ref[...]
    cexcl = jnp.dot(tri, chosen.astype(BF16), preferred_element_type=F32) + before
    rank1 = jnp.sum(jnp.where(oh1, cexcl, 0.0), axis=-1, keepdims=True)
    rank2 = jnp.sum(jnp.where(oh2, cexcl, 0.0), axis=-1, keepdims=True)
    info = jnp.where(lane == 0, i1, jnp.where(lane == 1, i2, jnp.where(lane == 2, g1, jnp.where(
        lane == 3, g2, jnp.where(lane == 4, rank1, jnp.where(lane == 5, rank2, 0.0))))))
    info_ref[...] = info
    before_ref[0] = jnp.broadcast_to(before, before_ref.shape[1:])
    total = before + jnp.sum(chosen, axis=0, keepdims=True)
    cnt_ref[...] = total
    total_ref[...] = jnp.broadcast_to(total, total_ref.shape)


def _router(h, g, router):
    t, d = h.shape
    n_exp = router.shape[1]
    tm = min(TOK_WIN, t)
    ntw = t // tm
    rp = jnp.pad(router.astype(F32), ((0, 0), (0, LANES - n_exp)))
    return pl.pallas_call(
        functools.partial(_router_kernel, n_exp=n_exp),
        grid=(ntw,),
        in_specs=[pl.BlockSpec((tm, d), lambda i: (i, 0)),
                  pl.BlockSpec((1, d), lambda i: (0, 0)),
                  pl.BlockSpec((d, LANES), lambda i: (0, 0))],
        out_specs=[pl.BlockSpec((tm, d), lambda i: (i, 0)),
                   pl.BlockSpec((tm, LANES), lambda i: (i, 0)),
                   pl.BlockSpec((1, 8, LANES), lambda i: (i, 0, 0)),
                   pl.BlockSpec((8, LANES), lambda i: (0, 0))],
        out_shape=[jax.ShapeDtypeStruct((t, d), BF16),
                   jax.ShapeDtypeStruct((t, LANES), F32),
                   jax.ShapeDtypeStruct((ntw, 8, LANES), F32),
                   jax.ShapeDtypeStruct((8, LANES), F32)],
        scratch_shapes=[pltpu.VMEM((1, LANES), F32)],
        compiler_params=_cparams("arbitrary"),
        name="moe_router",
    )(h, g.astype(F32).reshape(1, d), rp)


def _gather_kernel(ilo_ref, ihi_ref, dest_ref, hn_hbm, xs_ref, buf_ref, sem, acc_ref):
    b = pl.program_id(0)
    tmo = acc_ref.shape[0]
    win = buf_ref.shape[0]
    acc_ref[...] = jnp.zeros_like(acc_ref)
    slot = b * tmo + lax.broadcasted_iota(I32, (tmo, win), 0)

    def body(i, carry):
        cp = pltpu.make_async_copy(hn_hbm.at[pl.ds(pl.multiple_of(i * win, win), win), :], buf_ref, sem)
        cp.start()
        d = dest_ref[i]
        hit = jnp.where(d[0:1, :] == slot, 1.0, jnp.where(d[1:2, :] == slot, 1.0, 0.0)).astype(BF16)
        cp.wait()
        acc_ref[...] += jnp.dot(hit, buf_ref[...], preferred_element_type=F32)
        return carry

    lax.fori_loop(ilo_ref[b], ihi_ref[b] + 1, body, 0)
    xs_ref[...] = acc_ref[...].astype(BF16)


def _gather(hn, dest, ilo, ihi, n_tiles):
    t, d = hn.shape
    win = dest.shape[-1]
    return pl.pallas_call(
        _gather_kernel,
        grid_spec=pltpu.PrefetchScalarGridSpec(
            num_scalar_prefetch=2,
            grid=(n_tiles,),
            in_specs=[pl.BlockSpec(dest.shape, lambda b, lo, hi: (0, 0, 0)),
                      pl.BlockSpec(memory_space=pl.ANY)],
            out_specs=pl.BlockSpec((MOE_TILE, d), lambda b, lo, hi: (b, 0)),
            scratch_shapes=[pltpu.VMEM((win, d), BF16), pltpu.SemaphoreType.DMA(()),
                            pltpu.VMEM((MOE_TILE, d), F32)]),
        out_shape=jax.ShapeDtypeStruct((n_tiles * MOE_TILE, d), BF16),
        compiler_params=_cparams("arbitrary"),
        name="moe_gather",
    )(ilo, ihi, dest, hn)


def _expert_kernel(te_ref, tv_ref, x_ref, w1_ref, w3_ref, w2_ref, y_ref, acc_ref):
    b = pl.program_id(0)
    j = pl.program_id(1)

    @pl.when(j == 0)
    def _():
        acc_ref[...] = jnp.zeros_like(acc_ref)

    @pl.when(tv_ref[b] == 1)
    def _():
        x = x_ref[...]
        a = jnp.dot(x, w1_ref[...], preferred_element_type=F32)
        c = jnp.dot(x, w3_ref[...], preferred_element_type=F32)
        gact = (a * _sigmoid(a) * c).astype(BF16)
        acc_ref[...] += jnp.dot(gact, w2_ref[...], preferred_element_type=F32)

    @pl.when(j == pl.num_programs(1) - 1)
    def _():
        y_ref[...] = acc_ref[...].astype(BF16)


def _experts(xs, tile_e, tile_v, w1, w3, w2):
    ns, d = xs.shape
    f = w1.shape[2]
    tf = MOE_COL_TILE
    n_tiles = ns // MOE_TILE
    return pl.pallas_call(
        _expert_kernel,
        grid_spec=pltpu.PrefetchScalarGridSpec(
            num_scalar_prefetch=2,
            grid=(n_tiles, f // tf),
            in_specs=[pl.BlockSpec((MOE_TILE, d), lambda b, j, te, tv: (b, 0)),
                      pl.BlockSpec((None, d, tf), lambda b, j, te, tv: (te[b], 0, j)),
                      pl.BlockSpec((None, d, tf), lambda b, j, te, tv: (te[b], 0, j)),
                      pl.BlockSpec((None, tf, d), lambda b, j, te, tv: (te[b], j, 0))],
            out_specs=pl.BlockSpec((MOE_TILE, d), lambda b, j, te, tv: (b, 0)),
            scratch_shapes=[pltpu.VMEM((MOE_TILE, d), F32)]),
        out_shape=jax.ShapeDtypeStruct((ns, d), BF16),
        compiler_params=_cparams("arbitrary", "arbitrary"),
        name="moe_experts",
    )(tile_e, tile_v, xs, w1, w3, w2)


def _combine_kernel(ws_ref, po_ref, info_ref, h_ref, yb_hbm, o_ref, buf_ref, sem, *, n_exp):
    i = pl.program_id(0)
    win = buf_ref.shape[1]

    def copy(e):
        start = pl.multiple_of(ws_ref[i * n_exp + e], SLOT_ALIGN)
        return pltpu.make_async_copy(yb_hbm.at[pl.ds(start, win), :], buf_ref.at[e], sem.at[e])

    for e in range(n_exp):
        copy(e).start()
    info = info_ref[...]
    tm = info.shape[0]
    e1, e2 = info[:, 0:1], info[:, 1:2]
    g1, g2 = info[:, 2:3], info[:, 3:4]
    r1, r2 = info[:, 4:5], info[:, 5:6]
    lanef = lax.broadcasted_iota(I32, (tm, win), 1).astype(F32)
    acc = h_ref[...]
    for e in range(n_exp):
        off = (ws_ref[i * n_exp + e] - po_ref[e]).astype(F32)
        w1 = jnp.where((e1 == float(e)) & (r1 - off == lanef), g1, 0.0)
        w2 = jnp.where((e2 == float(e)) & (r2 - off == lanef), g2, 0.0)
        pt = (w1 + w2).astype(BF16)
        copy(e).wait()
        acc = acc + jnp.dot(pt, buf_ref[e], preferred_element_type=F32)
    o_ref[...] = acc


def _combine(info, h, yb, win_start, pstart):
    t, d = h.shape
    n_exp = pstart.shape[0]
    tm = min(TOK_WIN, t)
    return pl.pallas_call(
        functools.partial(_combine_kernel, n_exp=n_exp),
        grid_spec=pltpu.PrefetchScalarGridSpec(
            num_scalar_prefetch=2,
            grid=(t // tm,),
            in_specs=[pl.BlockSpec((tm, LANES), lambda i, ws, po: (i, 0)),
                      pl.BlockSpec((tm, d), lambda i, ws, po: (i, 0)),
                      pl.BlockSpec(memory_space=pl.ANY)],
            out_specs=pl.BlockSpec((tm, d), lambda i, ws, po: (i, 0)),
            scratch_shapes=[pltpu.VMEM((n_exp, SLOT_WIN, d), BF16),
                            pltpu.SemaphoreType.DMA((n_exp,))]),
        out_shape=jax.ShapeDtypeStruct((t, d), F32),
        compiler_params=_cparams("arbitrary"),
        name="moe_combine",
    )(win_start, pstart, info, h, yb)


def _moe(h, g, router, w1, w3, w2):
    t, d = h.shape
    n_exp = router.shape[1]
    tw = min(TOK_WIN, t)
    ntw = t // tw
    hn, info, before, total = _router(h, g, router)

    counts = total[0, :n_exp].astype(I32)
    padded = (counts + MOE_TILE - 1) // MOE_TILE * MOE_TILE
    pend = jnp.cumsum(padded)
    pstart = pend - padded
    n_tiles = (t * TOP_K) // MOE_TILE + n_exp
    n_slots = n_tiles * MOE_TILE
    tile0 = jnp.arange(n_tiles, dtype=I32) * MOE_TILE
    tile_e = jnp.clip(jnp.searchsorted(pend, tile0, side='right'), 0, n_exp - 1).astype(I32)
    tile_v = (tile0 < pend[-1]).astype(I32)
    r0 = tile0 - pstart[tile_e]
    cum = jnp.concatenate([before[:, 0, :n_exp], total[0:1, :n_exp]], axis=0).astype(I32)
    cum_t = cum[:, tile_e]
    ilo = jnp.sum((cum_t[1:] <= r0[None, :]).astype(I32), axis=0)
    ihi = jnp.sum((cum_t[:-1] < (r0 + MOE_TILE)[None, :]).astype(I32), axis=0) - 1
    ilo = jnp.where(tile_v == 1, ilo, 0).astype(I32)
    ihi = jnp.where(tile_v == 1, ihi, -1).astype(I32)

    e12 = info[:, 0:2].astype(I32)
    dest12 = pstart[e12] + info[:, 4:6].astype(I32)
    dest = jnp.full((ntw, 8, tw), -1, I32).at[:, 0:2, :].set(
        dest12.reshape(ntw, tw, 2).transpose(0, 2, 1))

    xs = _gather(hn, dest, ilo, ihi, n_tiles)
    yb = _experts(xs, tile_e, tile_v, w1, w3, w2)

    lo = pstart[None, :] + cum[:-1]
    win_start = jnp.minimum(lo // SLOT_ALIGN * SLOT_ALIGN, n_slots - SLOT_WIN).astype(I32)
    return _combine(info, h, yb, win_start.reshape(-1), pstart.astype(I32))


def _final_norm_kernel(h_ref, g_ref, o_ref):
    o_ref[...] = _rms(h_ref[...], g_ref[...])


def _final_norm(h, g):
    t, d = h.shape
    tm = min(ROW_TILE, t)
    return pl.pallas_call(
        _final_norm_kernel,
        grid=(t // tm,),
        in_specs=[pl.BlockSpec((tm, d), lambda i: (i, 0)), pl.BlockSpec((1, d), lambda i: (0, 0))],
        out_specs=pl.BlockSpec((tm, d), lambda i: (i, 0)),
        out_shape=jax.ShapeDtypeStruct((t, d), F32),
        compiler_params=_cparams("parallel"),
        name="final_norm",
    )(h, g.astype(F32).reshape(1, d))


def kernel(x, attn_norm, ffn_norm, final_norm, w_in, w_out, s5_lambda_re, s5_lambda_im, s5_log_dt, s5_b_re, s5_b_im, s5_c_re, s5_c_im, s5_d, s5_glu, s5_out_norm, conv_w, conv_out_norm, hg_lower_bounds, hg_out_norm, ffn_w1, ffn_w3, ffn_w2, moe_router, moe_w1, moe_w3, moe_w2):
    bsz, seq, d = x.shape
    depth = w_in.shape[0]
    t = bsz * seq
    assert bsz == 1, "token mixers are written for a single sequence"
    lb_soft = jax.nn.softmax(hg_lower_bounds.astype(F32), axis=0)
    lb_all = jnp.cumsum(lb_soft, axis=0) - lb_soft[0]
    n_scan = int(math.log2(t // S5_CHUNK))
    h = x.reshape(t, d).astype(F32)
    for l in range(depth):
        proj = _norm_inproj(h, attn_norm[l].astype(F32), w_in[l].astype(BF16))
        ops = _s5_operators(s5_lambda_re[l], s5_lambda_im[l], s5_log_dt[l], s5_b_re[l], s5_b_im[l],
                            s5_c_re[l], s5_c_im[l], n_scan)
        ys5 = _s5_conv(proj[:, :S5_WIDTH], ops)
        yhg = _hgrn(proj, lb_all[l], hg_out_norm[l])
        h = _mix_out(ys5, proj, yhg, h, s5_d[l], s5_glu[l].astype(BF16), s5_out_norm[l],
                     conv_w[l], conv_out_norm[l], w_out[l].astype(BF16))
        j = l // 2
        if l % 2 == 0:
            h = _ffn(h, ffn_norm[l], ffn_w1[j].astype(BF16), ffn_w3[j].astype(BF16), ffn_w2[j].astype(BF16))
        else:
            h = _moe(h, ffn_norm[l], moe_router[j], moe_w1[j].astype(BF16), moe_w3[j].astype(BF16),
                     moe_w2[j].astype(BF16))
    return _final_norm(h, final_norm).reshape(bsz, seq, d)
```

```python
import functools
import math

import numpy as np
import jax
import jax.numpy as jnp
from jax import lax
from jax.experimental import pallas as pl
from jax.experimental.pallas import tpu as pltpu

F32 = jnp.float32
BF16 = jnp.bfloat16
I32 = jnp.int32

NORM_EPS = 1e-6
LB_FLOOR = 1e-30
TOP_K = 2

S5_WIDTH = 256
S5_GROUP = 16
S5_STATE = 64
CONV_WIDTH = 256
CONV_K = 3
HG_WIDTH = 512
HG_HEAD_DIM = 128
HG_HEADS = HG_WIDTH // HG_HEAD_DIM
LANES = 128
COL_CONV = S5_WIDTH
COL_HG = S5_WIDTH + 3 * CONV_WIDTH

VMEM_LIMIT = 56 * 1024 * 1024
S5_CHUNK = 32
HG_CHUNK = 64
HG_LEVELS = 6
HG_TBLOCK = 256
ROW_TILE = 512
FFN_ROW_TILE = 1024
FFN_COL_TILE = 512
MOE_TILE = 1024
MOE_COL_TILE = 512
GATHER_TILE = 256
TOK_WIN = 256
SLOT_ALIGN = 16
SLOT_WIN = TOK_WIN + SLOT_ALIGN


def _cparams(*sem):
    return pltpu.CompilerParams(dimension_semantics=sem, vmem_limit_bytes=VMEM_LIMIT)


def _rms(x, g):
    ms = jnp.mean(x * x, axis=-1, keepdims=True)
    return x * lax.rsqrt(ms + NORM_EPS) * g


def _sigmoid(x):
    return 1.0 / (1.0 + jnp.exp(-x))


def _norm_inproj_kernel(h_ref, g_ref, w_ref, o_ref):
    xn = _rms(h_ref[...], g_ref[...]).astype(BF16)
    o_ref[...] = jnp.dot(xn, w_ref[...], preferred_element_type=F32)


def _norm_inproj(h, g, w):
    t, d = h.shape
    n = w.shape[1]
    tm = min(ROW_TILE, t)
    return pl.pallas_call(
        _norm_inproj_kernel,
        grid=(t // tm,),
        in_specs=[pl.BlockSpec((tm, d), lambda i: (i, 0)),
                  pl.BlockSpec((1, d), lambda i: (0, 0)),
                  pl.BlockSpec((d, n), lambda i: (0, 0))],
        out_specs=pl.BlockSpec((tm, n), lambda i: (i, 0)),
        out_shape=jax.ShapeDtypeStruct((t, n), F32),
        compiler_params=_cparams("parallel"),
        name="norm_inproj",
    )(h, g.reshape(1, d), w)


def _s5_operators(lam_re, lam_im, log_dt, b_re, b_im, c_re, c_im, n_scan):
    lc = S5_CHUNK
    hi = lax.Precision.HIGHEST
    lr, li = lam_re.astype(F32), lam_im.astype(F32)
    dt = jnp.exp(log_dt.astype(F32))[:, None]
    zr, zi = lr * dt, li * dt
    taus = jnp.arange(lc + 1, dtype=F32)[:, None, None]
    mag = jnp.exp(zr[None] * taus)
    pwr, pwi = mag * jnp.cos(zi[None] * taus), mag * jnp.sin(zi[None] * taus)
    nr, ni = pwr[1] - 1.0, pwi[1]
    den = lr * lr + li * li
    qr, qi = (nr * lr + ni * li) / den, (ni * lr - nr * li) / den
    br, bi = b_re.astype(F32), b_im.astype(F32)
    bbr = qr[..., None] * br - qi[..., None] * bi
    bbi = qr[..., None] * bi + qi[..., None] * br
    cr, ci = c_re.astype(F32), c_im.astype(F32)
    g_, p_ = lr.shape

    def c_times_pw(lo):
        wr, wi = pwr[lo:lo + lc, :, None, :], pwi[lo:lo + lc, :, None, :]
        return cr[None] * wr - ci[None] * wi, cr[None] * wi + ci[None] * wr

    cpr, cpi = c_times_pw(0)
    kt = (jnp.einsum('tgcp,gpi->tgci', cpr, bbr, precision=hi)
          - jnp.einsum('tgcp,gpi->tgci', cpi, bbi, precision=hi))
    kflat = kt.transpose(1, 3, 0, 2).reshape(g_, S5_GROUP, lc * S5_GROUP)

    wr, wi = pwr[lc - 1::-1][:, :, :, None], pwi[lc - 1::-1][:, :, :, None]
    msr = (wr * bbr[None] - wi * bbi[None]).transpose(1, 0, 3, 2).reshape(g_, lc * S5_GROUP, p_)
    msi = (wr * bbi[None] + wi * bbr[None]).transpose(1, 0, 3, 2).reshape(g_, lc * S5_GROUP, p_)
    m_state = jnp.concatenate([msr, msi], axis=-1)

    c1r, c1i = c_times_pw(1)
    c1r = c1r.transpose(1, 3, 0, 2).reshape(g_, p_, lc * S5_GROUP)
    c1i = c1i.transpose(1, 3, 0, 2).reshape(g_, p_, lc * S5_GROUP)
    m_carry = jnp.concatenate([c1r, -c1i], axis=1)

    akr, aki = [pwr[lc]], [pwi[lc]]
    for _ in range(n_scan - 1):
        r, i = akr[-1], aki[-1]
        akr.append(r * r - i * i)
        aki.append(2.0 * r * i)
    akr, aki = jnp.stack(akr, axis=1), jnp.stack(aki, axis=1)
    ar = jnp.concatenate([akr, akr], axis=-1)
    ai = jnp.concatenate([-aki, aki], axis=-1)
    kpad = -(-n_scan // 8) * 8
    ar = jnp.pad(ar, ((0, 0), (0, kpad - n_scan), (0, 0)))
    ai = jnp.pad(ai, ((0, 0), (0, kpad - n_scan), (0, 0)))
    return kflat, m_state.astype(BF16), m_carry.astype(BF16), ar, ai


def _s5_kernel(u_ref, kf_ref, ms_ref, mc_ref, ar_ref, ai_ref, y_ref, mi_ref, *, n_scan):
    kf = kf_ref[0]
    lane = lax.broadcasted_iota(I32, kf.shape, 1)
    for s in range(S5_CHUNK):
        sh = s * S5_GROUP
        blk = kf if s == 0 else jnp.where(lane >= sh, pltpu.roll(kf, sh, axis=1), 0.0)
        mi_ref[sh:sh + S5_GROUP, :] = blk.astype(BF16)

    u = u_ref[0]
    x = jnp.dot(u, ms_ref[0], preferred_element_type=F32)
    row = lax.broadcasted_iota(I32, x.shape, 0)
    half = x.shape[1] // 2
    for k in range(n_scan):
        d = 1 << k
        s = jnp.where(row >= d, pltpu.roll(x, d, axis=0), 0.0)
        x = x + ar_ref[0, k:k + 1, :] * s + ai_ref[0, k:k + 1, :] * pltpu.roll(s, half, axis=1)
    xe = jnp.where(row >= 1, pltpu.roll(x, 1, axis=0), 0.0)
    y = jnp.dot(u, mi_ref[...], preferred_element_type=F32)
    y = y + jnp.dot(xe.astype(BF16), mc_ref[0], preferred_element_type=F32)
    y_ref[0] = y


def _s5_conv(u, ops):
    kflat, m_state, m_carry, ar, ai = ops
    t = u.shape[0]
    lc = S5_CHUNK
    g_ = S5_WIDTH // S5_GROUP
    nch = t // lc
    n_scan = int(math.log2(nch))
    assert (1 << n_scan) == nch
    w = lc * S5_GROUP
    p2 = m_state.shape[-1]
    ug = u.astype(BF16).reshape(nch, lc, g_, S5_GROUP).transpose(2, 0, 1, 3).reshape(g_, nch, w)
    y = pl.pallas_call(
        functools.partial(_s5_kernel, n_scan=n_scan),
        grid=(g_,),
        in_specs=[pl.BlockSpec((1, nch, w), lambda g: (g, 0, 0)),
                  pl.BlockSpec((1, S5_GROUP, w), lambda g: (g, 0, 0)),
                  pl.BlockSpec((1, w, p2), lambda g: (g, 0, 0)),
                  pl.BlockSpec((1, p2, w), lambda g: (g, 0, 0)),
                  pl.BlockSpec((1, ar.shape[1], p2), lambda g: (g, 0, 0)),
                  pl.BlockSpec((1, ai.shape[1], p2), lambda g: (g, 0, 0))],
        out_specs=pl.BlockSpec((1, nch, w), lambda g: (g, 0, 0)),
        out_shape=jax.ShapeDtypeStruct((g_, nch, w), F32),
        scratch_shapes=[pltpu.VMEM((w, w), BF16)],
        compiler_params=_cparams("parallel"),
        name="s5_conv",
    )(ug, kflat, m_state, m_carry, ar, ai)
    return y.reshape(g_, nch, lc, S5_GROUP).transpose(1, 2, 0, 3).reshape(t, S5_WIDTH)


def _hg_select_matrix():
    c = HG_CHUNK
    r = np.zeros((HG_LEVELS * c, c), np.float32)
    for lev in range(HG_LEVELS):
        m = c >> lev
        for t in range(c):
            r[lev * c + t, (t // m) * m + m // 2 - 1] = 1.0
    return r


def _hgrn_kernel(q_ref, f_ref, i_ref, gt_ref, lb_ref, lbf_ref, ng_ref, sel_ref, o_ref, st_ref):
    c = HG_CHUNK

    @pl.when(pl.program_id(1) == 0)
    def _():
        st_ref[...] = jnp.zeros_like(st_ref)

    fp = f_ref[...]
    tb = fp.shape[0]
    lb = lb_ref[0]
    en = jnp.exp(-jnp.abs(fp))
    rc = 1.0 / (1.0 + en)
    pos_f = fp >= 0.0
    sig_p = jnp.where(pos_f, rc, en * rc)
    sig_n = jnp.where(pos_f, en * rc, rc)
    lf = jnp.log(lbf_ref[0] + (1.0 - lb) * sig_p)
    kk = (1.0 - lb) * sig_n
    q = q_ref[...]
    qs = q * _sigmoid(q)
    v = i_ref[...]
    gt = gt_ref[...]
    gts = gt * _sigmoid(gt)

    row = lax.broadcasted_iota(I32, (tb, LANES), 0)
    pos = jnp.bitwise_and(row, c - 1)
    b = lf
    for k in range(HG_LEVELS):
        d = 1 << k
        b = b + jnp.where(pos >= d, pltpu.roll(b, d, axis=0), 0.0)

    prow = lax.broadcasted_iota(I32, (c, LANES), 0)
    ti = lax.broadcasted_iota(I32, (c, c), 0)
    si = lax.broadcasted_iota(I32, (c, c), 1)
    sel = sel_ref[...]
    ng = ng_ref[0]
    st = st_ref[...]
    nt = (((1,), (1,)), ((), ()))
    tn = (((0,), (0,)), ((), ()))
    for n in range(tb // c):
        sl = slice(n * c, (n + 1) * c)
        bc, qc, kc, vc = b[sl], qs[sl], kk[sl], v[sl]
        b_hi = bc.astype(BF16)
        b_lo = (bc - b_hi.astype(F32)).astype(BF16)
        beta = (jnp.dot(sel, b_hi, preferred_element_type=F32)
                + jnp.dot(sel, b_lo, preferred_element_type=F32))
        a = jnp.zeros((c, c), F32)
        for lev in range(HG_LEVELS):
            m = c >> lev
            upper = jnp.bitwise_and(prow, m - 1) >= (m // 2)
            e = bc - beta[lev * c:(lev + 1) * c]
            w = jnp.exp(jnp.where(upper, e, -e))
            qt = jnp.where(upper, qc * w, 0.0).astype(BF16)
            kt = jnp.where(upper, 0.0, kc * w).astype(BF16)
            s = lax.dot_general(qt, kt, nt, preferred_element_type=F32)
            same = jnp.right_shift(ti, HG_LEVELS - lev) == jnp.right_shift(si, HG_LEVELS - lev)
            a = a + jnp.where(same, s, 0.0)
        vb = vc.astype(BF16)
        o = jnp.dot(a.astype(BF16), vb, preferred_element_type=F32)
        o = o + jnp.sum(qc * kc, axis=-1, keepdims=True) * vc
        o = o + lax.dot_general((qc * jnp.exp(bc)).astype(BF16), st.astype(BF16), nt,
                                preferred_element_type=F32)
        bl = bc[c - 1:c, :]
        khat = (kc * jnp.exp(bl - bc)).astype(BF16)
        st = st * jnp.exp(bl) + lax.dot_general(vb, khat, tn, preferred_element_type=F32)
        o_ref[sl, :] = _rms(o, ng) * gts[sl]
    st_ref[...] = st


def _hgrn(proj, lb, norm_g):
    t = proj.shape[0]
    tb = min(HG_TBLOCK, t)
    lbh = jnp.clip(lb.astype(F32), 0.0, 1.0 - 1e-6).reshape(HG_HEADS, 1, LANES)
    lbf = jnp.maximum(lbh, LB_FLOOR)
    ng = norm_g.astype(F32).reshape(HG_HEADS, 1, LANES)
    sel = jnp.asarray(_hg_select_matrix(), BF16)
    cb = COL_HG // LANES

    def col(k):
        return pl.BlockSpec((tb, LANES), lambda h, i, k=k: (i, cb + k * HG_HEADS + h))

    vec = pl.BlockSpec((1, 1, LANES), lambda h, i: (h, 0, 0))
    return pl.pallas_call(
        _hgrn_kernel,
        grid=(HG_HEADS, t // tb),
        in_specs=[col(0), col(1), col(2), col(3), vec, vec, vec,
                  pl.BlockSpec(sel.shape, lambda h, i: (0, 0))],
        out_specs=pl.BlockSpec((tb, LANES), lambda h, i: (i, h)),
        out_shape=jax.ShapeDtypeStruct((t, HG_WIDTH), F32),
        scratch_shapes=[pltpu.VMEM((LANES, LANES), F32)],
        compiler_params=_cparams("parallel", "arbitrary"),
        name="hgrn2",
    )(proj, proj, proj, proj, lbh, lbf, ng, sel)


def _mix_out_kernel(ys_ref, u_ref, cb_ref, cc_ref, cv_ref, hg_ref, h_ref,
                    d_ref, glu_ref, sn_ref, cw_ref, cn_ref, wo_ref, o_ref, carry_ref):
    @pl.when(pl.program_id(0) == 0)
    def _():
        carry_ref[...] = jnp.zeros_like(carry_ref)

    y = ys_ref[...] + d_ref[...] * u_ref[...]
    y = jax.nn.gelu(y)
    y = y * _sigmoid(jnp.dot(y.astype(BF16), glu_ref[...], preferred_element_type=F32))
    y_s5 = _rms(y, sn_ref[...])

    z = cc_ref[...] * cv_ref[...]
    tm = z.shape[0]
    row = lax.broadcasted_iota(I32, z.shape, 0)
    p1 = carry_ref[7:8, :]
    p2 = carry_ref[6:7, :]
    z1 = jnp.where(row == 0, p1, pltpu.roll(z, 1, axis=0))
    z2 = jnp.where(row == 0, p2, jnp.where(row == 1, p1, pltpu.roll(z, 2, axis=0)))
    carry_ref[...] = z[tm - 8:tm, :]
    yc = cb_ref[...] * (z2 * cw_ref[0:1, :] + z1 * cw_ref[1:2, :] + z * cw_ref[2:3, :])
    y_cv = _rms(yc, cn_ref[...])

    acc = h_ref[...]
    hg_row = S5_WIDTH + CONV_WIDTH
    acc = acc + jnp.dot(y_s5.astype(BF16), wo_ref[0:S5_WIDTH, :], preferred_element_type=F32)
    acc = acc + jnp.dot(y_cv.astype(BF16), wo_ref[S5_WIDTH:hg_row, :], preferred_element_type=F32)
    acc = acc + jnp.dot(hg_ref[...].astype(BF16), wo_ref[hg_row:, :], preferred_element_type=F32)
    o_ref[...] = acc


def _mix_out(ys5, proj, yhg, h, s5_d, s5_glu, s5_norm, conv_w, conv_norm, w_out):
    t, d = h.shape
    tm = min(ROW_TILE, t)
    cw = jnp.pad(conv_w.astype(F32), ((0, 8 - CONV_K), (0, 0)))
    wq = S5_WIDTH

    def rowblk(width, colblk):
        return pl.BlockSpec((tm, width), lambda i, c=colblk: (i, c))

    def full(shape):
        return pl.BlockSpec(shape, lambda i: (0,) * len(shape))

    return pl.pallas_call(
        _mix_out_kernel,
        grid=(t // tm,),
        in_specs=[rowblk(wq, 0), rowblk(wq, 0), rowblk(wq, 1), rowblk(wq, 2), rowblk(wq, 3),
                  rowblk(HG_WIDTH, 0), rowblk(d, 0),
                  full((1, wq)), full((wq, wq)), full((1, wq)), full((8, wq)), full((1, wq)),
                  full(w_out.shape)],
        out_specs=rowblk(d, 0),
        out_shape=jax.ShapeDtypeStruct((t, d), F32),
        scratch_shapes=[pltpu.VMEM((8, CONV_WIDTH), F32)],
        compiler_params=_cparams("arbitrary"),
        name="mix_out",
    )(ys5, proj, proj, proj, proj, yhg, h,
      s5_d.astype(F32).reshape(1, wq), s5_glu, s5_norm.astype(F32).reshape(1, wq), cw,
      conv_norm.astype(F32).reshape(1, wq), w_out)


def _ffn_kernel(h_ref, g_ref, w1_ref, w3_ref, w2_ref, o_ref, hn_ref, acc_ref):
    j = pl.program_id(1)

    @pl.when(j == 0)
    def _():
        x = h_ref[...]
        hn_ref[...] = _rms(x, g_ref[...]).astype(BF16)
        acc_ref[...] = x

    hn = hn_ref[...]
    a = jnp.dot(hn, w1_ref[...].astype(BF16), preferred_element_type=F32)
    b = jnp.dot(hn, w3_ref[...].astype(BF16), preferred_element_type=F32)
    gact = (a * _sigmoid(a) * b).astype(BF16)
    acc_ref[...] += jnp.dot(gact, w2_ref[...].astype(BF16), preferred_element_type=F32)

    @pl.when(j == pl.num_programs(1) - 1)
    def _():
        o_ref[...] = acc_ref[...]


def _ffn(h, g, w1, w3, w2, layer):
    t, d = h.shape
    f = w1.shape[2]
    tm = min(FFN_ROW_TILE, t)
    tf = FFN_COL_TILE
    return pl.pallas_call(
        _ffn_kernel,
        grid=(t // tm, f // tf),
        in_specs=[pl.BlockSpec((tm, d), lambda i, j: (i, 0)),
                  pl.BlockSpec((1, d), lambda i, j: (0, 0)),
                  pl.BlockSpec((None, d, tf), lambda i, j: (layer, 0, j)),
                  pl.BlockSpec((None, d, tf), lambda i, j: (layer, 0, j)),
                  pl.BlockSpec((None, tf, d), lambda i, j: (layer, j, 0))],
        out_specs=pl.BlockSpec((tm, d), lambda i, j: (i, 0)),
        out_shape=jax.ShapeDtypeStruct((t, d), F32),
        scratch_shapes=[pltpu.VMEM((tm, d), BF16), pltpu.VMEM((tm, d), F32)],
        compiler_params=_cparams("parallel", "arbitrary"),
        name="ffn_swiglu",
    )(h, g.astype(F32).reshape(1, d), w1, w3, w2)


def _router_kernel(h_ref, g_ref, r_ref, hn_ref, info_ref, before_ref, total_ref, cnt_ref, *, n_exp):
    @pl.when(pl.program_id(0) == 0)
    def _():
        cnt_ref[...] = jnp.zeros_like(cnt_ref)

    xn = _rms(h_ref[...], g_ref[...])
    hn_ref[...] = xn.astype(BF16)
    logits = jnp.dot(xn, r_ref[...], preferred_element_type=F32, precision=lax.Precision.HIGHEST)
    tm = logits.shape[0]
    lane = lax.broadcasted_iota(I32, logits.shape, 1)
    lanef = lane.astype(F32)
    neg = jnp.float32(-jnp.inf)
    lg = jnp.where(lane < n_exp, logits, neg)
    m1 = jnp.max(lg, axis=-1, keepdims=True)
    i1 = jnp.min(jnp.where(lg == m1, lanef, float(LANES)), axis=-1, keepdims=True)
    oh1 = lanef == i1
    lg2 = jnp.where(oh1, neg, lg)
    m2 = jnp.max(lg2, axis=-1, keepdims=True)
    i2 = jnp.min(jnp.where(lg2 == m2, lanef, float(LANES)), axis=-1, keepdims=True)
    oh2 = lanef == i2
    ex = jnp.exp(m2 - m1)
    g1 = 1.0 / (1.0 + ex)
    g2 = ex * g1
    chosen = jnp.where(oh1, 1.0, jnp.where(oh2, 1.0, 0.0))
    ri = lax.broadcasted_iota(I32, (tm, tm), 0)
    ci = lax.broadcasted_iota(I32, (tm, tm), 1)
    tri = jnp.where(ri > ci, 1.0, 0.0).astype(BF16)
    before = cnt_ref[...]
    cexcl = jnp.dot(tri, chosen.astype(BF16), preferred_element_type=F32) + before
    rank1 = jnp.sum(jnp.where(oh1, cexcl, 0.0), axis=-1, keepdims=True)
    rank2 = jnp.sum(jnp.where(oh2, cexcl, 0.0), axis=-1, keepdims=True)
    info = jnp.where(lane == 0, i1, jnp.where(lane == 1, i2, jnp.where(lane == 2, g1, jnp.where(
        lane == 3, g2, jnp.where(lane == 4, rank1, jnp.where(lane == 5, rank2, 0.0))))))
    info_ref[...] = info
    before_ref[0] = jnp.broadcast_to(before, before_ref.shape[1:])
    total = before + jnp.sum(chosen, axis=0, keepdims=True)
    cnt_ref[...] = total
    total_ref[...] = jnp.broadcast_to(total, total_ref.shape)


def _router(h, g, router):
    t, d = h.shape
    n_exp = router.shape[1]
    tm = min(TOK_WIN, t)
    ntw = t // tm
    rp = jnp.pad(router.astype(F32), ((0, 0), (0, LANES - n_exp)))
    return pl.pallas_call(
        functools.partial(_router_kernel, n_exp=n_exp),
        grid=(ntw,),
        in_specs=[pl.BlockSpec((tm, d), lambda i: (i, 0)),
                  pl.BlockSpec((1, d), lambda i: (0, 0)),
                  pl.BlockSpec((d, LANES), lambda i: (0, 0))],
        out_specs=[pl.BlockSpec((tm, d), lambda i: (i, 0)),
                   pl.BlockSpec((tm, LANES), lambda i: (i, 0)),
                   pl.BlockSpec((1, 8, LANES), lambda i: (i, 0, 0)),
                   pl.BlockSpec((8, LANES), lambda i: (0, 0))],
        out_shape=[jax.ShapeDtypeStruct((t, d), BF16),
                   jax.ShapeDtypeStruct((t, LANES), F32),
                   jax.ShapeDtypeStruct((ntw, 8, LANES), F32),
                   jax.ShapeDtypeStruct((8, LANES), F32)],
        scratch_shapes=[pltpu.VMEM((1, LANES), F32)],
        compiler_params=_cparams("arbitrary"),
        name="moe_router",
    )(h, g.astype(F32).reshape(1, d), rp)


FLAG_FIRST, FLAG_WORK, FLAG_LAST = 1, 2, 4


def _gather_kernel(blk_ref, win_ref, flag_ref, dest_ref, hn_ref, xs_ref, acc_ref):
    k = pl.program_id(0)
    flag = flag_ref[k]
    rows = acc_ref.shape[0]
    win = hn_ref.shape[0]

    @pl.when(jnp.bitwise_and(flag, FLAG_FIRST) != 0)
    def _():
        acc_ref[...] = jnp.zeros_like(acc_ref)

    @pl.when(jnp.bitwise_and(flag, FLAG_WORK) != 0)
    def _():
        slot = blk_ref[k] * rows + lax.broadcasted_iota(I32, (rows, win), 0)
        d = dest_ref[win_ref[k]]
        hit = jnp.where(d[0:1, :] == slot, 1.0, jnp.where(d[1:2, :] == slot, 1.0, 0.0)).astype(BF16)
        acc_ref[...] += jnp.dot(hit, hn_ref[...], preferred_element_type=F32)

    @pl.when(jnp.bitwise_and(flag, FLAG_LAST) != 0)
    def _():
        xs_ref[...] = acc_ref[...].astype(BF16)


def _gather(hn, dest, blk, win, flag, n_slots):
    t, d = hn.shape
    tw = dest.shape[-1]
    return pl.pallas_call(
        _gather_kernel,
        grid_spec=pltpu.PrefetchScalarGridSpec(
            num_scalar_prefetch=3,
            grid=(blk.shape[0],),
            in_specs=[pl.BlockSpec(dest.shape, lambda k, b, w, f: (0, 0, 0)),
                      pl.BlockSpec((tw, d), lambda k, b, w, f: (w[k], 0))],
            out_specs=pl.BlockSpec((GATHER_TILE, d), lambda k, b, w, f: (b[k], 0)),
            scratch_shapes=[pltpu.VMEM((GATHER_TILE, d), F32)]),
        out_shape=jax.ShapeDtypeStruct((n_slots, d), BF16),
        compiler_params=_cparams("arbitrary"),
        name="moe_gather",
    )(blk, win, flag, dest, hn)


def _expert_kernel(te_ref, tv_ref, x_ref, w1_ref, w3_ref, w2_ref, y_ref, acc_ref):
    b = pl.program_id(0)
    j = pl.program_id(1)

    @pl.when(j == 0)
    def _():
        acc_ref[...] = jnp.zeros_like(acc_ref)

    @pl.when(tv_ref[b] == 1)
    def _():
        x = x_ref[...]
        a = jnp.dot(x, w1_ref[...].astype(BF16), preferred_element_type=F32)
        c = jnp.dot(x, w3_ref[...].astype(BF16), preferred_element_type=F32)
        gact = (a * _sigmoid(a) * c).astype(BF16)
        acc_ref[...] += jnp.dot(gact, w2_ref[...].astype(BF16), preferred_element_type=F32)

    @pl.when(j == pl.num_programs(1) - 1)
    def _():
        y_ref[...] = acc_ref[...].astype(BF16)


def _experts(xs, tile_e, tile_v, w1, w3, w2, layer):
    ns, d = xs.shape
    f = w1.shape[3]
    tf = MOE_COL_TILE
    nj = f // tf
    n_tiles = ns // MOE_TILE

    def col(b, j, tv):
        return j * tv[b] + (nj - 1) * (1 - tv[b])

    return pl.pallas_call(
        _expert_kernel,
        grid_spec=pltpu.PrefetchScalarGridSpec(
            num_scalar_prefetch=2,
            grid=(n_tiles, nj),
            in_specs=[pl.BlockSpec((MOE_TILE, d), lambda b, j, te, tv: (b, 0)),
                      pl.BlockSpec((None, None, d, tf), lambda b, j, te, tv: (layer, te[b], 0, col(b, j, tv))),
                      pl.BlockSpec((None, None, d, tf), lambda b, j, te, tv: (layer, te[b], 0, col(b, j, tv))),
                      pl.BlockSpec((None, None, tf, d), lambda b, j, te, tv: (layer, te[b], col(b, j, tv), 0))],
            out_specs=pl.BlockSpec((MOE_TILE, d), lambda b, j, te, tv: (b, 0)),
            scratch_shapes=[pltpu.VMEM((MOE_TILE, d), F32)]),
        out_shape=jax.ShapeDtypeStruct((ns, d), BF16),
        compiler_params=_cparams("arbitrary", "arbitrary"),
        name="moe_experts",
    )(tile_e, tile_v, xs, w1, w3, w2)


def _combine_kernel(ws_ref, po_ref, ex_ref, info_ref, h_ref, yb_hbm, o_ref, buf_ref, sem, *, n_exp):
    i = pl.program_id(0)
    cur = lax.rem(i, 2)

    def copy(step, e, s):
        start = pl.multiple_of(ws_ref[step * n_exp + e], SLOT_ALIGN)
        return pltpu.make_async_copy(yb_hbm.at[pl.ds(start, SLOT_WIN), :], buf_ref.at[s, e], sem.at[s, e])

    @pl.when(i == 0)
    def _():
        for e in range(n_exp):
            copy(0, e, 0).start()

    @pl.when(i + 1 < pl.num_programs(0))
    def _():
        for e in range(n_exp):
            copy(i + 1, e, 1 - cur).start()

    info = info_ref[...]
    tm = info.shape[0]
    e1, e2 = info[:, 0:1], info[:, 1:2]
    g1, g2 = info[:, 2:3], info[:, 3:4]
    r1, r2 = info[:, 4:5], info[:, 5:6]
    lanef = lax.broadcasted_iota(I32, (tm, TOK_WIN), 1).astype(F32)

    def weights(e, shift, lo_lane):
        off = (ws_ref[i * n_exp + e] - po_ref[e] + shift).astype(F32)
        keep = lanef >= float(lo_lane)
        w1 = jnp.where((e1 == float(e)) & (r1 - off == lanef) & keep, g1, 0.0)
        w2 = jnp.where((e2 == float(e)) & (r2 - off == lanef) & keep, g2, 0.0)
        return (w1 + w2).astype(BF16)

    acc = h_ref[...]
    for e in range(n_exp):
        pt = weights(e, 0, 0)
        copy(i, e, cur).wait()
        acc = acc + jnp.dot(pt, buf_ref[cur, e, 0:TOK_WIN, :], preferred_element_type=F32)
    o_ref[...] = acc

    tail = SLOT_WIN - TOK_WIN
    for e in range(n_exp):
        @pl.when(ex_ref[i * n_exp + e] == 1)
        def _(e=e):
            pt = weights(e, tail, TOK_WIN - tail)
            o_ref[...] += jnp.dot(pt, buf_ref[cur, e, tail:SLOT_WIN, :], preferred_element_type=F32)


def _combine(info, h, yb, win_start, pstart, extra):
    t, d = h.shape
    n_exp = pstart.shape[0]
    tm = min(TOK_WIN, t)
    return pl.pallas_call(
        functools.partial(_combine_kernel, n_exp=n_exp),
        grid_spec=pltpu.PrefetchScalarGridSpec(
            num_scalar_prefetch=3,
            grid=(t // tm,),
            in_specs=[pl.BlockSpec((tm, LANES), lambda i, ws, po, ex: (i, 0)),
                      pl.BlockSpec((tm, d), lambda i, ws, po, ex: (i, 0)),
                      pl.BlockSpec(memory_space=pl.ANY)],
            out_specs=pl.BlockSpec((tm, d), lambda i, ws, po, ex: (i, 0)),
            scratch_shapes=[pltpu.VMEM((2, n_exp, SLOT_WIN, d), BF16),
                            pltpu.SemaphoreType.DMA((2, n_exp))]),
        out_shape=jax.ShapeDtypeStruct((t, d), F32),
        compiler_params=_cparams("arbitrary"),
        name="moe_combine",
    )(win_start, pstart, extra, info, h, yb)


def _moe(h, g, router, w1, w3, w2, layer):
    t, d = h.shape
    n_exp = router.shape[1]
    tw = min(TOK_WIN, t)
    ntw = t // tw
    hn, info, before, total = _router(h, g, router)

    counts = total[0, :n_exp].astype(I32)
    padded = (counts + MOE_TILE - 1) // MOE_TILE * MOE_TILE
    pend = jnp.cumsum(padded)
    pstart = pend - padded
    n_tiles = (t * TOP_K) // MOE_TILE + n_exp
    n_slots = n_tiles * MOE_TILE
    tile0 = jnp.arange(n_tiles, dtype=I32) * MOE_TILE
    tile_e = jnp.clip(jnp.searchsorted(pend, tile0, side='right'), 0, n_exp - 1).astype(I32)
    tile_v = (tile0 < pend[-1]).astype(I32)
    cum = jnp.concatenate([before[:, 0, :n_exp], total[0:1, :n_exp]], axis=0).astype(I32)

    n_gb = n_slots // GATHER_TILE
    gb0 = jnp.arange(n_gb, dtype=I32) * GATHER_TILE
    gb_e = jnp.clip(jnp.searchsorted(pend, gb0, side='right'), 0, n_exp - 1)
    r0 = gb0 - pstart[gb_e]
    cum_b = cum[:, gb_e]
    ilo = jnp.sum((cum_b[1:] <= r0[None, :]).astype(I32), axis=0)
    ihi = jnp.sum((cum_b[:-1] < (r0 + GATHER_TILE)[None, :]).astype(I32), axis=0) - 1
    n_work = jnp.where(gb0 < pend[-1], jnp.maximum(ihi - ilo + 1, 0), 0)
    n_step = jnp.maximum(n_work, 1)
    ends = jnp.cumsum(n_step)
    offs = ends - n_step
    steps = jnp.arange(n_exp * ntw + n_gb, dtype=I32)
    blk = jnp.minimum(jnp.searchsorted(ends, steps, side='right'), n_gb - 1).astype(I32)
    jb = steps - offs[blk]
    live = steps < ends[-1]
    win = jnp.clip(ilo[blk] + jb, 0, ntw - 1).astype(I32)
    flag = (jnp.where(live & (jb == 0), FLAG_FIRST, 0) + jnp.where(live & (jb < n_work[blk]), FLAG_WORK, 0)
            + jnp.where(live & (jb == n_step[blk] - 1), FLAG_LAST, 0)).astype(I32)

    e12 = info[:, 0:2].astype(I32)
    dest12 = pstart[e12] + info[:, 4:6].astype(I32)
    dest = jnp.full((ntw, 8, tw), -1, I32).at[:, 0:2, :].set(
        dest12.reshape(ntw, tw, 2).transpose(0, 2, 1))

    xs = _gather(hn, dest, blk, win, flag, n_slots)
    yb = _experts(xs, tile_e, tile_v, w1, w3, w2, layer)

    lo = pstart[None, :] + cum[:-1]
    win_start = jnp.minimum(lo // SLOT_ALIGN * SLOT_ALIGN, n_slots - SLOT_WIN).astype(I32)
    extra = ((lo - win_start + cum[1:] - cum[:-1]) > TOK_WIN).astype(I32)
    return _combine(info, h, yb, win_start.reshape(-1), pstart.astype(I32), extra.reshape(-1))


def _final_norm_kernel(h_ref, g_ref, o_ref):
    o_ref[...] = _rms(h_ref[...], g_ref[...])


def _final_norm(h, g):
    t, d = h.shape
    tm = min(ROW_TILE, t)
    return pl.pallas_call(
        _final_norm_kernel,
        grid=(t // tm,),
        in_specs=[pl.BlockSpec((tm, d), lambda i: (i, 0)), pl.BlockSpec((1, d), lambda i: (0, 0))],
        out_specs=pl.BlockSpec((tm, d), lambda i: (i, 0)),
        out_shape=jax.ShapeDtypeStruct((t, d), F32),
        compiler_params=_cparams("parallel"),
        name="final_norm",
    )(h, g.astype(F32).reshape(1, d))


def kernel(x, attn_norm, ffn_norm, final_norm, w_in, w_out, s5_lambda_re, s5_lambda_im, s5_log_dt, s5_b_re, s5_b_im, s5_c_re, s5_c_im, s5_d, s5_glu, s5_out_norm, conv_w, conv_out_norm, hg_lower_bounds, hg_out_norm, ffn_w1, ffn_w3, ffn_w2, moe_router, moe_w1, moe_w3, moe_w2):
    bsz, seq, d = x.shape
    depth = w_in.shape[0]
    t = bsz * seq
    assert bsz == 1, "token mixers are written for a single sequence"
    lb_soft = jax.nn.softmax(hg_lower_bounds.astype(F32), axis=0)
    lb_all = jnp.cumsum(lb_soft, axis=0) - lb_soft[0]
    n_scan = int(math.log2(t // S5_CHUNK))
    s5_ops = jax.vmap(functools.partial(_s5_operators, n_scan=n_scan))(
        s5_lambda_re, s5_lambda_im, s5_log_dt, s5_b_re, s5_b_im, s5_c_re, s5_c_im)
    h = x.reshape(t, d).astype(F32)
    for l in range(depth):
        proj = _norm_inproj(h, attn_norm[l].astype(F32), w_in[l].astype(BF16))
        ys5 = _s5_conv(proj[:, :S5_WIDTH], [op[l] for op in s5_ops])
        yhg = _hgrn(proj, lb_all[l], hg_out_norm[l])
        h = _mix_out(ys5, proj, yhg, h, s5_d[l], s5_glu[l].astype(BF16), s5_out_norm[l],
                     conv_w[l], conv_out_norm[l], w_out[l].astype(BF16))
        j = l // 2
        if l % 2 == 0:
            h = _ffn(h, ffn_norm[l], ffn_w1, ffn_w3, ffn_w2, j)
        else:
            h = _moe(h, ffn_norm[l], moe_router[j], moe_w1, moe_w3, moe_w2, j)
    return _final_norm(h, final_norm).reshape(bsz, seq, d)
```

```python
import functools
import math

import numpy as np
import jax
import jax.numpy as jnp
from jax import lax
from jax.experimental import pallas as pl
from jax.experimental.pallas import tpu as pltpu

F32 = jnp.float32
BF16 = jnp.bfloat16
I32 = jnp.int32

NORM_EPS = 1e-6
LB_FLOOR = 1e-30
TOP_K = 2

S5_WIDTH = 256
S5_GROUP = 16
S5_STATE = 64
CONV_WIDTH = 256
CONV_K = 3
HG_WIDTH = 512
HG_HEAD_DIM = 128
HG_HEADS = HG_WIDTH // HG_HEAD_DIM
LANES = 128
COL_CONV = S5_WIDTH
COL_HG = S5_WIDTH + 3 * CONV_WIDTH

VMEM_LIMIT = 56 * 1024 * 1024
S5_CHUNK = 32
HG_CHUNK = 64
HG_LEVELS = 6
HG_TBLOCK = 512
ROW_TILE = 512
FFN_ROW_TILE = 1024
FFN_COL_TILE = 512
MOE_TILE = 1024
MOE_COL_TILE = 512
GATHER_TILE = 256
TOK_WIN = 256
SLOT_ALIGN = 16
SLOT_WIN = TOK_WIN + SLOT_ALIGN


def _cparams(*sem):
    return pltpu.CompilerParams(dimension_semantics=sem, vmem_limit_bytes=VMEM_LIMIT)


def _rms(x, g):
    ms = jnp.mean(x * x, axis=-1, keepdims=True)
    return x * lax.rsqrt(ms + NORM_EPS) * g


def _sigmoid(x):
    return 1.0 / (1.0 + jnp.exp(-x))


def _norm_inproj_kernel(h_ref, g_ref, w_ref, o_ref):
    xn = _rms(h_ref[...], g_ref[...]).astype(BF16)
    o_ref[...] = jnp.dot(xn, w_ref[...], preferred_element_type=F32)


def _norm_inproj(h, g, w):
    t, d = h.shape
    n = w.shape[1]
    tm = min(ROW_TILE, t)
    return pl.pallas_call(
        _norm_inproj_kernel,
        grid=(t // tm,),
        in_specs=[pl.BlockSpec((tm, d), lambda i: (i, 0)),
                  pl.BlockSpec((1, d), lambda i: (0, 0)),
                  pl.BlockSpec((d, n), lambda i: (0, 0))],
        out_specs=pl.BlockSpec((tm, n), lambda i: (i, 0)),
        out_shape=jax.ShapeDtypeStruct((t, n), F32),
        compiler_params=_cparams("parallel"),
        name="norm_inproj",
    )(h, g.reshape(1, d), w)


def _s5_operators(lam_re, lam_im, log_dt, b_re, b_im, c_re, c_im, n_scan):
    lc = S5_CHUNK
    hi = lax.Precision.HIGHEST
    lr, li = lam_re.astype(F32), lam_im.astype(F32)
    dt = jnp.exp(log_dt.astype(F32))[:, None]
    zr, zi = lr * dt, li * dt
    taus = jnp.arange(lc + 1, dtype=F32)[:, None, None]
    mag = jnp.exp(zr[None] * taus)
    pwr, pwi = mag * jnp.cos(zi[None] * taus), mag * jnp.sin(zi[None] * taus)
    nr, ni = pwr[1] - 1.0, pwi[1]
    den = lr * lr + li * li
    qr, qi = (nr * lr + ni * li) / den, (ni * lr - nr * li) / den
    br, bi = b_re.astype(F32), b_im.astype(F32)
    bbr = qr[..., None] * br - qi[..., None] * bi
    bbi = qr[..., None] * bi + qi[..., None] * br
    cr, ci = c_re.astype(F32), c_im.astype(F32)
    g_, p_ = lr.shape

    def c_times_pw(lo):
        wr, wi = pwr[lo:lo + lc, :, None, :], pwi[lo:lo + lc, :, None, :]
        return cr[None] * wr - ci[None] * wi, cr[None] * wi + ci[None] * wr

    cpr, cpi = c_times_pw(0)
    kt = (jnp.einsum('tgcp,gpi->tgci', cpr, bbr, precision=hi)
          - jnp.einsum('tgcp,gpi->tgci', cpi, bbi, precision=hi))
    kflat = kt.transpose(1, 3, 0, 2).reshape(g_, S5_GROUP, lc * S5_GROUP)

    wr, wi = pwr[lc - 1::-1][:, :, :, None], pwi[lc - 1::-1][:, :, :, None]
    msr = (wr * bbr[None] - wi * bbi[None]).transpose(1, 0, 3, 2).reshape(g_, lc * S5_GROUP, p_)
    msi = (wr * bbi[None] + wi * bbr[None]).transpose(1, 0, 3, 2).reshape(g_, lc * S5_GROUP, p_)
    m_state = jnp.concatenate([msr, msi], axis=-1)

    c1r, c1i = c_times_pw(1)
    c1r = c1r.transpose(1, 3, 0, 2).reshape(g_, p_, lc * S5_GROUP)
    c1i = c1i.transpose(1, 3, 0, 2).reshape(g_, p_, lc * S5_GROUP)
    m_carry = jnp.concatenate([c1r, -c1i], axis=1)

    akr, aki = [pwr[lc]], [pwi[lc]]
    for _ in range(n_scan - 1):
        r, i = akr[-1], aki[-1]
        akr.append(r * r - i * i)
        aki.append(2.0 * r * i)
    akr, aki = jnp.stack(akr, axis=1), jnp.stack(aki, axis=1)
    ar = jnp.concatenate([akr, akr], axis=-1)
    ai = jnp.concatenate([-aki, aki], axis=-1)
    kpad = -(-n_scan // 8) * 8
    ar = jnp.pad(ar, ((0, 0), (0, kpad - n_scan), (0, 0)))
    ai = jnp.pad(ai, ((0, 0), (0, kpad - n_scan), (0, 0)))
    return kflat, m_state.astype(BF16), m_carry.astype(BF16), ar, ai


def _s5_kernel(u_ref, kf_ref, ms_ref, mc_ref, ar_ref, ai_ref, y_ref, mi_ref, *, n_scan):
    kf = kf_ref[0]
    lane = lax.broadcasted_iota(I32, kf.shape, 1)
    for s in range(S5_CHUNK):
        sh = s * S5_GROUP
        blk = kf if s == 0 else jnp.where(lane >= sh, pltpu.roll(kf, sh, axis=1), 0.0)
        mi_ref[sh:sh + S5_GROUP, :] = blk.astype(BF16)

    u = u_ref[0]
    x = jnp.dot(u, ms_ref[0], preferred_element_type=F32)
    row = lax.broadcasted_iota(I32, x.shape, 0)
    half = x.shape[1] // 2
    for k in range(n_scan):
        d = 1 << k
        s = jnp.where(row >= d, pltpu.roll(x, d, axis=0), 0.0)
        x = x + ar_ref[0, k:k + 1, :] * s + ai_ref[0, k:k + 1, :] * pltpu.roll(s, half, axis=1)
    xe = jnp.where(row >= 1, pltpu.roll(x, 1, axis=0), 0.0)
    y = jnp.dot(u, mi_ref[...], preferred_element_type=F32)
    y = y + jnp.dot(xe.astype(BF16), mc_ref[0], preferred_element_type=F32)
    y_ref[0] = y


def _s5_conv(u, ops):
    kflat, m_state, m_carry, ar, ai = ops
    t = u.shape[0]
    lc = S5_CHUNK
    g_ = S5_WIDTH // S5_GROUP
    nch = t // lc
    n_scan = int(math.log2(nch))
    assert (1 << n_scan) == nch
    w = lc * S5_GROUP
    p2 = m_state.shape[-1]
    ug = u.astype(BF16).reshape(nch, lc, g_, S5_GROUP).transpose(2, 0, 1, 3).reshape(g_, nch, w)
    y = pl.pallas_call(
        functools.partial(_s5_kernel, n_scan=n_scan),
        grid=(g_,),
        in_specs=[pl.BlockSpec((1, nch, w), lambda g: (g, 0, 0)),
                  pl.BlockSpec((1, S5_GROUP, w), lambda g: (g, 0, 0)),
                  pl.BlockSpec((1, w, p2), lambda g: (g, 0, 0)),
                  pl.BlockSpec((1, p2, w), lambda g: (g, 0, 0)),
                  pl.BlockSpec((1, ar.shape[1], p2), lambda g: (g, 0, 0)),
                  pl.BlockSpec((1, ai.shape[1], p2), lambda g: (g, 0, 0))],
        out_specs=pl.BlockSpec((1, nch, w), lambda g: (g, 0, 0)),
        out_shape=jax.ShapeDtypeStruct((g_, nch, w), F32),
        scratch_shapes=[pltpu.VMEM((w, w), BF16)],
        compiler_params=_cparams("parallel"),
        name="s5_conv",
    )(ug, kflat, m_state, m_carry, ar, ai)
    return y.reshape(g_, nch, lc, S5_GROUP).transpose(1, 2, 0, 3).reshape(t, S5_WIDTH)


SUBLANES = 8
HG_SEL_LEVELS = [lev for lev in range(HG_LEVELS) if 1 < (HG_CHUNK >> lev) // 2 < SUBLANES]


def _hg_select_matrix():
    c = HG_CHUNK
    r = np.zeros((len(HG_SEL_LEVELS) * c, c), np.float32)
    for n, lev in enumerate(HG_SEL_LEVELS):
        m = c >> lev
        for t in range(c):
            r[n * c + t, (t // m) * m + m // 2 - 1] = 1.0
    return r


def _hg_chunk(fp, q, v, gt, lb, lbf, ng, sel, st, masks):
    c = HG_CHUNK
    prow, ti, si = masks
    nt = (((1,), (1,)), ((), ()))
    tn = (((0,), (0,)), ((), ()))
    en = jnp.exp(-jnp.abs(fp))
    rc = 1.0 / (1.0 + en)
    pos_f = fp >= 0.0
    sig_p = jnp.where(pos_f, rc, en * rc)
    sig_n = jnp.where(pos_f, en * rc, rc)
    f = lbf + (1.0 - lb) * sig_p
    lf = jnp.log(f)
    kc = (1.0 - lb) * sig_n
    qc = q * _sigmoid(q)
    bc = lf
    for k in range(HG_LEVELS):
        d = 1 << k
        bc = bc + jnp.where(prow >= d, pltpu.roll(bc, d, axis=0), 0.0)

    if HG_SEL_LEVELS:
        b_hi = bc.astype(BF16)
        b_lo = (bc - b_hi.astype(F32)).astype(BF16)
        beta_sel = (jnp.dot(sel, b_hi, preferred_element_type=F32)
                    + jnp.dot(sel, b_lo, preferred_element_type=F32))
    a = None
    for lev in range(HG_LEVELS):
        m = c >> lev
        h = m // 2
        if h >= SUBLANES:
            qparts, kparts = [], []
            zero = jnp.zeros((h, LANES), F32)
            for j in range(c // m):
                lo, mid, hi = j * m, j * m + h, (j + 1) * m
                beta = bc[mid - 1:mid, :]
                kparts += [kc[lo:mid] * jnp.exp(beta - bc[lo:mid]), zero]
                qparts += [zero, qc[mid:hi] * jnp.exp(bc[mid:hi] - beta)]
            qt = jnp.concatenate(qparts, axis=0)
            kt = jnp.concatenate(kparts, axis=0)
        else:
            upper = jnp.bitwise_and(prow, m - 1) >= h
            if h == 1:
                w = f
            else:
                n = HG_SEL_LEVELS.index(lev)
                e = bc - beta_sel[n * c:(n + 1) * c]
                w = jnp.exp(jnp.where(upper, e, -e))
            qt = jnp.where(upper, qc * w, 0.0)
            kt = jnp.where(upper, 0.0, kc if h == 1 else kc * w)
        s = lax.dot_general(qt.astype(BF16), kt.astype(BF16), nt, preferred_element_type=F32)
        if lev > 0:
            same = jnp.right_shift(ti, HG_LEVELS - lev) == jnp.right_shift(si, HG_LEVELS - lev)
            s = jnp.where(same, s, 0.0)
        a = s if a is None else a + s
    vb = v.astype(BF16)
    o = jnp.dot(a.astype(BF16), vb, preferred_element_type=F32)
    o = o + jnp.sum(qc * kc, axis=-1, keepdims=True) * v
    o = o + lax.dot_general((qc * jnp.exp(bc)).astype(BF16), st.astype(BF16), nt,
                            preferred_element_type=F32)
    bl = bc[c - 1:c, :]
    khat = (kc * jnp.exp(bl - bc)).astype(BF16)
    st = st * jnp.exp(bl) + lax.dot_general(vb, khat, tn, preferred_element_type=F32)
    return _rms(o, ng) * (gt * _sigmoid(gt)), st


def _hgrn_kernel(q_ref, f_ref, i_ref, gt_ref, lb_ref, lbf_ref, ng_ref, sel_ref, o_ref, st_ref):
    c = HG_CHUNK

    @pl.when(pl.program_id(0) == 0)
    def _():
        st_ref[...] = jnp.zeros_like(st_ref)

    masks = (lax.broadcasted_iota(I32, (c, LANES), 0), lax.broadcasted_iota(I32, (c, c), 0),
             lax.broadcasted_iota(I32, (c, c), 1))
    sel = sel_ref[...]

    def chunk(n, carry):
        rows = pl.ds(pl.multiple_of(n * c, c), c)
        for hd in range(HG_HEADS):
            cols = slice(hd * LANES, (hd + 1) * LANES)
            o, st = _hg_chunk(f_ref[rows, cols], q_ref[rows, cols], i_ref[rows, cols], gt_ref[rows, cols],
                              lb_ref[:, cols], lbf_ref[:, cols], ng_ref[:, cols], sel, st_ref[hd], masks)
            o_ref[rows, cols] = o
            st_ref[hd] = st
        return carry

    lax.fori_loop(0, q_ref.shape[0] // c, chunk, 0)


def _hgrn(proj, lb, norm_g):
    t = proj.shape[0]
    tb = min(HG_TBLOCK, t)
    lbh = jnp.clip(lb.astype(F32), 0.0, 1.0 - 1e-6).reshape(1, HG_WIDTH)
    lbf = jnp.maximum(lbh, LB_FLOOR)
    ng = norm_g.astype(F32).reshape(1, HG_WIDTH)
    sel = jnp.asarray(_hg_select_matrix(), BF16)
    cb = COL_HG // HG_WIDTH

    def col(k):
        return pl.BlockSpec((tb, HG_WIDTH), lambda i, k=k: (i, cb + k))

    vec = pl.BlockSpec((1, HG_WIDTH), lambda i: (0, 0))
    return pl.pallas_call(
        _hgrn_kernel,
        grid=(t // tb,),
        in_specs=[col(0), col(1), col(2), col(3), vec, vec, vec,
                  pl.BlockSpec(sel.shape, lambda i: (0, 0))],
        out_specs=pl.BlockSpec((tb, HG_WIDTH), lambda i: (i, 0)),
        out_shape=jax.ShapeDtypeStruct((t, HG_WIDTH), F32),
        scratch_shapes=[pltpu.VMEM((HG_HEADS, LANES, LANES), F32)],
        compiler_params=_cparams("arbitrary"),
        name="hgrn2",
    )(proj, proj, proj, proj, lbh, lbf, ng, sel)


def _mix_out_kernel(ys_ref, u_ref, cb_ref, cc_ref, cv_ref, hg_ref, h_ref,
                    d_ref, glu_ref, sn_ref, cw_ref, cn_ref, wo_ref, o_ref, carry_ref):
    @pl.when(pl.program_id(0) == 0)
    def _():
        carry_ref[...] = jnp.zeros_like(carry_ref)

    y = ys_ref[...] + d_ref[...] * u_ref[...]
    y = jax.nn.gelu(y)
    y = y * _sigmoid(jnp.dot(y.astype(BF16), glu_ref[...], preferred_element_type=F32))
    y_s5 = _rms(y, sn_ref[...])

    z = cc_ref[...] * cv_ref[...]
    tm = z.shape[0]
    row = lax.broadcasted_iota(I32, z.shape, 0)
    p1 = carry_ref[7:8, :]
    p2 = carry_ref[6:7, :]
    z1 = jnp.where(row == 0, p1, pltpu.roll(z, 1, axis=0))
    z2 = jnp.where(row == 0, p2, jnp.where(row == 1, p1, pltpu.roll(z, 2, axis=0)))
    carry_ref[...] = z[tm - 8:tm, :]
    yc = cb_ref[...] * (z2 * cw_ref[0:1, :] + z1 * cw_ref[1:2, :] + z * cw_ref[2:3, :])
    y_cv = _rms(yc, cn_ref[...])

    acc = h_ref[...]
    hg_row = S5_WIDTH + CONV_WIDTH
    acc = acc + jnp.dot(y_s5.astype(BF16), wo_ref[0:S5_WIDTH, :], preferred_element_type=F32)
    acc = acc + jnp.dot(y_cv.astype(BF16), wo_ref[S5_WIDTH:hg_row, :], preferred_element_type=F32)
    acc = acc + jnp.dot(hg_ref[...].astype(BF16), wo_ref[hg_row:, :], preferred_element_type=F32)
    o_ref[...] = acc


def _mix_out(ys5, proj, yhg, h, s5_d, s5_glu, s5_norm, conv_w, conv_norm, w_out):
    t, d = h.shape
    tm = min(ROW_TILE, t)
    cw = jnp.pad(conv_w.astype(F32), ((0, 8 - CONV_K), (0, 0)))
    wq = S5_WIDTH

    def rowblk(width, colblk):
        return pl.BlockSpec((tm, width), lambda i, c=colblk: (i, c))

    def full(shape):
        return pl.BlockSpec(shape, lambda i: (0,) * len(shape))

    return pl.pallas_call(
        _mix_out_kernel,
        grid=(t // tm,),
        in_specs=[rowblk(wq, 0), rowblk(wq, 0), rowblk(wq, 1), rowblk(wq, 2), rowblk(wq, 3),
                  rowblk(HG_WIDTH, 0), rowblk(d, 0),
                  full((1, wq)), full((wq, wq)), full((1, wq)), full((8, wq)), full((1, wq)),
                  full(w_out.shape)],
        out_specs=rowblk(d, 0),
        out_shape=jax.ShapeDtypeStruct((t, d), F32),
        scratch_shapes=[pltpu.VMEM((8, CONV_WIDTH), F32)],
        compiler_params=_cparams("arbitrary"),
        name="mix_out",
    )(ys5, proj, proj, proj, proj, yhg, h,
      s5_d.astype(F32).reshape(1, wq), s5_glu, s5_norm.astype(F32).reshape(1, wq), cw,
      conv_norm.astype(F32).reshape(1, wq), w_out)


def _ffn_kernel(h_ref, g_ref, w1_ref, w3_ref, w2_ref, o_ref, hn_ref, acc_ref):
    j = pl.program_id(1)

    @pl.when(j == 0)
    def _():
        x = h_ref[...]
        hn_ref[...] = _rms(x, g_ref[...]).astype(BF16)
        acc_ref[...] = x

    hn = hn_ref[...]
    a = jnp.dot(hn, w1_ref[...].astype(BF16), preferred_element_type=F32)
    b = jnp.dot(hn, w3_ref[...].astype(BF16), preferred_element_type=F32)
    gact = (a * _sigmoid(a) * b).astype(BF16)
    acc_ref[...] += jnp.dot(gact, w2_ref[...].astype(BF16), preferred_element_type=F32)

    @pl.when(j == pl.num_programs(1) - 1)
    def _():
        o_ref[...] = acc_ref[...]


def _ffn(h, g, w1, w3, w2, layer):
    t, d = h.shape
    f = w1.shape[2]
    tm = min(FFN_ROW_TILE, t)
    tf = FFN_COL_TILE
    return pl.pallas_call(
        _ffn_kernel,
        grid=(t // tm, f // tf),
        in_specs=[pl.BlockSpec((tm, d), lambda i, j: (i, 0)),
                  pl.BlockSpec((1, d), lambda i, j: (0, 0)),
                  pl.BlockSpec((None, d, tf), lambda i, j: (layer, 0, j)),
                  pl.BlockSpec((None, d, tf), lambda i, j: (layer, 0, j)),
                  pl.BlockSpec((None, tf, d), lambda i, j: (layer, j, 0))],
        out_specs=pl.BlockSpec((tm, d), lambda i, j: (i, 0)),
        out_shape=jax.ShapeDtypeStruct((t, d), F32),
        scratch_shapes=[pltpu.VMEM((tm, d), BF16), pltpu.VMEM((tm, d), F32)],
        compiler_params=_cparams("parallel", "arbitrary"),
        name="ffn_swiglu",
    )(h, g.astype(F32).reshape(1, d), w1, w3, w2)


def _router_kernel(h_ref, g_ref, r_ref, hn_ref, info_ref, before_ref, total_ref, cnt_ref, *, n_exp):
    @pl.when(pl.program_id(0) == 0)
    def _():
        cnt_ref[...] = jnp.zeros_like(cnt_ref)

    xn = _rms(h_ref[...], g_ref[...])
    hn_ref[...] = xn.astype(BF16)
    logits = jnp.dot(xn, r_ref[...], preferred_element_type=F32, precision=lax.Precision.HIGHEST)
    tm = logits.shape[0]
    lane = lax.broadcasted_iota(I32, logits.shape, 1)
    lanef = lane.astype(F32)
    neg = jnp.float32(-jnp.inf)
    lg = jnp.where(lane < n_exp, logits, neg)
    m1 = jnp.max(lg, axis=-1, keepdims=True)
    i1 = jnp.min(jnp.where(lg == m1, lanef, float(LANES)), axis=-1, keepdims=True)
    oh1 = lanef == i1
    lg2 = jnp.where(oh1, neg, lg)
    m2 = jnp.max(lg2, axis=-1, keepdims=True)
    i2 = jnp.min(jnp.where(lg2 == m2, lanef, float(LANES)), axis=-1, keepdims=True)
    oh2 = lanef == i2
    ex = jnp.exp(m2 - m1)
    g1 = 1.0 / (1.0 + ex)
    g2 = ex * g1
    chosen = jnp.where(oh1, 1.0, jnp.where(oh2, 1.0, 0.0))
    ri = lax.broadcasted_iota(I32, (tm, tm), 0)
    ci = lax.broadcasted_iota(I32, (tm, tm), 1)
    tri = jnp.where(ri > ci, 1.0, 0.0).astype(BF16)
    before = cnt_ref[...]
    cexcl = jnp.dot(tri, chosen.astype(BF16), preferred_element_type=F32) + before
    rank1 = jnp.sum(jnp.where(oh1, cexcl, 0.0), axis=-1, keepdims=True)
    rank2 = jnp.sum(jnp.where(oh2, cexcl, 0.0), axis=-1, keepdims=True)
    info = jnp.where(lane == 0, i1, jnp.where(lane == 1, i2, jnp.where(lane == 2, g1, jnp.where(
        lane == 3, g2, jnp.where(lane == 4, rank1, jnp.where(lane == 5, rank2, 0.0))))))
    info_ref[...] = info
    before_ref[0] = jnp.broadcast_to(before, before_ref.shape[1:])
    total = before + jnp.sum(chosen, axis=0, keepdims=True)
    cnt_ref[...] = total
    total_ref[...] = jnp.broadcast_to(total, total_ref.shape)


def _router(h, g, router):
    t, d = h.shape
    n_exp = router.shape[1]
    tm = min(TOK_WIN, t)
    ntw = t // tm
    rp = jnp.pad(router.astype(F32), ((0, 0), (0, LANES - n_exp)))
    return pl.pallas_call(
        functools.partial(_router_kernel, n_exp=n_exp),
        grid=(ntw,),
        in_specs=[pl.BlockSpec((tm, d), lambda i: (i, 0)),
                  pl.BlockSpec((1, d), lambda i: (0, 0)),
                  pl.BlockSpec((d, LANES), lambda i: (0, 0))],
        out_specs=[pl.BlockSpec((tm, d), lambda i: (i, 0)),
                   pl.BlockSpec((tm, LANES), lambda i: (i, 0)),
                   pl.BlockSpec((1, 8, LANES), lambda i: (i, 0, 0)),
                   pl.BlockSpec((8, LANES), lambda i: (0, 0))],
        out_shape=[jax.ShapeDtypeStruct((t, d), BF16),
                   jax.ShapeDtypeStruct((t, LANES), F32),
                   jax.ShapeDtypeStruct((ntw, 8, LANES), F32),
                   jax.ShapeDtypeStruct((8, LANES), F32)],
        scratch_shapes=[pltpu.VMEM((1, LANES), F32)],
        compiler_params=_cparams("arbitrary"),
        name="moe_router",
    )(h, g.astype(F32).reshape(1, d), rp)


FLAG_FIRST, FLAG_WORK, FLAG_LAST = 1, 2, 4


def _gather_kernel(blk_ref, win_ref, flag_ref, dest_ref, hn_ref, xs_ref, acc_ref):
    k = pl.program_id(0)
    flag = flag_ref[k]
    rows = acc_ref.shape[0]
    win = hn_ref.shape[0]

    @pl.when(jnp.bitwise_and(flag, FLAG_FIRST) != 0)
    def _():
        acc_ref[...] = jnp.zeros_like(acc_ref)

    @pl.when(jnp.bitwise_and(flag, FLAG_WORK) != 0)
    def _():
        slot = blk_ref[k] * rows + lax.broadcasted_iota(I32, (rows, win), 0)
        d = dest_ref[win_ref[k]]
        hit = jnp.where(d[0:1, :] == slot, 1.0, jnp.where(d[1:2, :] == slot, 1.0, 0.0)).astype(BF16)
        acc_ref[...] += jnp.dot(hit, hn_ref[...], preferred_element_type=F32)

    @pl.when(jnp.bitwise_and(flag, FLAG_LAST) != 0)
    def _():
        xs_ref[...] = acc_ref[...].astype(BF16)


def _gather(hn, dest, blk, win, flag, n_slots):
    t, d = hn.shape
    tw = dest.shape[-1]
    return pl.pallas_call(
        _gather_kernel,
        grid_spec=pltpu.PrefetchScalarGridSpec(
            num_scalar_prefetch=3,
            grid=(blk.shape[0],),
            in_specs=[pl.BlockSpec(dest.shape, lambda k, b, w, f: (0, 0, 0)),
                      pl.BlockSpec((tw, d), lambda k, b, w, f: (w[k], 0))],
            out_specs=pl.BlockSpec((GATHER_TILE, d), lambda k, b, w, f: (b[k], 0)),
            scratch_shapes=[pltpu.VMEM((GATHER_TILE, d), F32)]),
        out_shape=jax.ShapeDtypeStruct((n_slots, d), BF16),
        compiler_params=_cparams("arbitrary"),
        name="moe_gather",
    )(blk, win, flag, dest, hn)


def _expert_kernel(te_ref, tv_ref, x_ref, w1_ref, w3_ref, w2_ref, y_ref, acc_ref):
    b = pl.program_id(0)
    j = pl.program_id(1)

    @pl.when(j == 0)
    def _():
        acc_ref[...] = jnp.zeros_like(acc_ref)

    @pl.when(tv_ref[b] == 1)
    def _():
        x = x_ref[...]
        a = jnp.dot(x, w1_ref[...].astype(BF16), preferred_element_type=F32)
        c = jnp.dot(x, w3_ref[...].astype(BF16), preferred_element_type=F32)
        gact = (a * _sigmoid(a) * c).astype(BF16)
        acc_ref[...] += jnp.dot(gact, w2_ref[...].astype(BF16), preferred_element_type=F32)

    @pl.when(j == pl.num_programs(1) - 1)
    def _():
        y_ref[...] = acc_ref[...].astype(BF16)


def _experts(xs, tile_e, tile_v, w1, w3, w2, layer):
    ns, d = xs.shape
    f = w1.shape[3]
    tf = MOE_COL_TILE
    nj = f // tf
    n_tiles = ns // MOE_TILE

    def col(b, j, tv):
        return j * tv[b] + (nj - 1) * (1 - tv[b])

    return pl.pallas_call(
        _expert_kernel,
        grid_spec=pltpu.PrefetchScalarGridSpec(
            num_scalar_prefetch=2,
            grid=(n_tiles, nj),
            in_specs=[pl.BlockSpec((MOE_TILE, d), lambda b, j, te, tv: (b, 0)),
                      pl.BlockSpec((None, None, d, tf), lambda b, j, te, tv: (layer, te[b], 0, col(b, j, tv))),
                      pl.BlockSpec((None, None, d, tf), lambda b, j, te, tv: (layer, te[b], 0, col(b, j, tv))),
                      pl.BlockSpec((None, None, tf, d), lambda b, j, te, tv: (layer, te[b], col(b, j, tv), 0))],
            out_specs=pl.BlockSpec((MOE_TILE, d), lambda b, j, te, tv: (b, 0)),
            scratch_shapes=[pltpu.VMEM((MOE_TILE, d), F32)]),
        out_shape=jax.ShapeDtypeStruct((ns, d), BF16),
        compiler_params=_cparams("arbitrary", "arbitrary"),
        name="moe_experts",
    )(tile_e, tile_v, xs, w1, w3, w2)


def _combine_kernel(ws_ref, po_ref, ex_ref, info_ref, h_ref, yb_hbm, o_ref, buf_ref, sem, *, n_exp):
    i = pl.program_id(0)
    cur = lax.rem(i, 2)

    def copy(step, e, s):
        start = pl.multiple_of(ws_ref[step * n_exp + e], SLOT_ALIGN)
        return pltpu.make_async_copy(yb_hbm.at[pl.ds(start, SLOT_WIN), :], buf_ref.at[s, e], sem.at[s, e])

    @pl.when(i == 0)
    def _():
        for e in range(n_exp):
            copy(0, e, 0).start()

    @pl.when(i + 1 < pl.num_programs(0))
    def _():
        for e in range(n_exp):
            copy(i + 1, e, 1 - cur).start()

    info = info_ref[...]
    tm = info.shape[0]
    e1, e2 = info[:, 0:1], info[:, 1:2]
    g1, g2 = info[:, 2:3], info[:, 3:4]
    r1, r2 = info[:, 4:5], info[:, 5:6]
    lanef = lax.broadcasted_iota(I32, (tm, TOK_WIN), 1).astype(F32)
    g1b = jnp.broadcast_to(g1, (tm, TOK_WIN))
    g2b = jnp.broadcast_to(g2, (tm, TOK_WIN))

    def weights(e, shift, lo_lane):
        off = (ws_ref[i * n_exp + e] - po_ref[e] + shift).astype(F32)
        k1 = jnp.where(e1 == float(e), r1 - off, -1.0)
        k2 = jnp.where(e2 == float(e), r2 - off, -1.0)
        k1 = jnp.where(k1 >= float(lo_lane), k1, -1.0)
        k2 = jnp.where(k2 >= float(lo_lane), k2, -1.0)
        return jnp.where(k1 == lanef, g1b, jnp.where(k2 == lanef, g2b, 0.0)).astype(BF16)

    acc = h_ref[...]
    for e in range(n_exp):
        pt = weights(e, 0, 0)
        copy(i, e, cur).wait()
        acc = acc + jnp.dot(pt, buf_ref[cur, e, 0:TOK_WIN, :], preferred_element_type=F32)
    o_ref[...] = acc

    tail = SLOT_WIN - TOK_WIN
    for e in range(n_exp):
        @pl.when(ex_ref[i * n_exp + e] == 1)
        def _(e=e):
            pt = weights(e, tail, TOK_WIN - tail)
            o_ref[...] += jnp.dot(pt, buf_ref[cur, e, tail:SLOT_WIN, :], preferred_element_type=F32)


def _combine(info, h, yb, win_start, pstart, extra):
    t, d = h.shape
    n_exp = pstart.shape[0]
    tm = min(TOK_WIN, t)
    return pl.pallas_call(
        functools.partial(_combine_kernel, n_exp=n_exp),
        grid_spec=pltpu.PrefetchScalarGridSpec(
            num_scalar_prefetch=3,
            grid=(t // tm,),
            in_specs=[pl.BlockSpec((tm, LANES), lambda i, ws, po, ex: (i, 0)),
                      pl.BlockSpec((tm, d), lambda i, ws, po, ex: (i, 0)),
                      pl.BlockSpec(memory_space=pl.ANY)],
            out_specs=pl.BlockSpec((tm, d), lambda i, ws, po, ex: (i, 0)),
            scratch_shapes=[pltpu.VMEM((2, n_exp, SLOT_WIN, d), BF16),
                            pltpu.SemaphoreType.DMA((2, n_exp))]),
        out_shape=jax.ShapeDtypeStruct((t, d), F32),
        compiler_params=_cparams("arbitrary"),
        name="moe_combine",
    )(win_start, pstart, extra, info, h, yb)


def _count_le(sorted_vals, x):
    return jnp.sum((sorted_vals[None, :] <= x[:, None]).astype(I32), axis=1)


def _moe(h, g, router, w1, w3, w2, layer):
    t, d = h.shape
    n_exp = router.shape[1]
    tw = min(TOK_WIN, t)
    ntw = t // tw
    hn, info, before, total = _router(h, g, router)

    counts = total[0, :n_exp].astype(I32)
    padded = (counts + MOE_TILE - 1) // MOE_TILE * MOE_TILE
    pend = jnp.cumsum(padded)
    pstart = pend - padded
    n_tiles = (t * TOP_K) // MOE_TILE + n_exp
    n_slots = n_tiles * MOE_TILE
    tile0 = jnp.arange(n_tiles, dtype=I32) * MOE_TILE
    tile_e = jnp.minimum(_count_le(pend, tile0), n_exp - 1)
    tile_v = (tile0 < pend[-1]).astype(I32)
    cum = jnp.concatenate([before[:, 0, :n_exp], total[0:1, :n_exp]], axis=0).astype(I32)

    n_gb = n_slots // GATHER_TILE
    gb0 = jnp.arange(n_gb, dtype=I32) * GATHER_TILE
    gb_e = jnp.minimum(_count_le(pend, gb0), n_exp - 1)
    r0 = gb0 - pstart[gb_e]
    cum_b = cum[:, gb_e]
    ilo = jnp.sum((cum_b[1:] <= r0[None, :]).astype(I32), axis=0)
    ihi = jnp.sum((cum_b[:-1] < (r0 + GATHER_TILE)[None, :]).astype(I32), axis=0) - 1
    n_work = jnp.where(gb0 < pend[-1], jnp.maximum(ihi - ilo + 1, 0), 0)
    n_step = jnp.maximum(n_work, 1)
    ends = jnp.cumsum(n_step)
    offs = ends - n_step
    steps = jnp.arange(n_exp * ntw + n_gb, dtype=I32)
    blk = jnp.minimum(_count_le(ends, steps), n_gb - 1)
    jb = steps - offs[blk]
    live = steps < ends[-1]
    win = jnp.clip(ilo[blk] + jb, 0, ntw - 1).astype(I32)
    flag = (jnp.where(live & (jb == 0), FLAG_FIRST, 0) + jnp.where(live & (jb < n_work[blk]), FLAG_WORK, 0)
            + jnp.where(live & (jb == n_step[blk] - 1), FLAG_LAST, 0)).astype(I32)

    e12 = info[:, 0:2].astype(I32)
    dest12 = pstart[e12] + info[:, 4:6].astype(I32)
    dest = jnp.full((ntw, 8, tw), -1, I32).at[:, 0:2, :].set(
        dest12.reshape(ntw, tw, 2).transpose(0, 2, 1))

    xs = _gather(hn, dest, blk, win, flag, n_slots)
    yb = _experts(xs, tile_e, tile_v, w1, w3, w2, layer)

    lo = pstart[None, :] + cum[:-1]
    win_start = jnp.minimum(lo // SLOT_ALIGN * SLOT_ALIGN, n_slots - SLOT_WIN).astype(I32)
    extra = ((lo - win_start + cum[1:] - cum[:-1]) > TOK_WIN).astype(I32)
    return _combine(info, h, yb, win_start.reshape(-1), pstart.astype(I32), extra.reshape(-1))


def _final_norm_kernel(h_ref, g_ref, o_ref):
    o_ref[...] = _rms(h_ref[...], g_ref[...])


def _final_norm(h, g):
    t, d = h.shape
    tm = min(ROW_TILE, t)
    return pl.pallas_call(
        _final_norm_kernel,
        grid=(t // tm,),
        in_specs=[pl.BlockSpec((tm, d), lambda i: (i, 0)), pl.BlockSpec((1, d), lambda i: (0, 0))],
        out_specs=pl.BlockSpec((tm, d), lambda i: (i, 0)),
        out_shape=jax.ShapeDtypeStruct((t, d), F32),
        compiler_params=_cparams("parallel"),
        name="final_norm",
    )(h, g.astype(F32).reshape(1, d))


def kernel(x, attn_norm, ffn_norm, final_norm, w_in, w_out, s5_lambda_re, s5_lambda_im, s5_log_dt, s5_b_re, s5_b_im, s5_c_re, s5_c_im, s5_d, s5_glu, s5_out_norm, conv_w, conv_out_norm, hg_lower_bounds, hg_out_norm, ffn_w1, ffn_w3, ffn_w2, moe_router, moe_w1, moe_w3, moe_w2):
    bsz, seq, d = x.shape
    depth = w_in.shape[0]
    t = bsz * seq
    assert bsz == 1, "token mixers are written for a single sequence"
    lb_soft = jax.nn.softmax(hg_lower_bounds.astype(F32), axis=0)
    lb_all = jnp.cumsum(lb_soft, axis=0) - lb_soft[0]
    n_scan = int(math.log2(t // S5_CHUNK))
    s5_ops = jax.vmap(functools.partial(_s5_operators, n_scan=n_scan))(
        s5_lambda_re, s5_lambda_im, s5_log_dt, s5_b_re, s5_b_im, s5_c_re, s5_c_im)
    h = x.reshape(t, d).astype(F32)
    for l in range(depth):
        proj = _norm_inproj(h, attn_norm[l].astype(F32), w_in[l].astype(BF16))
        ys5 = _s5_conv(proj[:, :S5_WIDTH], [op[l] for op in s5_ops])
        yhg = _hgrn(proj, lb_all[l], hg_out_norm[l])
        h = _mix_out(ys5, proj, yhg, h, s5_d[l], s5_glu[l].astype(BF16), s5_out_norm[l],
                     conv_w[l], conv_out_norm[l], w_out[l].astype(BF16))
        j = l // 2
        if l % 2 == 0:
            h = _ffn(h, ffn_norm[l], ffn_w1, ffn_w3, ffn_w2, j)
        else:
            h = _moe(h, ffn_norm[l], moe_router[j], moe_w1, moe_w3, moe_w2, j)
    return _final_norm(h, final_norm).reshape(bsz, seq, d)
```

```python
import functools
import math

import numpy as np
import jax
import jax.numpy as jnp
from jax import lax
from jax.experimental import pallas as pl
from jax.experimental.pallas import tpu as pltpu

F32 = jnp.float32
BF16 = jnp.bfloat16
I32 = jnp.int32

NORM_EPS = 1e-6
LB_FLOOR = 1e-30
TOP_K = 2

S5_WIDTH = 256
S5_GROUP = 16
S5_STATE = 64
CONV_WIDTH = 256
CONV_K = 3
HG_WIDTH = 512
HG_HEAD_DIM = 128
HG_HEADS = HG_WIDTH // HG_HEAD_DIM
LANES = 128
SUBLANES = 8
COL_CONV = S5_WIDTH
COL_HG = S5_WIDTH + 3 * CONV_WIDTH

VMEM_LIMIT = 56 * 1024 * 1024
S5_CHUNK = 32
HG_CHUNK = 128
HG_LEVELS = int(math.log2(HG_CHUNK))
HG_TBLOCK = 512
ROW_TILE = 512
FFN_ROW_TILE = 1024
FFN_COL_TILE = 512
MOE_TILE = 1024
MOE_COL_TILE = 512
GATHER_TILE = 256
TOK_WIN = 256
SLOT_ALIGN = 16
SLOT_WIN = TOK_WIN + SLOT_ALIGN


def _cparams(*sem):
    return pltpu.CompilerParams(dimension_semantics=sem, vmem_limit_bytes=VMEM_LIMIT)


def _rms(x, g):
    ms = jnp.mean(x * x, axis=-1, keepdims=True)
    return x * lax.rsqrt(ms + NORM_EPS) * g


def _sigmoid(x):
    return 1.0 / (1.0 + jnp.exp(-x))


def _norm_inproj_kernel(h_ref, g_ref, w_ref, o_ref):
    xn = _rms(h_ref[...], g_ref[...]).astype(BF16)
    o_ref[...] = jnp.dot(xn, w_ref[...], preferred_element_type=F32)


def _norm_inproj(h, g, w):
    t, d = h.shape
    n = w.shape[1]
    tm = min(ROW_TILE, t)
    return pl.pallas_call(
        _norm_inproj_kernel,
        grid=(t // tm,),
        in_specs=[pl.BlockSpec((tm, d), lambda i: (i, 0)),
                  pl.BlockSpec((1, d), lambda i: (0, 0)),
                  pl.BlockSpec((d, n), lambda i: (0, 0))],
        out_specs=pl.BlockSpec((tm, n), lambda i: (i, 0)),
        out_shape=jax.ShapeDtypeStruct((t, n), F32),
        compiler_params=_cparams("parallel"),
        name="norm_inproj",
    )(h, g.reshape(1, d), w)


def _s5_operators(lam_re, lam_im, log_dt, b_re, b_im, c_re, c_im, n_scan):
    lc = S5_CHUNK
    hi = lax.Precision.HIGHEST
    lr, li = lam_re.astype(F32), lam_im.astype(F32)
    dt = jnp.exp(log_dt.astype(F32))[:, None]
    zr, zi = lr * dt, li * dt
    taus = jnp.arange(lc + 1, dtype=F32)[:, None, None]
    mag = jnp.exp(zr[None] * taus)
    pwr, pwi = mag * jnp.cos(zi[None] * taus), mag * jnp.sin(zi[None] * taus)
    nr, ni = pwr[1] - 1.0, pwi[1]
    den = lr * lr + li * li
    qr, qi = (nr * lr + ni * li) / den, (ni * lr - nr * li) / den
    br, bi = b_re.astype(F32), b_im.astype(F32)
    bbr = qr[..., None] * br - qi[..., None] * bi
    bbi = qr[..., None] * bi + qi[..., None] * br
    cr, ci = c_re.astype(F32), c_im.astype(F32)
    g_, p_ = lr.shape

    def c_times_pw(lo):
        wr, wi = pwr[lo:lo + lc, :, None, :], pwi[lo:lo + lc, :, None, :]
        return cr[None] * wr - ci[None] * wi, cr[None] * wi + ci[None] * wr

    cpr, cpi = c_times_pw(0)
    kt = (jnp.einsum('tgcp,gpi->tgci', cpr, bbr, precision=hi)
          - jnp.einsum('tgcp,gpi->tgci', cpi, bbi, precision=hi))
    kflat = kt.transpose(1, 3, 0, 2).reshape(g_, S5_GROUP, lc * S5_GROUP)

    wr, wi = pwr[lc - 1::-1][:, :, :, None], pwi[lc - 1::-1][:, :, :, None]
    msr = (wr * bbr[None] - wi * bbi[None]).transpose(1, 0, 3, 2).reshape(g_, lc * S5_GROUP, p_)
    msi = (wr * bbi[None] + wi * bbr[None]).transpose(1, 0, 3, 2).reshape(g_, lc * S5_GROUP, p_)
    m_state = jnp.concatenate([msr, msi], axis=-1)

    c1r, c1i = c_times_pw(1)
    c1r = c1r.transpose(1, 3, 0, 2).reshape(g_, p_, lc * S5_GROUP)
    c1i = c1i.transpose(1, 3, 0, 2).reshape(g_, p_, lc * S5_GROUP)
    m_carry = jnp.concatenate([c1r, -c1i], axis=1)

    akr, aki = [pwr[lc]], [pwi[lc]]
    for _ in range(n_scan - 1):
        r, i = akr[-1], aki[-1]
        akr.append(r * r - i * i)
        aki.append(2.0 * r * i)
    akr, aki = jnp.stack(akr, axis=1), jnp.stack(aki, axis=1)
    ar = jnp.concatenate([akr, akr], axis=-1)
    ai = jnp.concatenate([-aki, aki], axis=-1)
    kpad = -(-n_scan // 8) * 8
    ar = jnp.pad(ar, ((0, 0), (0, kpad - n_scan), (0, 0)))
    ai = jnp.pad(ai, ((0, 0), (0, kpad - n_scan), (0, 0)))
    return kflat, m_state.astype(BF16), m_carry.astype(BF16), ar, ai


def _s5_kernel(u_ref, kf_ref, ms_ref, mc_ref, ar_ref, ai_ref, y_ref, mi_ref, *, n_scan):
    kf = kf_ref[0]
    lane = lax.broadcasted_iota(I32, kf.shape, 1)
    for s in range(S5_CHUNK):
        sh = s * S5_GROUP
        blk = kf if s == 0 else jnp.where(lane >= sh, pltpu.roll(kf, sh, axis=1), 0.0)
        mi_ref[sh:sh + S5_GROUP, :] = blk.astype(BF16)

    u = u_ref[0]
    x = jnp.dot(u, ms_ref[0], preferred_element_type=F32)
    row = lax.broadcasted_iota(I32, x.shape, 0)
    half = x.shape[1] // 2
    for k in range(n_scan):
        d = 1 << k
        s = jnp.where(row >= d, pltpu.roll(x, d, axis=0), 0.0)
        x = x + ar_ref[0, k:k + 1, :] * s + ai_ref[0, k:k + 1, :] * pltpu.roll(s, half, axis=1)
    xe = jnp.where(row >= 1, pltpu.roll(x, 1, axis=0), 0.0)
    y = jnp.dot(u, mi_ref[...], preferred_element_type=F32)
    y = y + jnp.dot(xe.astype(BF16), mc_ref[0], preferred_element_type=F32)
    y_ref[0] = y


def _s5_conv(u, ops):
    kflat, m_state, m_carry, ar, ai = ops
    t = u.shape[0]
    lc = S5_CHUNK
    g_ = S5_WIDTH // S5_GROUP
    nch = t // lc
    n_scan = int(math.log2(nch))
    assert (1 << n_scan) == nch
    w = lc * S5_GROUP
    p2 = m_state.shape[-1]
    ug = u.astype(BF16).reshape(nch, lc, g_, S5_GROUP).transpose(2, 0, 1, 3).reshape(g_, nch, w)
    y = pl.pallas_call(
        functools.partial(_s5_kernel, n_scan=n_scan),
        grid=(g_,),
        in_specs=[pl.BlockSpec((1, nch, w), lambda g: (g, 0, 0)),
                  pl.BlockSpec((1, S5_GROUP, w), lambda g: (g, 0, 0)),
                  pl.BlockSpec((1, w, p2), lambda g: (g, 0, 0)),
                  pl.BlockSpec((1, p2, w), lambda g: (g, 0, 0)),
                  pl.BlockSpec((1, ar.shape[1], p2), lambda g: (g, 0, 0)),
                  pl.BlockSpec((1, ai.shape[1], p2), lambda g: (g, 0, 0))],
        out_specs=pl.BlockSpec((1, nch, w), lambda g: (g, 0, 0)),
        out_shape=jax.ShapeDtypeStruct((g_, nch, w), F32),
        scratch_shapes=[pltpu.VMEM((w, w), BF16)],
        compiler_params=_cparams("parallel"),
        name="s5_conv",
    )(ug, kflat, m_state, m_carry, ar, ai)
    return y.reshape(g_, nch, lc, S5_GROUP).transpose(1, 2, 0, 3).reshape(t, S5_WIDTH)


LOG2E = float(np.log2(np.e))


def _hg_level_table():
    idx = np.arange(HG_CHUNK)
    t, s = idx[:, None], idx[None, :]
    top_bit = np.floor(np.log2(np.maximum(t ^ s, 1))).astype(np.int32)
    return np.where(s < t, HG_LEVELS - 1 - top_bit, -1).astype(np.int32)


def _hg_midpoint(bc, m):
    c = bc.shape[0]
    h = m // 2
    if h >= SUBLANES:
        return jnp.concatenate([jnp.broadcast_to(bc[j * m + h - 1:j * m + h, :], (m, LANES))
                                for j in range(c // m)], axis=0)
    x3 = bc.reshape(c // SUBLANES, SUBLANES, LANES)
    sub = lax.broadcasted_iota(I32, x3.shape, 1)
    beta = None
    for j in reversed(range(SUBLANES // m)):
        row = jnp.broadcast_to(x3[:, j * m + h - 1:j * m + h, :], x3.shape)
        beta = row if beta is None else jnp.where(sub < (j + 1) * m, row, beta)
    return beta.reshape(c, LANES)


def _hg_chunk(fp, q, v, gt, lb, lbf, ng, lvl, st):
    c = HG_CHUNK
    prow = lax.broadcasted_iota(I32, (c, LANES), 0)
    nt = (((1,), (1,)), ((), ()))
    tn = (((0,), (0,)), ((), ()))
    en = jnp.exp(-jnp.abs(fp))
    rc = 1.0 / (1.0 + en)
    pos_f = fp >= 0.0
    sig_p = jnp.where(pos_f, rc, en * rc)
    sig_n = jnp.where(pos_f, en * rc, rc)
    f = lbf + (1.0 - lb) * sig_p
    lf = jnp.log(f)
    kc = (1.0 - lb) * sig_n
    qc = q * _sigmoid(q)
    bc = lf
    for k in range(HG_LEVELS):
        d = 1 << k
        bc = bc + jnp.where(prow >= d, pltpu.roll(bc, d, axis=0), 0.0)

    a = jnp.zeros((c, c), F32)
    for lev in range(HG_LEVELS):
        m = c >> lev
        upper = jnp.bitwise_and(prow, m - 1) >= m // 2
        z = jnp.where(upper, qc, kc)
        if m == 2:
            zw = jnp.where(upper, z * f, z)
        else:
            dlt = bc - _hg_midpoint(bc, m)
            zw = z * jnp.exp2(dlt * jnp.where(upper, LOG2E, -LOG2E))
        zw = zw.astype(BF16)
        s = lax.dot_general(zw, zw, nt, preferred_element_type=F32)
        a = jnp.where(lvl == lev, s, a)
    vb = v.astype(BF16)
    o = jnp.dot(a.astype(BF16), vb, preferred_element_type=F32)
    o = o + jnp.sum(qc * kc, axis=-1, keepdims=True) * v
    o = o + lax.dot_general((qc * jnp.exp(bc)).astype(BF16), st.astype(BF16), nt,
                            preferred_element_type=F32)
    bl = bc[c - 1:c, :]
    khat = (kc * jnp.exp(bl - bc)).astype(BF16)
    st = st * jnp.exp(bl) + lax.dot_general(vb, khat, tn, preferred_element_type=F32)
    return _rms(o, ng) * (gt * _sigmoid(gt)), st


def _hgrn_kernel(q_ref, f_ref, i_ref, gt_ref, lb_ref, lbf_ref, ng_ref, lvl_ref, o_ref, st_ref):
    c = HG_CHUNK

    @pl.when(pl.program_id(0) == 0)
    def _():
        st_ref[...] = jnp.zeros_like(st_ref)

    def chunk(n, carry):
        rows = pl.ds(pl.multiple_of(n * c, c), c)
        for hd in range(HG_HEADS):
            cols = slice(hd * LANES, (hd + 1) * LANES)
            o, st = _hg_chunk(f_ref[rows, cols], q_ref[rows, cols], i_ref[rows, cols], gt_ref[rows, cols],
                              lb_ref[:, cols], lbf_ref[:, cols], ng_ref[:, cols], lvl_ref[...], st_ref[hd])
            o_ref[rows, cols] = o
            st_ref[hd] = st
        return carry

    lax.fori_loop(0, q_ref.shape[0] // c, chunk, 0)


def _hgrn(proj, lb, norm_g):
    t = proj.shape[0]
    tb = min(HG_TBLOCK, t)
    lbh = jnp.clip(lb.astype(F32), 0.0, 1.0 - 1e-6).reshape(1, HG_WIDTH)
    lbf = jnp.maximum(lbh, LB_FLOOR)
    ng = norm_g.astype(F32).reshape(1, HG_WIDTH)
    lvl = jnp.asarray(_hg_level_table())
    cb = COL_HG // HG_WIDTH

    def col(k):
        return pl.BlockSpec((tb, HG_WIDTH), lambda i, k=k: (i, cb + k))

    vec = pl.BlockSpec((1, HG_WIDTH), lambda i: (0, 0))
    return pl.pallas_call(
        _hgrn_kernel,
        grid=(t // tb,),
        in_specs=[col(0), col(1), col(2), col(3), vec, vec, vec,
                  pl.BlockSpec(lvl.shape, lambda i: (0, 0))],
        out_specs=pl.BlockSpec((tb, HG_WIDTH), lambda i: (i, 0)),
        out_shape=jax.ShapeDtypeStruct((t, HG_WIDTH), F32),
        scratch_shapes=[pltpu.VMEM((HG_HEADS, LANES, LANES), F32)],
        compiler_params=_cparams("arbitrary"),
        name="hgrn2",
    )(proj, proj, proj, proj, lbh, lbf, ng, lvl)


def _mix_out_kernel(ys_ref, u_ref, cb_ref, cc_ref, cv_ref, hg_ref, h_ref,
                    d_ref, glu_ref, sn_ref, cw_ref, cn_ref, wo_ref, o_ref, carry_ref):
    @pl.when(pl.program_id(0) == 0)
    def _():
        carry_ref[...] = jnp.zeros_like(carry_ref)

    y = ys_ref[...] + d_ref[...] * u_ref[...]
    y = jax.nn.gelu(y)
    y = y * _sigmoid(jnp.dot(y.astype(BF16), glu_ref[...], preferred_element_type=F32))
    y_s5 = _rms(y, sn_ref[...])

    z = cc_ref[...] * cv_ref[...]
    tm = z.shape[0]
    row = lax.broadcasted_iota(I32, z.shape, 0)
    p1 = carry_ref[7:8, :]
    p2 = carry_ref[6:7, :]
    z1 = jnp.where(row == 0, p1, pltpu.roll(z, 1, axis=0))
    z2 = jnp.where(row == 0, p2, jnp.where(row == 1, p1, pltpu.roll(z, 2, axis=0)))
    carry_ref[...] = z[tm - 8:tm, :]
    yc = cb_ref[...] * (z2 * cw_ref[0:1, :] + z1 * cw_ref[1:2, :] + z * cw_ref[2:3, :])
    y_cv = _rms(yc, cn_ref[...])

    acc = h_ref[...]
    hg_row = S5_WIDTH + CONV_WIDTH
    acc = acc + jnp.dot(y_s5.astype(BF16), wo_ref[0:S5_WIDTH, :], preferred_element_type=F32)
    acc = acc + jnp.dot(y_cv.astype(BF16), wo_ref[S5_WIDTH:hg_row, :], preferred_element_type=F32)
    acc = acc + jnp.dot(hg_ref[...].astype(BF16), wo_ref[hg_row:, :], preferred_element_type=F32)
    o_ref[...] = acc


def _mix_out(ys5, proj, yhg, h, s5_d, s5_glu, s5_norm, conv_w, conv_norm, w_out):
    t, d = h.shape
    tm = min(ROW_TILE, t)
    cw = jnp.pad(conv_w.astype(F32), ((0, 8 - CONV_K), (0, 0)))
    wq = S5_WIDTH

    def rowblk(width, colblk):
        return pl.BlockSpec((tm, width), lambda i, c=colblk: (i, c))

    def full(shape):
        return pl.BlockSpec(shape, lambda i: (0,) * len(shape))

    return pl.pallas_call(
        _mix_out_kernel,
        grid=(t // tm,),
        in_specs=[rowblk(wq, 0), rowblk(wq, 0), rowblk(wq, 1), rowblk(wq, 2), rowblk(wq, 3),
                  rowblk(HG_WIDTH, 0), rowblk(d, 0),
                  full((1, wq)), full((wq, wq)), full((1, wq)), full((8, wq)), full((1, wq)),
                  full(w_out.shape)],
        out_specs=rowblk(d, 0),
        out_shape=jax.ShapeDtypeStruct((t, d), F32),
        scratch_shapes=[pltpu.VMEM((8, CONV_WIDTH), F32)],
        compiler_params=_cparams("arbitrary"),
        name="mix_out",
    )(ys5, proj, proj, proj, proj, yhg, h,
      s5_d.astype(F32).reshape(1, wq), s5_glu, s5_norm.astype(F32).reshape(1, wq), cw,
      conv_norm.astype(F32).reshape(1, wq), w_out)


def _ffn_kernel(h_ref, g_ref, w1_ref, w3_ref, w2_ref, o_ref, hn_ref, acc_ref):
    j = pl.program_id(1)

    @pl.when(j == 0)
    def _():
        x = h_ref[...]
        hn_ref[...] = _rms(x, g_ref[...]).astype(BF16)
        acc_ref[...] = x

    hn = hn_ref[...]
    a = jnp.dot(hn, w1_ref[...].astype(BF16), preferred_element_type=F32)
    b = jnp.dot(hn, w3_ref[...].astype(BF16), preferred_element_type=F32)
    gact = (a * _sigmoid(a) * b).astype(BF16)
    acc_ref[...] += jnp.dot(gact, w2_ref[...].astype(BF16), preferred_element_type=F32)

    @pl.when(j == pl.num_programs(1) - 1)
    def _():
        o_ref[...] = acc_ref[...]


def _ffn(h, g, w1, w3, w2, layer):
    t, d = h.shape
    f = w1.shape[2]
    tm = min(FFN_ROW_TILE, t)
    tf = FFN_COL_TILE
    return pl.pallas_call(
        _ffn_kernel,
        grid=(t // tm, f // tf),
        in_specs=[pl.BlockSpec((tm, d), lambda i, j: (i, 0)),
                  pl.BlockSpec((1, d), lambda i, j: (0, 0)),
                  pl.BlockSpec((None, d, tf), lambda i, j: (layer, 0, j)),
                  pl.BlockSpec((None, d, tf), lambda i, j: (layer, 0, j)),
                  pl.BlockSpec((None, tf, d), lambda i, j: (layer, j, 0))],
        out_specs=pl.BlockSpec((tm, d), lambda i, j: (i, 0)),
        out_shape=jax.ShapeDtypeStruct((t, d), F32),
        scratch_shapes=[pltpu.VMEM((tm, d), BF16), pltpu.VMEM((tm, d), F32)],
        compiler_params=_cparams("parallel", "arbitrary"),
        name="ffn_swiglu",
    )(h, g.astype(F32).reshape(1, d), w1, w3, w2)


def _router_kernel(h_ref, g_ref, r_ref, hn_ref, info_ref, info_t_ref, before_ref, total_ref, cnt_ref, *, n_exp):
    @pl.when(pl.program_id(0) == 0)
    def _():
        cnt_ref[...] = jnp.zeros_like(cnt_ref)

    xn = _rms(h_ref[...], g_ref[...])
    hn_ref[...] = xn.astype(BF16)
    logits = jnp.dot(xn, r_ref[...], preferred_element_type=F32, precision=lax.Precision.HIGHEST)
    tm = logits.shape[0]
    lane = lax.broadcasted_iota(I32, logits.shape, 1)
    lanef = lane.astype(F32)
    neg = jnp.float32(-jnp.inf)
    lg = jnp.where(lane < n_exp, logits, neg)
    m1 = jnp.max(lg, axis=-1, keepdims=True)
    i1 = jnp.min(jnp.where(lg == m1, lanef, float(LANES)), axis=-1, keepdims=True)
    oh1 = lanef == i1
    lg2 = jnp.where(oh1, neg, lg)
    m2 = jnp.max(lg2, axis=-1, keepdims=True)
    i2 = jnp.min(jnp.where(lg2 == m2, lanef, float(LANES)), axis=-1, keepdims=True)
    oh2 = lanef == i2
    ex = jnp.exp(m2 - m1)
    g1 = 1.0 / (1.0 + ex)
    g2 = ex * g1
    chosen = jnp.where(oh1, 1.0, jnp.where(oh2, 1.0, 0.0))
    ri = lax.broadcasted_iota(I32, (tm, tm), 0)
    ci = lax.broadcasted_iota(I32, (tm, tm), 1)
    tri = jnp.where(ri > ci, 1.0, 0.0).astype(BF16)
    before = cnt_ref[...]
    cexcl = jnp.dot(tri, chosen.astype(BF16), preferred_element_type=F32) + before
    rank1 = jnp.sum(jnp.where(oh1, cexcl, 0.0), axis=-1, keepdims=True)
    rank2 = jnp.sum(jnp.where(oh2, cexcl, 0.0), axis=-1, keepdims=True)
    info = jnp.where(lane == 0, i1, jnp.where(lane == 1, i2, jnp.where(lane == 2, g1, jnp.where(
        lane == 3, g2, jnp.where(lane == 4, rank1, jnp.where(lane == 5, rank2, 0.0))))))
    info_ref[...] = info
    info_t_ref[...] = info.T[0:SUBLANES, :]
    before_ref[0] = jnp.broadcast_to(before, before_ref.shape[1:])
    total = before + jnp.sum(chosen, axis=0, keepdims=True)
    cnt_ref[...] = total
    total_ref[...] = jnp.broadcast_to(total, total_ref.shape)


def _router(h, g, router):
    t, d = h.shape
    n_exp = router.shape[1]
    tm = min(TOK_WIN, t)
    ntw = t // tm
    rp = jnp.pad(router.astype(F32), ((0, 0), (0, LANES - n_exp)))
    return pl.pallas_call(
        functools.partial(_router_kernel, n_exp=n_exp),
        grid=(ntw,),
        in_specs=[pl.BlockSpec((tm, d), lambda i: (i, 0)),
                  pl.BlockSpec((1, d), lambda i: (0, 0)),
                  pl.BlockSpec((d, LANES), lambda i: (0, 0))],
        out_specs=[pl.BlockSpec((tm, d), lambda i: (i, 0)),
                   pl.BlockSpec((tm, LANES), lambda i: (i, 0)),
                   pl.BlockSpec((SUBLANES, tm), lambda i: (0, i)),
                   pl.BlockSpec((1, 8, LANES), lambda i: (i, 0, 0)),
                   pl.BlockSpec((8, LANES), lambda i: (0, 0))],
        out_shape=[jax.ShapeDtypeStruct((t, d), BF16),
                   jax.ShapeDtypeStruct((t, LANES), F32),
                   jax.ShapeDtypeStruct((SUBLANES, t), F32),
                   jax.ShapeDtypeStruct((ntw, 8, LANES), F32),
                   jax.ShapeDtypeStruct((8, LANES), F32)],
        scratch_shapes=[pltpu.VMEM((1, LANES), F32)],
        compiler_params=_cparams("arbitrary"),
        name="moe_router",
    )(h, g.astype(F32).reshape(1, d), rp)


def _gather_kernel(ilo_ref, nw_ref, dest_ref, hn_hbm, xs_ref, buf_ref, sem, acc_ref, par_ref):
    b = pl.program_id(0)
    last_b = pl.num_programs(0) - 1
    rows = acc_ref.shape[0]
    win = buf_ref.shape[1]
    n = nw_ref[b]
    lo = ilo_ref[b]
    nxt = jnp.minimum(b + 1, last_b)
    next_lo = ilo_ref[nxt]
    next_work = jnp.logical_and(b < last_b, nw_ref[nxt] > 0)

    def copy(w, s):
        src = hn_hbm.at[pl.ds(pl.multiple_of(w * win, win), win), :]
        return pltpu.make_async_copy(src, buf_ref.at[s], sem.at[s])

    @pl.when(b == 0)
    def _():
        par_ref[0] = 0

        @pl.when(n > 0)
        def _():
            copy(lo, 0).start()

    p0 = par_ref[0]
    acc_ref[...] = jnp.zeros_like(acc_ref)
    slot = b * rows + lax.broadcasted_iota(I32, (rows, win), 0)

    def body(j, carry):
        s = jnp.bitwise_and(p0 + j, 1)
        is_last = j + 1 == n
        w_next = jnp.where(is_last, next_lo, lo + j + 1)

        @pl.when(jnp.logical_or(jnp.logical_not(is_last), next_work))
        def _():
            copy(w_next, 1 - s).start()

        col = pl.ds(pl.multiple_of((lo + j) * win, win), win)
        d1 = dest_ref[0:1, col]
        d2 = dest_ref[1:2, col]
        hit = jnp.where(d1 == slot, 1.0, jnp.where(d2 == slot, 1.0, 0.0)).astype(BF16)
        copy(lo + j, s).wait()
        acc_ref[...] += jnp.dot(hit, buf_ref[s], preferred_element_type=F32)
        return carry

    lax.fori_loop(0, n, body, 0)
    xs_ref[...] = acc_ref[...].astype(BF16)

    @pl.when(jnp.logical_and(n == 0, next_work))
    def _():
        copy(next_lo, p0).start()

    par_ref[0] = jnp.bitwise_and(p0 + n, 1)


def _gather(hn, dest, ilo, nw, n_slots):
    t, d = hn.shape
    return pl.pallas_call(
        _gather_kernel,
        grid_spec=pltpu.PrefetchScalarGridSpec(
            num_scalar_prefetch=2,
            grid=(n_slots // GATHER_TILE,),
            in_specs=[pl.BlockSpec(dest.shape, lambda b, lo, nw: (0, 0)),
                      pl.BlockSpec(memory_space=pl.ANY)],
            out_specs=pl.BlockSpec((GATHER_TILE, d), lambda b, lo, nw: (b, 0)),
            scratch_shapes=[pltpu.VMEM((2, TOK_WIN, d), BF16), pltpu.SemaphoreType.DMA((2,)),
                            pltpu.VMEM((GATHER_TILE, d), F32), pltpu.SMEM((1,), I32)]),
        out_shape=jax.ShapeDtypeStruct((n_slots, d), BF16),
        compiler_params=_cparams("arbitrary"),
        name="moe_gather",
    )(ilo, nw, dest, hn)


def _expert_kernel(te_ref, tv_ref, x_ref, w1_ref, w3_ref, w2_ref, y_ref, acc_ref):
    b = pl.program_id(0)
    j = pl.program_id(1)

    @pl.when(j == 0)
    def _():
        acc_ref[...] = jnp.zeros_like(acc_ref)

    @pl.when(tv_ref[b] == 1)
    def _():
        x = x_ref[...]
        a = jnp.dot(x, w1_ref[...].astype(BF16), preferred_element_type=F32)
        c = jnp.dot(x, w3_ref[...].astype(BF16), preferred_element_type=F32)
        gact = (a * _sigmoid(a) * c).astype(BF16)
        acc_ref[...] += jnp.dot(gact, w2_ref[...].astype(BF16), preferred_element_type=F32)

    @pl.when(j == pl.num_programs(1) - 1)
    def _():
        y_ref[...] = acc_ref[...].astype(BF16)


def _experts(xs, tile_e, tile_v, w1, w3, w2, layer):
    ns, d = xs.shape
    f = w1.shape[3]
    tf = MOE_COL_TILE
    nj = f // tf
    n_tiles = ns // MOE_TILE

    def col(b, j, tv):
        return j * tv[b] + (nj - 1) * (1 - tv[b])

    return pl.pallas_call(
        _expert_kernel,
        grid_spec=pltpu.PrefetchScalarGridSpec(
            num_scalar_prefetch=2,
            grid=(n_tiles, nj),
            in_specs=[pl.BlockSpec((MOE_TILE, d), lambda b, j, te, tv: (b, 0)),
                      pl.BlockSpec((None, None, d, tf), lambda b, j, te, tv: (layer, te[b], 0, col(b, j, tv))),
                      pl.BlockSpec((None, None, d, tf), lambda b, j, te, tv: (layer, te[b], 0, col(b, j, tv))),
                      pl.BlockSpec((None, None, tf, d), lambda b, j, te, tv: (layer, te[b], col(b, j, tv), 0))],
            out_specs=pl.BlockSpec((MOE_TILE, d), lambda b, j, te, tv: (b, 0)),
            scratch_shapes=[pltpu.VMEM((MOE_TILE, d), F32)]),
        out_shape=jax.ShapeDtypeStruct((ns, d), BF16),
        compiler_params=_cparams("arbitrary", "arbitrary"),
        name="moe_experts",
    )(tile_e, tile_v, xs, w1, w3, w2)


def _combine_kernel(ws_ref, po_ref, ex_ref, info_ref, h_ref, fin_ref, yb_hbm, o_ref, buf_ref, sem, *,
                    n_exp, final_norm):
    i = pl.program_id(0)
    cur = lax.rem(i, 2)

    def copy(step, e, s):
        start = pl.multiple_of(ws_ref[step * n_exp + e], SLOT_ALIGN)
        return pltpu.make_async_copy(yb_hbm.at[pl.ds(start, SLOT_WIN), :], buf_ref.at[s, e], sem.at[s, e])

    @pl.when(i == 0)
    def _():
        for e in range(n_exp):
            copy(0, e, 0).start()

    @pl.when(i + 1 < pl.num_programs(0))
    def _():
        for e in range(n_exp):
            copy(i + 1, e, 1 - cur).start()

    info = info_ref[...]
    tm = info.shape[0]
    e1, e2 = info[:, 0:1], info[:, 1:2]
    g1, g2 = info[:, 2:3], info[:, 3:4]
    r1, r2 = info[:, 4:5], info[:, 5:6]
    lanef = lax.broadcasted_iota(I32, (tm, TOK_WIN), 1).astype(F32)
    g1b = jnp.broadcast_to(g1, (tm, TOK_WIN))
    g2b = jnp.broadcast_to(g2, (tm, TOK_WIN))

    def weights(e, shift, lo_lane):
        off = (ws_ref[i * n_exp + e] - po_ref[e] + shift).astype(F32)
        k1 = jnp.where(e1 == float(e), r1 - off, -1.0)
        k2 = jnp.where(e2 == float(e), r2 - off, -1.0)
        k1 = jnp.where(k1 >= float(lo_lane), k1, -1.0)
        k2 = jnp.where(k2 >= float(lo_lane), k2, -1.0)
        return jnp.where(k1 == lanef, g1b, jnp.where(k2 == lanef, g2b, 0.0)).astype(BF16)

    acc = h_ref[...]
    for e in range(n_exp):
        pt = weights(e, 0, 0)
        copy(i, e, cur).wait()
        acc = acc + jnp.dot(pt, buf_ref[cur, e, 0:TOK_WIN, :], preferred_element_type=F32)
    o_ref[...] = acc

    tail = SLOT_WIN - TOK_WIN
    for e in range(n_exp):
        @pl.when(ex_ref[i * n_exp + e] == 1)
        def _(e=e):
            pt = weights(e, tail, TOK_WIN - tail)
            o_ref[...] += jnp.dot(pt, buf_ref[cur, e, tail:SLOT_WIN, :], preferred_element_type=F32)

    if final_norm:
        o_ref[...] = _rms(o_ref[...], fin_ref[...])


def _combine(info, h, yb, win_start, pstart, extra, fin_g, final_norm):
    t, d = h.shape
    n_exp = pstart.shape[0]
    tm = min(TOK_WIN, t)
    return pl.pallas_call(
        functools.partial(_combine_kernel, n_exp=n_exp, final_norm=final_norm),
        grid_spec=pltpu.PrefetchScalarGridSpec(
            num_scalar_prefetch=3,
            grid=(t // tm,),
            in_specs=[pl.BlockSpec((tm, LANES), lambda i, ws, po, ex: (i, 0)),
                      pl.BlockSpec((tm, d), lambda i, ws, po, ex: (i, 0)),
                      pl.BlockSpec((1, d), lambda i, ws, po, ex: (0, 0)),
                      pl.BlockSpec(memory_space=pl.ANY)],
            out_specs=pl.BlockSpec((tm, d), lambda i, ws, po, ex: (i, 0)),
            scratch_shapes=[pltpu.VMEM((2, n_exp, SLOT_WIN, d), BF16),
                            pltpu.SemaphoreType.DMA((2, n_exp))]),
        out_shape=jax.ShapeDtypeStruct((t, d), F32),
        compiler_params=_cparams("arbitrary"),
        name="moe_combine",
    )(win_start, pstart, extra, info, h, fin_g.astype(F32).reshape(1, d), yb)


def _count_le(sorted_vals, x):
    return jnp.sum((sorted_vals[None, :] <= x[:, None]).astype(I32), axis=1)


def _moe(h, g, router, w1, w3, w2, layer, fin_g, final_norm):
    t, d = h.shape
    n_exp = router.shape[1]
    tw = min(TOK_WIN, t)
    ntw = t // tw
    hn, info, info_t, before, total = _router(h, g, router)

    counts = total[0, :n_exp].astype(I32)
    padded = (counts + MOE_TILE - 1) // MOE_TILE * MOE_TILE
    pend = jnp.cumsum(padded)
    pstart = pend - padded
    n_tiles = (t * TOP_K) // MOE_TILE + n_exp
    n_slots = n_tiles * MOE_TILE
    tile0 = jnp.arange(n_tiles, dtype=I32) * MOE_TILE
    tile_e = jnp.minimum(_count_le(pend, tile0), n_exp - 1)
    tile_v = (tile0 < pend[-1]).astype(I32)
    cum = jnp.concatenate([before[:, 0, :n_exp], total[0:1, :n_exp]], axis=0).astype(I32)

    n_gb = n_slots // GATHER_TILE
    gb0 = jnp.arange(n_gb, dtype=I32) * GATHER_TILE
    gb_e = jnp.minimum(_count_le(pend, gb0), n_exp - 1)
    r0 = gb0 - pstart[gb_e]
    cum_b = cum[:, gb_e]
    ilo = jnp.sum((cum_b[1:] <= r0[None, :]).astype(I32), axis=0)
    ihi = jnp.sum((cum_b[:-1] < (r0 + GATHER_TILE)[None, :]).astype(I32), axis=0) - 1
    n_work = jnp.where(gb0 < pend[-1], jnp.maximum(ihi - ilo + 1, 0), 0).astype(I32)

    e_t = info_t[0:2].astype(I32)
    slot_t = jnp.sum(jnp.where(e_t[None] == jnp.arange(n_exp, dtype=I32)[:, None, None], pstart[:, None, None], 0),
                     axis=0) + info_t[4:6].astype(I32)
    dest = jnp.concatenate([slot_t, jnp.full((SUBLANES - TOP_K, t), -1, I32)], axis=0)

    xs = _gather(hn, dest, ilo.astype(I32), n_work, n_slots)
    yb = _experts(xs, tile_e, tile_v, w1, w3, w2, layer)

    lo = pstart[None, :] + cum[:-1]
    win_start = jnp.minimum(lo // SLOT_ALIGN * SLOT_ALIGN, n_slots - SLOT_WIN).astype(I32)
    extra = ((lo - win_start + cum[1:] - cum[:-1]) > TOK_WIN).astype(I32)
    return _combine(info, h, yb, win_start.reshape(-1), pstart.astype(I32), extra.reshape(-1), fin_g, final_norm)


def _final_norm_kernel(h_ref, g_ref, o_ref):
    o_ref[...] = _rms(h_ref[...], g_ref[...])


def _final_norm(h, g):
    t, d = h.shape
    tm = min(ROW_TILE, t)
    return pl.pallas_call(
        _final_norm_kernel,
        grid=(t // tm,),
        in_specs=[pl.BlockSpec((tm, d), lambda i: (i, 0)), pl.BlockSpec((1, d), lambda i: (0, 0))],
        out_specs=pl.BlockSpec((tm, d), lambda i: (i, 0)),
        out_shape=jax.ShapeDtypeStruct((t, d), F32),
        compiler_params=_cparams("parallel"),
        name="final_norm",
    )(h, g.astype(F32).reshape(1, d))


def kernel(x, attn_norm, ffn_norm, final_norm, w_in, w_out, s5_lambda_re, s5_lambda_im, s5_log_dt, s5_b_re, s5_b_im, s5_c_re, s5_c_im, s5_d, s5_glu, s5_out_norm, conv_w, conv_out_norm, hg_lower_bounds, hg_out_norm, ffn_w1, ffn_w3, ffn_w2, moe_router, moe_w1, moe_w3, moe_w2):
    bsz, seq, d = x.shape
    depth = w_in.shape[0]
    t = bsz * seq
    assert bsz == 1, "token mixers are written for a single sequence"
    lb_soft = jax.nn.softmax(hg_lower_bounds.astype(F32), axis=0)
    lb_all = jnp.cumsum(lb_soft, axis=0) - lb_soft[0]
    n_scan = int(math.log2(t // S5_CHUNK))
    s5_ops = jax.vmap(functools.partial(_s5_operators, n_scan=n_scan))(
        s5_lambda_re, s5_lambda_im, s5_log_dt, s5_b_re, s5_b_im, s5_c_re, s5_c_im)
    h = x.reshape(t, d).astype(F32)
    for l in range(depth):
        proj = _norm_inproj(h, attn_norm[l].astype(F32), w_in[l].astype(BF16))
        ys5 = _s5_conv(proj[:, :S5_WIDTH], [op[l] for op in s5_ops])
        yhg = _hgrn(proj, lb_all[l], hg_out_norm[l])
        h = _mix_out(ys5, proj, yhg, h, s5_d[l], s5_glu[l].astype(BF16), s5_out_norm[l],
                     conv_w[l], conv_out_norm[l], w_out[l].astype(BF16))
        j = l // 2
        if l % 2 == 0:
            h = _ffn(h, ffn_norm[l], ffn_w1, ffn_w3, ffn_w2, j)
        else:
            h = _moe(h, ffn_norm[l], moe_router[j], moe_w1, moe_w3, moe_w2, j, final_norm, l == depth - 1)
    if depth % 2 == 1:
        h = _final_norm(h, final_norm)
    return h.reshape(bsz, seq, d)
```

```python
import functools
import math

import numpy as np
import jax
import jax.numpy as jnp
from jax import lax
from jax.experimental import pallas as pl
from jax.experimental.pallas import tpu as pltpu

F32 = jnp.float32
BF16 = jnp.bfloat16
I32 = jnp.int32

NORM_EPS = 1e-6
LB_FLOOR = 1e-30
TOP_K = 2

S5_WIDTH = 256
S5_GROUP = 16
S5_STATE = 64
CONV_WIDTH = 256
CONV_K = 3
HG_WIDTH = 512
HG_HEAD_DIM = 128
HG_HEADS = HG_WIDTH // HG_HEAD_DIM
LANES = 128
SUBLANES = 8
COL_CONV = S5_WIDTH
COL_HG = S5_WIDTH + 3 * CONV_WIDTH

VMEM_LIMIT = 56 * 1024 * 1024
S5_CHUNK = 32
HG_CHUNK = 128
HG_LEVELS = int(math.log2(HG_CHUNK))
HG_TBLOCK = 512
ROW_TILE = 512
FFN_ROW_TILE = 1024
FFN_COL_TILE = 512
MOE_TILE = 1024
MOE_COL_TILE = 512
GATHER_TILE = 256
GATHER_DMA_SPLIT = 4
TOK_WIN = 256
SLOT_ALIGN = 16
SLOT_WIN = TOK_WIN + SLOT_ALIGN


def _cparams(*sem):
    return pltpu.CompilerParams(dimension_semantics=sem, vmem_limit_bytes=VMEM_LIMIT)


def _rms(x, g):
    ms = jnp.mean(x * x, axis=-1, keepdims=True)
    return x * lax.rsqrt(ms + NORM_EPS) * g


def _sigmoid(x):
    return 1.0 / (1.0 + jnp.exp(-x))


def _norm_inproj_kernel(h_ref, g_ref, w_ref, o_ref):
    xn = _rms(h_ref[...], g_ref[...]).astype(BF16)
    o_ref[...] = jnp.dot(xn, w_ref[...], preferred_element_type=F32)


def _norm_inproj(h, g, w):
    t, d = h.shape
    n = w.shape[1]
    tm = min(ROW_TILE, t)
    return pl.pallas_call(
        _norm_inproj_kernel,
        grid=(t // tm,),
        in_specs=[pl.BlockSpec((tm, d), lambda i: (i, 0)),
                  pl.BlockSpec((1, d), lambda i: (0, 0)),
                  pl.BlockSpec((d, n), lambda i: (0, 0))],
        out_specs=pl.BlockSpec((tm, n), lambda i: (i, 0)),
        out_shape=jax.ShapeDtypeStruct((t, n), F32),
        compiler_params=_cparams("parallel"),
        name="norm_inproj",
    )(h, g.reshape(1, d), w)


def _s5_operators(lam_re, lam_im, log_dt, b_re, b_im, c_re, c_im, n_scan):
    lc = S5_CHUNK
    hi = lax.Precision.HIGHEST
    lr, li = lam_re.astype(F32), lam_im.astype(F32)
    dt = jnp.exp(log_dt.astype(F32))[:, None]
    zr, zi = lr * dt, li * dt
    taus = jnp.arange(lc + 1, dtype=F32)[:, None, None]
    mag = jnp.exp(zr[None] * taus)
    pwr, pwi = mag * jnp.cos(zi[None] * taus), mag * jnp.sin(zi[None] * taus)
    nr, ni = pwr[1] - 1.0, pwi[1]
    den = lr * lr + li * li
    qr, qi = (nr * lr + ni * li) / den, (ni * lr - nr * li) / den
    br, bi = b_re.astype(F32), b_im.astype(F32)
    bbr = qr[..., None] * br - qi[..., None] * bi
    bbi = qr[..., None] * bi + qi[..., None] * br
    cr, ci = c_re.astype(F32), c_im.astype(F32)
    g_, p_ = lr.shape

    def c_times_pw(lo):
        wr, wi = pwr[lo:lo + lc, :, None, :], pwi[lo:lo + lc, :, None, :]
        return cr[None] * wr - ci[None] * wi, cr[None] * wi + ci[None] * wr

    cpr, cpi = c_times_pw(0)
    kt = jnp.sum(cpr[..., None] * bbr[None, :, None] - cpi[..., None] * bbi[None, :, None], axis=3)
    kflat = kt.transpose(1, 3, 0, 2).reshape(g_, S5_GROUP, lc * S5_GROUP)

    wr, wi = pwr[lc - 1::-1][:, :, :, None], pwi[lc - 1::-1][:, :, :, None]
    msr = (wr * bbr[None] - wi * bbi[None]).transpose(1, 0, 3, 2).reshape(g_, lc * S5_GROUP, p_)
    msi = (wr * bbi[None] + wi * bbr[None]).transpose(1, 0, 3, 2).reshape(g_, lc * S5_GROUP, p_)
    m_state = jnp.concatenate([msr, msi], axis=-1)

    c1r, c1i = c_times_pw(1)
    c1r = c1r.transpose(1, 3, 0, 2).reshape(g_, p_, lc * S5_GROUP)
    c1i = c1i.transpose(1, 3, 0, 2).reshape(g_, p_, lc * S5_GROUP)
    m_carry = jnp.concatenate([c1r, -c1i], axis=1)

    akr, aki = [pwr[lc]], [pwi[lc]]
    for _ in range(n_scan - 1):
        r, i = akr[-1], aki[-1]
        akr.append(r * r - i * i)
        aki.append(2.0 * r * i)
    akr, aki = jnp.stack(akr, axis=1), jnp.stack(aki, axis=1)
    ar = jnp.concatenate([akr, akr], axis=-1)
    ai = jnp.concatenate([-aki, aki], axis=-1)
    kpad = -(-n_scan // 8) * 8
    ar = jnp.pad(ar, ((0, 0), (0, kpad - n_scan), (0, 0)))
    ai = jnp.pad(ai, ((0, 0), (0, kpad - n_scan), (0, 0)))
    return kflat, m_state.astype(BF16), m_carry.astype(BF16), ar, ai


S5_PER_TILE = LANES // S5_GROUP


def _s5_kernel(u_ref, kf_ref, ms_ref, mc_ref, ar_ref, ai_ref, y_ref, mi_ref, *, n_scan):
    lc, gw, per = S5_CHUNK, S5_GROUP, S5_PER_TILE
    nch = u_ref.shape[0] // lc
    y_ref[...] = jnp.zeros_like(y_ref)
    lane_grp = lax.broadcasted_iota(I32, (1, LANES), 1) // gw
    kf_lane = lax.broadcasted_iota(I32, (gw, lc * gw), 1)

    def step_rows(s):
        return pl.ds(s, nch, stride=lc)

    def group(gl, carry):
        tiles = []
        for j in range(lc // per):
            tile = jnp.zeros((nch, LANES), F32)
            for k in range(per):
                moved = pltpu.roll(u_ref[step_rows(j * per + k), :], jnp.bitwise_and((k - gl) * gw, LANES - 1), axis=1)
                tile = jnp.where(lane_grp == k, moved, tile)
            tiles.append(tile)
        u = jnp.concatenate(tiles, axis=1).astype(BF16)

        kf = kf_ref[gl]
        for s in range(lc):
            blk = kf if s == 0 else jnp.where(kf_lane >= s * gw, pltpu.roll(kf, s * gw, axis=1), 0.0)
            mi_ref[s * gw:(s + 1) * gw, :] = blk.astype(BF16)

        x = jnp.dot(u, ms_ref[gl], preferred_element_type=F32)
        row = lax.broadcasted_iota(I32, x.shape, 0)
        half = x.shape[1] // 2
        for k in range(n_scan):
            d = 1 << k
            s = jnp.where(row >= d, pltpu.roll(x, d, axis=0), 0.0)
            x = x + ar_ref[gl, k:k + 1, :] * s + ai_ref[gl, k:k + 1, :] * pltpu.roll(s, half, axis=1)
        xe = jnp.where(row >= 1, pltpu.roll(x, 1, axis=0), 0.0)
        y = jnp.dot(u, mi_ref[...], preferred_element_type=F32)
        y = y + jnp.dot(xe.astype(BF16), mc_ref[gl], preferred_element_type=F32)

        for t in range(lc):
            j, k = divmod(t, per)
            moved = pltpu.roll(y[:, j * LANES:(j + 1) * LANES], jnp.bitwise_and((gl - k) * gw, LANES - 1), axis=1)
            y_ref[step_rows(t), :] = jnp.where(lane_grp == gl, moved, y_ref[step_rows(t), :])
        return carry

    lax.fori_loop(0, per, group, 0)


def _s5_conv(proj, ops):
    kflat, m_state, m_carry, ar, ai = ops
    t = proj.shape[0]
    lc = S5_CHUNK
    nch = t // lc
    n_scan = int(math.log2(nch))
    assert (1 << n_scan) == nch
    w = lc * S5_GROUP
    p2 = m_state.shape[-1]
    per = S5_PER_TILE

    def grp(shape):
        return pl.BlockSpec((per,) + shape, lambda hh: (hh, 0, 0))

    return pl.pallas_call(
        functools.partial(_s5_kernel, n_scan=n_scan),
        grid=(S5_WIDTH // LANES,),
        in_specs=[pl.BlockSpec((t, LANES), lambda hh: (0, hh)),
                  grp((S5_GROUP, w)), grp((w, p2)), grp((p2, w)),
                  grp((ar.shape[1], p2)), grp((ai.shape[1], p2))],
        out_specs=pl.BlockSpec((t, LANES), lambda hh: (0, hh)),
        out_shape=jax.ShapeDtypeStruct((t, S5_WIDTH), F32),
        scratch_shapes=[pltpu.VMEM((w, w), BF16)],
        compiler_params=_cparams("parallel"),
        name="s5_conv",
    )(proj, kflat, m_state, m_carry, ar, ai)


LOG2E = float(np.log2(np.e))


def _hg_level_table():
    idx = np.arange(HG_CHUNK)
    t, s = idx[:, None], idx[None, :]
    top_bit = np.floor(np.log2(np.maximum(t ^ s, 1))).astype(np.int32)
    return np.where(s < t, HG_LEVELS - 1 - top_bit, -1).astype(np.int32)


def _hg_midpoint(bc, m):
    c = bc.shape[0]
    h = m // 2
    if h >= SUBLANES:
        return jnp.concatenate([jnp.broadcast_to(bc[j * m + h - 1:j * m + h, :], (m, LANES))
                                for j in range(c // m)], axis=0)
    x3 = bc.reshape(c // SUBLANES, SUBLANES, LANES)
    sub = lax.broadcasted_iota(I32, x3.shape, 1)
    beta = None
    for j in reversed(range(SUBLANES // m)):
        row = jnp.broadcast_to(x3[:, j * m + h - 1:j * m + h, :], x3.shape)
        beta = row if beta is None else jnp.where(sub < (j + 1) * m, row, beta)
    return beta.reshape(c, LANES)


def _hg_chunk(fp, q, v, gt, lb, lbf, ng, lvl, st):
    c = HG_CHUNK
    prow = lax.broadcasted_iota(I32, (c, LANES), 0)
    nt = (((1,), (1,)), ((), ()))
    tn = (((0,), (0,)), ((), ()))
    en = jnp.exp(-jnp.abs(fp))
    rc = 1.0 / (1.0 + en)
    pos_f = fp >= 0.0
    sig_p = jnp.where(pos_f, rc, en * rc)
    sig_n = jnp.where(pos_f, en * rc, rc)
    f = lbf + (1.0 - lb) * sig_p
    lf = jnp.log(f)
    kc = (1.0 - lb) * sig_n
    qc = q * _sigmoid(q)
    bc = lf
    for k in range(HG_LEVELS):
        d = 1 << k
        bc = bc + jnp.where(prow >= d, pltpu.roll(bc, d, axis=0), 0.0)

    a = jnp.zeros((c, c), F32)
    for lev in range(HG_LEVELS):
        m = c >> lev
        upper = jnp.bitwise_and(prow, m - 1) >= m // 2
        z = jnp.where(upper, qc, kc)
        if m == 2:
            zw = jnp.where(upper, z * f, z)
        else:
            dlt = bc - _hg_midpoint(bc, m)
            zw = z * jnp.exp2(dlt * jnp.where(upper, LOG2E, -LOG2E))
        zw = zw.astype(BF16)
        s = lax.dot_general(zw, zw, nt, preferred_element_type=F32)
        a = jnp.where(lvl == lev, s, a)
    vb = v.astype(BF16)
    o = jnp.dot(a.astype(BF16), vb, preferred_element_type=F32)
    o = o + jnp.sum(qc * kc, axis=-1, keepdims=True) * v
    o = o + lax.dot_general((qc * jnp.exp(bc)).astype(BF16), st.astype(BF16), nt,
                            preferred_element_type=F32)
    bl = bc[c - 1:c, :]
    khat = (kc * jnp.exp(bl - bc)).astype(BF16)
    st = st * jnp.exp(bl) + lax.dot_general(vb, khat, tn, preferred_element_type=F32)
    return _rms(o, ng) * (gt * _sigmoid(gt)), st


def _hgrn_kernel(q_ref, f_ref, i_ref, gt_ref, lb_ref, lbf_ref, ng_ref, lvl_ref, o_ref, st_ref):
    c = HG_CHUNK

    @pl.when(pl.program_id(0) == 0)
    def _():
        st_ref[...] = jnp.zeros_like(st_ref)

    def chunk(n, carry):
        rows = pl.ds(pl.multiple_of(n * c, c), c)
        for hd in range(HG_HEADS):
            cols = slice(hd * LANES, (hd + 1) * LANES)
            o, st = _hg_chunk(f_ref[rows, cols], q_ref[rows, cols], i_ref[rows, cols], gt_ref[rows, cols],
                              lb_ref[:, cols], lbf_ref[:, cols], ng_ref[:, cols], lvl_ref[...], st_ref[hd])
            o_ref[rows, cols] = o
            st_ref[hd] = st
        return carry

    lax.fori_loop(0, q_ref.shape[0] // c, chunk, 0)


def _hgrn(proj, lb, norm_g):
    t = proj.shape[0]
    tb = min(HG_TBLOCK, t)
    lbh = jnp.clip(lb.astype(F32), 0.0, 1.0 - 1e-6).reshape(1, HG_WIDTH)
    lbf = jnp.maximum(lbh, LB_FLOOR)
    ng = norm_g.astype(F32).reshape(1, HG_WIDTH)
    lvl = jnp.asarray(_hg_level_table())
    cb = COL_HG // HG_WIDTH

    def col(k):
        return pl.BlockSpec((tb, HG_WIDTH), lambda i, k=k: (i, cb + k))

    vec = pl.BlockSpec((1, HG_WIDTH), lambda i: (0, 0))
    return pl.pallas_call(
        _hgrn_kernel,
        grid=(t // tb,),
        in_specs=[col(0), col(1), col(2), col(3), vec, vec, vec,
                  pl.BlockSpec(lvl.shape, lambda i: (0, 0))],
        out_specs=pl.BlockSpec((tb, HG_WIDTH), lambda i: (i, 0)),
        out_shape=jax.ShapeDtypeStruct((t, HG_WIDTH), F32),
        scratch_shapes=[pltpu.VMEM((HG_HEADS, LANES, LANES), F32)],
        compiler_params=_cparams("arbitrary"),
        name="hgrn2",
    )(proj, proj, proj, proj, lbh, lbf, ng, lvl)


def _mix_out_kernel(ys_ref, u_ref, cb_ref, cc_ref, cv_ref, hg_ref, h_ref,
                    d_ref, glu_ref, sn_ref, cw_ref, cn_ref, wo_ref, o_ref, carry_ref):
    @pl.when(pl.program_id(0) == 0)
    def _():
        carry_ref[...] = jnp.zeros_like(carry_ref)

    y = ys_ref[...] + d_ref[...] * u_ref[...]
    y = jax.nn.gelu(y)
    y = y * _sigmoid(jnp.dot(y.astype(BF16), glu_ref[...], preferred_element_type=F32))
    y_s5 = _rms(y, sn_ref[...])

    z = cc_ref[...] * cv_ref[...]
    tm = z.shape[0]
    row = lax.broadcasted_iota(I32, z.shape, 0)
    p1 = carry_ref[7:8, :]
    p2 = carry_ref[6:7, :]
    z1 = jnp.where(row == 0, p1, pltpu.roll(z, 1, axis=0))
    z2 = jnp.where(row == 0, p2, jnp.where(row == 1, p1, pltpu.roll(z, 2, axis=0)))
    carry_ref[...] = z[tm - 8:tm, :]
    yc = cb_ref[...] * (z2 * cw_ref[0:1, :] + z1 * cw_ref[1:2, :] + z * cw_ref[2:3, :])
    y_cv = _rms(yc, cn_ref[...])

    acc = h_ref[...]
    hg_row = S5_WIDTH + CONV_WIDTH
    acc = acc + jnp.dot(y_s5.astype(BF16), wo_ref[0:S5_WIDTH, :], preferred_element_type=F32)
    acc = acc + jnp.dot(y_cv.astype(BF16), wo_ref[S5_WIDTH:hg_row, :], preferred_element_type=F32)
    acc = acc + jnp.dot(hg_ref[...].astype(BF16), wo_ref[hg_row:, :], preferred_element_type=F32)
    o_ref[...] = acc


def _mix_out(ys5, proj, yhg, h, s5_d, s5_glu, s5_norm, conv_w, conv_norm, w_out):
    t, d = h.shape
    tm = min(ROW_TILE, t)
    cw = jnp.pad(conv_w.astype(F32), ((0, 8 - CONV_K), (0, 0)))
    wq = S5_WIDTH

    def rowblk(width, colblk):
        return pl.BlockSpec((tm, width), lambda i, c=colblk: (i, c))

    def full(shape):
        return pl.BlockSpec(shape, lambda i: (0,) * len(shape))

    return pl.pallas_call(
        _mix_out_kernel,
        grid=(t // tm,),
        in_specs=[rowblk(wq, 0), rowblk(wq, 0), rowblk(wq, 1), rowblk(wq, 2), rowblk(wq, 3),
                  rowblk(HG_WIDTH, 0), rowblk(d, 0),
                  full((1, wq)), full((wq, wq)), full((1, wq)), full((8, wq)), full((1, wq)),
                  full(w_out.shape)],
        out_specs=rowblk(d, 0),
        out_shape=jax.ShapeDtypeStruct((t, d), F32),
        scratch_shapes=[pltpu.VMEM((8, CONV_WIDTH), F32)],
        compiler_params=_cparams("arbitrary"),
        name="mix_out",
    )(ys5, proj, proj, proj, proj, yhg, h,
      s5_d.astype(F32).reshape(1, wq), s5_glu, s5_norm.astype(F32).reshape(1, wq), cw,
      conv_norm.astype(F32).reshape(1, wq), w_out)


def _ffn_kernel(h_ref, g_ref, w1_ref, w3_ref, w2_ref, o_ref, hn_ref, acc_ref):
    j = pl.program_id(1)

    @pl.when(j == 0)
    def _():
        x = h_ref[...]
        hn_ref[...] = _rms(x, g_ref[...]).astype(BF16)
        acc_ref[...] = x

    hn = hn_ref[...]
    a = jnp.dot(hn, w1_ref[...].astype(BF16), preferred_element_type=F32)
    b = jnp.dot(hn, w3_ref[...].astype(BF16), preferred_element_type=F32)
    gact = (a * _sigmoid(a) * b).astype(BF16)
    acc_ref[...] += jnp.dot(gact, w2_ref[...].astype(BF16), preferred_element_type=F32)

    @pl.when(j == pl.num_programs(1) - 1)
    def _():
        o_ref[...] = acc_ref[...]


def _ffn(h, g, w1, w3, w2, layer):
    t, d = h.shape
    f = w1.shape[2]
    tm = min(FFN_ROW_TILE, t)
    tf = FFN_COL_TILE
    return pl.pallas_call(
        _ffn_kernel,
        grid=(t // tm, f // tf),
        in_specs=[pl.BlockSpec((tm, d), lambda i, j: (i, 0)),
                  pl.BlockSpec((1, d), lambda i, j: (0, 0)),
                  pl.BlockSpec((None, d, tf), lambda i, j: (layer, 0, j)),
                  pl.BlockSpec((None, d, tf), lambda i, j: (layer, 0, j)),
                  pl.BlockSpec((None, tf, d), lambda i, j: (layer, j, 0))],
        out_specs=pl.BlockSpec((tm, d), lambda i, j: (i, 0)),
        out_shape=jax.ShapeDtypeStruct((t, d), F32),
        scratch_shapes=[pltpu.VMEM((tm, d), BF16), pltpu.VMEM((tm, d), F32)],
        compiler_params=_cparams("parallel", "arbitrary"),
        name="ffn_swiglu",
    )(h, g.astype(F32).reshape(1, d), w1, w3, w2)


def _router_kernel(h_ref, g_ref, r_ref, hn_ref, info_ref, info_t_ref, before_ref, total_ref, cnt_ref, *, n_exp):
    @pl.when(pl.program_id(0) == 0)
    def _():
        cnt_ref[...] = jnp.zeros_like(cnt_ref)

    xn = _rms(h_ref[...], g_ref[...])
    hn_ref[...] = xn.astype(BF16)
    logits = jnp.dot(xn, r_ref[...], preferred_element_type=F32, precision=lax.Precision.HIGHEST)
    tm = logits.shape[0]
    lane = lax.broadcasted_iota(I32, logits.shape, 1)
    lanef = lane.astype(F32)
    neg = jnp.float32(-jnp.inf)
    lg = jnp.where(lane < n_exp, logits, neg)
    m1 = jnp.max(lg, axis=-1, keepdims=True)
    i1 = jnp.min(jnp.where(lg == m1, lanef, float(LANES)), axis=-1, keepdims=True)
    oh1 = lanef == i1
    lg2 = jnp.where(oh1, neg, lg)
    m2 = jnp.max(lg2, axis=-1, keepdims=True)
    i2 = jnp.min(jnp.where(lg2 == m2, lanef, float(LANES)), axis=-1, keepdims=True)
    oh2 = lanef == i2
    ex = jnp.exp(m2 - m1)
    g1 = 1.0 / (1.0 + ex)
    g2 = ex * g1
    chosen = jnp.where(oh1, 1.0, jnp.where(oh2, 1.0, 0.0))
    ri = lax.broadcasted_iota(I32, (tm, tm), 0)
    ci = lax.broadcasted_iota(I32, (tm, tm), 1)
    tri = jnp.where(ri > ci, 1.0, 0.0).astype(BF16)
    before = cnt_ref[...]
    cexcl = jnp.dot(tri, chosen.astype(BF16), preferred_element_type=F32) + before
    rank1 = jnp.sum(jnp.where(oh1, cexcl, 0.0), axis=-1, keepdims=True)
    rank2 = jnp.sum(jnp.where(oh2, cexcl, 0.0), axis=-1, keepdims=True)
    info = jnp.where(lane == 0, i1, jnp.where(lane == 1, i2, jnp.where(lane == 2, g1, jnp.where(
        lane == 3, g2, jnp.where(lane == 4, rank1, jnp.where(lane == 5, rank2, 0.0))))))
    info_ref[...] = info
    info_t_ref[...] = info.T[0:SUBLANES, :]
    before_ref[0] = jnp.broadcast_to(before, before_ref.shape[1:])
    total = before + jnp.sum(chosen, axis=0, keepdims=True)
    cnt_ref[...] = total
    total_ref[...] = jnp.broadcast_to(total, total_ref.shape)


def _router(h, g, router):
    t, d = h.shape
    n_exp = router.shape[1]
    tm = min(TOK_WIN, t)
    ntw = t // tm
    rp = jnp.pad(router.astype(F32), ((0, 0), (0, LANES - n_exp)))
    return pl.pallas_call(
        functools.partial(_router_kernel, n_exp=n_exp),
        grid=(ntw,),
        in_specs=[pl.BlockSpec((tm, d), lambda i: (i, 0)),
                  pl.BlockSpec((1, d), lambda i: (0, 0)),
                  pl.BlockSpec((d, LANES), lambda i: (0, 0))],
        out_specs=[pl.BlockSpec((tm, d), lambda i: (i, 0)),
                   pl.BlockSpec((tm, LANES), lambda i: (i, 0)),
                   pl.BlockSpec((SUBLANES, tm), lambda i: (0, i)),
                   pl.BlockSpec((1, 8, LANES), lambda i: (i, 0, 0)),
                   pl.BlockSpec((8, LANES), lambda i: (0, 0))],
        out_shape=[jax.ShapeDtypeStruct((t, d), BF16),
                   jax.ShapeDtypeStruct((t, LANES), F32),
                   jax.ShapeDtypeStruct((SUBLANES, t), F32),
                   jax.ShapeDtypeStruct((ntw, 8, LANES), F32),
                   jax.ShapeDtypeStruct((8, LANES), F32)],
        scratch_shapes=[pltpu.VMEM((1, LANES), F32)],
        compiler_params=_cparams("arbitrary"),
        name="moe_router",
    )(h, g.astype(F32).reshape(1, d), rp)


def _gather_kernel(ilo_ref, nw_ref, dest_ref, hn_hbm, xs_ref, buf_ref, sem, acc_ref, par_ref):
    b = pl.program_id(0)
    last_b = pl.num_programs(0) - 1
    rows = acc_ref.shape[0]
    win = buf_ref.shape[1]
    n = nw_ref[b]
    lo = ilo_ref[b]
    nxt = jnp.minimum(b + 1, last_b)
    next_lo = ilo_ref[nxt]
    next_work = jnp.logical_and(b < last_b, nw_ref[nxt] > 0)

    part = win // GATHER_DMA_SPLIT

    class copy:
        def __init__(self, w, s):
            self.parts = [pltpu.make_async_copy(
                hn_hbm.at[pl.ds(pl.multiple_of(w * win + q * part, part), part), :],
                buf_ref.at[s, q * part:(q + 1) * part, :], sem.at[s, q]) for q in range(GATHER_DMA_SPLIT)]

        def start(self):
            for p in self.parts:
                p.start()

        def wait(self):
            for p in self.parts:
                p.wait()

    @pl.when(b == 0)
    def _():
        par_ref[0] = 0

        @pl.when(n > 0)
        def _():
            copy(lo, 0).start()

    p0 = par_ref[0]
    acc_ref[...] = jnp.zeros_like(acc_ref)
    slot = b * rows + lax.broadcasted_iota(I32, (rows, win), 0)

    def body(j, carry):
        s = jnp.bitwise_and(p0 + j, 1)
        is_last = j + 1 == n
        w_next = jnp.where(is_last, next_lo, lo + j + 1)

        @pl.when(jnp.logical_or(jnp.logical_not(is_last), next_work))
        def _():
            copy(w_next, 1 - s).start()

        col = pl.ds(pl.multiple_of((lo + j) * win, win), win)
        d1 = dest_ref[0:1, col]
        d2 = dest_ref[1:2, col]
        hit = jnp.where(d1 == slot, 1.0, jnp.where(d2 == slot, 1.0, 0.0)).astype(BF16)
        copy(lo + j, s).wait()
        acc_ref[...] += jnp.dot(hit, buf_ref[s], preferred_element_type=F32)
        return carry

    lax.fori_loop(0, n, body, 0)
    xs_ref[...] = acc_ref[...].astype(BF16)

    @pl.when(jnp.logical_and(n == 0, next_work))
    def _():
        copy(next_lo, p0).start()

    par_ref[0] = jnp.bitwise_and(p0 + n, 1)


def _gather(hn, dest, ilo, nw, n_slots):
    t, d = hn.shape
    return pl.pallas_call(
        _gather_kernel,
        grid_spec=pltpu.PrefetchScalarGridSpec(
            num_scalar_prefetch=2,
            grid=(n_slots // GATHER_TILE,),
            in_specs=[pl.BlockSpec(dest.shape, lambda b, lo, nw: (0, 0)),
                      pl.BlockSpec(memory_space=pl.ANY)],
            out_specs=pl.BlockSpec((GATHER_TILE, d), lambda b, lo, nw: (b, 0)),
            scratch_shapes=[pltpu.VMEM((2, TOK_WIN, d), BF16), pltpu.SemaphoreType.DMA((2, GATHER_DMA_SPLIT)),
                            pltpu.VMEM((GATHER_TILE, d), F32), pltpu.SMEM((1,), I32)]),
        out_shape=jax.ShapeDtypeStruct((n_slots, d), BF16),
        compiler_params=_cparams("arbitrary"),
        name="moe_gather",
    )(ilo, nw, dest, hn)


def _expert_kernel(te_ref, tv_ref, x_ref, w1_ref, w3_ref, w2_ref, y_ref, acc_ref):
    b = pl.program_id(0)
    j = pl.program_id(1)

    @pl.when(j == 0)
    def _():
        acc_ref[...] = jnp.zeros_like(acc_ref)

    @pl.when(tv_ref[b] == 1)
    def _():
        x = x_ref[...]
        a = jnp.dot(x, w1_ref[...].astype(BF16), preferred_element_type=F32)
        c = jnp.dot(x, w3_ref[...].astype(BF16), preferred_element_type=F32)
        gact = (a * _sigmoid(a) * c).astype(BF16)
        acc_ref[...] += jnp.dot(gact, w2_ref[...].astype(BF16), preferred_element_type=F32)

    @pl.when(j == pl.num_programs(1) - 1)
    def _():
        y_ref[...] = acc_ref[...].astype(BF16)


def _experts(xs, tile_e, tile_v, w1, w3, w2, layer):
    ns, d = xs.shape
    f = w1.shape[3]
    tf = MOE_COL_TILE
    nj = f // tf
    n_tiles = ns // MOE_TILE

    def col(b, j, tv):
        return j * tv[b] + (nj - 1) * (1 - tv[b])

    return pl.pallas_call(
        _expert_kernel,
        grid_spec=pltpu.PrefetchScalarGridSpec(
            num_scalar_prefetch=2,
            grid=(n_tiles, nj),
            in_specs=[pl.BlockSpec((MOE_TILE, d), lambda b, j, te, tv: (b, 0)),
                      pl.BlockSpec((None, None, d, tf), lambda b, j, te, tv: (layer, te[b], 0, col(b, j, tv))),
                      pl.BlockSpec((None, None, d, tf), lambda b, j, te, tv: (layer, te[b], 0, col(b, j, tv))),
                      pl.BlockSpec((None, None, tf, d), lambda b, j, te, tv: (layer, te[b], col(b, j, tv), 0))],
            out_specs=pl.BlockSpec((MOE_TILE, d), lambda b, j, te, tv: (b, 0)),
            scratch_shapes=[pltpu.VMEM((MOE_TILE, d), F32)]),
        out_shape=jax.ShapeDtypeStruct((ns, d), BF16),
        compiler_params=_cparams("arbitrary", "arbitrary"),
        name="moe_experts",
    )(tile_e, tile_v, xs, w1, w3, w2)


def _combine_kernel(ws_ref, po_ref, ex_ref, info_ref, h_ref, fin_ref, yb_hbm, o_ref, buf_ref, sem, *,
                    n_exp, final_norm):
    i = pl.program_id(0)
    cur = lax.rem(i, 2)

    def copy(step, e, s):
        start = pl.multiple_of(ws_ref[step * n_exp + e], SLOT_ALIGN)
        return pltpu.make_async_copy(yb_hbm.at[pl.ds(start, SLOT_WIN), :], buf_ref.at[s, e], sem.at[s, e])

    @pl.when(i == 0)
    def _():
        for e in range(n_exp):
            copy(0, e, 0).start()

    @pl.when(i + 1 < pl.num_programs(0))
    def _():
        for e in range(n_exp):
            copy(i + 1, e, 1 - cur).start()

    info = info_ref[...]
    tm = info.shape[0]
    e1, e2 = info[:, 0:1], info[:, 1:2]
    g1, g2 = info[:, 2:3], info[:, 3:4]
    r1, r2 = info[:, 4:5], info[:, 5:6]
    lanef = lax.broadcasted_iota(I32, (tm, TOK_WIN), 1).astype(F32)
    g1b = jnp.broadcast_to(g1, (tm, TOK_WIN))
    g2b = jnp.broadcast_to(g2, (tm, TOK_WIN))

    def weights(e, shift, lo_lane):
        off = (ws_ref[i * n_exp + e] - po_ref[e] + shift).astype(F32)
        k1 = jnp.where(e1 == float(e), r1 - off, -1.0)
        k2 = jnp.where(e2 == float(e), r2 - off, -1.0)
        k1 = jnp.where(k1 >= float(lo_lane), k1, -1.0)
        k2 = jnp.where(k2 >= float(lo_lane), k2, -1.0)
        return jnp.where(k1 == lanef, g1b, jnp.where(k2 == lanef, g2b, 0.0)).astype(BF16)

    acc = h_ref[...]
    for e in range(n_exp):
        pt = weights(e, 0, 0)
        copy(i, e, cur).wait()
        acc = acc + jnp.dot(pt, buf_ref[cur, e, 0:TOK_WIN, :], preferred_element_type=F32)
    o_ref[...] = acc

    tail = SLOT_WIN - TOK_WIN
    for e in range(n_exp):
        @pl.when(ex_ref[i * n_exp + e] == 1)
        def _(e=e):
            pt = weights(e, tail, TOK_WIN - tail)
            o_ref[...] += jnp.dot(pt, buf_ref[cur, e, tail:SLOT_WIN, :], preferred_element_type=F32)

    if final_norm:
        o_ref[...] = _rms(o_ref[...], fin_ref[...])


def _combine(info, h, yb, win_start, pstart, extra, fin_g, final_norm):
    t, d = h.shape
    n_exp = pstart.shape[0]
    tm = min(TOK_WIN, t)
    return pl.pallas_call(
        functools.partial(_combine_kernel, n_exp=n_exp, final_norm=final_norm),
        grid_spec=pltpu.PrefetchScalarGridSpec(
            num_scalar_prefetch=3,
            grid=(t // tm,),
            in_specs=[pl.BlockSpec((tm, LANES), lambda i, ws, po, ex: (i, 0)),
                      pl.BlockSpec((tm, d), lambda i, ws, po, ex: (i, 0)),
                      pl.BlockSpec((1, d), lambda i, ws, po, ex: (0, 0)),
                      pl.BlockSpec(memory_space=pl.ANY)],
            out_specs=pl.BlockSpec((tm, d), lambda i, ws, po, ex: (i, 0)),
            scratch_shapes=[pltpu.VMEM((2, n_exp, SLOT_WIN, d), BF16),
                            pltpu.SemaphoreType.DMA((2, n_exp))]),
        out_shape=jax.ShapeDtypeStruct((t, d), F32),
        compiler_params=_cparams("arbitrary"),
        name="moe_combine",
    )(win_start, pstart, extra, info, h, fin_g.astype(F32).reshape(1, d), yb)


def _count_le(sorted_vals, x):
    return jnp.sum((sorted_vals[None, :] <= x[:, None]).astype(I32), axis=1)


def _moe(h, g, router, w1, w3, w2, layer, fin_g, final_norm):
    t, d = h.shape
    n_exp = router.shape[1]
    tw = min(TOK_WIN, t)
    ntw = t // tw
    hn, info, info_t, before, total = _router(h, g, router)

    counts = total[0, :n_exp].astype(I32)
    padded = (counts + MOE_TILE - 1) // MOE_TILE * MOE_TILE
    pend = jnp.cumsum(padded)
    pstart = pend - padded
    n_tiles = (t * TOP_K) // MOE_TILE + n_exp
    n_slots = n_tiles * MOE_TILE
    tile0 = jnp.arange(n_tiles, dtype=I32) * MOE_TILE
    tile_e = jnp.minimum(_count_le(pend, tile0), n_exp - 1)
    tile_v = (tile0 < pend[-1]).astype(I32)
    cum = jnp.concatenate([before[:, 0, :n_exp], total[0:1, :n_exp]], axis=0).astype(I32)

    n_gb = n_slots // GATHER_TILE
    gb0 = jnp.arange(n_gb, dtype=I32) * GATHER_TILE
    gb_e = jnp.minimum(_count_le(pend, gb0), n_exp - 1)
    r0 = gb0 - pstart[gb_e]
    cum_b = cum[:, gb_e]
    ilo = jnp.sum((cum_b[1:] <= r0[None, :]).astype(I32), axis=0)
    ihi = jnp.sum((cum_b[:-1] < (r0 + GATHER_TILE)[None, :]).astype(I32), axis=0) - 1
    n_work = jnp.where(gb0 < pend[-1], jnp.maximum(ihi - ilo + 1, 0), 0).astype(I32)

    e_t = info_t[0:2].astype(I32)
    slot_t = jnp.sum(jnp.where(e_t[None] == jnp.arange(n_exp, dtype=I32)[:, None, None], pstart[:, None, None], 0),
                     axis=0) + info_t[4:6].astype(I32)
    dest = jnp.concatenate([slot_t, jnp.full((SUBLANES - TOP_K, t), -1, I32)], axis=0)

    xs = _gather(hn, dest, ilo.astype(I32), n_work, n_slots)
    yb = _experts(xs, tile_e, tile_v, w1, w3, w2, layer)

    lo = pstart[None, :] + cum[:-1]
    win_start = jnp.minimum(lo // SLOT_ALIGN * SLOT_ALIGN, n_slots - SLOT_WIN).astype(I32)
    extra = ((lo - win_start + cum[1:] - cum[:-1]) > TOK_WIN).astype(I32)
    return _combine(info, h, yb, win_start.reshape(-1), pstart.astype(I32), extra.reshape(-1), fin_g, final_norm)


def _final_norm_kernel(h_ref, g_ref, o_ref):
    o_ref[...] = _rms(h_ref[...], g_ref[...])


def _final_norm(h, g):
    t, d = h.shape
    tm = min(ROW_TILE, t)
    return pl.pallas_call(
        _final_norm_kernel,
        grid=(t // tm,),
        in_specs=[pl.BlockSpec((tm, d), lambda i: (i, 0)), pl.BlockSpec((1, d), lambda i: (0, 0))],
        out_specs=pl.BlockSpec((tm, d), lambda i: (i, 0)),
        out_shape=jax.ShapeDtypeStruct((t, d), F32),
        compiler_params=_cparams("parallel"),
        name="final_norm",
    )(h, g.astype(F32).reshape(1, d))


def kernel(x, attn_norm, ffn_norm, final_norm, w_in, w_out, s5_lambda_re, s5_lambda_im, s5_log_dt, s5_b_re, s5_b_im, s5_c_re, s5_c_im, s5_d, s5_glu, s5_out_norm, conv_w, conv_out_norm, hg_lower_bounds, hg_out_norm, ffn_w1, ffn_w3, ffn_w2, moe_router, moe_w1, moe_w3, moe_w2):
    bsz, seq, d = x.shape
    depth = w_in.shape[0]
    t = bsz * seq
    assert bsz == 1, "token mixers are written for a single sequence"
    lb_soft = jax.nn.softmax(hg_lower_bounds.astype(F32), axis=0)
    lb_all = jnp.cumsum(lb_soft, axis=0) - lb_soft[0]
    n_scan = int(math.log2(t // S5_CHUNK))
    s5_ops = jax.vmap(functools.partial(_s5_operators, n_scan=n_scan))(
        s5_lambda_re, s5_lambda_im, s5_log_dt, s5_b_re, s5_b_im, s5_c_re, s5_c_im)
    h = x.reshape(t, d).astype(F32)
    for l in range(depth):
        proj = _norm_inproj(h, attn_norm[l].astype(F32), w_in[l].astype(BF16))
        ys5 = _s5_conv(proj, [op[l] for op in s5_ops])
        yhg = _hgrn(proj, lb_all[l], hg_out_norm[l])
        h = _mix_out(ys5, proj, yhg, h, s5_d[l], s5_glu[l].astype(BF16), s5_out_norm[l],
                     conv_w[l], conv_out_norm[l], w_out[l].astype(BF16))
        j = l // 2
        if l % 2 == 0:
            h = _ffn(h, ffn_norm[l], ffn_w1, ffn_w3, ffn_w2, j)
        else:
            h = _moe(h, ffn_norm[l], moe_router[j], moe_w1, moe_w3, moe_w2, j, final_norm, l == depth - 1)
    if depth % 2 == 1:
        h = _final_norm(h, final_norm)
    return h.reshape(bsz, seq, d)
```

```python
import functools
import math

import numpy as np
import jax
import jax.numpy as jnp
from jax import lax
from jax.experimental import pallas as pl
from jax.experimental.pallas import tpu as pltpu

F32 = jnp.float32
BF16 = jnp.bfloat16
I32 = jnp.int32

NORM_EPS = 1e-6
LB_FLOOR = 1e-30
TOP_K = 2

S5_WIDTH = 256
S5_GROUP = 16
S5_STATE = 64
CONV_WIDTH = 256
CONV_K = 3
HG_WIDTH = 512
HG_HEAD_DIM = 128
HG_HEADS = HG_WIDTH // HG_HEAD_DIM
LANES = 128
SUBLANES = 8
COL_CONV = S5_WIDTH
COL_HG = S5_WIDTH + 3 * CONV_WIDTH

VMEM_LIMIT = 56 * 1024 * 1024
S5_CHUNK = 32
HG_CHUNK = 128
HG_LEVELS = int(math.log2(HG_CHUNK))
HG_TBLOCK = 512
ROW_TILE = 512
FFN_ROW_TILE = 1024
FFN_COL_TILE = 512
MOE_TILE = 1024
MOE_COL_TILE = 512
GATHER_TILE = 256
GATHER_AHEAD = 4
TOK_WIN = 256
SLOT_ALIGN = 16
SLOT_WIN = TOK_WIN + SLOT_ALIGN


def _cparams(*sem):
    return pltpu.CompilerParams(dimension_semantics=sem, vmem_limit_bytes=VMEM_LIMIT)


def _rms(x, g):
    ms = jnp.mean(x * x, axis=-1, keepdims=True)
    return x * lax.rsqrt(ms + NORM_EPS) * g


def _sigmoid(x):
    return 1.0 / (1.0 + jnp.exp(-x))


def _norm_inproj_kernel(h_ref, g_ref, w_ref, o_ref):
    xn = _rms(h_ref[...], g_ref[...]).astype(BF16)
    o_ref[...] = jnp.dot(xn, w_ref[...], preferred_element_type=F32)


def _norm_inproj(h, g, w):
    t, d = h.shape
    n = w.shape[1]
    tm = min(ROW_TILE, t)
    return pl.pallas_call(
        _norm_inproj_kernel,
        grid=(t // tm,),
        in_specs=[pl.BlockSpec((tm, d), lambda i: (i, 0)),
                  pl.BlockSpec((1, d), lambda i: (0, 0)),
                  pl.BlockSpec((d, n), lambda i: (0, 0))],
        out_specs=pl.BlockSpec((tm, n), lambda i: (i, 0)),
        out_shape=jax.ShapeDtypeStruct((t, n), F32),
        compiler_params=_cparams("parallel"),
        name="norm_inproj",
    )(h, g.reshape(1, d), w)


def _s5_operators(lam_re, lam_im, log_dt, b_re, b_im, c_re, c_im, n_scan):
    lc = S5_CHUNK
    hi = lax.Precision.HIGHEST
    lr, li = lam_re.astype(F32), lam_im.astype(F32)
    dt = jnp.exp(log_dt.astype(F32))[:, None]
    zr, zi = lr * dt, li * dt
    taus = jnp.arange(lc + 1, dtype=F32)[:, None, None]
    mag = jnp.exp(zr[None] * taus)
    pwr, pwi = mag * jnp.cos(zi[None] * taus), mag * jnp.sin(zi[None] * taus)
    nr, ni = pwr[1] - 1.0, pwi[1]
    den = lr * lr + li * li
    qr, qi = (nr * lr + ni * li) / den, (ni * lr - nr * li) / den
    br, bi = b_re.astype(F32), b_im.astype(F32)
    bbr = qr[..., None] * br - qi[..., None] * bi
    bbi = qr[..., None] * bi + qi[..., None] * br
    cr, ci = c_re.astype(F32), c_im.astype(F32)
    g_, p_ = lr.shape

    def c_times_pw(lo):
        wr, wi = pwr[lo:lo + lc, :, None, :], pwi[lo:lo + lc, :, None, :]
        return cr[None] * wr - ci[None] * wi, cr[None] * wi + ci[None] * wr

    cpr, cpi = c_times_pw(0)
    kt = jnp.sum(cpr[..., None] * bbr[None, :, None] - cpi[..., None] * bbi[None, :, None], axis=3)
    kflat = kt.transpose(1, 3, 0, 2).reshape(g_, S5_GROUP, lc * S5_GROUP)

    wr, wi = pwr[lc - 1::-1][:, :, :, None], pwi[lc - 1::-1][:, :, :, None]
    msr = (wr * bbr[None] - wi * bbi[None]).transpose(1, 0, 3, 2).reshape(g_, lc * S5_GROUP, p_)
    msi = (wr * bbi[None] + wi * bbr[None]).transpose(1, 0, 3, 2).reshape(g_, lc * S5_GROUP, p_)
    m_state = jnp.concatenate([msr, msi], axis=-1)

    c1r, c1i = c_times_pw(1)
    c1r = c1r.transpose(1, 3, 0, 2).reshape(g_, p_, lc * S5_GROUP)
    c1i = c1i.transpose(1, 3, 0, 2).reshape(g_, p_, lc * S5_GROUP)
    m_carry = jnp.concatenate([c1r, -c1i], axis=1)

    akr, aki = [pwr[lc]], [pwi[lc]]
    for _ in range(n_scan - 1):
        r, i = akr[-1], aki[-1]
        akr.append(r * r - i * i)
        aki.append(2.0 * r * i)
    akr, aki = jnp.stack(akr, axis=1), jnp.stack(aki, axis=1)
    ar = jnp.concatenate([akr, akr], axis=-1)
    ai = jnp.concatenate([-aki, aki], axis=-1)
    kpad = -(-n_scan // 8) * 8
    ar = jnp.pad(ar, ((0, 0), (0, kpad - n_scan), (0, 0)))
    ai = jnp.pad(ai, ((0, 0), (0, kpad - n_scan), (0, 0)))
    return kflat, m_state.astype(BF16), m_carry.astype(BF16), ar, ai


S5_PER_TILE = LANES // S5_GROUP


def _s5_kernel(proj_hbm, kf_ref, ms_ref, mc_ref, ar_ref, ai_ref, y_hbm, us_ref, ys_ref, mi_ref, sem_in, sem_out,
               *, n_scan):
    lc, gw, per = S5_CHUNK, S5_GROUP, S5_PER_TILE
    nch = us_ref.shape[1]
    cols = pl.ds(pl.multiple_of(pl.program_id(0) * LANES, LANES), LANES)

    def in_copy(s):
        return pltpu.make_async_copy(proj_hbm.at[:, s, cols], us_ref.at[s], sem_in.at[s])

    def out_copy(s):
        return pltpu.make_async_copy(ys_ref.at[s], y_hbm.at[:, s, cols], sem_out.at[s])

    for s in range(lc):
        in_copy(s).start()
    ys_ref[...] = jnp.zeros_like(ys_ref)
    for s in range(lc):
        in_copy(s).wait()

    lane_grp = lax.broadcasted_iota(I32, (1, LANES), 1) // gw
    kf_lane = lax.broadcasted_iota(I32, (gw, lc * gw), 1)

    def group(gl, carry):
        tiles = []
        for j in range(lc // per):
            tile = jnp.zeros((nch, LANES), F32)
            for k in range(per):
                moved = pltpu.roll(us_ref[j * per + k], jnp.bitwise_and((k - gl) * gw, LANES - 1), axis=1)
                tile = jnp.where(lane_grp == k, moved, tile)
            tiles.append(tile)
        u = jnp.concatenate(tiles, axis=1).astype(BF16)

        kf = kf_ref[gl]
        for s in range(lc):
            blk = kf if s == 0 else jnp.where(kf_lane >= s * gw, pltpu.roll(kf, s * gw, axis=1), 0.0)
            mi_ref[s * gw:(s + 1) * gw, :] = blk.astype(BF16)

        x = jnp.dot(u, ms_ref[gl], preferred_element_type=F32)
        row = lax.broadcasted_iota(I32, x.shape, 0)
        half = x.shape[1] // 2
        for k in range(n_scan):
            d = 1 << k
            s = jnp.where(row >= d, pltpu.roll(x, d, axis=0), 0.0)
            x = x + ar_ref[gl, k:k + 1, :] * s + ai_ref[gl, k:k + 1, :] * pltpu.roll(s, half, axis=1)
        xe = jnp.where(row >= 1, pltpu.roll(x, 1, axis=0), 0.0)
        y = jnp.dot(u, mi_ref[...], preferred_element_type=F32)
        y = y + jnp.dot(xe.astype(BF16), mc_ref[gl], preferred_element_type=F32)

        for t in range(lc):
            j, k = divmod(t, per)
            moved = pltpu.roll(y[:, j * LANES:(j + 1) * LANES], jnp.bitwise_and((gl - k) * gw, LANES - 1), axis=1)
            ys_ref[t] = jnp.where(lane_grp == gl, moved, ys_ref[t])
        return carry

    lax.fori_loop(0, per, group, 0)
    for s in range(lc):
        out_copy(s).start()
    for s in range(lc):
        out_copy(s).wait()


def _s5_conv(proj, ops):
    kflat, m_state, m_carry, ar, ai = ops
    t, n = proj.shape
    lc = S5_CHUNK
    nch = t // lc
    n_scan = int(math.log2(nch))
    assert (1 << n_scan) == nch
    w = lc * S5_GROUP
    p2 = m_state.shape[-1]
    per = S5_PER_TILE

    def grp(shape):
        return pl.BlockSpec((per,) + shape, lambda hh: (hh, 0, 0))

    hbm = pl.BlockSpec(memory_space=pl.ANY)
    y = pl.pallas_call(
        functools.partial(_s5_kernel, n_scan=n_scan),
        grid=(S5_WIDTH // LANES,),
        in_specs=[hbm, grp((S5_GROUP, w)), grp((w, p2)), grp((p2, w)),
                  grp((ar.shape[1], p2)), grp((ai.shape[1], p2))],
        out_specs=hbm,
        out_shape=jax.ShapeDtypeStruct((nch, lc, S5_WIDTH), F32),
        scratch_shapes=[pltpu.VMEM((lc, nch, LANES), F32), pltpu.VMEM((lc, nch, LANES), F32),
                        pltpu.VMEM((w, w), BF16), pltpu.SemaphoreType.DMA((lc,)), pltpu.SemaphoreType.DMA((lc,))],
        compiler_params=_cparams("arbitrary"),
        name="s5_conv",
    )(proj.reshape(nch, lc, n), kflat, m_state, m_carry, ar, ai)
    return y.reshape(t, S5_WIDTH)


LOG2E = float(np.log2(np.e))


def _hg_level_table():
    idx = np.arange(HG_CHUNK)
    t, s = idx[:, None], idx[None, :]
    top_bit = np.floor(np.log2(np.maximum(t ^ s, 1))).astype(np.int32)
    return np.where(s < t, HG_LEVELS - 1 - top_bit, -1).astype(np.int32)


def _hg_midpoint(bc, m):
    c = bc.shape[0]
    h = m // 2
    if h >= SUBLANES:
        return jnp.concatenate([jnp.broadcast_to(bc[j * m + h - 1:j * m + h, :], (m, LANES))
                                for j in range(c // m)], axis=0)
    x3 = bc.reshape(c // SUBLANES, SUBLANES, LANES)
    sub = lax.broadcasted_iota(I32, x3.shape, 1)
    beta = None
    for j in reversed(range(SUBLANES // m)):
        row = jnp.broadcast_to(x3[:, j * m + h - 1:j * m + h, :], x3.shape)
        beta = row if beta is None else jnp.where(sub < (j + 1) * m, row, beta)
    return beta.reshape(c, LANES)


def _hg_chunk(fp, q, v, gt, lb, lbf, ng, lvl, st):
    c = HG_CHUNK
    prow = lax.broadcasted_iota(I32, (c, LANES), 0)
    nt = (((1,), (1,)), ((), ()))
    tn = (((0,), (0,)), ((), ()))
    en = jnp.exp(-jnp.abs(fp))
    rc = 1.0 / (1.0 + en)
    pos_f = fp >= 0.0
    sig_p = jnp.where(pos_f, rc, en * rc)
    sig_n = jnp.where(pos_f, en * rc, rc)
    f = lbf + (1.0 - lb) * sig_p
    lf = jnp.log(f)
    kc = (1.0 - lb) * sig_n
    qc = q * _sigmoid(q)
    bc = lf
    for k in range(HG_LEVELS):
        d = 1 << k
        bc = bc + jnp.where(prow >= d, pltpu.roll(bc, d, axis=0), 0.0)

    a = jnp.zeros((c, c), F32)
    for lev in range(HG_LEVELS):
        m = c >> lev
        upper = jnp.bitwise_and(prow, m - 1) >= m // 2
        z = jnp.where(upper, qc, kc)
        if m == 2:
            zw = jnp.where(upper, z * f, z)
        else:
            dlt = bc - _hg_midpoint(bc, m)
            zw = z * jnp.exp2(dlt * jnp.where(upper, LOG2E, -LOG2E))
        zw = zw.astype(BF16)
        s = lax.dot_general(zw, zw, nt, preferred_element_type=F32)
        a = jnp.where(lvl == lev, s, a)
    vb = v.astype(BF16)
    o = jnp.dot(a.astype(BF16), vb, preferred_element_type=F32)
    o = o + jnp.sum(qc * kc, axis=-1, keepdims=True) * v
    o = o + lax.dot_general((qc * jnp.exp(bc)).astype(BF16), st.astype(BF16), nt,
                            preferred_element_type=F32)
    bl = bc[c - 1:c, :]
    khat = (kc * jnp.exp(bl - bc)).astype(BF16)
    st = st * jnp.exp(bl) + lax.dot_general(vb, khat, tn, preferred_element_type=F32)
    return _rms(o, ng) * (gt * _sigmoid(gt)), st


def _hgrn_kernel(q_ref, f_ref, i_ref, gt_ref, lb_ref, lbf_ref, ng_ref, lvl_ref, o_ref, st_ref):
    c = HG_CHUNK

    @pl.when(pl.program_id(0) == 0)
    def _():
        st_ref[...] = jnp.zeros_like(st_ref)

    def chunk(n, carry):
        rows = pl.ds(pl.multiple_of(n * c, c), c)
        for hd in range(HG_HEADS):
            cols = slice(hd * LANES, (hd + 1) * LANES)
            o, st = _hg_chunk(f_ref[rows, cols], q_ref[rows, cols], i_ref[rows, cols], gt_ref[rows, cols],
                              lb_ref[:, cols], lbf_ref[:, cols], ng_ref[:, cols], lvl_ref[...], st_ref[hd])
            o_ref[rows, cols] = o
            st_ref[hd] = st
        return carry

    lax.fori_loop(0, q_ref.shape[0] // c, chunk, 0)


def _hgrn(proj, lb, norm_g):
    t = proj.shape[0]
    tb = min(HG_TBLOCK, t)
    lbh = jnp.clip(lb.astype(F32), 0.0, 1.0 - 1e-6).reshape(1, HG_WIDTH)
    lbf = jnp.maximum(lbh, LB_FLOOR)
    ng = norm_g.astype(F32).reshape(1, HG_WIDTH)
    lvl = jnp.asarray(_hg_level_table())
    cb = COL_HG // HG_WIDTH

    def col(k):
        return pl.BlockSpec((tb, HG_WIDTH), lambda i, k=k: (i, cb + k))

    vec = pl.BlockSpec((1, HG_WIDTH), lambda i: (0, 0))
    return pl.pallas_call(
        _hgrn_kernel,
        grid=(t // tb,),
        in_specs=[col(0), col(1), col(2), col(3), vec, vec, vec,
                  pl.BlockSpec(lvl.shape, lambda i: (0, 0))],
        out_specs=pl.BlockSpec((tb, HG_WIDTH), lambda i: (i, 0)),
        out_shape=jax.ShapeDtypeStruct((t, HG_WIDTH), F32),
        scratch_shapes=[pltpu.VMEM((HG_HEADS, LANES, LANES), F32)],
        compiler_params=_cparams("arbitrary"),
        name="hgrn2",
    )(proj, proj, proj, proj, lbh, lbf, ng, lvl)


def _mix_out_kernel(ys_ref, u_ref, cb_ref, cc_ref, cv_ref, hg_ref, h_ref,
                    d_ref, glu_ref, sn_ref, cw_ref, cn_ref, wo_ref, o_ref, carry_ref):
    @pl.when(pl.program_id(0) == 0)
    def _():
        carry_ref[...] = jnp.zeros_like(carry_ref)

    y = ys_ref[...] + d_ref[...] * u_ref[...]
    y = jax.nn.gelu(y)
    y = y * _sigmoid(jnp.dot(y.astype(BF16), glu_ref[...], preferred_element_type=F32))
    y_s5 = _rms(y, sn_ref[...])

    z = cc_ref[...] * cv_ref[...]
    tm = z.shape[0]
    row = lax.broadcasted_iota(I32, z.shape, 0)
    p1 = carry_ref[7:8, :]
    p2 = carry_ref[6:7, :]
    z1 = jnp.where(row == 0, p1, pltpu.roll(z, 1, axis=0))
    z2 = jnp.where(row == 0, p2, jnp.where(row == 1, p1, pltpu.roll(z, 2, axis=0)))
    carry_ref[...] = z[tm - 8:tm, :]
    yc = cb_ref[...] * (z2 * cw_ref[0:1, :] + z1 * cw_ref[1:2, :] + z * cw_ref[2:3, :])
    y_cv = _rms(yc, cn_ref[...])

    acc = h_ref[...]
    hg_row = S5_WIDTH + CONV_WIDTH
    acc = acc + jnp.dot(y_s5.astype(BF16), wo_ref[0:S5_WIDTH, :], preferred_element_type=F32)
    acc = acc + jnp.dot(y_cv.astype(BF16), wo_ref[S5_WIDTH:hg_row, :], preferred_element_type=F32)
    acc = acc + jnp.dot(hg_ref[...].astype(BF16), wo_ref[hg_row:, :], preferred_element_type=F32)
    o_ref[...] = acc


def _mix_out(ys5, proj, yhg, h, s5_d, s5_glu, s5_norm, conv_w, conv_norm, w_out):
    t, d = h.shape
    tm = min(ROW_TILE, t)
    cw = jnp.pad(conv_w.astype(F32), ((0, 8 - CONV_K), (0, 0)))
    wq = S5_WIDTH

    def rowblk(width, colblk):
        return pl.BlockSpec((tm, width), lambda i, c=colblk: (i, c))

    def full(shape):
        return pl.BlockSpec(shape, lambda i: (0,) * len(shape))

    return pl.pallas_call(
        _mix_out_kernel,
        grid=(t // tm,),
        in_specs=[rowblk(wq, 0), rowblk(wq, 0), rowblk(wq, 1), rowblk(wq, 2), rowblk(wq, 3),
                  rowblk(HG_WIDTH, 0), rowblk(d, 0),
                  full((1, wq)), full((wq, wq)), full((1, wq)), full((8, wq)), full((1, wq)),
                  full(w_out.shape)],
        out_specs=rowblk(d, 0),
        out_shape=jax.ShapeDtypeStruct((t, d), F32),
        scratch_shapes=[pltpu.VMEM((8, CONV_WIDTH), F32)],
        compiler_params=_cparams("arbitrary"),
        name="mix_out",
    )(ys5, proj, proj, proj, proj, yhg, h,
      s5_d.astype(F32).reshape(1, wq), s5_glu, s5_norm.astype(F32).reshape(1, wq), cw,
      conv_norm.astype(F32).reshape(1, wq), w_out)


def _ffn_kernel(h_ref, g_ref, w1_ref, w3_ref, w2_ref, o_ref, hn_ref, acc_ref):
    j = pl.program_id(1)

    @pl.when(j == 0)
    def _():
        x = h_ref[...]
        hn_ref[...] = _rms(x, g_ref[...]).astype(BF16)
        acc_ref[...] = x

    hn = hn_ref[...]
    a = jnp.dot(hn, w1_ref[...].astype(BF16), preferred_element_type=F32)
    b = jnp.dot(hn, w3_ref[...].astype(BF16), preferred_element_type=F32)
    gact = (a * _sigmoid(a) * b).astype(BF16)
    acc_ref[...] += jnp.dot(gact, w2_ref[...].astype(BF16), preferred_element_type=F32)

    @pl.when(j == pl.num_programs(1) - 1)
    def _():
        o_ref[...] = acc_ref[...]


def _ffn(h, g, w1, w3, w2, layer):
    t, d = h.shape
    f = w1.shape[2]
    tm = min(FFN_ROW_TILE, t)
    tf = FFN_COL_TILE
    return pl.pallas_call(
        _ffn_kernel,
        grid=(t // tm, f // tf),
        in_specs=[pl.BlockSpec((tm, d), lambda i, j: (i, 0)),
                  pl.BlockSpec((1, d), lambda i, j: (0, 0)),
                  pl.BlockSpec((None, d, tf), lambda i, j: (layer, 0, j)),
                  pl.BlockSpec((None, d, tf), lambda i, j: (layer, 0, j)),
                  pl.BlockSpec((None, tf, d), lambda i, j: (layer, j, 0))],
        out_specs=pl.BlockSpec((tm, d), lambda i, j: (i, 0)),
        out_shape=jax.ShapeDtypeStruct((t, d), F32),
        scratch_shapes=[pltpu.VMEM((tm, d), BF16), pltpu.VMEM((tm, d), F32)],
        compiler_params=_cparams("parallel", "arbitrary"),
        name="ffn_swiglu",
    )(h, g.astype(F32).reshape(1, d), w1, w3, w2)


def _router_kernel(h_ref, g_ref, r_ref, hn_ref, info_ref, info_t_ref, before_ref, total_ref, cnt_ref, *, n_exp):
    @pl.when(pl.program_id(0) == 0)
    def _():
        cnt_ref[...] = jnp.zeros_like(cnt_ref)

    xn = _rms(h_ref[...], g_ref[...])
    hn_ref[...] = xn.astype(BF16)
    logits = jnp.dot(xn, r_ref[...], preferred_element_type=F32, precision=lax.Precision.HIGHEST)
    tm = logits.shape[0]
    lane = lax.broadcasted_iota(I32, logits.shape, 1)
    lanef = lane.astype(F32)
    neg = jnp.float32(-jnp.inf)
    lg = jnp.where(lane < n_exp, logits, neg)
    m1 = jnp.max(lg, axis=-1, keepdims=True)
    i1 = jnp.min(jnp.where(lg == m1, lanef, float(LANES)), axis=-1, keepdims=True)
    oh1 = lanef == i1
    lg2 = jnp.where(oh1, neg, lg)
    m2 = jnp.max(lg2, axis=-1, keepdims=True)
    i2 = jnp.min(jnp.where(lg2 == m2, lanef, float(LANES)), axis=-1, keepdims=True)
    oh2 = lanef == i2
    ex = jnp.exp(m2 - m1)
    g1 = 1.0 / (1.0 + ex)
    g2 = ex * g1
    chosen = jnp.where(oh1, 1.0, jnp.where(oh2, 1.0, 0.0))
    ri = lax.broadcasted_iota(I32, (tm, tm), 0)
    ci = lax.broadcasted_iota(I32, (tm, tm), 1)
    tri = jnp.where(ri > ci, 1.0, 0.0).astype(BF16)
    before = cnt_ref[...]
    cexcl = jnp.dot(tri, chosen.astype(BF16), preferred_element_type=F32) + before
    rank1 = jnp.sum(jnp.where(oh1, cexcl, 0.0), axis=-1, keepdims=True)
    rank2 = jnp.sum(jnp.where(oh2, cexcl, 0.0), axis=-1, keepdims=True)
    info = jnp.where(lane == 0, i1, jnp.where(lane == 1, i2, jnp.where(lane == 2, g1, jnp.where(
        lane == 3, g2, jnp.where(lane == 4, rank1, jnp.where(lane == 5, rank2, 0.0))))))
    info_ref[...] = info
    info_t_ref[...] = info.T[0:SUBLANES, :]
    before_ref[0] = jnp.broadcast_to(before, before_ref.shape[1:])
    total = before + jnp.sum(chosen, axis=0, keepdims=True)
    cnt_ref[...] = total
    total_ref[...] = jnp.broadcast_to(total, total_ref.shape)


def _router(h, g, router):
    t, d = h.shape
    n_exp = router.shape[1]
    tm = min(TOK_WIN, t)
    ntw = t // tm
    rp = jnp.pad(router.astype(F32), ((0, 0), (0, LANES - n_exp)))
    return pl.pallas_call(
        functools.partial(_router_kernel, n_exp=n_exp),
        grid=(ntw,),
        in_specs=[pl.BlockSpec((tm, d), lambda i: (i, 0)),
                  pl.BlockSpec((1, d), lambda i: (0, 0)),
                  pl.BlockSpec((d, LANES), lambda i: (0, 0))],
        out_specs=[pl.BlockSpec((tm, d), lambda i: (i, 0)),
                   pl.BlockSpec((tm, LANES), lambda i: (i, 0)),
                   pl.BlockSpec((SUBLANES, tm), lambda i: (0, i)),
                   pl.BlockSpec((1, 8, LANES), lambda i: (i, 0, 0)),
                   pl.BlockSpec((8, LANES), lambda i: (0, 0))],
        out_shape=[jax.ShapeDtypeStruct((t, d), BF16),
                   jax.ShapeDtypeStruct((t, LANES), F32),
                   jax.ShapeDtypeStruct((SUBLANES, t), F32),
                   jax.ShapeDtypeStruct((ntw, 8, LANES), F32),
                   jax.ShapeDtypeStruct((8, LANES), F32)],
        scratch_shapes=[pltpu.VMEM((1, LANES), F32)],
        compiler_params=_cparams("arbitrary"),
        name="moe_router",
    )(h, g.astype(F32).reshape(1, d), rp)


def _gather_kernel(off_ref, nw_ref, wl_ref, tot_ref, dest_ref, hn_hbm, xs_ref, buf_ref, sem, acc_ref):
    b = pl.program_id(0)
    rows = acc_ref.shape[0]
    n_buf, win = buf_ref.shape[0], buf_ref.shape[1]
    n = nw_ref[b]
    q0 = off_ref[b]
    total = tot_ref[0]

    def copy(q):
        w = wl_ref[q]
        s = lax.rem(q, n_buf)
        return pltpu.make_async_copy(hn_hbm.at[pl.ds(pl.multiple_of(w * win, win), win), :], buf_ref.at[s],
                                     sem.at[s])

    @pl.when(b == 0)
    def _():
        for q in range(GATHER_AHEAD):
            @pl.when(q < total)
            def _(q=q):
                copy(q).start()

    acc_ref[...] = jnp.zeros_like(acc_ref)
    slot = b * rows + lax.broadcasted_iota(I32, (rows, win), 0)

    def body(j, carry):
        q = q0 + j

        @pl.when(q + GATHER_AHEAD < total)
        def _():
            copy(q + GATHER_AHEAD).start()

        col = pl.ds(pl.multiple_of(wl_ref[q] * win, win), win)
        d1 = dest_ref[0:1, col]
        d2 = dest_ref[1:2, col]
        hit = jnp.where(d1 == slot, 1.0, jnp.where(d2 == slot, 1.0, 0.0)).astype(BF16)
        copy(q).wait()
        acc_ref[...] += jnp.dot(hit, buf_ref[lax.rem(q, n_buf)], preferred_element_type=F32)
        return carry

    lax.fori_loop(0, n, body, 0)
    xs_ref[...] = acc_ref[...].astype(BF16)


def _gather(hn, dest, off, nw, wlist, total, n_slots):
    t, d = hn.shape
    return pl.pallas_call(
        _gather_kernel,
        grid_spec=pltpu.PrefetchScalarGridSpec(
            num_scalar_prefetch=4,
            grid=(n_slots // GATHER_TILE,),
            in_specs=[pl.BlockSpec(dest.shape, lambda b, *_: (0, 0)),
                      pl.BlockSpec(memory_space=pl.ANY)],
            out_specs=pl.BlockSpec((GATHER_TILE, d), lambda b, *_: (b, 0)),
            scratch_shapes=[pltpu.VMEM((GATHER_AHEAD + 1, TOK_WIN, d), BF16),
                            pltpu.SemaphoreType.DMA((GATHER_AHEAD + 1,)),
                            pltpu.VMEM((GATHER_TILE, d), F32)]),
        out_shape=jax.ShapeDtypeStruct((n_slots, d), BF16),
        compiler_params=_cparams("arbitrary"),
        name="moe_gather",
    )(off, nw, wlist, total, dest, hn)


def _expert_kernel(te_ref, tv_ref, x_ref, w1_ref, w3_ref, w2_ref, y_ref, acc_ref):
    b = pl.program_id(0)
    j = pl.program_id(1)

    @pl.when(j == 0)
    def _():
        acc_ref[...] = jnp.zeros_like(acc_ref)

    @pl.when(tv_ref[b] == 1)
    def _():
        x = x_ref[...]
        a = jnp.dot(x, w1_ref[...].astype(BF16), preferred_element_type=F32)
        c = jnp.dot(x, w3_ref[...].astype(BF16), preferred_element_type=F32)
        gact = (a * _sigmoid(a) * c).astype(BF16)
        acc_ref[...] += jnp.dot(gact, w2_ref[...].astype(BF16), preferred_element_type=F32)

    @pl.when(j == pl.num_programs(1) - 1)
    def _():
        y_ref[...] = acc_ref[...].astype(BF16)


def _experts(xs, tile_e, tile_v, w1, w3, w2, layer):
    ns, d = xs.shape
    f = w1.shape[3]
    tf = MOE_COL_TILE
    nj = f // tf
    n_tiles = ns // MOE_TILE

    def col(b, j, tv):
        return j * tv[b] + (nj - 1) * (1 - tv[b])

    return pl.pallas_call(
        _expert_kernel,
        grid_spec=pltpu.PrefetchScalarGridSpec(
            num_scalar_prefetch=2,
            grid=(n_tiles, nj),
            in_specs=[pl.BlockSpec((MOE_TILE, d), lambda b, j, te, tv: (b, 0)),
                      pl.BlockSpec((None, None, d, tf), lambda b, j, te, tv: (layer, te[b], 0, col(b, j, tv))),
                      pl.BlockSpec((None, None, d, tf), lambda b, j, te, tv: (layer, te[b], 0, col(b, j, tv))),
                      pl.BlockSpec((None, None, tf, d), lambda b, j, te, tv: (layer, te[b], col(b, j, tv), 0))],
            out_specs=pl.BlockSpec((MOE_TILE, d), lambda b, j, te, tv: (b, 0)),
            scratch_shapes=[pltpu.VMEM((MOE_TILE, d), F32)]),
        out_shape=jax.ShapeDtypeStruct((ns, d), BF16),
        compiler_params=_cparams("arbitrary", "arbitrary"),
        name="moe_experts",
    )(tile_e, tile_v, xs, w1, w3, w2)


def _combine_kernel(ws_ref, po_ref, ex_ref, info_ref, h_ref, fin_ref, yb_hbm, o_ref, buf_ref, sem, *,
                    n_exp, final_norm):
    i = pl.program_id(0)
    cur = lax.rem(i, 2)

    def copy(step, e, s):
        start = pl.multiple_of(ws_ref[step * n_exp + e], SLOT_ALIGN)
        return pltpu.make_async_copy(yb_hbm.at[pl.ds(start, SLOT_WIN), :], buf_ref.at[s, e], sem.at[s, e])

    @pl.when(i == 0)
    def _():
        for e in range(n_exp):
            copy(0, e, 0).start()

    @pl.when(i + 1 < pl.num_programs(0))
    def _():
        for e in range(n_exp):
            copy(i + 1, e, 1 - cur).start()

    info = info_ref[...]
    tm = info.shape[0]
    e1, e2 = info[:, 0:1], info[:, 1:2]
    g1, g2 = info[:, 2:3], info[:, 3:4]
    r1, r2 = info[:, 4:5], info[:, 5:6]
    lanef = lax.broadcasted_iota(I32, (tm, TOK_WIN), 1).astype(F32)
    g1b = jnp.broadcast_to(g1, (tm, TOK_WIN))
    g2b = jnp.broadcast_to(g2, (tm, TOK_WIN))

    def weights(e, shift, lo_lane):
        off = (ws_ref[i * n_exp + e] - po_ref[e] + shift).astype(F32)
        k1 = jnp.where(e1 == float(e), r1 - off, -1.0)
        k2 = jnp.where(e2 == float(e), r2 - off, -1.0)
        k1 = jnp.where(k1 >= float(lo_lane), k1, -1.0)
        k2 = jnp.where(k2 >= float(lo_lane), k2, -1.0)
        return jnp.where(k1 == lanef, g1b, jnp.where(k2 == lanef, g2b, 0.0)).astype(BF16)

    acc = h_ref[...]
    for e in range(n_exp):
        pt = weights(e, 0, 0)
        copy(i, e, cur).wait()
        acc = acc + jnp.dot(pt, buf_ref[cur, e, 0:TOK_WIN, :], preferred_element_type=F32)
    o_ref[...] = acc

    tail = SLOT_WIN - TOK_WIN
    for e in range(n_exp):
        @pl.when(ex_ref[i * n_exp + e] == 1)
        def _(e=e):
            pt = weights(e, tail, TOK_WIN - tail)
            o_ref[...] += jnp.dot(pt, buf_ref[cur, e, tail:SLOT_WIN, :], preferred_element_type=F32)

    if final_norm:
        o_ref[...] = _rms(o_ref[...], fin_ref[...])


def _combine(info, h, yb, win_start, pstart, extra, fin_g, final_norm):
    t, d = h.shape
    n_exp = pstart.shape[0]
    tm = min(TOK_WIN, t)
    return pl.pallas_call(
        functools.partial(_combine_kernel, n_exp=n_exp, final_norm=final_norm),
        grid_spec=pltpu.PrefetchScalarGridSpec(
            num_scalar_prefetch=3,
            grid=(t // tm,),
            in_specs=[pl.BlockSpec((tm, LANES), lambda i, ws, po, ex: (i, 0)),
                      pl.BlockSpec((tm, d), lambda i, ws, po, ex: (i, 0)),
                      pl.BlockSpec((1, d), lambda i, ws, po, ex: (0, 0)),
                      pl.BlockSpec(memory_space=pl.ANY)],
            out_specs=pl.BlockSpec((tm, d), lambda i, ws, po, ex: (i, 0)),
            scratch_shapes=[pltpu.VMEM((2, n_exp, SLOT_WIN, d), BF16),
                            pltpu.SemaphoreType.DMA((2, n_exp))]),
        out_shape=jax.ShapeDtypeStruct((t, d), F32),
        compiler_params=_cparams("arbitrary"),
        name="moe_combine",
    )(win_start, pstart, extra, info, h, fin_g.astype(F32).reshape(1, d), yb)


def _count_le(sorted_vals, x):
    return jnp.sum((sorted_vals[None, :] <= x[:, None]).astype(I32), axis=1)


def _moe(h, g, router, w1, w3, w2, layer, fin_g, final_norm):
    t, d = h.shape
    n_exp = router.shape[1]
    tw = min(TOK_WIN, t)
    ntw = t // tw
    hn, info, info_t, before, total = _router(h, g, router)

    counts = total[0, :n_exp].astype(I32)
    padded = (counts + MOE_TILE - 1) // MOE_TILE * MOE_TILE
    pend = jnp.cumsum(padded)
    pstart = pend - padded
    n_tiles = (t * TOP_K) // MOE_TILE + n_exp
    n_slots = n_tiles * MOE_TILE
    tile0 = jnp.arange(n_tiles, dtype=I32) * MOE_TILE
    tile_e = jnp.minimum(_count_le(pend, tile0), n_exp - 1)
    tile_v = (tile0 < pend[-1]).astype(I32)
    cum = jnp.concatenate([before[:, 0, :n_exp], total[0:1, :n_exp]], axis=0).astype(I32)

    n_gb = n_slots // GATHER_TILE
    gb0 = jnp.arange(n_gb, dtype=I32) * GATHER_TILE
    gb_e = jnp.minimum(_count_le(pend, gb0), n_exp - 1)
    r0 = gb0 - pstart[gb_e]
    cum_b = cum[:, gb_e]
    ilo = jnp.sum((cum_b[1:] <= r0[None, :]).astype(I32), axis=0)
    ihi = jnp.sum((cum_b[:-1] < (r0 + GATHER_TILE)[None, :]).astype(I32), axis=0) - 1
    n_work = jnp.where(gb0 < pend[-1], jnp.maximum(ihi - ilo + 1, 0), 0).astype(I32)
    ends = jnp.cumsum(n_work)
    offs = (ends - n_work).astype(I32)
    entry = jnp.arange(n_exp * ntw + n_gb, dtype=I32)
    entry_blk = jnp.minimum(_count_le(ends, entry), n_gb - 1)
    wlist = jnp.clip(ilo[entry_blk] + entry - offs[entry_blk], 0, ntw - 1).astype(I32)

    e_t = info_t[0:2].astype(I32)
    slot_t = jnp.sum(jnp.where(e_t[None] == jnp.arange(n_exp, dtype=I32)[:, None, None], pstart[:, None, None], 0),
                     axis=0) + info_t[4:6].astype(I32)
    dest = jnp.concatenate([slot_t, jnp.full((SUBLANES - TOP_K, t), -1, I32)], axis=0)

    xs = _gather(hn, dest, offs, n_work, wlist, ends[-1:].astype(I32), n_slots)
    yb = _experts(xs, tile_e, tile_v, w1, w3, w2, layer)

    lo = pstart[None, :] + cum[:-1]
    win_start = jnp.minimum(lo // SLOT_ALIGN * SLOT_ALIGN, n_slots - SLOT_WIN).astype(I32)
    extra = ((lo - win_start + cum[1:] - cum[:-1]) > TOK_WIN).astype(I32)
    return _combine(info, h, yb, win_start.reshape(-1), pstart.astype(I32), extra.reshape(-1), fin_g, final_norm)


def _final_norm_kernel(h_ref, g_ref, o_ref):
    o_ref[...] = _rms(h_ref[...], g_ref[...])


def _final_norm(h, g):
    t, d = h.shape
    tm = min(ROW_TILE, t)
    return pl.pallas_call(
        _final_norm_kernel,
        grid=(t // tm,),
        in_specs=[pl.BlockSpec((tm, d), lambda i: (i, 0)), pl.BlockSpec((1, d), lambda i: (0, 0))],
        out_specs=pl.BlockSpec((tm, d), lambda i: (i, 0)),
        out_shape=jax.ShapeDtypeStruct((t, d), F32),
        compiler_params=_cparams("parallel"),
        name="final_norm",
    )(h, g.astype(F32).reshape(1, d))


def kernel(x, attn_norm, ffn_norm, final_norm, w_in, w_out, s5_lambda_re, s5_lambda_im, s5_log_dt, s5_b_re, s5_b_im, s5_c_re, s5_c_im, s5_d, s5_glu, s5_out_norm, conv_w, conv_out_norm, hg_lower_bounds, hg_out_norm, ffn_w1, ffn_w3, ffn_w2, moe_router, moe_w1, moe_w3, moe_w2):
    bsz, seq, d = x.shape
    depth = w_in.shape[0]
    t = bsz * seq
    assert bsz == 1, "token mixers are written for a single sequence"
    lb_soft = jax.nn.softmax(hg_lower_bounds.astype(F32), axis=0)
    lb_all = jnp.cumsum(lb_soft, axis=0) - lb_soft[0]
    n_scan = int(math.log2(t // S5_CHUNK))
    s5_ops = jax.vmap(functools.partial(_s5_operators, n_scan=n_scan))(
        s5_lambda_re, s5_lambda_im, s5_log_dt, s5_b_re, s5_b_im, s5_c_re, s5_c_im)
    h = x.reshape(t, d).astype(F32)
    for l in range(depth):
        proj = _norm_inproj(h, attn_norm[l].astype(F32), w_in[l].astype(BF16))
        ys5 = _s5_conv(proj, [op[l] for op in s5_ops])
        yhg = _hgrn(proj, lb_all[l], hg_out_norm[l])
        h = _mix_out(ys5, proj, yhg, h, s5_d[l], s5_glu[l].astype(BF16), s5_out_norm[l],
                     conv_w[l], conv_out_norm[l], w_out[l].astype(BF16))
        j = l // 2
        if l % 2 == 0:
            h = _ffn(h, ffn_norm[l], ffn_w1, ffn_w3, ffn_w2, j)
        else:
            h = _moe(h, ffn_norm[l], moe_router[j], moe_w1, moe_w3, moe_w2, j, final_norm, l == depth - 1)
    if depth % 2 == 1:
        h = _final_norm(h, final_norm)
    return h.reshape(bsz, seq, d)
```

```python
import functools
import math

import numpy as np
import jax
import jax.numpy as jnp
from jax import lax
from jax.experimental import pallas as pl
from jax.experimental.pallas import tpu as pltpu

F32 = jnp.float32
BF16 = jnp.bfloat16
I32 = jnp.int32

NORM_EPS = 1e-6
LB_FLOOR = 1e-30
TOP_K = 2

S5_WIDTH = 256
S5_GROUP = 16
S5_STATE = 64
CONV_WIDTH = 256
CONV_K = 3
HG_WIDTH = 512
HG_HEAD_DIM = 128
HG_HEADS = HG_WIDTH // HG_HEAD_DIM
LANES = 128
SUBLANES = 8
COL_CONV = S5_WIDTH
COL_HG = S5_WIDTH + 3 * CONV_WIDTH

VMEM_LIMIT = 56 * 1024 * 1024
S5_CHUNK = 32
HG_CHUNK = 128
HG_LEVELS = int(math.log2(HG_CHUNK))
HG_TBLOCK = 512
ROW_TILE = 512
FFN_ROW_TILE = 1024
FFN_COL_TILE = 512
MOE_TILE = 1024
MOE_COL_TILE = 512
GATHER_TILE = 256
GATHER_AHEAD = 4
GATHER_BUFS = GATHER_AHEAD + 2
TOK_WIN = 256
SLOT_ALIGN = 16
SLOT_WIN = TOK_WIN + SLOT_ALIGN


def _cparams(*sem):
    return pltpu.CompilerParams(dimension_semantics=sem, vmem_limit_bytes=VMEM_LIMIT)


def _rms(x, g):
    ms = jnp.mean(x * x, axis=-1, keepdims=True)
    return x * lax.rsqrt(ms + NORM_EPS) * g


def _sigmoid(x):
    return 1.0 / (1.0 + jnp.exp(-x))


def _norm_inproj_kernel(h_ref, g_ref, w_ref, o_ref):
    xn = _rms(h_ref[...], g_ref[...]).astype(BF16)
    o_ref[...] = jnp.dot(xn, w_ref[...], preferred_element_type=F32)


def _norm_inproj(h, g, w):
    t, d = h.shape
    n = w.shape[1]
    tm = min(ROW_TILE, t)
    return pl.pallas_call(
        _norm_inproj_kernel,
        grid=(t // tm,),
        in_specs=[pl.BlockSpec((tm, d), lambda i: (i, 0)),
                  pl.BlockSpec((1, d), lambda i: (0, 0)),
                  pl.BlockSpec((d, n), lambda i: (0, 0))],
        out_specs=pl.BlockSpec((tm, n), lambda i: (i, 0)),
        out_shape=jax.ShapeDtypeStruct((t, n), F32),
        compiler_params=_cparams("parallel"),
        name="norm_inproj",
    )(h, g.reshape(1, d), w)


def _s5_operators(lam_re, lam_im, log_dt, b_re, b_im, c_re, c_im, n_scan):
    lc = S5_CHUNK
    hi = lax.Precision.HIGHEST
    lr, li = lam_re.astype(F32), lam_im.astype(F32)
    dt = jnp.exp(log_dt.astype(F32))[:, None]
    zr, zi = lr * dt, li * dt
    taus = jnp.arange(lc + 1, dtype=F32)[:, None, None]
    mag = jnp.exp(zr[None] * taus)
    pwr, pwi = mag * jnp.cos(zi[None] * taus), mag * jnp.sin(zi[None] * taus)
    nr, ni = pwr[1] - 1.0, pwi[1]
    den = lr * lr + li * li
    qr, qi = (nr * lr + ni * li) / den, (ni * lr - nr * li) / den
    br, bi = b_re.astype(F32), b_im.astype(F32)
    bbr = qr[..., None] * br - qi[..., None] * bi
    bbi = qr[..., None] * bi + qi[..., None] * br
    cr, ci = c_re.astype(F32), c_im.astype(F32)
    g_, p_ = lr.shape

    def c_times_pw(lo):
        wr, wi = pwr[lo:lo + lc, :, None, :], pwi[lo:lo + lc, :, None, :]
        return cr[None] * wr - ci[None] * wi, cr[None] * wi + ci[None] * wr

    cpr, cpi = c_times_pw(0)
    cp = jnp.concatenate([cpr, cpi], axis=-1).transpose(1, 0, 2, 3).reshape(g_, lc * S5_GROUP, 2 * p_)
    kt = jnp.matmul(cp, jnp.concatenate([bbr, -bbi], axis=1), precision=hi)
    kflat = kt.transpose(0, 2, 1)

    wr, wi = pwr[lc - 1::-1][:, :, :, None], pwi[lc - 1::-1][:, :, :, None]
    msr = (wr * bbr[None] - wi * bbi[None]).transpose(1, 0, 3, 2).reshape(g_, lc * S5_GROUP, p_)
    msi = (wr * bbi[None] + wi * bbr[None]).transpose(1, 0, 3, 2).reshape(g_, lc * S5_GROUP, p_)
    m_state = jnp.concatenate([msr, msi], axis=-1)

    c1r, c1i = c_times_pw(1)
    c1r = c1r.transpose(1, 3, 0, 2).reshape(g_, p_, lc * S5_GROUP)
    c1i = c1i.transpose(1, 3, 0, 2).reshape(g_, p_, lc * S5_GROUP)
    m_carry = jnp.concatenate([c1r, -c1i], axis=1)

    akr, aki = [pwr[lc]], [pwi[lc]]
    for _ in range(n_scan - 1):
        r, i = akr[-1], aki[-1]
        akr.append(r * r - i * i)
        aki.append(2.0 * r * i)
    akr, aki = jnp.stack(akr, axis=1), jnp.stack(aki, axis=1)
    ar = jnp.concatenate([akr, akr], axis=-1)
    ai = jnp.concatenate([-aki, aki], axis=-1)
    kpad = -(-n_scan // 8) * 8
    ar = jnp.pad(ar, ((0, 0), (0, kpad - n_scan), (0, 0)))
    ai = jnp.pad(ai, ((0, 0), (0, kpad - n_scan), (0, 0)))
    return kflat, m_state.astype(BF16), m_carry.astype(BF16), ar, ai


S5_PER_TILE = LANES // S5_GROUP


def _s5_kernel(proj_hbm, kf_ref, ms_ref, mc_ref, ar_ref, ai_ref, y_hbm, us_ref, ys_ref, mi_ref, sem_in, sem_out,
               *, n_scan):
    lc, gw, per = S5_CHUNK, S5_GROUP, S5_PER_TILE
    nch = us_ref.shape[1]
    cols = pl.ds(pl.multiple_of(pl.program_id(0) * LANES, LANES), LANES)

    def in_copy(s):
        return pltpu.make_async_copy(proj_hbm.at[:, s, cols], us_ref.at[s], sem_in.at[s])

    def out_copy(s):
        return pltpu.make_async_copy(ys_ref.at[s], y_hbm.at[:, s, cols], sem_out.at[s])

    for s in range(lc):
        in_copy(s).start()
    ys_ref[...] = jnp.zeros_like(ys_ref)
    for s in range(lc):
        in_copy(s).wait()

    lane_grp = lax.broadcasted_iota(I32, (1, LANES), 1) // gw
    kf_lane = lax.broadcasted_iota(I32, (gw, lc * gw), 1)

    def group(gl, carry):
        tiles = []
        for j in range(lc // per):
            tile = jnp.zeros((nch, LANES), F32)
            for k in range(per):
                moved = pltpu.roll(us_ref[j * per + k], jnp.bitwise_and((k - gl) * gw, LANES - 1), axis=1)
                tile = jnp.where(lane_grp == k, moved, tile)
            tiles.append(tile)
        u = jnp.concatenate(tiles, axis=1).astype(BF16)

        kf = kf_ref[gl]
        for s in range(lc):
            blk = kf if s == 0 else jnp.where(kf_lane >= s * gw, pltpu.roll(kf, s * gw, axis=1), 0.0)
            mi_ref[s * gw:(s + 1) * gw, :] = blk.astype(BF16)

        x = jnp.dot(u, ms_ref[gl], preferred_element_type=F32)
        row = lax.broadcasted_iota(I32, x.shape, 0)
        half = x.shape[1] // 2
        for k in range(n_scan):
            d = 1 << k
            s = jnp.where(row >= d, pltpu.roll(x, d, axis=0), 0.0)
            x = x + ar_ref[gl, k:k + 1, :] * s + ai_ref[gl, k:k + 1, :] * pltpu.roll(s, half, axis=1)
        xe = jnp.where(row >= 1, pltpu.roll(x, 1, axis=0), 0.0)
        y = jnp.dot(u, mi_ref[...], preferred_element_type=F32)
        y = y + jnp.dot(xe.astype(BF16), mc_ref[gl], preferred_element_type=F32)

        for t in range(lc):
            j, k = divmod(t, per)
            moved = pltpu.roll(y[:, j * LANES:(j + 1) * LANES], jnp.bitwise_and((gl - k) * gw, LANES - 1), axis=1)
            ys_ref[t] = jnp.where(lane_grp == gl, moved, ys_ref[t])
        return carry

    lax.fori_loop(0, per, group, 0)
    for s in range(lc):
        out_copy(s).start()
    for s in range(lc):
        out_copy(s).wait()


def _s5_conv(proj, ops):
    kflat, m_state, m_carry, ar, ai = ops
    t, n = proj.shape
    lc = S5_CHUNK
    nch = t // lc
    n_scan = int(math.log2(nch))
    assert (1 << n_scan) == nch
    w = lc * S5_GROUP
    p2 = m_state.shape[-1]
    per = S5_PER_TILE

    def grp(shape):
        return pl.BlockSpec((per,) + shape, lambda hh: (hh, 0, 0))

    hbm = pl.BlockSpec(memory_space=pl.ANY)
    y = pl.pallas_call(
        functools.partial(_s5_kernel, n_scan=n_scan),
        grid=(S5_WIDTH // LANES,),
        in_specs=[hbm, grp((S5_GROUP, w)), grp((w, p2)), grp((p2, w)),
                  grp((ar.shape[1], p2)), grp((ai.shape[1], p2))],
        out_specs=hbm,
        out_shape=jax.ShapeDtypeStruct((nch, lc, S5_WIDTH), F32),
        scratch_shapes=[pltpu.VMEM((lc, nch, LANES), F32), pltpu.VMEM((lc, nch, LANES), F32),
                        pltpu.VMEM((w, w), BF16), pltpu.SemaphoreType.DMA((lc,)), pltpu.SemaphoreType.DMA((lc,))],
        compiler_params=_cparams("arbitrary"),
        name="s5_conv",
    )(proj.reshape(nch, lc, n), kflat, m_state, m_carry, ar, ai)
    return y.reshape(t, S5_WIDTH)


LOG2E = float(np.log2(np.e))


def _hg_level_table():
    idx = np.arange(HG_CHUNK)
    t, s = idx[:, None], idx[None, :]
    top_bit = np.floor(np.log2(np.maximum(t ^ s, 1))).astype(np.int32)
    return np.where(s < t, HG_LEVELS - 1 - top_bit, -1).astype(np.int32)


def _hg_midpoint(bc, m):
    c = bc.shape[0]
    h = m // 2
    if h >= SUBLANES:
        return jnp.concatenate([jnp.broadcast_to(bc[j * m + h - 1:j * m + h, :], (m, LANES))
                                for j in range(c // m)], axis=0)
    x3 = bc.reshape(c // SUBLANES, SUBLANES, LANES)
    sub = lax.broadcasted_iota(I32, x3.shape, 1)
    beta = None
    for j in reversed(range(SUBLANES // m)):
        row = jnp.broadcast_to(x3[:, j * m + h - 1:j * m + h, :], x3.shape)
        beta = row if beta is None else jnp.where(sub < (j + 1) * m, row, beta)
    return beta.reshape(c, LANES)


def _bf16_terms(x, n):
    terms = []
    for _ in range(n - 1):
        t = x.astype(BF16)
        terms.append(t)
        x = x - t.astype(F32)
    return terms + [x.astype(BF16)]


def _hg_chunk(fp, q, v, gt, lb, lbf, ng, lvl, tri, st):
    c = HG_CHUNK
    prow = lax.broadcasted_iota(I32, (c, LANES), 0)
    nt = (((1,), (1,)), ((), ()))
    tn = (((0,), (0,)), ((), ()))
    en = jnp.exp(-jnp.abs(fp))
    rc = 1.0 / (1.0 + en)
    pos_f = fp >= 0.0
    sig_p = jnp.where(pos_f, rc, en * rc)
    sig_n = jnp.where(pos_f, en * rc, rc)
    f = lbf + (1.0 - lb) * sig_p
    kc = (1.0 - lb) * sig_n
    qc = q * _sigmoid(q)
    b2 = sum(jnp.dot(tri, t, preferred_element_type=F32) for t in _bf16_terms(jnp.log(f), 3)) * LOG2E

    a = jnp.zeros((c, c), F32)
    for lev in range(HG_LEVELS):
        m = c >> lev
        h = m // 2
        if h >= SUBLANES:
            parts = []
            for j in range(c // m):
                lo, mid, hi = j * m, j * m + h, (j + 1) * m
                beta = b2[mid - 1:mid, :]
                parts += [kc[lo:mid] * jnp.exp2(beta - b2[lo:mid]), qc[mid:hi] * jnp.exp2(b2[mid:hi] - beta)]
            zw = jnp.concatenate(parts, axis=0)
        else:
            upper = jnp.bitwise_and(prow, m - 1) >= h
            z = jnp.where(upper, qc, kc)
            if m == 2:
                zw = jnp.where(upper, z * f, z)
            else:
                dlt = b2 - _hg_midpoint(b2, m)
                zw = z * jnp.exp2(jnp.where(upper, dlt, -dlt))
        zw = zw.astype(BF16)
        s = lax.dot_general(zw, zw, nt, preferred_element_type=F32)
        a = jnp.where(lvl == lev, s, a)
    vb = v.astype(BF16)
    o = jnp.dot(a.astype(BF16), vb, preferred_element_type=F32)
    o = o + jnp.sum(qc * kc, axis=-1, keepdims=True) * v
    o = o + lax.dot_general((qc * jnp.exp2(b2)).astype(BF16), st.astype(BF16), nt,
                            preferred_element_type=F32)
    bl = b2[c - 1:c, :]
    khat = (kc * jnp.exp2(bl - b2)).astype(BF16)
    st = st * jnp.exp2(bl) + lax.dot_general(vb, khat, tn, preferred_element_type=F32)
    return _rms(o, ng) * (gt * _sigmoid(gt)), st


def _hgrn_kernel(q_ref, f_ref, i_ref, gt_ref, lb_ref, lbf_ref, ng_ref, lvl_ref, tri_ref, o_ref, st_ref):
    c = HG_CHUNK

    @pl.when(pl.program_id(0) == 0)
    def _():
        st_ref[...] = jnp.zeros_like(st_ref)

    def chunk(n, carry):
        rows = pl.ds(pl.multiple_of(n * c, c), c)
        for hd in range(HG_HEADS):
            cols = slice(hd * LANES, (hd + 1) * LANES)
            o, st = _hg_chunk(f_ref[rows, cols], q_ref[rows, cols], i_ref[rows, cols], gt_ref[rows, cols],
                              lb_ref[:, cols], lbf_ref[:, cols], ng_ref[:, cols], lvl_ref[...], tri_ref[...],
                              st_ref[hd])
            o_ref[rows, cols] = o
            st_ref[hd] = st
        return carry

    lax.fori_loop(0, q_ref.shape[0] // c, chunk, 0)


def _hgrn(proj, lb, norm_g):
    t = proj.shape[0]
    tb = min(HG_TBLOCK, t)
    lbh = jnp.clip(lb.astype(F32), 0.0, 1.0 - 1e-6).reshape(1, HG_WIDTH)
    lbf = jnp.maximum(lbh, LB_FLOOR)
    ng = norm_g.astype(F32).reshape(1, HG_WIDTH)
    lvl = jnp.asarray(_hg_level_table())
    tri = jnp.asarray(np.tril(np.ones((HG_CHUNK, HG_CHUNK), np.float32)), BF16)
    cb = COL_HG // HG_WIDTH

    def col(k):
        return pl.BlockSpec((tb, HG_WIDTH), lambda i, k=k: (i, cb + k))

    vec = pl.BlockSpec((1, HG_WIDTH), lambda i: (0, 0))
    return pl.pallas_call(
        _hgrn_kernel,
        grid=(t // tb,),
        in_specs=[col(0), col(1), col(2), col(3), vec, vec, vec,
                  pl.BlockSpec(lvl.shape, lambda i: (0, 0)), pl.BlockSpec(tri.shape, lambda i: (0, 0))],
        out_specs=pl.BlockSpec((tb, HG_WIDTH), lambda i: (i, 0)),
        out_shape=jax.ShapeDtypeStruct((t, HG_WIDTH), F32),
        scratch_shapes=[pltpu.VMEM((HG_HEADS, LANES, LANES), F32)],
        compiler_params=_cparams("arbitrary"),
        name="hgrn2",
    )(proj, proj, proj, proj, lbh, lbf, ng, lvl, tri)


def _mix_out_kernel(ys_ref, u_ref, cb_ref, cc_ref, cv_ref, hg_ref, h_ref,
                    d_ref, glu_ref, sn_ref, cw_ref, cn_ref, wo_ref, o_ref, carry_ref):
    @pl.when(pl.program_id(0) == 0)
    def _():
        carry_ref[...] = jnp.zeros_like(carry_ref)

    y = ys_ref[...] + d_ref[...] * u_ref[...]
    y = jax.nn.gelu(y)
    y = y * _sigmoid(jnp.dot(y.astype(BF16), glu_ref[...], preferred_element_type=F32))
    y_s5 = _rms(y, sn_ref[...])

    z = cc_ref[...] * cv_ref[...]
    tm = z.shape[0]
    row = lax.broadcasted_iota(I32, z.shape, 0)
    p1 = carry_ref[7:8, :]
    p2 = carry_ref[6:7, :]
    z1 = jnp.where(row == 0, p1, pltpu.roll(z, 1, axis=0))
    z2 = jnp.where(row == 0, p2, jnp.where(row == 1, p1, pltpu.roll(z, 2, axis=0)))
    carry_ref[...] = z[tm - 8:tm, :]
    yc = cb_ref[...] * (z2 * cw_ref[0:1, :] + z1 * cw_ref[1:2, :] + z * cw_ref[2:3, :])
    y_cv = _rms(yc, cn_ref[...])

    acc = h_ref[...]
    hg_row = S5_WIDTH + CONV_WIDTH
    acc = acc + jnp.dot(y_s5.astype(BF16), wo_ref[0:S5_WIDTH, :], preferred_element_type=F32)
    acc = acc + jnp.dot(y_cv.astype(BF16), wo_ref[S5_WIDTH:hg_row, :], preferred_element_type=F32)
    acc = acc + jnp.dot(hg_ref[...].astype(BF16), wo_ref[hg_row:, :], preferred_element_type=F32)
    o_ref[...] = acc


def _mix_out(ys5, proj, yhg, h, s5_d, s5_glu, s5_norm, conv_w, conv_norm, w_out):
    t, d = h.shape
    tm = min(ROW_TILE, t)
    cw = jnp.pad(conv_w.astype(F32), ((0, 8 - CONV_K), (0, 0)))
    wq = S5_WIDTH

    def rowblk(width, colblk):
        return pl.BlockSpec((tm, width), lambda i, c=colblk: (i, c))

    def full(shape):
        return pl.BlockSpec(shape, lambda i: (0,) * len(shape))

    return pl.pallas_call(
        _mix_out_kernel,
        grid=(t // tm,),
        in_specs=[rowblk(wq, 0), rowblk(wq, 0), rowblk(wq, 1), rowblk(wq, 2), rowblk(wq, 3),
                  rowblk(HG_WIDTH, 0), rowblk(d, 0),
                  full((1, wq)), full((wq, wq)), full((1, wq)), full((8, wq)), full((1, wq)),
                  full(w_out.shape)],
        out_specs=rowblk(d, 0),
        out_shape=jax.ShapeDtypeStruct((t, d), F32),
        scratch_shapes=[pltpu.VMEM((8, CONV_WIDTH), F32)],
        compiler_params=_cparams("arbitrary"),
        name="mix_out",
    )(ys5, proj, proj, proj, proj, yhg, h,
      s5_d.astype(F32).reshape(1, wq), s5_glu, s5_norm.astype(F32).reshape(1, wq), cw,
      conv_norm.astype(F32).reshape(1, wq), w_out)


def _ffn_kernel(h_ref, g_ref, w1_ref, w3_ref, w2_ref, o_ref, hn_ref, acc_ref):
    j = pl.program_id(1)

    @pl.when(j == 0)
    def _():
        x = h_ref[...]
        hn_ref[...] = _rms(x, g_ref[...]).astype(BF16)
        acc_ref[...] = x

    hn = hn_ref[...]
    a = jnp.dot(hn, w1_ref[...].astype(BF16), preferred_element_type=F32)
    b = jnp.dot(hn, w3_ref[...].astype(BF16), preferred_element_type=F32)
    gact = (a * _sigmoid(a) * b).astype(BF16)
    acc_ref[...] += jnp.dot(gact, w2_ref[...].astype(BF16), preferred_element_type=F32)

    @pl.when(j == pl.num_programs(1) - 1)
    def _():
        o_ref[...] = acc_ref[...]


def _ffn(h, g, w1, w3, w2, layer):
    t, d = h.shape
    f = w1.shape[2]
    tm = min(FFN_ROW_TILE, t)
    tf = FFN_COL_TILE
    return pl.pallas_call(
        _ffn_kernel,
        grid=(t // tm, f // tf),
        in_specs=[pl.BlockSpec((tm, d), lambda i, j: (i, 0)),
                  pl.BlockSpec((1, d), lambda i, j: (0, 0)),
                  pl.BlockSpec((None, d, tf), lambda i, j: (layer, 0, j)),
                  pl.BlockSpec((None, d, tf), lambda i, j: (layer, 0, j)),
                  pl.BlockSpec((None, tf, d), lambda i, j: (layer, j, 0))],
        out_specs=pl.BlockSpec((tm, d), lambda i, j: (i, 0)),
        out_shape=jax.ShapeDtypeStruct((t, d), F32),
        scratch_shapes=[pltpu.VMEM((tm, d), BF16), pltpu.VMEM((tm, d), F32)],
        compiler_params=_cparams("parallel", "arbitrary"),
        name="ffn_swiglu",
    )(h, g.astype(F32).reshape(1, d), w1, w3, w2)


def _router_kernel(h_ref, g_ref, r_ref, hn_ref, info_ref, info_t_ref, before_ref, total_ref, cnt_ref, *, n_exp):
    @pl.when(pl.program_id(0) == 0)
    def _():
        cnt_ref[...] = jnp.zeros_like(cnt_ref)

    xn = _rms(h_ref[...], g_ref[...])
    x_hi, x_lo = _bf16_terms(xn, 2)
    hn_ref[...] = x_hi
    r_hi, r_lo = _bf16_terms(r_ref[...], 2)
    logits = (jnp.dot(x_hi, r_hi, preferred_element_type=F32) + jnp.dot(x_lo, r_hi, preferred_element_type=F32)
              + jnp.dot(x_hi, r_lo, preferred_element_type=F32))
    tm = logits.shape[0]
    lane = lax.broadcasted_iota(I32, logits.shape, 1)
    lanef = lane.astype(F32)
    neg = jnp.float32(-jnp.inf)
    lg = jnp.where(lane < n_exp, logits, neg)
    m1 = jnp.max(lg, axis=-1, keepdims=True)
    i1 = jnp.min(jnp.where(lg == m1, lanef, float(LANES)), axis=-1, keepdims=True)
    oh1 = lanef == i1
    lg2 = jnp.where(oh1, neg, lg)
    m2 = jnp.max(lg2, axis=-1, keepdims=True)
    i2 = jnp.min(jnp.where(lg2 == m2, lanef, float(LANES)), axis=-1, keepdims=True)
    oh2 = lanef == i2
    ex = jnp.exp(m2 - m1)
    g1 = 1.0 / (1.0 + ex)
    g2 = ex * g1
    chosen = jnp.where(oh1, 1.0, jnp.where(oh2, 1.0, 0.0))
    ri = lax.broadcasted_iota(I32, (tm, tm), 0)
    ci = lax.broadcasted_iota(I32, (tm, tm), 1)
    tri = jnp.where(ri > ci, 1.0, 0.0).astype(BF16)
    before = cnt_ref[...]
    cexcl = jnp.dot(tri, chosen.astype(BF16), preferred_element_type=F32) + before
    rank1 = jnp.sum(jnp.where(oh1, cexcl, 0.0), axis=-1, keepdims=True)
    rank2 = jnp.sum(jnp.where(oh2, cexcl, 0.0), axis=-1, keepdims=True)
    info = jnp.where(lane == 0, i1, jnp.where(lane == 1, i2, jnp.where(lane == 2, g1, jnp.where(
        lane == 3, g2, jnp.where(lane == 4, rank1, jnp.where(lane == 5, rank2, 0.0))))))
    info_ref[...] = info
    info_t_ref[...] = info.T[0:SUBLANES, :]
    before_ref[0] = jnp.broadcast_to(before, before_ref.shape[1:])
    total = before + jnp.sum(chosen, axis=0, keepdims=True)
    cnt_ref[...] = total
    total_ref[...] = jnp.broadcast_to(total, total_ref.shape)


def _router(h, g, router):
    t, d = h.shape
    n_exp = router.shape[1]
    tm = min(TOK_WIN, t)
    ntw = t // tm
    rp = jnp.pad(router.astype(F32), ((0, 0), (0, LANES - n_exp)))
    return pl.pallas_call(
        functools.partial(_router_kernel, n_exp=n_exp),
        grid=(ntw,),
        in_specs=[pl.BlockSpec((tm, d), lambda i: (i, 0)),
                  pl.BlockSpec((1, d), lambda i: (0, 0)),
                  pl.BlockSpec((d, LANES), lambda i: (0, 0))],
        out_specs=[pl.BlockSpec((tm, d), lambda i: (i, 0)),
                   pl.BlockSpec((tm, LANES), lambda i: (i, 0)),
                   pl.BlockSpec((SUBLANES, tm), lambda i: (0, i)),
                   pl.BlockSpec((1, 8, LANES), lambda i: (i, 0, 0)),
                   pl.BlockSpec((8, LANES), lambda i: (0, 0))],
        out_shape=[jax.ShapeDtypeStruct((t, d), BF16),
                   jax.ShapeDtypeStruct((t, LANES), F32),
                   jax.ShapeDtypeStruct((SUBLANES, t), F32),
                   jax.ShapeDtypeStruct((ntw, 8, LANES), F32),
                   jax.ShapeDtypeStruct((8, LANES), F32)],
        scratch_shapes=[pltpu.VMEM((1, LANES), F32)],
        compiler_params=_cparams("arbitrary"),
        name="moe_router",
    )(h, g.astype(F32).reshape(1, d), rp)


def _gather_kernel(off_ref, nw_ref, wl_ref, tot_ref, dest_ref, hn_hbm, xs_ref, buf_ref, sem, acc_ref):
    b = pl.program_id(0)
    rows = acc_ref.shape[0]
    n_buf, win = buf_ref.shape[0], buf_ref.shape[1]
    n = nw_ref[b]
    q0 = off_ref[b]
    total = tot_ref[0]

    def copy(q):
        w = wl_ref[q]
        s = lax.rem(q, n_buf)
        return pltpu.make_async_copy(hn_hbm.at[pl.ds(pl.multiple_of(w * win, win), win), :], buf_ref.at[s],
                                     sem.at[s])

    @pl.when(b == 0)
    def _():
        for q in range(GATHER_AHEAD):
            @pl.when(q < total)
            def _(q=q):
                copy(q).start()

    acc_ref[...] = jnp.zeros_like(acc_ref)
    slot = b * rows + lax.broadcasted_iota(I32, (rows, win), 0)

    def gathered(q):
        @pl.when(q + GATHER_AHEAD < total)
        def _():
            copy(q + GATHER_AHEAD).start()

        col = pl.ds(pl.multiple_of(wl_ref[q] * win, win), win)
        d1 = dest_ref[0:1, col]
        d2 = dest_ref[1:2, col]
        hit = jnp.where(d1 == slot, 1.0, jnp.where(d2 == slot, 1.0, 0.0)).astype(BF16)
        copy(q).wait()
        return jnp.dot(hit, buf_ref[lax.rem(q, n_buf)], preferred_element_type=F32)

    def pair(j, carry):
        q = q0 + 2 * j
        acc_ref[...] += gathered(q) + gathered(q + 1)
        return carry

    lax.fori_loop(0, lax.shift_right_logical(n, 1), pair, 0)

    @pl.when(jnp.bitwise_and(n, 1) == 1)
    def _():
        acc_ref[...] += gathered(q0 + n - 1)

    xs_ref[...] = acc_ref[...].astype(BF16)


def _gather(hn, dest, off, nw, wlist, total, n_slots):
    t, d = hn.shape
    return pl.pallas_call(
        _gather_kernel,
        grid_spec=pltpu.PrefetchScalarGridSpec(
            num_scalar_prefetch=4,
            grid=(n_slots // GATHER_TILE,),
            in_specs=[pl.BlockSpec(dest.shape, lambda b, *_: (0, 0)),
                      pl.BlockSpec(memory_space=pl.ANY)],
            out_specs=pl.BlockSpec((GATHER_TILE, d), lambda b, *_: (b, 0)),
            scratch_shapes=[pltpu.VMEM((GATHER_BUFS, TOK_WIN, d), BF16),
                            pltpu.SemaphoreType.DMA((GATHER_BUFS,)),
                            pltpu.VMEM((GATHER_TILE, d), F32)]),
        out_shape=jax.ShapeDtypeStruct((n_slots, d), BF16),
        compiler_params=_cparams("arbitrary"),
        name="moe_gather",
    )(off, nw, wlist, total, dest, hn)


def _expert_kernel(te_ref, tv_ref, x_ref, w1_ref, w3_ref, w2_ref, y_ref, acc_ref):
    b = pl.program_id(0)
    j = pl.program_id(1)

    @pl.when(j == 0)
    def _():
        acc_ref[...] = jnp.zeros_like(acc_ref)

    @pl.when(tv_ref[b] == 1)
    def _():
        x = x_ref[...]
        a = jnp.dot(x, w1_ref[...].astype(BF16), preferred_element_type=F32)
        c = jnp.dot(x, w3_ref[...].astype(BF16), preferred_element_type=F32)
        gact = (a * _sigmoid(a) * c).astype(BF16)
        acc_ref[...] += jnp.dot(gact, w2_ref[...].astype(BF16), preferred_element_type=F32)

    @pl.when(j == pl.num_programs(1) - 1)
    def _():
        y_ref[...] = acc_ref[...].astype(BF16)


def _experts(xs, tile_e, tile_v, w1, w3, w2, layer):
    ns, d = xs.shape
    f = w1.shape[3]
    tf = MOE_COL_TILE
    nj = f // tf
    n_tiles = ns // MOE_TILE

    def col(b, j, tv):
        return j * tv[b] + (nj - 1) * (1 - tv[b])

    return pl.pallas_call(
        _expert_kernel,
        grid_spec=pltpu.PrefetchScalarGridSpec(
            num_scalar_prefetch=2,
            grid=(n_tiles, nj),
            in_specs=[pl.BlockSpec((MOE_TILE, d), lambda b, j, te, tv: (b, 0)),
                      pl.BlockSpec((None, None, d, tf), lambda b, j, te, tv: (layer, te[b], 0, col(b, j, tv))),
                      pl.BlockSpec((None, None, d, tf), lambda b, j, te, tv: (layer, te[b], 0, col(b, j, tv))),
                      pl.BlockSpec((None, None, tf, d), lambda b, j, te, tv: (layer, te[b], col(b, j, tv), 0))],
            out_specs=pl.BlockSpec((MOE_TILE, d), lambda b, j, te, tv: (b, 0)),
            scratch_shapes=[pltpu.VMEM((MOE_TILE, d), F32)]),
        out_shape=jax.ShapeDtypeStruct((ns, d), BF16),
        compiler_params=_cparams("arbitrary", "arbitrary"),
        name="moe_experts",
    )(tile_e, tile_v, xs, w1, w3, w2)


def _combine_kernel(ws_ref, po_ref, ex_ref, info_ref, h_ref, fin_ref, yb_hbm, o_ref, buf_ref, sem, *,
                    n_exp, final_norm):
    i = pl.program_id(0)
    cur = lax.rem(i, 2)

    def copy(step, e, s):
        start = pl.multiple_of(ws_ref[step * n_exp + e], SLOT_ALIGN)
        return pltpu.make_async_copy(yb_hbm.at[pl.ds(start, SLOT_WIN), :], buf_ref.at[s, e], sem.at[s, e])

    @pl.when(i == 0)
    def _():
        for e in range(n_exp):
            copy(0, e, 0).start()

    @pl.when(i + 1 < pl.num_programs(0))
    def _():
        for e in range(n_exp):
            copy(i + 1, e, 1 - cur).start()

    info = info_ref[...]
    tm = info.shape[0]
    e1, e2 = info[:, 0:1], info[:, 1:2]
    g1, g2 = info[:, 2:3], info[:, 3:4]
    r1, r2 = info[:, 4:5], info[:, 5:6]
    lanef = lax.broadcasted_iota(I32, (tm, TOK_WIN), 1).astype(F32)
    g1b = jnp.broadcast_to(g1, (tm, TOK_WIN))
    g2b = jnp.broadcast_to(g2, (tm, TOK_WIN))

    def weights(e, shift, lo_lane):
        off = (ws_ref[i * n_exp + e] - po_ref[e] + shift).astype(F32)
        k1 = jnp.where(e1 == float(e), r1 - off, -1.0)
        k2 = jnp.where(e2 == float(e), r2 - off, -1.0)
        k1 = jnp.where(k1 >= float(lo_lane), k1, -1.0)
        k2 = jnp.where(k2 >= float(lo_lane), k2, -1.0)
        return jnp.where(k1 == lanef, g1b, jnp.where(k2 == lanef, g2b, 0.0)).astype(BF16)

    acc = h_ref[...]
    for e in range(n_exp):
        pt = weights(e, 0, 0)
        copy(i, e, cur).wait()
        acc = acc + jnp.dot(pt, buf_ref[cur, e, 0:TOK_WIN, :], preferred_element_type=F32)
    o_ref[...] = acc

    tail = SLOT_WIN - TOK_WIN
    for e in range(n_exp):
        @pl.when(ex_ref[i * n_exp + e] == 1)
        def _(e=e):
            pt = weights(e, tail, TOK_WIN - tail)
            o_ref[...] += jnp.dot(pt, buf_ref[cur, e, tail:SLOT_WIN, :], preferred_element_type=F32)

    if final_norm:
        o_ref[...] = _rms(o_ref[...], fin_ref[...])


def _combine(info, h, yb, win_start, pstart, extra, fin_g, final_norm):
    t, d = h.shape
    n_exp = pstart.shape[0]
    tm = min(TOK_WIN, t)
    return pl.pallas_call(
        functools.partial(_combine_kernel, n_exp=n_exp, final_norm=final_norm),
        grid_spec=pltpu.PrefetchScalarGridSpec(
            num_scalar_prefetch=3,
            grid=(t // tm,),
            in_specs=[pl.BlockSpec((tm, LANES), lambda i, ws, po, ex: (i, 0)),
                      pl.BlockSpec((tm, d), lambda i, ws, po, ex: (i, 0)),
                      pl.BlockSpec((1, d), lambda i, ws, po, ex: (0, 0)),
                      pl.BlockSpec(memory_space=pl.ANY)],
            out_specs=pl.BlockSpec((tm, d), lambda i, ws, po, ex: (i, 0)),
            scratch_shapes=[pltpu.VMEM((2, n_exp, SLOT_WIN, d), BF16),
                            pltpu.SemaphoreType.DMA((2, n_exp))]),
        out_shape=jax.ShapeDtypeStruct((t, d), F32),
        compiler_params=_cparams("arbitrary"),
        name="moe_combine",
    )(win_start, pstart, extra, info, h, fin_g.astype(F32).reshape(1, d), yb)


def _count_le(sorted_vals, x):
    return jnp.sum((sorted_vals[None, :] <= x[:, None]).astype(I32), axis=1)


def _moe(h, g, router, w1, w3, w2, layer, fin_g, final_norm):
    t, d = h.shape
    n_exp = router.shape[1]
    tw = min(TOK_WIN, t)
    ntw = t // tw
    hn, info, info_t, before, total = _router(h, g, router)

    counts = total[0, :n_exp].astype(I32)
    padded = (counts + MOE_TILE - 1) // MOE_TILE * MOE_TILE
    pend = jnp.cumsum(padded)
    pstart = pend - padded
    n_tiles = (t * TOP_K) // MOE_TILE + n_exp
    n_slots = n_tiles * MOE_TILE
    tile0 = jnp.arange(n_tiles, dtype=I32) * MOE_TILE
    tile_e = jnp.minimum(_count_le(pend, tile0), n_exp - 1)
    tile_v = (tile0 < pend[-1]).astype(I32)
    cum = jnp.concatenate([before[:, 0, :n_exp], total[0:1, :n_exp]], axis=0).astype(I32)

    n_gb = n_slots // GATHER_TILE
    gb0 = jnp.arange(n_gb, dtype=I32) * GATHER_TILE
    gb_e = jnp.minimum(_count_le(pend, gb0), n_exp - 1)
    r0 = gb0 - pstart[gb_e]
    cum_b = cum[:, gb_e]
    ilo = jnp.sum((cum_b[1:] <= r0[None, :]).astype(I32), axis=0)
    ihi = jnp.sum((cum_b[:-1] < (r0 + GATHER_TILE)[None, :]).astype(I32), axis=0) - 1
    n_work = jnp.where(gb0 < pend[-1], jnp.maximum(ihi - ilo + 1, 0), 0).astype(I32)
    ends = jnp.cumsum(n_work)
    offs = (ends - n_work).astype(I32)
    entry = jnp.arange(n_exp * ntw + n_gb, dtype=I32)
    entry_blk = jnp.minimum(_count_le(ends, entry), n_gb - 1)
    wlist = jnp.clip(ilo[entry_blk] + entry - offs[entry_blk], 0, ntw - 1).astype(I32)

    e_t = info_t[0:2].astype(I32)
    slot_t = jnp.sum(jnp.where(e_t[None] == jnp.arange(n_exp, dtype=I32)[:, None, None], pstart[:, None, None], 0),
                     axis=0) + info_t[4:6].astype(I32)
    dest = jnp.concatenate([slot_t, jnp.full((SUBLANES - TOP_K, t), -1, I32)], axis=0)

    xs = _gather(hn, dest, offs, n_work, wlist, ends[-1:].astype(I32), n_slots)
    yb = _experts(xs, tile_e, tile_v, w1, w3, w2, layer)

    lo = pstart[None, :] + cum[:-1]
    win_start = jnp.minimum(lo // SLOT_ALIGN * SLOT_ALIGN, n_slots - SLOT_WIN).astype(I32)
    extra = ((lo - win_start + cum[1:] - cum[:-1]) > TOK_WIN).astype(I32)
    return _combine(info, h, yb, win_start.reshape(-1), pstart.astype(I32), extra.reshape(-1), fin_g, final_norm)


def _final_norm_kernel(h_ref, g_ref, o_ref):
    o_ref[...] = _rms(h_ref[...], g_ref[...])


def _final_norm(h, g):
    t, d = h.shape
    tm = min(ROW_TILE, t)
    return pl.pallas_call(
        _final_norm_kernel,
        grid=(t // tm,),
        in_specs=[pl.BlockSpec((tm, d), lambda i: (i, 0)), pl.BlockSpec((1, d), lambda i: (0, 0))],
        out_specs=pl.BlockSpec((tm, d), lambda i: (i, 0)),
        out_shape=jax.ShapeDtypeStruct((t, d), F32),
        compiler_params=_cparams("parallel"),
        name="final_norm",
    )(h, g.astype(F32).reshape(1, d))


def kernel(x, attn_norm, ffn_norm, final_norm, w_in, w_out, s5_lambda_re, s5_lambda_im, s5_log_dt, s5_b_re, s5_b_im, s5_c_re, s5_c_im, s5_d, s5_glu, s5_out_norm, conv_w, conv_out_norm, hg_lower_bounds, hg_out_norm, ffn_w1, ffn_w3, ffn_w2, moe_router, moe_w1, moe_w3, moe_w2):
    bsz, seq, d = x.shape
    depth = w_in.shape[0]
    t = bsz * seq
    assert bsz == 1, "token mixers are written for a single sequence"
    lb_soft = jax.nn.softmax(hg_lower_bounds.astype(F32), axis=0)
    lb_all = jnp.cumsum(lb_soft, axis=0) - lb_soft[0]
    n_scan = int(math.log2(t // S5_CHUNK))
    s5_ops = jax.vmap(functools.partial(_s5_operators, n_scan=n_scan))(
        s5_lambda_re, s5_lambda_im, s5_log_dt, s5_b_re, s5_b_im, s5_c_re, s5_c_im)
    h = x.reshape(t, d).astype(F32)
    for l in range(depth):
        proj = _norm_inproj(h, attn_norm[l].astype(F32), w_in[l].astype(BF16))
        ys5 = _s5_conv(proj, [op[l] for op in s5_ops])
        yhg = _hgrn(proj, lb_all[l], hg_out_norm[l])
        h = _mix_out(ys5, proj, yhg, h, s5_d[l], s5_glu[l].astype(BF16), s5_out_norm[l],
                     conv_w[l], conv_out_norm[l], w_out[l].astype(BF16))
        j = l // 2
        if l % 2 == 0:
            h = _ffn(h, ffn_norm[l], ffn_w1, ffn_w3, ffn_w2, j)
        else:
            h = _moe(h, ffn_norm[l], moe_router[j], moe_w1, moe_w3, moe_w2, j, final_norm, l == depth - 1)
    if depth % 2 == 1:
        h = _final_norm(h, final_norm)
    return h.reshape(bsz, seq, d)
```

```python
import functools
import math

import numpy as np
import jax
import jax.numpy as jnp
from jax import lax
from jax.experimental import pallas as pl
from jax.experimental.pallas import tpu as pltpu

F32 = jnp.float32
BF16 = jnp.bfloat16
I32 = jnp.int32

NORM_EPS = 1e-6
LB_FLOOR = 1e-30
TOP_K = 2

S5_WIDTH = 256
S5_GROUP = 16
S5_STATE = 64
CONV_WIDTH = 256
CONV_K = 3
HG_WIDTH = 512
HG_HEAD_DIM = 128
HG_HEADS = HG_WIDTH // HG_HEAD_DIM
LANES = 128
SUBLANES = 8
COL_CONV = S5_WIDTH
COL_HG = S5_WIDTH + 3 * CONV_WIDTH

VMEM_LIMIT = 56 * 1024 * 1024
S5_CHUNK = 32
HG_CHUNK = 128
HG_LEVELS = int(math.log2(HG_CHUNK))
HG_TBLOCK = 512
ROW_TILE = 512
FFN_ROW_TILE = 1024
FFN_COL_TILE = 512
MOE_TILE = 1024
MOE_COL_TILE = 512
GATHER_TILE = 256
GATHER_AHEAD = 4
GATHER_BUFS = GATHER_AHEAD + 2
TOK_WIN = 256
SLOT_ALIGN = 16
SLOT_WIN = TOK_WIN + SLOT_ALIGN


def _cparams(*sem):
    return pltpu.CompilerParams(dimension_semantics=sem, vmem_limit_bytes=VMEM_LIMIT)


def _rms(x, g):
    ms = jnp.mean(x * x, axis=-1, keepdims=True)
    return x * lax.rsqrt(ms + NORM_EPS) * g


def _sigmoid(x):
    return 1.0 / (1.0 + jnp.exp(-x))


def _norm_inproj_kernel(h_ref, g_ref, w_ref, o_ref):
    xn = _rms(h_ref[...], g_ref[...]).astype(BF16)
    o_ref[...] = jnp.dot(xn, w_ref[...], preferred_element_type=F32)


def _norm_inproj(h, g, w):
    t, d = h.shape
    n = w.shape[1]
    tm = min(ROW_TILE, t)
    return pl.pallas_call(
        _norm_inproj_kernel,
        grid=(t // tm,),
        in_specs=[pl.BlockSpec((tm, d), lambda i: (i, 0)),
                  pl.BlockSpec((1, d), lambda i: (0, 0)),
                  pl.BlockSpec((d, n), lambda i: (0, 0))],
        out_specs=pl.BlockSpec((tm, n), lambda i: (i, 0)),
        out_shape=jax.ShapeDtypeStruct((t, n), F32),
        compiler_params=_cparams("parallel"),
        name="norm_inproj",
    )(h, g.reshape(1, d), w)


def _s5_operators(lam_re, lam_im, log_dt, b_re, b_im, c_re, c_im, n_scan):
    lc = S5_CHUNK
    hi = lax.Precision.HIGHEST
    lr, li = lam_re.astype(F32), lam_im.astype(F32)
    dt = jnp.exp(log_dt.astype(F32))[:, None]
    zr, zi = lr * dt, li * dt
    taus = jnp.arange(lc + 1, dtype=F32)[:, None, None]
    mag = jnp.exp(zr[None] * taus)
    pwr, pwi = mag * jnp.cos(zi[None] * taus), mag * jnp.sin(zi[None] * taus)
    nr, ni = pwr[1] - 1.0, pwi[1]
    den = lr * lr + li * li
    qr, qi = (nr * lr + ni * li) / den, (ni * lr - nr * li) / den
    br, bi = b_re.astype(F32), b_im.astype(F32)
    bbr = qr[..., None] * br - qi[..., None] * bi
    bbi = qr[..., None] * bi + qi[..., None] * br
    cr, ci = c_re.astype(F32), c_im.astype(F32)
    g_, p_ = lr.shape

    def c_times_pw(lo):
        wr, wi = pwr[lo:lo + lc, :, None, :], pwi[lo:lo + lc, :, None, :]
        return cr[None] * wr - ci[None] * wi, cr[None] * wi + ci[None] * wr

    cpr, cpi = c_times_pw(0)
    cp = jnp.concatenate([cpr, cpi], axis=-1).transpose(1, 0, 2, 3).reshape(g_, lc * S5_GROUP, 2 * p_)
    kt = jnp.matmul(cp, jnp.concatenate([bbr, -bbi], axis=1), precision=hi)
    kflat = kt.transpose(0, 2, 1)

    wr, wi = pwr[lc - 1::-1][:, :, :, None], pwi[lc - 1::-1][:, :, :, None]
    msr = (wr * bbr[None] - wi * bbi[None]).transpose(1, 0, 3, 2).reshape(g_, lc * S5_GROUP, p_)
    msi = (wr * bbi[None] + wi * bbr[None]).transpose(1, 0, 3, 2).reshape(g_, lc * S5_GROUP, p_)
    m_state = jnp.concatenate([msr, msi], axis=-1)

    c1r, c1i = c_times_pw(1)
    c1r = c1r.transpose(1, 3, 0, 2).reshape(g_, p_, lc * S5_GROUP)
    c1i = c1i.transpose(1, 3, 0, 2).reshape(g_, p_, lc * S5_GROUP)
    m_carry = jnp.concatenate([c1r, -c1i], axis=1)

    akr, aki = [pwr[lc]], [pwi[lc]]
    for _ in range(n_scan - 1):
        r, i = akr[-1], aki[-1]
        akr.append(r * r - i * i)
        aki.append(2.0 * r * i)
    akr, aki = jnp.stack(akr, axis=1), jnp.stack(aki, axis=1)
    ar = jnp.concatenate([akr, akr], axis=-1)
    ai = jnp.concatenate([-aki, aki], axis=-1)
    kpad = -(-n_scan // 8) * 8
    ar = jnp.pad(ar, ((0, 0), (0, kpad - n_scan), (0, 0)))
    ai = jnp.pad(ai, ((0, 0), (0, kpad - n_scan), (0, 0)))
    return kflat, m_state.astype(BF16), m_carry.astype(BF16), ar, ai


S5_PER_TILE = LANES // S5_GROUP


def _s5_kernel(proj_hbm, kf_ref, ms_ref, mc_ref, ar_ref, ai_ref, y_hbm, us_ref, ys_ref, mi_ref, sem_in, sem_out,
               *, n_scan):
    lc, gw, per = S5_CHUNK, S5_GROUP, S5_PER_TILE
    nch = us_ref.shape[1]
    cols = pl.ds(pl.multiple_of(pl.program_id(0) * LANES, LANES), LANES)

    def in_copy(s):
        return pltpu.make_async_copy(proj_hbm.at[:, s, cols], us_ref.at[s], sem_in.at[s])

    def out_copy(s):
        return pltpu.make_async_copy(ys_ref.at[s], y_hbm.at[:, s, cols], sem_out.at[s])

    for s in range(lc):
        in_copy(s).start()
    ys_ref[...] = jnp.zeros_like(ys_ref)
    for s in range(lc):
        in_copy(s).wait()
        if s % per:
            us_ref[s] = pltpu.roll(us_ref[s], (s % per) * gw, axis=1)

    lane_grp = lax.broadcasted_iota(I32, (1, LANES), 1) // gw
    kf_lane = lax.broadcasted_iota(I32, (gw, lc * gw), 1)

    def group(gl, carry):
        rel_grp = jnp.bitwise_and(lane_grp - gl, per - 1)
        tiles = []
        for j in range(lc // per):
            merged = us_ref[j * per]
            for k in range(1, per):
                merged = jnp.where(rel_grp == k, us_ref[j * per + k], merged)
            tiles.append(pltpu.roll(merged, jnp.bitwise_and(-gl * gw, LANES - 1), axis=1))
        u = jnp.concatenate(tiles, axis=1).astype(BF16)

        kf = kf_ref[gl]
        for s in range(lc):
            blk = kf if s == 0 else jnp.where(kf_lane >= s * gw, pltpu.roll(kf, s * gw, axis=1), 0.0)
            mi_ref[s * gw:(s + 1) * gw, :] = blk.astype(BF16)

        x = jnp.dot(u, ms_ref[gl], preferred_element_type=F32)
        row = lax.broadcasted_iota(I32, x.shape, 0)
        half = x.shape[1] // 2
        for k in range(n_scan):
            d = 1 << k
            s = jnp.where(row >= d, pltpu.roll(x, d, axis=0), 0.0)
            x = x + ar_ref[gl, k:k + 1, :] * s + ai_ref[gl, k:k + 1, :] * pltpu.roll(s, half, axis=1)
        xe = jnp.where(row >= 1, pltpu.roll(x, 1, axis=0), 0.0)
        y = jnp.dot(u, mi_ref[...], preferred_element_type=F32)
        y = y + jnp.dot(xe.astype(BF16), mc_ref[gl], preferred_element_type=F32)

        for j in range(lc // per):
            moved = pltpu.roll(y[:, j * LANES:(j + 1) * LANES], jnp.bitwise_and(gl * gw, LANES - 1), axis=1)
            for k in range(per):
                ys_ref[j * per + k] = jnp.where(rel_grp == k, moved, ys_ref[j * per + k])
        return carry

    lax.fori_loop(0, per, group, 0)
    for s in range(lc):
        if s % per:
            ys_ref[s] = pltpu.roll(ys_ref[s], LANES - (s % per) * gw, axis=1)
        out_copy(s).start()
    for s in range(lc):
        out_copy(s).wait()


def _s5_conv(proj, ops):
    kflat, m_state, m_carry, ar, ai = ops
    t, n = proj.shape
    lc = S5_CHUNK
    nch = t // lc
    n_scan = int(math.log2(nch))
    assert (1 << n_scan) == nch
    w = lc * S5_GROUP
    p2 = m_state.shape[-1]
    per = S5_PER_TILE

    def grp(shape):
        return pl.BlockSpec((per,) + shape, lambda hh: (hh, 0, 0))

    hbm = pl.BlockSpec(memory_space=pl.ANY)
    y = pl.pallas_call(
        functools.partial(_s5_kernel, n_scan=n_scan),
        grid=(S5_WIDTH // LANES,),
        in_specs=[hbm, grp((S5_GROUP, w)), grp((w, p2)), grp((p2, w)),
                  grp((ar.shape[1], p2)), grp((ai.shape[1], p2))],
        out_specs=hbm,
        out_shape=jax.ShapeDtypeStruct((nch, lc, S5_WIDTH), F32),
        scratch_shapes=[pltpu.VMEM((lc, nch, LANES), F32), pltpu.VMEM((lc, nch, LANES), F32),
                        pltpu.VMEM((w, w), BF16), pltpu.SemaphoreType.DMA((lc,)), pltpu.SemaphoreType.DMA((lc,))],
        compiler_params=_cparams("arbitrary"),
        name="s5_conv",
    )(proj.reshape(nch, lc, n), kflat, m_state, m_carry, ar, ai)
    return y.reshape(t, S5_WIDTH)


LOG2E = float(np.log2(np.e))


def _hg_level_table():
    idx = np.arange(HG_CHUNK)
    t, s = idx[:, None], idx[None, :]
    top_bit = np.floor(np.log2(np.maximum(t ^ s, 1))).astype(np.int32)
    return np.where(s < t, HG_LEVELS - 1 - top_bit, -1).astype(np.int32)


def _hg_midpoint(bc, m):
    c = bc.shape[0]
    h = m // 2
    if h >= SUBLANES:
        return jnp.concatenate([jnp.broadcast_to(bc[j * m + h - 1:j * m + h, :], (m, LANES))
                                for j in range(c // m)], axis=0)
    x3 = bc.reshape(c // SUBLANES, SUBLANES, LANES)
    sub = lax.broadcasted_iota(I32, x3.shape, 1)
    beta = None
    for j in reversed(range(SUBLANES // m)):
        row = jnp.broadcast_to(x3[:, j * m + h - 1:j * m + h, :], x3.shape)
        beta = row if beta is None else jnp.where(sub < (j + 1) * m, row, beta)
    return beta.reshape(c, LANES)


def _bf16_terms(x, n):
    terms = []
    for _ in range(n - 1):
        t = x.astype(BF16)
        terms.append(t)
        x = x - t.astype(F32)
    return terms + [x.astype(BF16)]


def _hg_chunk(fp, q, v, gt, lb, lbf, ng, lvl, st):
    c = HG_CHUNK
    prow = lax.broadcasted_iota(I32, (c, LANES), 0)
    nt = (((1,), (1,)), ((), ()))
    tn = (((0,), (0,)), ((), ()))
    en = jnp.exp(-jnp.abs(fp))
    rc = 1.0 / (1.0 + en)
    pos_f = fp >= 0.0
    sig_p = jnp.where(pos_f, rc, en * rc)
    sig_n = jnp.where(pos_f, en * rc, rc)
    f = lbf + (1.0 - lb) * sig_p
    lf = jnp.log(f)
    kc = (1.0 - lb) * sig_n
    qc = q * _sigmoid(q)
    bc = lf
    for k in range(HG_LEVELS):
        d = 1 << k
        bc = bc + jnp.where(prow >= d, pltpu.roll(bc, d, axis=0), 0.0)

    a = jnp.zeros((c, c), F32)
    for lev in range(HG_LEVELS):
        m = c >> lev
        upper = jnp.bitwise_and(prow, m - 1) >= m // 2
        z = jnp.where(upper, qc, kc)
        if m == 2:
            zw = jnp.where(upper, z * f, z)
        else:
            dlt = bc - _hg_midpoint(bc, m)
            zw = z * jnp.exp2(dlt * jnp.where(upper, LOG2E, -LOG2E))
        zw = zw.astype(BF16)
        s = lax.dot_general(zw, zw, nt, preferred_element_type=F32)
        a = jnp.where(lvl == lev, s, a)
    vb = v.astype(BF16)
    o = jnp.dot(a.astype(BF16), vb, preferred_element_type=F32)
    o = o + jnp.sum(qc * kc, axis=-1, keepdims=True) * v
    o = o + lax.dot_general((qc * jnp.exp(bc)).astype(BF16), st.astype(BF16), nt,
                            preferred_element_type=F32)
    bl = bc[c - 1:c, :]
    khat = (kc * jnp.exp(bl - bc)).astype(BF16)
    st = st * jnp.exp(bl) + lax.dot_general(vb, khat, tn, preferred_element_type=F32)
    return _rms(o, ng) * (gt * _sigmoid(gt)), st


def _hgrn_kernel(q_ref, f_ref, i_ref, gt_ref, lb_ref, lbf_ref, ng_ref, lvl_ref, o_ref, st_ref):
    c = HG_CHUNK

    @pl.when(pl.program_id(0) == 0)
    def _():
        st_ref[...] = jnp.zeros_like(st_ref)

    def chunk(n, carry):
        rows = pl.ds(pl.multiple_of(n * c, c), c)
        for hd in range(HG_HEADS):
            cols = slice(hd * LANES, (hd + 1) * LANES)
            o, st = _hg_chunk(f_ref[rows, cols], q_ref[rows, cols], i_ref[rows, cols], gt_ref[rows, cols],
                              lb_ref[:, cols], lbf_ref[:, cols], ng_ref[:, cols], lvl_ref[...], st_ref[hd])
            o_ref[rows, cols] = o
            st_ref[hd] = st
        return carry

    lax.fori_loop(0, q_ref.shape[0] // c, chunk, 0)


def _hgrn(proj, lb, norm_g):
    t = proj.shape[0]
    tb = min(HG_TBLOCK, t)
    lbh = jnp.clip(lb.astype(F32), 0.0, 1.0 - 1e-6).reshape(1, HG_WIDTH)
    lbf = jnp.maximum(lbh, LB_FLOOR)
    ng = norm_g.astype(F32).reshape(1, HG_WIDTH)
    lvl = jnp.asarray(_hg_level_table())
    cb = COL_HG // HG_WIDTH

    def col(k):
        return pl.BlockSpec((tb, HG_WIDTH), lambda i, k=k: (i, cb + k))

    vec = pl.BlockSpec((1, HG_WIDTH), lambda i: (0, 0))
    return pl.pallas_call(
        _hgrn_kernel,
        grid=(t // tb,),
        in_specs=[col(0), col(1), col(2), col(3), vec, vec, vec,
                  pl.BlockSpec(lvl.shape, lambda i: (0, 0))],
        out_specs=pl.BlockSpec((tb, HG_WIDTH), lambda i: (i, 0)),
        out_shape=jax.ShapeDtypeStruct((t, HG_WIDTH), F32),
        scratch_shapes=[pltpu.VMEM((HG_HEADS, LANES, LANES), F32)],
        compiler_params=_cparams("arbitrary"),
        name="hgrn2",
    )(proj, proj, proj, proj, lbh, lbf, ng, lvl)


def _mix_out_kernel(ys_ref, u_ref, cb_ref, cc_ref, cv_ref, hg_ref, h_ref,
                    d_ref, glu_ref, sn_ref, cw_ref, cn_ref, wo_ref, o_ref, carry_ref):
    @pl.when(pl.program_id(0) == 0)
    def _():
        carry_ref[...] = jnp.zeros_like(carry_ref)

    y = ys_ref[...] + d_ref[...] * u_ref[...]
    y = jax.nn.gelu(y)
    y = y * _sigmoid(jnp.dot(y.astype(BF16), glu_ref[...], preferred_element_type=F32))
    y_s5 = _rms(y, sn_ref[...])

    z = cc_ref[...] * cv_ref[...]
    tm = z.shape[0]
    row = lax.broadcasted_iota(I32, z.shape, 0)
    p1 = carry_ref[7:8, :]
    p2 = carry_ref[6:7, :]
    z1 = jnp.where(row == 0, p1, pltpu.roll(z, 1, axis=0))
    z2 = jnp.where(row == 0, p2, jnp.where(row == 1, p1, pltpu.roll(z, 2, axis=0)))
    carry_ref[...] = z[tm - 8:tm, :]
    yc = cb_ref[...] * (z2 * cw_ref[0:1, :] + z1 * cw_ref[1:2, :] + z * cw_ref[2:3, :])
    y_cv = _rms(yc, cn_ref[...])

    acc = h_ref[...]
    hg_row = S5_WIDTH + CONV_WIDTH
    acc = acc + jnp.dot(y_s5.astype(BF16), wo_ref[0:S5_WIDTH, :], preferred_element_type=F32)
    acc = acc + jnp.dot(y_cv.astype(BF16), wo_ref[S5_WIDTH:hg_row, :], preferred_element_type=F32)
    acc = acc + jnp.dot(hg_ref[...].astype(BF16), wo_ref[hg_row:, :], preferred_element_type=F32)
    o_ref[...] = acc


def _mix_out(ys5, proj, yhg, h, s5_d, s5_glu, s5_norm, conv_w, conv_norm, w_out):
    t, d = h.shape
    tm = min(ROW_TILE, t)
    cw = jnp.pad(conv_w.astype(F32), ((0, 8 - CONV_K), (0, 0)))
    wq = S5_WIDTH

    def rowblk(width, colblk):
        return pl.BlockSpec((tm, width), lambda i, c=colblk: (i, c))

    def full(shape):
        return pl.BlockSpec(shape, lambda i: (0,) * len(shape))

    return pl.pallas_call(
        _mix_out_kernel,
        grid=(t // tm,),
        in_specs=[rowblk(wq, 0), rowblk(wq, 0), rowblk(wq, 1), rowblk(wq, 2), rowblk(wq, 3),
                  rowblk(HG_WIDTH, 0), rowblk(d, 0),
                  full((1, wq)), full((wq, wq)), full((1, wq)), full((8, wq)), full((1, wq)),
                  full(w_out.shape)],
        out_specs=rowblk(d, 0),
        out_shape=jax.ShapeDtypeStruct((t, d), F32),
        scratch_shapes=[pltpu.VMEM((8, CONV_WIDTH), F32)],
        compiler_params=_cparams("arbitrary"),
        name="mix_out",
    )(ys5, proj, proj, proj, proj, yhg, h,
      s5_d.astype(F32).reshape(1, wq), s5_glu, s5_norm.astype(F32).reshape(1, wq), cw,
      conv_norm.astype(F32).reshape(1, wq), w_out)


def _ffn_kernel(h_ref, g_ref, w1_ref, w3_ref, w2_ref, o_ref, hn_ref, acc_ref):
    j = pl.program_id(1)

    @pl.when(j == 0)
    def _():
        x = h_ref[...]
        hn_ref[...] = _rms(x, g_ref[...]).astype(BF16)
        acc_ref[...] = x

    hn = hn_ref[...]
    a = jnp.dot(hn, w1_ref[...].astype(BF16), preferred_element_type=F32)
    b = jnp.dot(hn, w3_ref[...].astype(BF16), preferred_element_type=F32)
    gact = (a * _sigmoid(a) * b).astype(BF16)
    acc_ref[...] += jnp.dot(gact, w2_ref[...].astype(BF16), preferred_element_type=F32)

    @pl.when(j == pl.num_programs(1) - 1)
    def _():
        o_ref[...] = acc_ref[...]


def _ffn(h, g, w1, w3, w2, layer):
    t, d = h.shape
    f = w1.shape[2]
    tm = min(FFN_ROW_TILE, t)
    tf = FFN_COL_TILE
    return pl.pallas_call(
        _ffn_kernel,
        grid=(t // tm, f // tf),
        in_specs=[pl.BlockSpec((tm, d), lambda i, j: (i, 0)),
                  pl.BlockSpec((1, d), lambda i, j: (0, 0)),
                  pl.BlockSpec((None, d, tf), lambda i, j: (layer, 0, j)),
                  pl.BlockSpec((None, d, tf), lambda i, j: (layer, 0, j)),
                  pl.BlockSpec((None, tf, d), lambda i, j: (layer, j, 0))],
        out_specs=pl.BlockSpec((tm, d), lambda i, j: (i, 0)),
        out_shape=jax.ShapeDtypeStruct((t, d), F32),
        scratch_shapes=[pltpu.VMEM((tm, d), BF16), pltpu.VMEM((tm, d), F32)],
        compiler_params=_cparams("parallel", "arbitrary"),
        name="ffn_swiglu",
    )(h, g.astype(F32).reshape(1, d), w1, w3, w2)


def _router_kernel(h_ref, g_ref, r_ref, hn_ref, info_ref, info_t_ref, before_ref, total_ref, cnt_ref, *, n_exp):
    @pl.when(pl.program_id(0) == 0)
    def _():
        cnt_ref[...] = jnp.zeros_like(cnt_ref)

    xn = _rms(h_ref[...], g_ref[...])
    x_hi, x_lo = _bf16_terms(xn, 2)
    hn_ref[...] = x_hi
    r_hi, r_lo = _bf16_terms(r_ref[...], 2)
    logits = (jnp.dot(x_hi, r_hi, preferred_element_type=F32) + jnp.dot(x_lo, r_hi, preferred_element_type=F32)
              + jnp.dot(x_hi, r_lo, preferred_element_type=F32))
    tm = logits.shape[0]
    lane = lax.broadcasted_iota(I32, logits.shape, 1)
    lanef = lane.astype(F32)
    neg = jnp.float32(-jnp.inf)
    lg = jnp.where(lane < n_exp, logits, neg)
    m1 = jnp.max(lg, axis=-1, keepdims=True)
    i1 = jnp.min(jnp.where(lg == m1, lanef, float(LANES)), axis=-1, keepdims=True)
    oh1 = lanef == i1
    lg2 = jnp.where(oh1, neg, lg)
    m2 = jnp.max(lg2, axis=-1, keepdims=True)
    i2 = jnp.min(jnp.where(lg2 == m2, lanef, float(LANES)), axis=-1, keepdims=True)
    oh2 = lanef == i2
    ex = jnp.exp(m2 - m1)
    g1 = 1.0 / (1.0 + ex)
    g2 = ex * g1
    chosen = jnp.where(oh1, 1.0, jnp.where(oh2, 1.0, 0.0))
    ri = lax.broadcasted_iota(I32, (tm, tm), 0)
    ci = lax.broadcasted_iota(I32, (tm, tm), 1)
    tri = jnp.where(ri > ci, 1.0, 0.0).astype(BF16)
    before = cnt_ref[...]
    cexcl = jnp.dot(tri, chosen.astype(BF16), preferred_element_type=F32) + before
    rank1 = jnp.sum(jnp.where(oh1, cexcl, 0.0), axis=-1, keepdims=True)
    rank2 = jnp.sum(jnp.where(oh2, cexcl, 0.0), axis=-1, keepdims=True)
    info = jnp.where(lane == 0, i1, jnp.where(lane == 1, i2, jnp.where(lane == 2, g1, jnp.where(
        lane == 3, g2, jnp.where(lane == 4, rank1, jnp.where(lane == 5, rank2, 0.0))))))
    info_ref[...] = info
    info_t_ref[...] = info.T[0:SUBLANES, :]
    before_ref[0] = jnp.broadcast_to(before, before_ref.shape[1:])
    total = before + jnp.sum(chosen, axis=0, keepdims=True)
    cnt_ref[...] = total
    total_ref[...] = jnp.broadcast_to(total, total_ref.shape)


def _router(h, g, router):
    t, d = h.shape
    n_exp = router.shape[1]
    tm = min(TOK_WIN, t)
    ntw = t // tm
    rp = jnp.pad(router.astype(F32), ((0, 0), (0, LANES - n_exp)))
    return pl.pallas_call(
        functools.partial(_router_kernel, n_exp=n_exp),
        grid=(ntw,),
        in_specs=[pl.BlockSpec((tm, d), lambda i: (i, 0)),
                  pl.BlockSpec((1, d), lambda i: (0, 0)),
                  pl.BlockSpec((d, LANES), lambda i: (0, 0))],
        out_specs=[pl.BlockSpec((tm, d), lambda i: (i, 0)),
                   pl.BlockSpec((tm, LANES), lambda i: (i, 0)),
                   pl.BlockSpec((SUBLANES, tm), lambda i: (0, i)),
                   pl.BlockSpec((1, 8, LANES), lambda i: (i, 0, 0)),
                   pl.BlockSpec((8, LANES), lambda i: (0, 0))],
        out_shape=[jax.ShapeDtypeStruct((t, d), BF16),
                   jax.ShapeDtypeStruct((t, LANES), F32),
                   jax.ShapeDtypeStruct((SUBLANES, t), F32),
                   jax.ShapeDtypeStruct((ntw, 8, LANES), F32),
                   jax.ShapeDtypeStruct((8, LANES), F32)],
        scratch_shapes=[pltpu.VMEM((1, LANES), F32)],
        compiler_params=_cparams("arbitrary"),
        name="moe_router",
    )(h, g.astype(F32).reshape(1, d), rp)


def _gather_kernel(off_ref, nw_ref, wl_ref, tot_ref, dest_ref, hn_hbm, xs_ref, buf_ref, sem, acc_ref):
    b = pl.program_id(0)
    rows = acc_ref.shape[0]
    n_buf, win = buf_ref.shape[0], buf_ref.shape[1]
    n = nw_ref[b]
    q0 = off_ref[b]
    total = tot_ref[0]

    def copy(q):
        w = wl_ref[q]
        s = lax.rem(q, n_buf)
        return pltpu.make_async_copy(hn_hbm.at[pl.ds(pl.multiple_of(w * win, win), win), :], buf_ref.at[s],
                                     sem.at[s])

    @pl.when(b == 0)
    def _():
        for q in range(GATHER_AHEAD):
            @pl.when(q < total)
            def _(q=q):
                copy(q).start()

    acc_ref[...] = jnp.zeros_like(acc_ref)
    slot = b * rows + lax.broadcasted_iota(I32, (rows, win), 0)

    def gathered(q):
        @pl.when(q + GATHER_AHEAD < total)
        def _():
            copy(q + GATHER_AHEAD).start()

        col = pl.ds(pl.multiple_of(wl_ref[q] * win, win), win)
        d1 = dest_ref[0:1, col]
        d2 = dest_ref[1:2, col]
        hit = jnp.where(d1 == slot, 1.0, jnp.where(d2 == slot, 1.0, 0.0)).astype(BF16)
        copy(q).wait()
        return jnp.dot(hit, buf_ref[lax.rem(q, n_buf)], preferred_element_type=F32)

    def pair(j, carry):
        q = q0 + 2 * j
        acc_ref[...] += gathered(q) + gathered(q + 1)
        return carry

    lax.fori_loop(0, lax.shift_right_logical(n, 1), pair, 0)

    @pl.when(jnp.bitwise_and(n, 1) == 1)
    def _():
        acc_ref[...] += gathered(q0 + n - 1)

    xs_ref[...] = acc_ref[...].astype(BF16)


def _gather(hn, dest, off, nw, wlist, total, n_slots):
    t, d = hn.shape
    return pl.pallas_call(
        _gather_kernel,
        grid_spec=pltpu.PrefetchScalarGridSpec(
            num_scalar_prefetch=4,
            grid=(n_slots // GATHER_TILE,),
            in_specs=[pl.BlockSpec(dest.shape, lambda b, *_: (0, 0)),
                      pl.BlockSpec(memory_space=pl.ANY)],
            out_specs=pl.BlockSpec((GATHER_TILE, d), lambda b, *_: (b, 0)),
            scratch_shapes=[pltpu.VMEM((GATHER_BUFS, TOK_WIN, d), BF16),
                            pltpu.SemaphoreType.DMA((GATHER_BUFS,)),
                            pltpu.VMEM((GATHER_TILE, d), F32)]),
        out_shape=jax.ShapeDtypeStruct((n_slots, d), BF16),
        compiler_params=_cparams("arbitrary"),
        name="moe_gather",
    )(off, nw, wlist, total, dest, hn)


def _expert_kernel(te_ref, tv_ref, x_ref, w1_ref, w3_ref, w2_ref, y_ref, acc_ref):
    b = pl.program_id(0)
    j = pl.program_id(1)

    @pl.when(j == 0)
    def _():
        acc_ref[...] = jnp.zeros_like(acc_ref)

    @pl.when(tv_ref[b] == 1)
    def _():
        x = x_ref[...]
        a = jnp.dot(x, w1_ref[...].astype(BF16), preferred_element_type=F32)
        c = jnp.dot(x, w3_ref[...].astype(BF16), preferred_element_type=F32)
        gact = (a * _sigmoid(a) * c).astype(BF16)
        acc_ref[...] += jnp.dot(gact, w2_ref[...].astype(BF16), preferred_element_type=F32)

    @pl.when(j == pl.num_programs(1) - 1)
    def _():
        y_ref[...] = acc_ref[...].astype(BF16)


def _experts(xs, tile_e, tile_v, w1, w3, w2, layer):
    ns, d = xs.shape
    f = w1.shape[3]
    tf = MOE_COL_TILE
    nj = f // tf
    n_tiles = ns // MOE_TILE

    def col(b, j, tv):
        return j * tv[b] + (nj - 1) * (1 - tv[b])

    return pl.pallas_call(
        _expert_kernel,
        grid_spec=pltpu.PrefetchScalarGridSpec(
            num_scalar_prefetch=2,
            grid=(n_tiles, nj),
            in_specs=[pl.BlockSpec((MOE_TILE, d), lambda b, j, te, tv: (b, 0)),
                      pl.BlockSpec((None, None, d, tf), lambda b, j, te, tv: (layer, te[b], 0, col(b, j, tv))),
                      pl.BlockSpec((None, None, d, tf), lambda b, j, te, tv: (layer, te[b], 0, col(b, j, tv))),
                      pl.BlockSpec((None, None, tf, d), lambda b, j, te, tv: (layer, te[b], col(b, j, tv), 0))],
            out_specs=pl.BlockSpec((MOE_TILE, d), lambda b, j, te, tv: (b, 0)),
            scratch_shapes=[pltpu.VMEM((MOE_TILE, d), F32)]),
        out_shape=jax.ShapeDtypeStruct((ns, d), BF16),
        compiler_params=_cparams("arbitrary", "arbitrary"),
        name="moe_experts",
    )(tile_e, tile_v, xs, w1, w3, w2)


def _combine_kernel(ws_ref, po_ref, ex_ref, info_ref, h_ref, fin_ref, yb_hbm, o_ref, buf_ref, sem, *,
                    n_exp, final_norm):
    i = pl.program_id(0)
    cur = lax.rem(i, 2)

    def copy(step, e, s):
        start = pl.multiple_of(ws_ref[step * n_exp + e], SLOT_ALIGN)
        return pltpu.make_async_copy(yb_hbm.at[pl.ds(start, SLOT_WIN), :], buf_ref.at[s, e], sem.at[s, e])

    @pl.when(i == 0)
    def _():
        for e in range(n_exp):
            copy(0, e, 0).start()

    @pl.when(i + 1 < pl.num_programs(0))
    def _():
        for e in range(n_exp):
            copy(i + 1, e, 1 - cur).start()

    info = info_ref[...]
    tm = info.shape[0]
    e1, e2 = info[:, 0:1], info[:, 1:2]
    g1, g2 = info[:, 2:3], info[:, 3:4]
    r1, r2 = info[:, 4:5], info[:, 5:6]
    lanef = lax.broadcasted_iota(I32, (tm, TOK_WIN), 1).astype(F32)
    g1b = jnp.broadcast_to(g1, (tm, TOK_WIN))
    g2b = jnp.broadcast_to(g2, (tm, TOK_WIN))

    def weights(e, shift, lo_lane):
        off = (ws_ref[i * n_exp + e] - po_ref[e] + shift).astype(F32)
        k1 = jnp.where(e1 == float(e), r1 - off, -1.0)
        k2 = jnp.where(e2 == float(e), r2 - off, -1.0)
        k1 = jnp.where(k1 >= float(lo_lane), k1, -1.0)
        k2 = jnp.where(k2 >= float(lo_lane), k2, -1.0)
        return jnp.where(k1 == lanef, g1b, jnp.where(k2 == lanef, g2b, 0.0)).astype(BF16)

    acc = h_ref[...]
    for e in range(n_exp):
        pt = weights(e, 0, 0)
        copy(i, e, cur).wait()
        acc = acc + jnp.dot(pt, buf_ref[cur, e, 0:TOK_WIN, :], preferred_element_type=F32)
    o_ref[...] = acc

    tail = SLOT_WIN - TOK_WIN
    for e in range(n_exp):
        @pl.when(ex_ref[i * n_exp + e] == 1)
        def _(e=e):
            pt = weights(e, tail, TOK_WIN - tail)
            o_ref[...] += jnp.dot(pt, buf_ref[cur, e, tail:SLOT_WIN, :], preferred_element_type=F32)

    if final_norm:
        o_ref[...] = _rms(o_ref[...], fin_ref[...])


def _combine(info, h, yb, win_start, pstart, extra, fin_g, final_norm):
    t, d = h.shape
    n_exp = pstart.shape[0]
    tm = min(TOK_WIN, t)
    return pl.pallas_call(
        functools.partial(_combine_kernel, n_exp=n_exp, final_norm=final_norm),
        grid_spec=pltpu.PrefetchScalarGridSpec(
            num_scalar_prefetch=3,
            grid=(t // tm,),
            in_specs=[pl.BlockSpec((tm, LANES), lambda i, ws, po, ex: (i, 0)),
                      pl.BlockSpec((tm, d), lambda i, ws, po, ex: (i, 0)),
                      pl.BlockSpec((1, d), lambda i, ws, po, ex: (0, 0)),
                      pl.BlockSpec(memory_space=pl.ANY)],
            out_specs=pl.BlockSpec((tm, d), lambda i, ws, po, ex: (i, 0)),
            scratch_shapes=[pltpu.VMEM((2, n_exp, SLOT_WIN, d), BF16),
                            pltpu.SemaphoreType.DMA((2, n_exp))]),
        out_shape=jax.ShapeDtypeStruct((t, d), F32),
        compiler_params=_cparams("arbitrary"),
        name="moe_combine",
    )(win_start, pstart, extra, info, h, fin_g.astype(F32).reshape(1, d), yb)


def _count_le(sorted_vals, x):
    return jnp.sum((sorted_vals[None, :] <= x[:, None]).astype(I32), axis=1)


def _moe(h, g, router, w1, w3, w2, layer, fin_g, final_norm):
    t, d = h.shape
    n_exp = router.shape[1]
    tw = min(TOK_WIN, t)
    ntw = t // tw
    hn, info, info_t, before, total = _router(h, g, router)

    counts = total[0, :n_exp].astype(I32)
    padded = (counts + MOE_TILE - 1) // MOE_TILE * MOE_TILE
    pend = jnp.cumsum(padded)
    pstart = pend - padded
    n_tiles = (t * TOP_K) // MOE_TILE + n_exp
    n_slots = n_tiles * MOE_TILE
    tile0 = jnp.arange(n_tiles, dtype=I32) * MOE_TILE
    tile_e = jnp.minimum(_count_le(pend, tile0), n_exp - 1)
    tile_v = (tile0 < pend[-1]).astype(I32)
    cum = jnp.concatenate([before[:, 0, :n_exp], total[0:1, :n_exp]], axis=0).astype(I32)

    n_gb = n_slots // GATHER_TILE
    gb0 = jnp.arange(n_gb, dtype=I32) * GATHER_TILE
    gb_e = jnp.minimum(_count_le(pend, gb0), n_exp - 1)
    r0 = gb0 - pstart[gb_e]
    cum_b = cum[:, gb_e]
    ilo = jnp.sum((cum_b[1:] <= r0[None, :]).astype(I32), axis=0)
    ihi = jnp.sum((cum_b[:-1] < (r0 + GATHER_TILE)[None, :]).astype(I32), axis=0) - 1
    n_work = jnp.where(gb0 < pend[-1], jnp.maximum(ihi - ilo + 1, 0), 0).astype(I32)
    ends = jnp.cumsum(n_work)
    offs = (ends - n_work).astype(I32)
    entry = jnp.arange(n_exp * ntw + n_gb, dtype=I32)
    entry_blk = jnp.minimum(_count_le(ends, entry), n_gb - 1)
    wlist = jnp.clip(ilo[entry_blk] + entry - offs[entry_blk], 0, ntw - 1).astype(I32)

    e_t = info_t[0:2].astype(I32)
    slot_t = jnp.sum(jnp.where(e_t[None] == jnp.arange(n_exp, dtype=I32)[:, None, None], pstart[:, None, None], 0),
                     axis=0) + info_t[4:6].astype(I32)
    dest = jnp.concatenate([slot_t, jnp.full((SUBLANES - TOP_K, t), -1, I32)], axis=0)

    xs = _gather(hn, dest, offs, n_work, wlist, ends[-1:].astype(I32), n_slots)
    yb = _experts(xs, tile_e, tile_v, w1, w3, w2, layer)

    lo = pstart[None, :] + cum[:-1]
    win_start = jnp.minimum(lo // SLOT_ALIGN * SLOT_ALIGN, n_slots - SLOT_WIN).astype(I32)
    extra = ((lo - win_start + cum[1:] - cum[:-1]) > TOK_WIN).astype(I32)
    return _combine(info, h, yb, win_start.reshape(-1), pstart.astype(I32), extra.reshape(-1), fin_g, final_norm)


def _final_norm_kernel(h_ref, g_ref, o_ref):
    o_ref[...] = _rms(h_ref[...], g_ref[...])


def _final_norm(h, g):
    t, d = h.shape
    tm = min(ROW_TILE, t)
    return pl.pallas_call(
        _final_norm_kernel,
        grid=(t // tm,),
        in_specs=[pl.BlockSpec((tm, d), lambda i: (i, 0)), pl.BlockSpec((1, d), lambda i: (0, 0))],
        out_specs=pl.BlockSpec((tm, d), lambda i: (i, 0)),
        out_shape=jax.ShapeDtypeStruct((t, d), F32),
        compiler_params=_cparams("parallel"),
        name="final_norm",
    )(h, g.astype(F32).reshape(1, d))


def kernel(x, attn_norm, ffn_norm, final_norm, w_in, w_out, s5_lambda_re, s5_lambda_im, s5_log_dt, s5_b_re, s5_b_im, s5_c_re, s5_c_im, s5_d, s5_glu, s5_out_norm, conv_w, conv_out_norm, hg_lower_bounds, hg_out_norm, ffn_w1, ffn_w3, ffn_w2, moe_router, moe_w1, moe_w3, moe_w2):
    bsz, seq, d = x.shape
    depth = w_in.shape[0]
    t = bsz * seq
    assert bsz == 1, "token mixers are written for a single sequence"
    lb_soft = jax.nn.softmax(hg_lower_bounds.astype(F32), axis=0)
    lb_all = jnp.cumsum(lb_soft, axis=0) - lb_soft[0]
    n_scan = int(math.log2(t // S5_CHUNK))
    s5_ops = jax.vmap(functools.partial(_s5_operators, n_scan=n_scan))(
        s5_lambda_re, s5_lambda_im, s5_log_dt, s5_b_re, s5_b_im, s5_c_re, s5_c_im)
    h = x.reshape(t, d).astype(F32)
    for l in range(depth):
        proj = _norm_inproj(h, attn_norm[l].astype(F32), w_in[l].astype(BF16))
        ys5 = _s5_conv(proj, [op[l] for op in s5_ops])
        yhg = _hgrn(proj, lb_all[l], hg_out_norm[l])
        h = _mix_out(ys5, proj, yhg, h, s5_d[l], s5_glu[l].astype(BF16), s5_out_norm[l],
                     conv_w[l], conv_out_norm[l], w_out[l].astype(BF16))
        j = l // 2
        if l % 2 == 0:
            h = _ffn(h, ffn_norm[l], ffn_w1, ffn_w3, ffn_w2, j)
        else:
            h = _moe(h, ffn_norm[l], moe_router[j], moe_w1, moe_w3, moe_w2, j, final_norm, l == depth - 1)
    if depth % 2 == 1:
        h = _final_norm(h, final_norm)
    return h.reshape(bsz, seq, d)
```

```python
import functools
import math

import numpy as np
import jax
import jax.numpy as jnp
from jax import lax
from jax.experimental import pallas as pl
from jax.experimental.pallas import tpu as pltpu

F32 = jnp.float32
BF16 = jnp.bfloat16
I32 = jnp.int32

NORM_EPS = 1e-6
LB_FLOOR = 1e-30
TOP_K = 2

S5_WIDTH = 256
S5_GROUP = 16
S5_STATE = 64
CONV_WIDTH = 256
CONV_K = 3
HG_WIDTH = 512
HG_HEAD_DIM = 128
HG_HEADS = HG_WIDTH // HG_HEAD_DIM
LANES = 128
SUBLANES = 8
COL_CONV = S5_WIDTH
COL_HG = S5_WIDTH + 3 * CONV_WIDTH

VMEM_LIMIT = 56 * 1024 * 1024
S5_CHUNK = 32
HG_CHUNK = 128
HG_LEVELS = int(math.log2(HG_CHUNK))
HG_TBLOCK = 512
ROW_TILE = 512
FFN_ROW_TILE = 1024
FFN_COL_TILE = 512
MOE_TILE = 1024
MOE_COL_TILE = 512
GATHER_TILE = 256
GATHER_AHEAD = 4
GATHER_BUFS = GATHER_AHEAD + 2
TOK_WIN = 256
SLOT_ALIGN = 16
SLOT_WIN = TOK_WIN + SLOT_ALIGN
NARROW_WIN = 128


def _cparams(*sem):
    return pltpu.CompilerParams(dimension_semantics=sem, vmem_limit_bytes=VMEM_LIMIT)


def _rms(x, g):
    ms = jnp.mean(x * x, axis=-1, keepdims=True)
    return x * lax.rsqrt(ms + NORM_EPS) * g


def _sigmoid(x):
    return 1.0 / (1.0 + jnp.exp(-x))


def _norm_inproj_kernel(h_ref, g_ref, w_ref, o_ref):
    xn = _rms(h_ref[...], g_ref[...]).astype(BF16)
    o_ref[...] = jnp.dot(xn, w_ref[...], preferred_element_type=F32)


def _norm_inproj(h, g, w):
    t, d = h.shape
    n = w.shape[1]
    tm = min(ROW_TILE, t)
    return pl.pallas_call(
        _norm_inproj_kernel,
        grid=(t // tm,),
        in_specs=[pl.BlockSpec((tm, d), lambda i: (i, 0)),
                  pl.BlockSpec((1, d), lambda i: (0, 0)),
                  pl.BlockSpec((d, n), lambda i: (0, 0))],
        out_specs=pl.BlockSpec((tm, n), lambda i: (i, 0)),
        out_shape=jax.ShapeDtypeStruct((t, n), F32),
        compiler_params=_cparams("parallel"),
        name="norm_inproj",
    )(h, g.reshape(1, d), w)


def _s5_operators(lam_re, lam_im, log_dt, b_re, b_im, c_re, c_im, n_scan):
    lc = S5_CHUNK
    hi = lax.Precision.HIGHEST
    lr, li = lam_re.astype(F32), lam_im.astype(F32)
    dt = jnp.exp(log_dt.astype(F32))[:, None]
    zr, zi = lr * dt, li * dt
    taus = jnp.arange(lc + 1, dtype=F32)[:, None, None]
    mag = jnp.exp(zr[None] * taus)
    pwr, pwi = mag * jnp.cos(zi[None] * taus), mag * jnp.sin(zi[None] * taus)
    nr, ni = pwr[1] - 1.0, pwi[1]
    den = lr * lr + li * li
    qr, qi = (nr * lr + ni * li) / den, (ni * lr - nr * li) / den
    br, bi = b_re.astype(F32), b_im.astype(F32)
    bbr = qr[..., None] * br - qi[..., None] * bi
    bbi = qr[..., None] * bi + qi[..., None] * br
    cr, ci = c_re.astype(F32), c_im.astype(F32)
    g_, p_ = lr.shape

    def c_times_pw(lo):
        wr, wi = pwr[lo:lo + lc, :, None, :], pwi[lo:lo + lc, :, None, :]
        return cr[None] * wr - ci[None] * wi, cr[None] * wi + ci[None] * wr

    cpr, cpi = c_times_pw(0)
    cp = jnp.concatenate([cpr, cpi], axis=-1).transpose(1, 0, 2, 3).reshape(g_, lc * S5_GROUP, 2 * p_)
    kt = jnp.matmul(cp, jnp.concatenate([bbr, -bbi], axis=1), precision=hi)
    kflat = kt.transpose(0, 2, 1)

    wr, wi = pwr[lc - 1::-1][:, :, :, None], pwi[lc - 1::-1][:, :, :, None]
    msr = (wr * bbr[None] - wi * bbi[None]).transpose(1, 0, 3, 2).reshape(g_, lc * S5_GROUP, p_)
    msi = (wr * bbi[None] + wi * bbr[None]).transpose(1, 0, 3, 2).reshape(g_, lc * S5_GROUP, p_)
    m_state = jnp.concatenate([msr, msi], axis=-1)

    c1r, c1i = c_times_pw(1)
    c1r = c1r.transpose(1, 3, 0, 2).reshape(g_, p_, lc * S5_GROUP)
    c1i = c1i.transpose(1, 3, 0, 2).reshape(g_, p_, lc * S5_GROUP)
    m_carry = jnp.concatenate([c1r, -c1i], axis=1)

    akr, aki = [pwr[lc]], [pwi[lc]]
    for _ in range(n_scan - 1):
        r, i = akr[-1], aki[-1]
        akr.append(r * r - i * i)
        aki.append(2.0 * r * i)
    akr, aki = jnp.stack(akr, axis=1), jnp.stack(aki, axis=1)
    ar = jnp.concatenate([akr, akr], axis=-1)
    ai = jnp.concatenate([-aki, aki], axis=-1)
    kpad = -(-n_scan // 8) * 8
    ar = jnp.pad(ar, ((0, 0), (0, kpad - n_scan), (0, 0)))
    ai = jnp.pad(ai, ((0, 0), (0, kpad - n_scan), (0, 0)))
    return kflat, m_state.astype(BF16), m_carry.astype(BF16), ar, ai


S5_PER_TILE = LANES // S5_GROUP


def _s5_kernel(proj_hbm, kf_ref, ms_ref, mc_ref, ar_ref, ai_ref, y_hbm, us_ref, ys_ref, mi_ref, sem_in, sem_out,
               *, n_scan):
    lc, gw, per = S5_CHUNK, S5_GROUP, S5_PER_TILE
    nch = us_ref.shape[1]
    cols = pl.ds(pl.multiple_of(pl.program_id(0) * LANES, LANES), LANES)

    def in_copy(s):
        return pltpu.make_async_copy(proj_hbm.at[:, s, cols], us_ref.at[s], sem_in.at[s])

    def out_copy(s):
        return pltpu.make_async_copy(ys_ref.at[s], y_hbm.at[:, s, cols], sem_out.at[s])

    for s in range(lc):
        in_copy(s).start()
    ys_ref[...] = jnp.zeros_like(ys_ref)
    for s in range(lc):
        in_copy(s).wait()
        if s % per:
            us_ref[s] = pltpu.roll(us_ref[s], (s % per) * gw, axis=1)

    lane_grp = lax.broadcasted_iota(I32, (1, LANES), 1) // gw
    kf_lane = lax.broadcasted_iota(I32, (gw, lc * gw), 1)

    def group(gl, carry):
        rel_grp = jnp.bitwise_and(lane_grp - gl, per - 1)
        tiles = []
        for j in range(lc // per):
            merged = us_ref[j * per]
            for k in range(1, per):
                merged = jnp.where(rel_grp == k, us_ref[j * per + k], merged)
            tiles.append(pltpu.roll(merged, jnp.bitwise_and(-gl * gw, LANES - 1), axis=1))
        u = jnp.concatenate(tiles, axis=1).astype(BF16)

        kf = kf_ref[gl]
        for s in range(lc):
            blk = kf if s == 0 else jnp.where(kf_lane >= s * gw, pltpu.roll(kf, s * gw, axis=1), 0.0)
            mi_ref[s * gw:(s + 1) * gw, :] = blk.astype(BF16)

        x = jnp.dot(u, ms_ref[gl], preferred_element_type=F32)
        row = lax.broadcasted_iota(I32, x.shape, 0)
        half = x.shape[1] // 2
        for k in range(n_scan):
            d = 1 << k
            s = jnp.where(row >= d, pltpu.roll(x, d, axis=0), 0.0)
            x = x + ar_ref[gl, k:k + 1, :] * s + ai_ref[gl, k:k + 1, :] * pltpu.roll(s, half, axis=1)
        xe = jnp.where(row >= 1, pltpu.roll(x, 1, axis=0), 0.0)
        y = jnp.dot(u, mi_ref[...], preferred_element_type=F32)
        y = y + jnp.dot(xe.astype(BF16), mc_ref[gl], preferred_element_type=F32)

        for j in range(lc // per):
            moved = pltpu.roll(y[:, j * LANES:(j + 1) * LANES], jnp.bitwise_and(gl * gw, LANES - 1), axis=1)
            for k in range(per):
                ys_ref[j * per + k] = jnp.where(rel_grp == k, moved, ys_ref[j * per + k])
        return carry

    lax.fori_loop(0, per, group, 0)
    for s in range(lc):
        if s % per:
            ys_ref[s] = pltpu.roll(ys_ref[s], LANES - (s % per) * gw, axis=1)
        out_copy(s).start()
    for s in range(lc):
        out_copy(s).wait()


def _s5_conv(proj, ops):
    kflat, m_state, m_carry, ar, ai = ops
    t, n = proj.shape
    lc = S5_CHUNK
    nch = t // lc
    n_scan = int(math.log2(nch))
    assert (1 << n_scan) == nch
    w = lc * S5_GROUP
    p2 = m_state.shape[-1]
    per = S5_PER_TILE

    def grp(shape):
        return pl.BlockSpec((per,) + shape, lambda hh: (hh, 0, 0))

    hbm = pl.BlockSpec(memory_space=pl.ANY)
    y = pl.pallas_call(
        functools.partial(_s5_kernel, n_scan=n_scan),
        grid=(S5_WIDTH // LANES,),
        in_specs=[hbm, grp((S5_GROUP, w)), grp((w, p2)), grp((p2, w)),
                  grp((ar.shape[1], p2)), grp((ai.shape[1], p2))],
        out_specs=hbm,
        out_shape=jax.ShapeDtypeStruct((nch, lc, S5_WIDTH), F32),
        scratch_shapes=[pltpu.VMEM((lc, nch, LANES), F32), pltpu.VMEM((lc, nch, LANES), F32),
                        pltpu.VMEM((w, w), BF16), pltpu.SemaphoreType.DMA((lc,)), pltpu.SemaphoreType.DMA((lc,))],
        compiler_params=_cparams("arbitrary"),
        name="s5_conv",
    )(proj.reshape(nch, lc, n), kflat, m_state, m_carry, ar, ai)
    return y.reshape(t, S5_WIDTH)


LOG2E = float(np.log2(np.e))


def _hg_level_table():
    idx = np.arange(HG_CHUNK)
    t, s = idx[:, None], idx[None, :]
    top_bit = np.floor(np.log2(np.maximum(t ^ s, 1))).astype(np.int32)
    return np.where(s < t, HG_LEVELS - 1 - top_bit, -1).astype(np.int32)


def _hg_midpoint(bc, m):
    c = bc.shape[0]
    h = m // 2
    if h >= SUBLANES:
        return jnp.concatenate([jnp.broadcast_to(bc[j * m + h - 1:j * m + h, :], (m, LANES))
                                for j in range(c // m)], axis=0)
    x3 = bc.reshape(c // SUBLANES, SUBLANES, LANES)
    sub = lax.broadcasted_iota(I32, x3.shape, 1)
    beta = None
    for j in reversed(range(SUBLANES // m)):
        row = jnp.broadcast_to(x3[:, j * m + h - 1:j * m + h, :], x3.shape)
        beta = row if beta is None else jnp.where(sub < (j + 1) * m, row, beta)
    return beta.reshape(c, LANES)


def _bf16_terms(x, n):
    terms = []
    for _ in range(n - 1):
        t = x.astype(BF16)
        terms.append(t)
        x = x - t.astype(F32)
    return terms + [x.astype(BF16)]


def _hg_chunk(fp, q, v, gt, lb, lbf, ng, lvl, st):
    c = HG_CHUNK
    prow = lax.broadcasted_iota(I32, (c, LANES), 0)
    nt = (((1,), (1,)), ((), ()))
    tn = (((0,), (0,)), ((), ()))
    en = jnp.exp(-jnp.abs(fp))
    rc = 1.0 / (1.0 + en)
    pos_f = fp >= 0.0
    sig_p = jnp.where(pos_f, rc, en * rc)
    sig_n = jnp.where(pos_f, en * rc, rc)
    f = lbf + (1.0 - lb) * sig_p
    lf = jnp.log(f)
    kc = (1.0 - lb) * sig_n
    qc = q * _sigmoid(q)
    bc = lf
    for k in range(HG_LEVELS):
        d = 1 << k
        bc = bc + jnp.where(prow >= d, pltpu.roll(bc, d, axis=0), 0.0)

    a = jnp.zeros((c, c), F32)
    for lev in range(HG_LEVELS):
        m = c >> lev
        upper = jnp.bitwise_and(prow, m - 1) >= m // 2
        z = jnp.where(upper, qc, kc)
        if m == 2:
            zw = jnp.where(upper, z * f, z)
        else:
            dlt = bc - _hg_midpoint(bc, m)
            zw = z * jnp.exp2(dlt * jnp.where(upper, LOG2E, -LOG2E))
        zw = zw.astype(BF16)
        s = lax.dot_general(zw, zw, nt, preferred_element_type=F32)
        a = jnp.where(lvl == lev, s, a)
    vb = v.astype(BF16)
    o = jnp.dot(a.astype(BF16), vb, preferred_element_type=F32)
    o = o + jnp.sum(qc * kc, axis=-1, keepdims=True) * v
    o = o + lax.dot_general((qc * jnp.exp(bc)).astype(BF16), st.astype(BF16), nt,
                            preferred_element_type=F32)
    bl = bc[c - 1:c, :]
    khat = (kc * jnp.exp(bl - bc)).astype(BF16)
    st = st * jnp.exp(bl) + lax.dot_general(vb, khat, tn, preferred_element_type=F32)
    return _rms(o, ng) * (gt * _sigmoid(gt)), st


def _hgrn_kernel(q_ref, f_ref, i_ref, gt_ref, lb_ref, lbf_ref, ng_ref, lvl_ref, o_ref, st_ref):
    c = HG_CHUNK

    @pl.when(pl.program_id(0) == 0)
    def _():
        st_ref[...] = jnp.zeros_like(st_ref)

    def chunk(n, carry):
        rows = pl.ds(pl.multiple_of(n * c, c), c)
        for hd in range(HG_HEADS):
            cols = slice(hd * LANES, (hd + 1) * LANES)
            o, st = _hg_chunk(f_ref[rows, cols], q_ref[rows, cols], i_ref[rows, cols], gt_ref[rows, cols],
                              lb_ref[:, cols], lbf_ref[:, cols], ng_ref[:, cols], lvl_ref[...], st_ref[hd])
            o_ref[rows, cols] = o
            st_ref[hd] = st
        return carry

    lax.fori_loop(0, q_ref.shape[0] // c, chunk, 0)


def _hgrn(proj, lb, norm_g):
    t = proj.shape[0]
    tb = min(HG_TBLOCK, t)
    lbh = jnp.clip(lb.astype(F32), 0.0, 1.0 - 1e-6).reshape(1, HG_WIDTH)
    lbf = jnp.maximum(lbh, LB_FLOOR)
    ng = norm_g.astype(F32).reshape(1, HG_WIDTH)
    lvl = jnp.asarray(_hg_level_table())
    cb = COL_HG // HG_WIDTH

    def col(k):
        return pl.BlockSpec((tb, HG_WIDTH), lambda i, k=k: (i, cb + k))

    vec = pl.BlockSpec((1, HG_WIDTH), lambda i: (0, 0))
    return pl.pallas_call(
        _hgrn_kernel,
        grid=(t // tb,),
        in_specs=[col(0), col(1), col(2), col(3), vec, vec, vec,
                  pl.BlockSpec(lvl.shape, lambda i: (0, 0))],
        out_specs=pl.BlockSpec((tb, HG_WIDTH), lambda i: (i, 0)),
        out_shape=jax.ShapeDtypeStruct((t, HG_WIDTH), F32),
        scratch_shapes=[pltpu.VMEM((HG_HEADS, LANES, LANES), F32)],
        compiler_params=_cparams("arbitrary"),
        name="hgrn2",
    )(proj, proj, proj, proj, lbh, lbf, ng, lvl)


def _mix_out_kernel(ys_ref, u_ref, cb_ref, cc_ref, cv_ref, hg_ref, h_ref,
                    d_ref, glu_ref, sn_ref, cw_ref, cn_ref, wo_ref, o_ref, carry_ref):
    @pl.when(pl.program_id(0) == 0)
    def _():
        carry_ref[...] = jnp.zeros_like(carry_ref)

    y = ys_ref[...] + d_ref[...] * u_ref[...]
    y = jax.nn.gelu(y)
    y = y * _sigmoid(jnp.dot(y.astype(BF16), glu_ref[...], preferred_element_type=F32))
    y_s5 = _rms(y, sn_ref[...])

    z = cc_ref[...] * cv_ref[...]
    tm = z.shape[0]
    row = lax.broadcasted_iota(I32, z.shape, 0)
    p1 = carry_ref[7:8, :]
    p2 = carry_ref[6:7, :]
    z1 = jnp.where(row == 0, p1, pltpu.roll(z, 1, axis=0))
    z2 = jnp.where(row == 0, p2, jnp.where(row == 1, p1, pltpu.roll(z, 2, axis=0)))
    carry_ref[...] = z[tm - 8:tm, :]
    yc = cb_ref[...] * (z2 * cw_ref[0:1, :] + z1 * cw_ref[1:2, :] + z * cw_ref[2:3, :])
    y_cv = _rms(yc, cn_ref[...])

    acc = h_ref[...]
    hg_row = S5_WIDTH + CONV_WIDTH
    acc = acc + jnp.dot(y_s5.astype(BF16), wo_ref[0:S5_WIDTH, :], preferred_element_type=F32)
    acc = acc + jnp.dot(y_cv.astype(BF16), wo_ref[S5_WIDTH:hg_row, :], preferred_element_type=F32)
    acc = acc + jnp.dot(hg_ref[...].astype(BF16), wo_ref[hg_row:, :], preferred_element_type=F32)
    o_ref[...] = acc


def _mix_out(ys5, proj, yhg, h, s5_d, s5_glu, s5_norm, conv_w, conv_norm, w_out):
    t, d = h.shape
    tm = min(ROW_TILE, t)
    cw = jnp.pad(conv_w.astype(F32), ((0, 8 - CONV_K), (0, 0)))
    wq = S5_WIDTH

    def rowblk(width, colblk):
        return pl.BlockSpec((tm, width), lambda i, c=colblk: (i, c))

    def full(shape):
        return pl.BlockSpec(shape, lambda i: (0,) * len(shape))

    return pl.pallas_call(
        _mix_out_kernel,
        grid=(t // tm,),
        in_specs=[rowblk(wq, 0), rowblk(wq, 0), rowblk(wq, 1), rowblk(wq, 2), rowblk(wq, 3),
                  rowblk(HG_WIDTH, 0), rowblk(d, 0),
                  full((1, wq)), full((wq, wq)), full((1, wq)), full((8, wq)), full((1, wq)),
                  full(w_out.shape)],
        out_specs=rowblk(d, 0),
        out_shape=jax.ShapeDtypeStruct((t, d), F32),
        scratch_shapes=[pltpu.VMEM((8, CONV_WIDTH), F32)],
        compiler_params=_cparams("arbitrary"),
        name="mix_out",
    )(ys5, proj, proj, proj, proj, yhg, h,
      s5_d.astype(F32).reshape(1, wq), s5_glu, s5_norm.astype(F32).reshape(1, wq), cw,
      conv_norm.astype(F32).reshape(1, wq), w_out)


def _ffn_kernel(h_ref, g_ref, w1_ref, w3_ref, w2_ref, o_ref, hn_ref, acc_ref):
    j = pl.program_id(1)

    @pl.when(j == 0)
    def _():
        x = h_ref[...]
        hn_ref[...] = _rms(x, g_ref[...]).astype(BF16)
        acc_ref[...] = x

    hn = hn_ref[...]
    a = jnp.dot(hn, w1_ref[...].astype(BF16), preferred_element_type=F32)
    b = jnp.dot(hn, w3_ref[...].astype(BF16), preferred_element_type=F32)
    gact = (a * _sigmoid(a) * b).astype(BF16)
    acc_ref[...] += jnp.dot(gact, w2_ref[...].astype(BF16), preferred_element_type=F32)

    @pl.when(j == pl.num_programs(1) - 1)
    def _():
        o_ref[...] = acc_ref[...]


def _ffn(h, g, w1, w3, w2, layer):
    t, d = h.shape
    f = w1.shape[2]
    tm = min(FFN_ROW_TILE, t)
    tf = FFN_COL_TILE
    return pl.pallas_call(
        _ffn_kernel,
        grid=(t // tm, f // tf),
        in_specs=[pl.BlockSpec((tm, d), lambda i, j: (i, 0)),
                  pl.BlockSpec((1, d), lambda i, j: (0, 0)),
                  pl.BlockSpec((None, d, tf), lambda i, j: (layer, 0, j)),
                  pl.BlockSpec((None, d, tf), lambda i, j: (layer, 0, j)),
                  pl.BlockSpec((None, tf, d), lambda i, j: (layer, j, 0))],
        out_specs=pl.BlockSpec((tm, d), lambda i, j: (i, 0)),
        out_shape=jax.ShapeDtypeStruct((t, d), F32),
        scratch_shapes=[pltpu.VMEM((tm, d), BF16), pltpu.VMEM((tm, d), F32)],
        compiler_params=_cparams("parallel", "arbitrary"),
        name="ffn_swiglu",
    )(h, g.astype(F32).reshape(1, d), w1, w3, w2)


def _router_kernel(h_ref, g_ref, r_ref, hn_ref, info_ref, info_t_ref, before_ref, total_ref, cnt_ref, *, n_exp):
    @pl.when(pl.program_id(0) == 0)
    def _():
        cnt_ref[...] = jnp.zeros_like(cnt_ref)

    xn = _rms(h_ref[...], g_ref[...])
    x_hi, x_lo = _bf16_terms(xn, 2)
    hn_ref[...] = x_hi
    r_hi, r_lo = _bf16_terms(r_ref[...], 2)
    logits = (jnp.dot(x_hi, r_hi, preferred_element_type=F32) + jnp.dot(x_lo, r_hi, preferred_element_type=F32)
              + jnp.dot(x_hi, r_lo, preferred_element_type=F32))
    tm = logits.shape[0]
    lane = lax.broadcasted_iota(I32, logits.shape, 1)
    lanef = lane.astype(F32)
    neg = jnp.float32(-jnp.inf)
    lg = jnp.where(lane < n_exp, logits, neg)
    m1 = jnp.max(lg, axis=-1, keepdims=True)
    i1 = jnp.min(jnp.where(lg == m1, lanef, float(LANES)), axis=-1, keepdims=True)
    oh1 = lanef == i1
    lg2 = jnp.where(oh1, neg, lg)
    m2 = jnp.max(lg2, axis=-1, keepdims=True)
    i2 = jnp.min(jnp.where(lg2 == m2, lanef, float(LANES)), axis=-1, keepdims=True)
    oh2 = lanef == i2
    ex = jnp.exp(m2 - m1)
    g1 = 1.0 / (1.0 + ex)
    g2 = ex * g1
    chosen = jnp.where(oh1, 1.0, jnp.where(oh2, 1.0, 0.0))
    ri = lax.broadcasted_iota(I32, (tm, tm), 0)
    ci = lax.broadcasted_iota(I32, (tm, tm), 1)
    tri = jnp.where(ri > ci, 1.0, 0.0).astype(BF16)
    before = cnt_ref[...]
    cexcl = jnp.dot(tri, chosen.astype(BF16), preferred_element_type=F32) + before
    rank1 = jnp.sum(jnp.where(oh1, cexcl, 0.0), axis=-1, keepdims=True)
    rank2 = jnp.sum(jnp.where(oh2, cexcl, 0.0), axis=-1, keepdims=True)
    info = jnp.where(lane == 0, i1, jnp.where(lane == 1, i2, jnp.where(lane == 2, g1, jnp.where(
        lane == 3, g2, jnp.where(lane == 4, rank1, jnp.where(lane == 5, rank2, 0.0))))))
    info_ref[...] = info
    info_t_ref[...] = info.T[0:SUBLANES, :]
    before_ref[0] = jnp.broadcast_to(before, before_ref.shape[1:])
    total = before + jnp.sum(chosen, axis=0, keepdims=True)
    cnt_ref[...] = total
    total_ref[...] = jnp.broadcast_to(total, total_ref.shape)


def _router(h, g, router):
    t, d = h.shape
    n_exp = router.shape[1]
    tm = min(TOK_WIN, t)
    ntw = t // tm
    rp = jnp.pad(router.astype(F32), ((0, 0), (0, LANES - n_exp)))
    return pl.pallas_call(
        functools.partial(_router_kernel, n_exp=n_exp),
        grid=(ntw,),
        in_specs=[pl.BlockSpec((tm, d), lambda i: (i, 0)),
                  pl.BlockSpec((1, d), lambda i: (0, 0)),
                  pl.BlockSpec((d, LANES), lambda i: (0, 0))],
        out_specs=[pl.BlockSpec((tm, d), lambda i: (i, 0)),
                   pl.BlockSpec((tm, LANES), lambda i: (i, 0)),
                   pl.BlockSpec((SUBLANES, tm), lambda i: (0, i)),
                   pl.BlockSpec((1, 8, LANES), lambda i: (i, 0, 0)),
                   pl.BlockSpec((8, LANES), lambda i: (0, 0))],
        out_shape=[jax.ShapeDtypeStruct((t, d), BF16),
                   jax.ShapeDtypeStruct((t, LANES), F32),
                   jax.ShapeDtypeStruct((SUBLANES, t), F32),
                   jax.ShapeDtypeStruct((ntw, 8, LANES), F32),
                   jax.ShapeDtypeStruct((8, LANES), F32)],
        scratch_shapes=[pltpu.VMEM((1, LANES), F32)],
        compiler_params=_cparams("arbitrary"),
        name="moe_router",
    )(h, g.astype(F32).reshape(1, d), rp)


def _gather_kernel(off_ref, nw_ref, wl_ref, tot_ref, dest_ref, hn_hbm, xs_ref, buf_ref, sem, acc_ref):
    b = pl.program_id(0)
    rows = acc_ref.shape[0]
    n_buf, win = buf_ref.shape[0], buf_ref.shape[1]
    n = nw_ref[b]
    q0 = off_ref[b]
    total = tot_ref[0]

    def copy(q):
        w = wl_ref[q]
        s = lax.rem(q, n_buf)
        return pltpu.make_async_copy(hn_hbm.at[pl.ds(pl.multiple_of(w * win, win), win), :], buf_ref.at[s],
                                     sem.at[s])

    @pl.when(b == 0)
    def _():
        for q in range(GATHER_AHEAD):
            @pl.when(q < total)
            def _(q=q):
                copy(q).start()

    acc_ref[...] = jnp.zeros_like(acc_ref)
    slot = b * rows + lax.broadcasted_iota(I32, (rows, win), 0)

    def gathered(q):
        @pl.when(q + GATHER_AHEAD < total)
        def _():
            copy(q + GATHER_AHEAD).start()

        col = pl.ds(pl.multiple_of(wl_ref[q] * win, win), win)
        d1 = dest_ref[0:1, col]
        d2 = dest_ref[1:2, col]
        hit = jnp.where(d1 == slot, 1.0, jnp.where(d2 == slot, 1.0, 0.0)).astype(BF16)
        copy(q).wait()
        return jnp.dot(hit, buf_ref[lax.rem(q, n_buf)], preferred_element_type=F32)

    def pair(j, carry):
        q = q0 + 2 * j
        acc_ref[...] += gathered(q) + gathered(q + 1)
        return carry

    lax.fori_loop(0, lax.shift_right_logical(n, 1), pair, 0)

    @pl.when(jnp.bitwise_and(n, 1) == 1)
    def _():
        acc_ref[...] += gathered(q0 + n - 1)

    xs_ref[...] = acc_ref[...].astype(BF16)


def _gather(hn, dest, off, nw, wlist, total, n_slots):
    t, d = hn.shape
    return pl.pallas_call(
        _gather_kernel,
        grid_spec=pltpu.PrefetchScalarGridSpec(
            num_scalar_prefetch=4,
            grid=(n_slots // GATHER_TILE,),
            in_specs=[pl.BlockSpec(dest.shape, lambda b, *_: (0, 0)),
                      pl.BlockSpec(memory_space=pl.ANY)],
            out_specs=pl.BlockSpec((GATHER_TILE, d), lambda b, *_: (b, 0)),
            scratch_shapes=[pltpu.VMEM((GATHER_BUFS, TOK_WIN, d), BF16),
                            pltpu.SemaphoreType.DMA((GATHER_BUFS,)),
                            pltpu.VMEM((GATHER_TILE, d), F32)]),
        out_shape=jax.ShapeDtypeStruct((n_slots, d), BF16),
        compiler_params=_cparams("arbitrary"),
        name="moe_gather",
    )(off, nw, wlist, total, dest, hn)


def _expert_kernel(te_ref, tv_ref, x_ref, w1_ref, w3_ref, w2_ref, y_ref, acc_ref):
    b = pl.program_id(0)
    j = pl.program_id(1)

    @pl.when(j == 0)
    def _():
        acc_ref[...] = jnp.zeros_like(acc_ref)

    @pl.when(tv_ref[b] == 1)
    def _():
        x = x_ref[...]
        a = jnp.dot(x, w1_ref[...].astype(BF16), preferred_element_type=F32)
        c = jnp.dot(x, w3_ref[...].astype(BF16), preferred_element_type=F32)
        gact = (a * _sigmoid(a) * c).astype(BF16)
        acc_ref[...] += jnp.dot(gact, w2_ref[...].astype(BF16), preferred_element_type=F32)

    @pl.when(j == pl.num_programs(1) - 1)
    def _():
        y_ref[...] = acc_ref[...].astype(BF16)


def _experts(xs, tile_e, tile_v, w1, w3, w2, layer):
    ns, d = xs.shape
    f = w1.shape[3]
    tf = MOE_COL_TILE
    nj = f // tf
    n_tiles = ns // MOE_TILE

    def col(b, j, tv):
        return j * tv[b] + (nj - 1) * (1 - tv[b])

    return pl.pallas_call(
        _expert_kernel,
        grid_spec=pltpu.PrefetchScalarGridSpec(
            num_scalar_prefetch=2,
            grid=(n_tiles, nj),
            in_specs=[pl.BlockSpec((MOE_TILE, d), lambda b, j, te, tv: (b, 0)),
                      pl.BlockSpec((None, None, d, tf), lambda b, j, te, tv: (layer, te[b], 0, col(b, j, tv))),
                      pl.BlockSpec((None, None, d, tf), lambda b, j, te, tv: (layer, te[b], 0, col(b, j, tv))),
                      pl.BlockSpec((None, None, tf, d), lambda b, j, te, tv: (layer, te[b], col(b, j, tv), 0))],
            out_specs=pl.BlockSpec((MOE_TILE, d), lambda b, j, te, tv: (b, 0)),
            scratch_shapes=[pltpu.VMEM((MOE_TILE, d), F32)]),
        out_shape=jax.ShapeDtypeStruct((ns, d), BF16),
        compiler_params=_cparams("arbitrary", "arbitrary"),
        name="moe_experts",
    )(tile_e, tile_v, xs, w1, w3, w2)


def _combine_kernel(ws_ref, po_ref, ex_ref, nar_ref, info_ref, h_ref, fin_ref, yb_hbm, o_ref,
                    buf_ref, nbuf_ref, sem, *, n_exp, final_norm):
    i = pl.program_id(0)
    cur = lax.rem(i, 2)

    def wide_copy(step, e, s):
        start = pl.multiple_of(ws_ref[step * n_exp + e], SLOT_ALIGN)
        return pltpu.make_async_copy(yb_hbm.at[pl.ds(start, SLOT_WIN), :], buf_ref.at[s, e], sem.at[s, e])

    def narrow_copy(step, e, s):
        start = pl.multiple_of(ws_ref[step * n_exp + e], SLOT_ALIGN)
        return pltpu.make_async_copy(yb_hbm.at[pl.ds(start, NARROW_WIN), :],
                                     nbuf_ref.at[s, e * NARROW_WIN:(e + 1) * NARROW_WIN, :], sem.at[s, e])

    def for_step(step, s, act):
        @pl.when(nar_ref[step] == 1)
        def _():
            for e in range(n_exp):
                act(narrow_copy(step, e, s))

        @pl.when(nar_ref[step] == 0)
        def _():
            for e in range(n_exp):
                act(wide_copy(step, e, s))

    @pl.when(i == 0)
    def _():
        for_step(0, 0, lambda c: c.start())

    @pl.when(i + 1 < pl.num_programs(0))
    def _():
        for_step(i + 1, 1 - cur, lambda c: c.start())

    info = info_ref[...]
    tm = info.shape[0]
    e1, e2 = info[:, 0:1], info[:, 1:2]
    g1, g2 = info[:, 2:3], info[:, 3:4]
    r1, r2 = info[:, 4:5], info[:, 5:6]

    def rank_offset(e, shift):
        return (ws_ref[i * n_exp + e] - po_ref[e] + shift).astype(F32)

    for_step(i, cur, lambda c: c.wait())

    @pl.when(nar_ref[i] == 1)
    def _():
        k1 = jnp.full_like(r1, -1.0)
        k2 = jnp.full_like(r2, -1.0)
        for e in range(n_exp):
            k1 = jnp.where(e1 == float(e), r1 - rank_offset(e, -e * NARROW_WIN), k1)
            k2 = jnp.where(e2 == float(e), r2 - rank_offset(e, -e * NARROW_WIN), k2)
        width = n_exp * NARROW_WIN
        lanef = lax.broadcasted_iota(I32, (tm, width), 1).astype(F32)
        pt = jnp.where(k1 == lanef, jnp.broadcast_to(g1, (tm, width)),
                       jnp.where(k2 == lanef, jnp.broadcast_to(g2, (tm, width)), 0.0)).astype(BF16)
        o_ref[...] = h_ref[...] + jnp.dot(pt, nbuf_ref[cur], preferred_element_type=F32)

    @pl.when(nar_ref[i] == 0)
    def _():
        lanef = lax.broadcasted_iota(I32, (tm, TOK_WIN), 1).astype(F32)
        g1b = jnp.broadcast_to(g1, (tm, TOK_WIN))
        g2b = jnp.broadcast_to(g2, (tm, TOK_WIN))

        def weights(e, shift, lo_lane):
            off = rank_offset(e, shift)
            k1 = jnp.where(e1 == float(e), r1 - off, -1.0)
            k2 = jnp.where(e2 == float(e), r2 - off, -1.0)
            k1 = jnp.where(k1 >= float(lo_lane), k1, -1.0)
            k2 = jnp.where(k2 >= float(lo_lane), k2, -1.0)
            return jnp.where(k1 == lanef, g1b, jnp.where(k2 == lanef, g2b, 0.0)).astype(BF16)

        acc = h_ref[...]
        for e in range(n_exp):
            acc = acc + jnp.dot(weights(e, 0, 0), buf_ref[cur, e, 0:TOK_WIN, :], preferred_element_type=F32)
        o_ref[...] = acc

        tail = SLOT_WIN - TOK_WIN
        for e in range(n_exp):
            @pl.when(ex_ref[i * n_exp + e] == 1)
            def _(e=e):
                pt = weights(e, tail, TOK_WIN - tail)
                o_ref[...] += jnp.dot(pt, buf_ref[cur, e, tail:SLOT_WIN, :], preferred_element_type=F32)

    if final_norm:
        o_ref[...] = _rms(o_ref[...], fin_ref[...])


def _combine(info, h, yb, win_start, pstart, extra, narrow, fin_g, final_norm):
    t, d = h.shape
    n_exp = pstart.shape[0]
    tm = min(TOK_WIN, t)
    return pl.pallas_call(
        functools.partial(_combine_kernel, n_exp=n_exp, final_norm=final_norm),
        grid_spec=pltpu.PrefetchScalarGridSpec(
            num_scalar_prefetch=4,
            grid=(t // tm,),
            in_specs=[pl.BlockSpec((tm, LANES), lambda i, *_: (i, 0)),
                      pl.BlockSpec((tm, d), lambda i, *_: (i, 0)),
                      pl.BlockSpec((1, d), lambda i, *_: (0, 0)),
                      pl.BlockSpec(memory_space=pl.ANY)],
            out_specs=pl.BlockSpec((tm, d), lambda i, *_: (i, 0)),
            scratch_shapes=[pltpu.VMEM((2, n_exp, SLOT_WIN, d), BF16),
                            pltpu.VMEM((2, n_exp * NARROW_WIN, d), BF16),
                            pltpu.SemaphoreType.DMA((2, n_exp))]),
        out_shape=jax.ShapeDtypeStruct((t, d), F32),
        compiler_params=_cparams("arbitrary"),
        name="moe_combine",
    )(win_start, pstart, extra, narrow, info, h, fin_g.astype(F32).reshape(1, d), yb)


def _count_le(sorted_vals, x):
    return jnp.sum((sorted_vals[None, :] <= x[:, None]).astype(I32), axis=1)


def _moe(h, g, router, w1, w3, w2, layer, fin_g, final_norm):
    t, d = h.shape
    n_exp = router.shape[1]
    tw = min(TOK_WIN, t)
    ntw = t // tw
    hn, info, info_t, before, total = _router(h, g, router)

    counts = total[0, :n_exp].astype(I32)
    padded = (counts + MOE_TILE - 1) // MOE_TILE * MOE_TILE
    pend = jnp.cumsum(padded)
    pstart = pend - padded
    n_tiles = (t * TOP_K) // MOE_TILE + n_exp
    n_slots = n_tiles * MOE_TILE
    tile0 = jnp.arange(n_tiles, dtype=I32) * MOE_TILE
    tile_e = jnp.minimum(_count_le(pend, tile0), n_exp - 1)
    tile_v = (tile0 < pend[-1]).astype(I32)
    cum = jnp.concatenate([before[:, 0, :n_exp], total[0:1, :n_exp]], axis=0).astype(I32)

    n_gb = n_slots // GATHER_TILE
    gb0 = jnp.arange(n_gb, dtype=I32) * GATHER_TILE
    gb_e = jnp.minimum(_count_le(pend, gb0), n_exp - 1)
    r0 = gb0 - pstart[gb_e]
    cum_b = cum[:, gb_e]
    ilo = jnp.sum((cum_b[1:] <= r0[None, :]).astype(I32), axis=0)
    ihi = jnp.sum((cum_b[:-1] < (r0 + GATHER_TILE)[None, :]).astype(I32), axis=0) - 1
    n_work = jnp.where(gb0 < pend[-1], jnp.maximum(ihi - ilo + 1, 0), 0).astype(I32)
    ends = jnp.cumsum(n_work)
    offs = (ends - n_work).astype(I32)
    entry = jnp.arange(n_exp * ntw + n_gb, dtype=I32)
    entry_blk = jnp.minimum(_count_le(ends, entry), n_gb - 1)
    wlist = jnp.clip(ilo[entry_blk] + entry - offs[entry_blk], 0, ntw - 1).astype(I32)

    e_t = info_t[0:2].astype(I32)
    slot_t = jnp.sum(jnp.where(e_t[None] == jnp.arange(n_exp, dtype=I32)[:, None, None], pstart[:, None, None], 0),
                     axis=0) + info_t[4:6].astype(I32)
    dest = jnp.concatenate([slot_t, jnp.full((SUBLANES - TOP_K, t), -1, I32)], axis=0)

    xs = _gather(hn, dest, offs, n_work, wlist, ends[-1:].astype(I32), n_slots)
    yb = _experts(xs, tile_e, tile_v, w1, w3, w2, layer)

    lo = pstart[None, :] + cum[:-1]
    win_start = jnp.minimum(lo // SLOT_ALIGN * SLOT_ALIGN, n_slots - SLOT_WIN).astype(I32)
    reach = lo - win_start + cum[1:] - cum[:-1]
    extra = (reach > TOK_WIN).astype(I32)
    narrow = jnp.all(reach <= NARROW_WIN, axis=1).astype(I32)
    return _combine(info, h, yb, win_start.reshape(-1), pstart.astype(I32), extra.reshape(-1), narrow, fin_g,
                    final_norm)


def _final_norm_kernel(h_ref, g_ref, o_ref):
    o_ref[...] = _rms(h_ref[...], g_ref[...])


def _final_norm(h, g):
    t, d = h.shape
    tm = min(ROW_TILE, t)
    return pl.pallas_call(
        _final_norm_kernel,
        grid=(t // tm,),
        in_specs=[pl.BlockSpec((tm, d), lambda i: (i, 0)), pl.BlockSpec((1, d), lambda i: (0, 0))],
        out_specs=pl.BlockSpec((tm, d), lambda i: (i, 0)),
        out_shape=jax.ShapeDtypeStruct((t, d), F32),
        compiler_params=_cparams("parallel"),
        name="final_norm",
    )(h, g.astype(F32).reshape(1, d))


def kernel(x, attn_norm, ffn_norm, final_norm, w_in, w_out, s5_lambda_re, s5_lambda_im, s5_log_dt, s5_b_re, s5_b_im, s5_c_re, s5_c_im, s5_d, s5_glu, s5_out_norm, conv_w, conv_out_norm, hg_lower_bounds, hg_out_norm, ffn_w1, ffn_w3, ffn_w2, moe_router, moe_w1, moe_w3, moe_w2):
    bsz, seq, d = x.shape
    depth = w_in.shape[0]
    t = bsz * seq
    assert bsz == 1, "token mixers are written for a single sequence"
    lb_soft = jax.nn.softmax(hg_lower_bounds.astype(F32), axis=0)
    lb_all = jnp.cumsum(lb_soft, axis=0) - lb_soft[0]
    n_scan = int(math.log2(t // S5_CHUNK))
    s5_ops = jax.vmap(functools.partial(_s5_operators, n_scan=n_scan))(
        s5_lambda_re, s5_lambda_im, s5_log_dt, s5_b_re, s5_b_im, s5_c_re, s5_c_im)
    h = x.reshape(t, d).astype(F32)
    for l in range(depth):
        proj = _norm_inproj(h, attn_norm[l].astype(F32), w_in[l].astype(BF16))
        ys5 = _s5_conv(proj, [op[l] for op in s5_ops])
        yhg = _hgrn(proj, lb_all[l], hg_out_norm[l])
        h = _mix_out(ys5, proj, yhg, h, s5_d[l], s5_glu[l].astype(BF16), s5_out_norm[l],
                     conv_w[l], conv_out_norm[l], w_out[l].astype(BF16))
        j = l // 2
        if l % 2 == 0:
            h = _ffn(h, ffn_norm[l], ffn_w1, ffn_w3, ffn_w2, j)
        else:
            h = _moe(h, ffn_norm[l], moe_router[j], moe_w1, moe_w3, moe_w2, j, final_norm, l == depth - 1)
    if depth % 2 == 1:
        h = _final_norm(h, final_norm)
    return h.reshape(bsz, seq, d)
```

```python
import functools
import math

import numpy as np
import jax
import jax.numpy as jnp
from jax import lax
from jax.experimental import pallas as pl
from jax.experimental.pallas import tpu as pltpu

F32 = jnp.float32
BF16 = jnp.bfloat16
I32 = jnp.int32

NORM_EPS = 1e-6
LB_FLOOR = 1e-30
TOP_K = 2

S5_WIDTH = 256
S5_GROUP = 16
S5_STATE = 64
CONV_WIDTH = 256
CONV_K = 3
HG_WIDTH = 512
HG_HEAD_DIM = 128
HG_HEADS = HG_WIDTH // HG_HEAD_DIM
LANES = 128
SUBLANES = 8
COL_CONV = S5_WIDTH
COL_HG = S5_WIDTH + 3 * CONV_WIDTH

VMEM_LIMIT = 56 * 1024 * 1024
S5_CHUNK = 32
HG_CHUNK = 128
HG_LEVELS = int(math.log2(HG_CHUNK))
HG_TBLOCK = 512
ROW_TILE = 512
FFN_ROW_TILE = 1024
FFN_COL_TILE = 512
MOE_TILE = 1024
MOE_COL_TILE = 512
GATHER_TILE = 256
GATHER_AHEAD = 6
GATHER_UNROLL = 4
GATHER_BUFS = GATHER_AHEAD + GATHER_UNROLL
TOK_WIN = 256
SLOT_ALIGN = 16
SLOT_WIN = TOK_WIN + SLOT_ALIGN
NARROW_WIN = 128
COMBINE_SETS = 3


def _cparams(*sem):
    return pltpu.CompilerParams(dimension_semantics=sem, vmem_limit_bytes=VMEM_LIMIT)


def _rms(x, g):
    ms = jnp.mean(x * x, axis=-1, keepdims=True)
    return x * lax.rsqrt(ms + NORM_EPS) * g


def _sigmoid(x):
    return 1.0 / (1.0 + jnp.exp(-x))


def _norm_inproj_kernel(h_ref, g_ref, w_ref, o_ref):
    xn = _rms(h_ref[...], g_ref[...]).astype(BF16)
    o_ref[...] = jnp.dot(xn, w_ref[...], preferred_element_type=F32)


def _norm_inproj(h, g, w):
    t, d = h.shape
    n = w.shape[1]
    tm = min(ROW_TILE, t)
    return pl.pallas_call(
        _norm_inproj_kernel,
        grid=(t // tm,),
        in_specs=[pl.BlockSpec((tm, d), lambda i: (i, 0)),
                  pl.BlockSpec((1, d), lambda i: (0, 0)),
                  pl.BlockSpec((d, n), lambda i: (0, 0))],
        out_specs=pl.BlockSpec((tm, n), lambda i: (i, 0)),
        out_shape=jax.ShapeDtypeStruct((t, n), F32),
        compiler_params=_cparams("parallel"),
        name="norm_inproj",
    )(h, g.reshape(1, d), w)


def _s5_operators(lam_re, lam_im, log_dt, b_re, b_im, c_re, c_im, n_scan):
    lc = S5_CHUNK
    hi = lax.Precision.HIGHEST
    lr, li = lam_re.astype(F32), lam_im.astype(F32)
    dt = jnp.exp(log_dt.astype(F32))[:, None]
    zr, zi = lr * dt, li * dt
    taus = jnp.arange(lc + 1, dtype=F32)[:, None, None]
    mag = jnp.exp(zr[None] * taus)
    pwr, pwi = mag * jnp.cos(zi[None] * taus), mag * jnp.sin(zi[None] * taus)
    nr, ni = pwr[1] - 1.0, pwi[1]
    den = lr * lr + li * li
    qr, qi = (nr * lr + ni * li) / den, (ni * lr - nr * li) / den
    br, bi = b_re.astype(F32), b_im.astype(F32)
    bbr = qr[..., None] * br - qi[..., None] * bi
    bbi = qr[..., None] * bi + qi[..., None] * br
    cr, ci = c_re.astype(F32), c_im.astype(F32)
    g_, p_ = lr.shape

    def c_times_pw(lo):
        wr, wi = pwr[lo:lo + lc, :, None, :], pwi[lo:lo + lc, :, None, :]
        return cr[None] * wr - ci[None] * wi, cr[None] * wi + ci[None] * wr

    cpr, cpi = c_times_pw(0)
    cp = jnp.concatenate([cpr, cpi], axis=-1).transpose(1, 0, 2, 3).reshape(g_, lc * S5_GROUP, 2 * p_)
    kt = jnp.matmul(cp, jnp.concatenate([bbr, -bbi], axis=1), precision=hi)
    kflat = kt.transpose(0, 2, 1)

    wr, wi = pwr[lc - 1::-1][:, :, :, None], pwi[lc - 1::-1][:, :, :, None]
    msr = (wr * bbr[None] - wi * bbi[None]).transpose(1, 0, 3, 2).reshape(g_, lc * S5_GROUP, p_)
    msi = (wr * bbi[None] + wi * bbr[None]).transpose(1, 0, 3, 2).reshape(g_, lc * S5_GROUP, p_)
    m_state = jnp.concatenate([msr, msi], axis=-1)

    c1r, c1i = c_times_pw(1)
    c1r = c1r.transpose(1, 3, 0, 2).reshape(g_, p_, lc * S5_GROUP)
    c1i = c1i.transpose(1, 3, 0, 2).reshape(g_, p_, lc * S5_GROUP)
    m_carry = jnp.concatenate([c1r, -c1i], axis=1)

    akr, aki = [pwr[lc]], [pwi[lc]]
    for _ in range(n_scan - 1):
        r, i = akr[-1], aki[-1]
        akr.append(r * r - i * i)
        aki.append(2.0 * r * i)
    akr, aki = jnp.stack(akr, axis=1), jnp.stack(aki, axis=1)
    ar = jnp.concatenate([akr, akr], axis=-1)
    ai = jnp.concatenate([-aki, aki], axis=-1)
    kpad = -(-n_scan // 8) * 8
    ar = jnp.pad(ar, ((0, 0), (0, kpad - n_scan), (0, 0)))
    ai = jnp.pad(ai, ((0, 0), (0, kpad - n_scan), (0, 0)))
    return kflat, m_state.astype(BF16), m_carry.astype(BF16), ar, ai


S5_PER_TILE = LANES // S5_GROUP


def _s5_kernel(proj_hbm, kf_ref, ms_ref, mc_ref, ar_ref, ai_ref, y_hbm, us_ref, ys_ref, mi_ref, sem_in, sem_out,
               *, n_scan):
    lc, gw, per = S5_CHUNK, S5_GROUP, S5_PER_TILE
    nch = us_ref.shape[1]
    cols = pl.ds(pl.multiple_of(pl.program_id(0) * LANES, LANES), LANES)

    def in_copy(s):
        return pltpu.make_async_copy(proj_hbm.at[:, s, cols], us_ref.at[s], sem_in.at[s])

    def out_copy(s):
        return pltpu.make_async_copy(ys_ref.at[s], y_hbm.at[:, s, cols], sem_out.at[s])

    for s in range(lc):
        in_copy(s).start()
    ys_ref[...] = jnp.zeros_like(ys_ref)
    for s in range(lc):
        in_copy(s).wait()
        if s % per:
            us_ref[s] = pltpu.roll(us_ref[s], (s % per) * gw, axis=1)

    lane_grp = lax.broadcasted_iota(I32, (1, LANES), 1) // gw
    kf_lane = lax.broadcasted_iota(I32, (gw, lc * gw), 1)

    def group(gl, carry):
        rel_grp = jnp.bitwise_and(lane_grp - gl, per - 1)
        tiles = []
        for j in range(lc // per):
            merged = us_ref[j * per]
            for k in range(1, per):
                merged = jnp.where(rel_grp == k, us_ref[j * per + k], merged)
            tiles.append(pltpu.roll(merged, jnp.bitwise_and(-gl * gw, LANES - 1), axis=1))
        u = jnp.concatenate(tiles, axis=1).astype(BF16)

        kf = kf_ref[gl]
        for s in range(lc):
            blk = kf if s == 0 else jnp.where(kf_lane >= s * gw, pltpu.roll(kf, s * gw, axis=1), 0.0)
            mi_ref[s * gw:(s + 1) * gw, :] = blk.astype(BF16)

        x = jnp.dot(u, ms_ref[gl], preferred_element_type=F32)
        row = lax.broadcasted_iota(I32, x.shape, 0)
        half = x.shape[1] // 2
        for k in range(n_scan):
            d = 1 << k
            s = jnp.where(row >= d, pltpu.roll(x, d, axis=0), 0.0)
            x = x + ar_ref[gl, k:k + 1, :] * s + ai_ref[gl, k:k + 1, :] * pltpu.roll(s, half, axis=1)
        xe = jnp.where(row >= 1, pltpu.roll(x, 1, axis=0), 0.0)
        y = jnp.dot(u, mi_ref[...], preferred_element_type=F32)
        y = y + jnp.dot(xe.astype(BF16), mc_ref[gl], preferred_element_type=F32)

        for j in range(lc // per):
            moved = pltpu.roll(y[:, j * LANES:(j + 1) * LANES], jnp.bitwise_and(gl * gw, LANES - 1), axis=1)
            for k in range(per):
                ys_ref[j * per + k] = jnp.where(rel_grp == k, moved, ys_ref[j * per + k])
        return carry

    lax.fori_loop(0, per, group, 0)
    for s in range(lc):
        if s % per:
            ys_ref[s] = pltpu.roll(ys_ref[s], LANES - (s % per) * gw, axis=1)
        out_copy(s).start()
    for s in range(lc):
        out_copy(s).wait()


def _s5_conv(proj, ops):
    kflat, m_state, m_carry, ar, ai = ops
    t, n = proj.shape
    lc = S5_CHUNK
    nch = t // lc
    n_scan = int(math.log2(nch))
    assert (1 << n_scan) == nch
    w = lc * S5_GROUP
    p2 = m_state.shape[-1]
    per = S5_PER_TILE

    def grp(shape):
        return pl.BlockSpec((per,) + shape, lambda hh: (hh, 0, 0))

    hbm = pl.BlockSpec(memory_space=pl.ANY)
    y = pl.pallas_call(
        functools.partial(_s5_kernel, n_scan=n_scan),
        grid=(S5_WIDTH // LANES,),
        in_specs=[hbm, grp((S5_GROUP, w)), grp((w, p2)), grp((p2, w)),
                  grp((ar.shape[1], p2)), grp((ai.shape[1], p2))],
        out_specs=hbm,
        out_shape=jax.ShapeDtypeStruct((nch, lc, S5_WIDTH), F32),
        scratch_shapes=[pltpu.VMEM((lc, nch, LANES), F32), pltpu.VMEM((lc, nch, LANES), F32),
                        pltpu.VMEM((w, w), BF16), pltpu.SemaphoreType.DMA((lc,)), pltpu.SemaphoreType.DMA((lc,))],
        compiler_params=_cparams("arbitrary"),
        name="s5_conv",
    )(proj.reshape(nch, lc, n), kflat, m_state, m_carry, ar, ai)
    return y.reshape(t, S5_WIDTH)


LOG2E = float(np.log2(np.e))


def _hg_level_table():
    idx = np.arange(HG_CHUNK)
    t, s = idx[:, None], idx[None, :]
    top_bit = np.floor(np.log2(np.maximum(t ^ s, 1))).astype(np.int32)
    return np.where(s < t, HG_LEVELS - 1 - top_bit, -1).astype(np.int32)


def _hg_midpoint(bc, m):
    c = bc.shape[0]
    h = m // 2
    if h >= SUBLANES:
        return jnp.concatenate([jnp.broadcast_to(bc[j * m + h - 1:j * m + h, :], (m, LANES))
                                for j in range(c // m)], axis=0)
    x3 = bc.reshape(c // SUBLANES, SUBLANES, LANES)
    sub = lax.broadcasted_iota(I32, x3.shape, 1)
    beta = None
    for j in reversed(range(SUBLANES // m)):
        row = jnp.broadcast_to(x3[:, j * m + h - 1:j * m + h, :], x3.shape)
        beta = row if beta is None else jnp.where(sub < (j + 1) * m, row, beta)
    return beta.reshape(c, LANES)


def _bf16_terms(x, n):
    terms = []
    for _ in range(n - 1):
        t = x.astype(BF16)
        terms.append(t)
        x = x - t.astype(F32)
    return terms + [x.astype(BF16)]


def _hg_chunk(fp, q, v, gt, lb, lbf, ng, lvl, st):
    c = HG_CHUNK
    prow = lax.broadcasted_iota(I32, (c, LANES), 0)
    nt = (((1,), (1,)), ((), ()))
    tn = (((0,), (0,)), ((), ()))
    en = jnp.exp(-jnp.abs(fp))
    rc = 1.0 / (1.0 + en)
    pos_f = fp >= 0.0
    sig_p = jnp.where(pos_f, rc, en * rc)
    sig_n = jnp.where(pos_f, en * rc, rc)
    f = lbf + (1.0 - lb) * sig_p
    lf = jnp.log(f)
    kc = (1.0 - lb) * sig_n
    qc = q * _sigmoid(q)
    bc = lf
    for k in range(HG_LEVELS):
        d = 1 << k
        bc = bc + jnp.where(prow >= d, pltpu.roll(bc, d, axis=0), 0.0)

    a = jnp.zeros((c, c), F32)
    for lev in range(HG_LEVELS):
        m = c >> lev
        upper = jnp.bitwise_and(prow, m - 1) >= m // 2
        z = jnp.where(upper, qc, kc)
        if m == 2:
            zw = jnp.where(upper, z * f, z)
        else:
            dlt = bc - _hg_midpoint(bc, m)
            zw = z * jnp.exp2(dlt * jnp.where(upper, LOG2E, -LOG2E))
        zw = zw.astype(BF16)
        s = lax.dot_general(zw, zw, nt, preferred_element_type=F32)
        a = jnp.where(lvl == lev, s, a)
    vb = v.astype(BF16)
    o = jnp.dot(a.astype(BF16), vb, preferred_element_type=F32)
    o = o + jnp.sum(qc * kc, axis=-1, keepdims=True) * v
    o = o + lax.dot_general((qc * jnp.exp(bc)).astype(BF16), st.astype(BF16), nt,
                            preferred_element_type=F32)
    bl = bc[c - 1:c, :]
    khat = (kc * jnp.exp(bl - bc)).astype(BF16)
    st = st * jnp.exp(bl) + lax.dot_general(vb, khat, tn, preferred_element_type=F32)
    return _rms(o, ng) * (gt * _sigmoid(gt)), st


def _hgrn_kernel(q_ref, f_ref, i_ref, gt_ref, lb_ref, lbf_ref, ng_ref, lvl_ref, o_ref, st_ref):
    c = HG_CHUNK

    @pl.when(pl.program_id(0) == 0)
    def _():
        st_ref[...] = jnp.zeros_like(st_ref)

    def chunk(n, carry):
        rows = pl.ds(pl.multiple_of(n * c, c), c)
        for hd in range(HG_HEADS):
            cols = slice(hd * LANES, (hd + 1) * LANES)
            o, st = _hg_chunk(f_ref[rows, cols], q_ref[rows, cols], i_ref[rows, cols], gt_ref[rows, cols],
                              lb_ref[:, cols], lbf_ref[:, cols], ng_ref[:, cols], lvl_ref[...], st_ref[hd])
            o_ref[rows, cols] = o
            st_ref[hd] = st
        return carry

    lax.fori_loop(0, q_ref.shape[0] // c, chunk, 0)


def _hgrn(proj, lb, norm_g):
    t = proj.shape[0]
    tb = min(HG_TBLOCK, t)
    lbh = jnp.clip(lb.astype(F32), 0.0, 1.0 - 1e-6).reshape(1, HG_WIDTH)
    lbf = jnp.maximum(lbh, LB_FLOOR)
    ng = norm_g.astype(F32).reshape(1, HG_WIDTH)
    lvl = jnp.asarray(_hg_level_table())
    cb = COL_HG // HG_WIDTH

    def col(k):
        return pl.BlockSpec((tb, HG_WIDTH), lambda i, k=k: (i, cb + k))

    vec = pl.BlockSpec((1, HG_WIDTH), lambda i: (0, 0))
    return pl.pallas_call(
        _hgrn_kernel,
        grid=(t // tb,),
        in_specs=[col(0), col(1), col(2), col(3), vec, vec, vec,
                  pl.BlockSpec(lvl.shape, lambda i: (0, 0))],
        out_specs=pl.BlockSpec((tb, HG_WIDTH), lambda i: (i, 0)),
        out_shape=jax.ShapeDtypeStruct((t, HG_WIDTH), F32),
        scratch_shapes=[pltpu.VMEM((HG_HEADS, LANES, LANES), F32)],
        compiler_params=_cparams("arbitrary"),
        name="hgrn2",
    )(proj, proj, proj, proj, lbh, lbf, ng, lvl)


def _mix_out_kernel(ys_ref, u_ref, cb_ref, cc_ref, cv_ref, hg_ref, h_ref,
                    d_ref, glu_ref, sn_ref, cw_ref, cn_ref, wo_ref, o_ref, carry_ref):
    @pl.when(pl.program_id(0) == 0)
    def _():
        carry_ref[...] = jnp.zeros_like(carry_ref)

    y = ys_ref[...] + d_ref[...] * u_ref[...]
    y = jax.nn.gelu(y)
    y = y * _sigmoid(jnp.dot(y.astype(BF16), glu_ref[...], preferred_element_type=F32))
    y_s5 = _rms(y, sn_ref[...])

    z = cc_ref[...] * cv_ref[...]
    tm = z.shape[0]
    row = lax.broadcasted_iota(I32, z.shape, 0)
    p1 = carry_ref[7:8, :]
    p2 = carry_ref[6:7, :]
    z1 = jnp.where(row == 0, p1, pltpu.roll(z, 1, axis=0))
    z2 = jnp.where(row == 0, p2, jnp.where(row == 1, p1, pltpu.roll(z, 2, axis=0)))
    carry_ref[...] = z[tm - 8:tm, :]
    yc = cb_ref[...] * (z2 * cw_ref[0:1, :] + z1 * cw_ref[1:2, :] + z * cw_ref[2:3, :])
    y_cv = _rms(yc, cn_ref[...])

    acc = h_ref[...]
    hg_row = S5_WIDTH + CONV_WIDTH
    acc = acc + jnp.dot(y_s5.astype(BF16), wo_ref[0:S5_WIDTH, :], preferred_element_type=F32)
    acc = acc + jnp.dot(y_cv.astype(BF16), wo_ref[S5_WIDTH:hg_row, :], preferred_element_type=F32)
    acc = acc + jnp.dot(hg_ref[...].astype(BF16), wo_ref[hg_row:, :], preferred_element_type=F32)
    o_ref[...] = acc


def _mix_out(ys5, proj, yhg, h, s5_d, s5_glu, s5_norm, conv_w, conv_norm, w_out):
    t, d = h.shape
    tm = min(ROW_TILE, t)
    cw = jnp.pad(conv_w.astype(F32), ((0, 8 - CONV_K), (0, 0)))
    wq = S5_WIDTH

    def rowblk(width, colblk):
        return pl.BlockSpec((tm, width), lambda i, c=colblk: (i, c))

    def full(shape):
        return pl.BlockSpec(shape, lambda i: (0,) * len(shape))

    return pl.pallas_call(
        _mix_out_kernel,
        grid=(t // tm,),
        in_specs=[rowblk(wq, 0), rowblk(wq, 0), rowblk(wq, 1), rowblk(wq, 2), rowblk(wq, 3),
                  rowblk(HG_WIDTH, 0), rowblk(d, 0),
                  full((1, wq)), full((wq, wq)), full((1, wq)), full((8, wq)), full((1, wq)),
                  full(w_out.shape)],
        out_specs=rowblk(d, 0),
        out_shape=jax.ShapeDtypeStruct((t, d), F32),
        scratch_shapes=[pltpu.VMEM((8, CONV_WIDTH), F32)],
        compiler_params=_cparams("arbitrary"),
        name="mix_out",
    )(ys5, proj, proj, proj, proj, yhg, h,
      s5_d.astype(F32).reshape(1, wq), s5_glu, s5_norm.astype(F32).reshape(1, wq), cw,
      conv_norm.astype(F32).reshape(1, wq), w_out)


def _ffn_kernel(h_ref, g_ref, w1_ref, w3_ref, w2_ref, o_ref, hn_ref, acc_ref):
    j = pl.program_id(1)

    @pl.when(j == 0)
    def _():
        x = h_ref[...]
        hn_ref[...] = _rms(x, g_ref[...]).astype(BF16)
        acc_ref[...] = x

    hn = hn_ref[...]
    a = jnp.dot(hn, w1_ref[...].astype(BF16), preferred_element_type=F32)
    b = jnp.dot(hn, w3_ref[...].astype(BF16), preferred_element_type=F32)
    gact = (a * _sigmoid(a) * b).astype(BF16)
    acc_ref[...] += jnp.dot(gact, w2_ref[...].astype(BF16), preferred_element_type=F32)

    @pl.when(j == pl.num_programs(1) - 1)
    def _():
        o_ref[...] = acc_ref[...]


def _ffn(h, g, w1, w3, w2, layer):
    t, d = h.shape
    f = w1.shape[2]
    tm = min(FFN_ROW_TILE, t)
    tf = FFN_COL_TILE
    return pl.pallas_call(
        _ffn_kernel,
        grid=(t // tm, f // tf),
        in_specs=[pl.BlockSpec((tm, d), lambda i, j: (i, 0)),
                  pl.BlockSpec((1, d), lambda i, j: (0, 0)),
                  pl.BlockSpec((None, d, tf), lambda i, j: (layer, 0, j)),
                  pl.BlockSpec((None, d, tf), lambda i, j: (layer, 0, j)),
                  pl.BlockSpec((None, tf, d), lambda i, j: (layer, j, 0))],
        out_specs=pl.BlockSpec((tm, d), lambda i, j: (i, 0)),
        out_shape=jax.ShapeDtypeStruct((t, d), F32),
        scratch_shapes=[pltpu.VMEM((tm, d), BF16), pltpu.VMEM((tm, d), F32)],
        compiler_params=_cparams("parallel", "arbitrary"),
        name="ffn_swiglu",
    )(h, g.astype(F32).reshape(1, d), w1, w3, w2)


def _router_kernel(h_ref, g_ref, r_ref, hn_ref, info_ref, info_t_ref, before_ref, total_ref, cnt_ref, *, n_exp):
    @pl.when(pl.program_id(0) == 0)
    def _():
        cnt_ref[...] = jnp.zeros_like(cnt_ref)

    xn = _rms(h_ref[...], g_ref[...])
    x_hi, x_lo = _bf16_terms(xn, 2)
    hn_ref[...] = x_hi
    r_hi, r_lo = _bf16_terms(r_ref[...], 2)
    logits = (jnp.dot(x_hi, r_hi, preferred_element_type=F32) + jnp.dot(x_lo, r_hi, preferred_element_type=F32)
              + jnp.dot(x_hi, r_lo, preferred_element_type=F32))
    tm = logits.shape[0]
    lane = lax.broadcasted_iota(I32, logits.shape, 1)
    lanef = lane.astype(F32)
    neg = jnp.float32(-jnp.inf)
    lg = jnp.where(lane < n_exp, logits, neg)
    m1 = jnp.max(lg, axis=-1, keepdims=True)
    i1 = jnp.min(jnp.where(lg == m1, lanef, float(LANES)), axis=-1, keepdims=True)
    oh1 = lanef == i1
    lg2 = jnp.where(oh1, neg, lg)
    m2 = jnp.max(lg2, axis=-1, keepdims=True)
    i2 = jnp.min(jnp.where(lg2 == m2, lanef, float(LANES)), axis=-1, keepdims=True)
    oh2 = lanef == i2
    ex = jnp.exp(m2 - m1)
    g1 = 1.0 / (1.0 + ex)
    g2 = ex * g1
    chosen = jnp.where(oh1, 1.0, jnp.where(oh2, 1.0, 0.0))
    ri = lax.broadcasted_iota(I32, (tm, tm), 0)
    ci = lax.broadcasted_iota(I32, (tm, tm), 1)
    tri = jnp.where(ri > ci, 1.0, 0.0).astype(BF16)
    before = cnt_ref[...]
    cexcl = jnp.dot(tri, chosen.astype(BF16), preferred_element_type=F32) + before
    rank1 = jnp.sum(jnp.where(oh1, cexcl, 0.0), axis=-1, keepdims=True)
    rank2 = jnp.sum(jnp.where(oh2, cexcl, 0.0), axis=-1, keepdims=True)
    info = jnp.where(lane == 0, i1, jnp.where(lane == 1, i2, jnp.where(lane == 2, g1, jnp.where(
        lane == 3, g2, jnp.where(lane == 4, rank1, jnp.where(lane == 5, rank2, 0.0))))))
    info_ref[...] = info
    info_t_ref[...] = info.T[0:SUBLANES, :]
    before_ref[0] = jnp.broadcast_to(before, before_ref.shape[1:])
    total = before + jnp.sum(chosen, axis=0, keepdims=True)
    cnt_ref[...] = total
    total_ref[...] = jnp.broadcast_to(total, total_ref.shape)


def _router(h, g, router):
    t, d = h.shape
    n_exp = router.shape[1]
    tm = min(TOK_WIN, t)
    ntw = t // tm
    rp = jnp.pad(router.astype(F32), ((0, 0), (0, LANES - n_exp)))
    return pl.pallas_call(
        functools.partial(_router_kernel, n_exp=n_exp),
        grid=(ntw,),
        in_specs=[pl.BlockSpec((tm, d), lambda i: (i, 0)),
                  pl.BlockSpec((1, d), lambda i: (0, 0)),
                  pl.BlockSpec((d, LANES), lambda i: (0, 0))],
        out_specs=[pl.BlockSpec((tm, d), lambda i: (i, 0)),
                   pl.BlockSpec((tm, LANES), lambda i: (i, 0)),
                   pl.BlockSpec((SUBLANES, tm), lambda i: (0, i)),
                   pl.BlockSpec((1, 8, LANES), lambda i: (i, 0, 0)),
                   pl.BlockSpec((8, LANES), lambda i: (0, 0))],
        out_shape=[jax.ShapeDtypeStruct((t, d), BF16),
                   jax.ShapeDtypeStruct((t, LANES), F32),
                   jax.ShapeDtypeStruct((SUBLANES, t), F32),
                   jax.ShapeDtypeStruct((ntw, 8, LANES), F32),
                   jax.ShapeDtypeStruct((8, LANES), F32)],
        scratch_shapes=[pltpu.VMEM((1, LANES), F32)],
        compiler_params=_cparams("arbitrary"),
        name="moe_router",
    )(h, g.astype(F32).reshape(1, d), rp)


def _gather_kernel(off_ref, nw_ref, wl_ref, tot_ref, dest_ref, hn_hbm, xs_ref, buf_ref, sem, acc_ref):
    b = pl.program_id(0)
    rows = acc_ref.shape[0]
    n_buf, win = buf_ref.shape[0], buf_ref.shape[1]
    n = nw_ref[b]
    q0 = off_ref[b]
    total = tot_ref[0]

    def copy(q):
        w = wl_ref[q]
        s = lax.rem(q, n_buf)
        return pltpu.make_async_copy(hn_hbm.at[pl.ds(pl.multiple_of(w * win, win), win), :], buf_ref.at[s],
                                     sem.at[s])

    @pl.when(b == 0)
    def _():
        for q in range(GATHER_AHEAD):
            @pl.when(q < total)
            def _(q=q):
                copy(q).start()

    acc_ref[...] = jnp.zeros_like(acc_ref)
    slot = b * rows + lax.broadcasted_iota(I32, (rows, win), 0)

    def gathered(q):
        @pl.when(q + GATHER_AHEAD < total)
        def _():
            copy(q + GATHER_AHEAD).start()

        col = pl.ds(pl.multiple_of(wl_ref[q] * win, win), win)
        d1 = dest_ref[0:1, col]
        d2 = dest_ref[1:2, col]
        hit = jnp.where(d1 == slot, 1.0, jnp.where(d2 == slot, 1.0, 0.0)).astype(BF16)
        copy(q).wait()
        return jnp.dot(hit, buf_ref[lax.rem(q, n_buf)], preferred_element_type=F32)

    def several(j, carry):
        q = q0 + GATHER_UNROLL * j
        acc_ref[...] += sum(gathered(q + r) for r in range(GATHER_UNROLL))
        return carry

    n_full = n // GATHER_UNROLL
    lax.fori_loop(0, n_full, several, 0)

    def single(j, carry):
        acc_ref[...] += gathered(q0 + n_full * GATHER_UNROLL + j)
        return carry

    lax.fori_loop(0, n - n_full * GATHER_UNROLL, single, 0)
    xs_ref[...] = acc_ref[...].astype(BF16)


def _gather(hn, dest, off, nw, wlist, total, n_slots):
    t, d = hn.shape
    return pl.pallas_call(
        _gather_kernel,
        grid_spec=pltpu.PrefetchScalarGridSpec(
            num_scalar_prefetch=4,
            grid=(n_slots // GATHER_TILE,),
            in_specs=[pl.BlockSpec(dest.shape, lambda b, *_: (0, 0)),
                      pl.BlockSpec(memory_space=pl.ANY)],
            out_specs=pl.BlockSpec((GATHER_TILE, d), lambda b, *_: (b, 0)),
            scratch_shapes=[pltpu.VMEM((GATHER_BUFS, TOK_WIN, d), BF16),
                            pltpu.SemaphoreType.DMA((GATHER_BUFS,)),
                            pltpu.VMEM((GATHER_TILE, d), F32)]),
        out_shape=jax.ShapeDtypeStruct((n_slots, d), BF16),
        compiler_params=_cparams("arbitrary"),
        name="moe_gather",
    )(off, nw, wlist, total, dest, hn)


def _expert_kernel(te_ref, tv_ref, x_ref, w1_ref, w3_ref, w2_ref, y_ref, acc_ref):
    b = pl.program_id(0)
    j = pl.program_id(1)

    @pl.when(j == 0)
    def _():
        acc_ref[...] = jnp.zeros_like(acc_ref)

    @pl.when(tv_ref[b] == 1)
    def _():
        x = x_ref[...]
        a = jnp.dot(x, w1_ref[...].astype(BF16), preferred_element_type=F32)
        c = jnp.dot(x, w3_ref[...].astype(BF16), preferred_element_type=F32)
        gact = (a * _sigmoid(a) * c).astype(BF16)
        acc_ref[...] += jnp.dot(gact, w2_ref[...].astype(BF16), preferred_element_type=F32)

    @pl.when(j == pl.num_programs(1) - 1)
    def _():
        y_ref[...] = acc_ref[...].astype(BF16)


def _experts(xs, tile_e, tile_v, w1, w3, w2, layer):
    ns, d = xs.shape
    f = w1.shape[3]
    tf = MOE_COL_TILE
    nj = f // tf
    n_tiles = ns // MOE_TILE

    def col(b, j, tv):
        return j * tv[b] + (nj - 1) * (1 - tv[b])

    return pl.pallas_call(
        _expert_kernel,
        grid_spec=pltpu.PrefetchScalarGridSpec(
            num_scalar_prefetch=2,
            grid=(n_tiles, nj),
            in_specs=[pl.BlockSpec((MOE_TILE, d), lambda b, j, te, tv: (b, 0)),
                      pl.BlockSpec((None, None, d, tf), lambda b, j, te, tv: (layer, te[b], 0, col(b, j, tv))),
                      pl.BlockSpec((None, None, d, tf), lambda b, j, te, tv: (layer, te[b], 0, col(b, j, tv))),
                      pl.BlockSpec((None, None, tf, d), lambda b, j, te, tv: (layer, te[b], col(b, j, tv), 0))],
            out_specs=pl.BlockSpec((MOE_TILE, d), lambda b, j, te, tv: (b, 0)),
            scratch_shapes=[pltpu.VMEM((MOE_TILE, d), F32)]),
        out_shape=jax.ShapeDtypeStruct((ns, d), BF16),
        compiler_params=_cparams("arbitrary", "arbitrary"),
        name="moe_experts",
    )(tile_e, tile_v, xs, w1, w3, w2)


def _combine_kernel(ws_ref, po_ref, ex_ref, nar_ref, info_ref, h_ref, fin_ref, yb_hbm, o_ref,
                    buf_ref, nbuf_ref, sem, *, n_exp, final_norm):
    i = pl.program_id(0)
    n_steps = pl.num_programs(0)
    n_set = COMBINE_SETS
    cur = lax.rem(i, n_set)

    def wide_copy(step, e, s):
        start = pl.multiple_of(ws_ref[step * n_exp + e], SLOT_ALIGN)
        return pltpu.make_async_copy(yb_hbm.at[pl.ds(start, SLOT_WIN), :], buf_ref.at[s, e], sem.at[s, e])

    def narrow_copy(step, e, s):
        start = pl.multiple_of(ws_ref[step * n_exp + e], SLOT_ALIGN)
        return pltpu.make_async_copy(yb_hbm.at[pl.ds(start, NARROW_WIN), :],
                                     nbuf_ref.at[s, e * NARROW_WIN:(e + 1) * NARROW_WIN, :], sem.at[s, e])

    def for_step(step, s, act):
        @pl.when(nar_ref[step] == 1)
        def _():
            for e in range(n_exp):
                act(narrow_copy(step, e, s))

        @pl.when(nar_ref[step] == 0)
        def _():
            for e in range(n_exp):
                act(wide_copy(step, e, s))

    @pl.when(i == 0)
    def _():
        for s in range(n_set - 1):
            @pl.when(s < n_steps)
            def _(s=s):
                for_step(s, s, lambda c: c.start())

    ahead = i + n_set - 1

    @pl.when(ahead < n_steps)
    def _():
        for_step(ahead, lax.rem(ahead, n_set), lambda c: c.start())

    info = info_ref[...]
    tm = info.shape[0]
    e1, e2 = info[:, 0:1], info[:, 1:2]
    g1, g2 = info[:, 2:3], info[:, 3:4]
    r1, r2 = info[:, 4:5], info[:, 5:6]

    def rank_offset(e, shift):
        return (ws_ref[i * n_exp + e] - po_ref[e] + shift).astype(F32)

    for_step(i, cur, lambda c: c.wait())

    @pl.when(nar_ref[i] == 1)
    def _():
        k1 = jnp.full_like(r1, -1.0)
        k2 = jnp.full_like(r2, -1.0)
        for e in range(n_exp):
            k1 = jnp.where(e1 == float(e), r1 - rank_offset(e, -e * NARROW_WIN), k1)
            k2 = jnp.where(e2 == float(e), r2 - rank_offset(e, -e * NARROW_WIN), k2)
        width = n_exp * NARROW_WIN
        lanef = lax.broadcasted_iota(I32, (tm, width), 1).astype(F32)
        pt = jnp.where(k1 == lanef, jnp.broadcast_to(g1, (tm, width)),
                       jnp.where(k2 == lanef, jnp.broadcast_to(g2, (tm, width)), 0.0)).astype(BF16)
        o_ref[...] = h_ref[...] + jnp.dot(pt, nbuf_ref[cur], preferred_element_type=F32)

    @pl.when(nar_ref[i] == 0)
    def _():
        lanef = lax.broadcasted_iota(I32, (tm, TOK_WIN), 1).astype(F32)
        g1b = jnp.broadcast_to(g1, (tm, TOK_WIN))
        g2b = jnp.broadcast_to(g2, (tm, TOK_WIN))

        def weights(e, shift, lo_lane):
            off = rank_offset(e, shift)
            k1 = jnp.where(e1 == float(e), r1 - off, -1.0)
            k2 = jnp.where(e2 == float(e), r2 - off, -1.0)
            k1 = jnp.where(k1 >= float(lo_lane), k1, -1.0)
            k2 = jnp.where(k2 >= float(lo_lane), k2, -1.0)
            return jnp.where(k1 == lanef, g1b, jnp.where(k2 == lanef, g2b, 0.0)).astype(BF16)

        acc = h_ref[...]
        for e in range(n_exp):
            acc = acc + jnp.dot(weights(e, 0, 0), buf_ref[cur, e, 0:TOK_WIN, :], preferred_element_type=F32)
        o_ref[...] = acc

        tail = SLOT_WIN - TOK_WIN
        for e in range(n_exp):
            @pl.when(ex_ref[i * n_exp + e] == 1)
            def _(e=e):
                pt = weights(e, tail, TOK_WIN - tail)
                o_ref[...] += jnp.dot(pt, buf_ref[cur, e, tail:SLOT_WIN, :], preferred_element_type=F32)

    if final_norm:
        o_ref[...] = _rms(o_ref[...], fin_ref[...])


def _combine(info, h, yb, win_start, pstart, extra, narrow, fin_g, final_norm):
    t, d = h.shape
    n_exp = pstart.shape[0]
    tm = min(TOK_WIN, t)
    return pl.pallas_call(
        functools.partial(_combine_kernel, n_exp=n_exp, final_norm=final_norm),
        grid_spec=pltpu.PrefetchScalarGridSpec(
            num_scalar_prefetch=4,
            grid=(t // tm,),
            in_specs=[pl.BlockSpec((tm, LANES), lambda i, *_: (i, 0)),
                      pl.BlockSpec((tm, d), lambda i, *_: (i, 0)),
                      pl.BlockSpec((1, d), lambda i, *_: (0, 0)),
                      pl.BlockSpec(memory_space=pl.ANY)],
            out_specs=pl.BlockSpec((tm, d), lambda i, *_: (i, 0)),
            scratch_shapes=[pltpu.VMEM((COMBINE_SETS, n_exp, SLOT_WIN, d), BF16),
                            pltpu.VMEM((COMBINE_SETS, n_exp * NARROW_WIN, d), BF16),
                            pltpu.SemaphoreType.DMA((COMBINE_SETS, n_exp))]),
        out_shape=jax.ShapeDtypeStruct((t, d), F32),
        compiler_params=_cparams("arbitrary"),
        name="moe_combine",
    )(win_start, pstart, extra, narrow, info, h, fin_g.astype(F32).reshape(1, d), yb)


def _count_le(sorted_vals, x):
    return jnp.sum((sorted_vals[None, :] <= x[:, None]).astype(I32), axis=1)


def _moe(h, g, router, w1, w3, w2, layer, fin_g, final_norm):
    t, d = h.shape
    n_exp = router.shape[1]
    tw = min(TOK_WIN, t)
    ntw = t // tw
    hn, info, info_t, before, total = _router(h, g, router)

    counts = total[0, :n_exp].astype(I32)
    padded = (counts + MOE_TILE - 1) // MOE_TILE * MOE_TILE
    pend = jnp.cumsum(padded)
    pstart = pend - padded
    n_tiles = (t * TOP_K) // MOE_TILE + n_exp
    n_slots = n_tiles * MOE_TILE
    tile0 = jnp.arange(n_tiles, dtype=I32) * MOE_TILE
    tile_e = jnp.minimum(_count_le(pend, tile0), n_exp - 1)
    tile_v = (tile0 < pend[-1]).astype(I32)
    cum = jnp.concatenate([before[:, 0, :n_exp], total[0:1, :n_exp]], axis=0).astype(I32)

    n_gb = n_slots // GATHER_TILE
    gb0 = jnp.arange(n_gb, dtype=I32) * GATHER_TILE
    gb_e = jnp.minimum(_count_le(pend, gb0), n_exp - 1)
    r0 = gb0 - pstart[gb_e]
    cum_b = cum[:, gb_e]
    ilo = jnp.sum((cum_b[1:] <= r0[None, :]).astype(I32), axis=0)
    ihi = jnp.sum((cum_b[:-1] < (r0 + GATHER_TILE)[None, :]).astype(I32), axis=0) - 1
    n_work = jnp.where(gb0 < pend[-1], jnp.maximum(ihi - ilo + 1, 0), 0).astype(I32)
    ends = jnp.cumsum(n_work)
    offs = (ends - n_work).astype(I32)
    entry = jnp.arange(n_exp * ntw + n_gb, dtype=I32)
    entry_blk = jnp.minimum(_count_le(ends, entry), n_gb - 1)
    wlist = jnp.clip(ilo[entry_blk] + entry - offs[entry_blk], 0, ntw - 1).astype(I32)

    e_t = info_t[0:2].astype(I32)
    slot_t = jnp.sum(jnp.where(e_t[None] == jnp.arange(n_exp, dtype=I32)[:, None, None], pstart[:, None, None], 0),
                     axis=0) + info_t[4:6].astype(I32)
    dest = jnp.concatenate([slot_t, jnp.full((SUBLANES - TOP_K, t), -1, I32)], axis=0)

    xs = _gather(hn, dest, offs, n_work, wlist, ends[-1:].astype(I32), n_slots)
    yb = _experts(xs, tile_e, tile_v, w1, w3, w2, layer)

    lo = pstart[None, :] + cum[:-1]
    win_start = jnp.minimum(lo // SLOT_ALIGN * SLOT_ALIGN, n_slots - SLOT_WIN).astype(I32)
    reach = lo - win_start + cum[1:] - cum[:-1]
    extra = (reach > TOK_WIN).astype(I32)
    narrow = jnp.all(reach <= NARROW_WIN, axis=1).astype(I32)
    return _combine(info, h, yb, win_start.reshape(-1), pstart.astype(I32), extra.reshape(-1), narrow, fin_g,
                    final_norm)


def _final_norm_kernel(h_ref, g_ref, o_ref):
    o_ref[...] = _rms(h_ref[...], g_ref[...])


def _final_norm(h, g):
    t, d = h.shape
    tm = min(ROW_TILE, t)
    return pl.pallas_call(
        _final_norm_kernel,
        grid=(t // tm,),
        in_specs=[pl.BlockSpec((tm, d), lambda i: (i, 0)), pl.BlockSpec((1, d), lambda i: (0, 0))],
        out_specs=pl.BlockSpec((tm, d), lambda i: (i, 0)),
        out_shape=jax.ShapeDtypeStruct((t, d), F32),
        compiler_params=_cparams("parallel"),
        name="final_norm",
    )(h, g.astype(F32).reshape(1, d))


def kernel(x, attn_norm, ffn_norm, final_norm, w_in, w_out, s5_lambda_re, s5_lambda_im, s5_log_dt, s5_b_re, s5_b_im, s5_c_re, s5_c_im, s5_d, s5_glu, s5_out_norm, conv_w, conv_out_norm, hg_lower_bounds, hg_out_norm, ffn_w1, ffn_w3, ffn_w2, moe_router, moe_w1, moe_w3, moe_w2):
    bsz, seq, d = x.shape
    depth = w_in.shape[0]
    t = bsz * seq
    assert bsz == 1, "token mixers are written for a single sequence"
    lb_soft = jax.nn.softmax(hg_lower_bounds.astype(F32), axis=0)
    lb_all = jnp.cumsum(lb_soft, axis=0) - lb_soft[0]
    n_scan = int(math.log2(t // S5_CHUNK))
    s5_ops = jax.vmap(functools.partial(_s5_operators, n_scan=n_scan))(
        s5_lambda_re, s5_lambda_im, s5_log_dt, s5_b_re, s5_b_im, s5_c_re, s5_c_im)
    h = x.reshape(t, d).astype(F32)
    for l in range(depth):
        proj = _norm_inproj(h, attn_norm[l].astype(F32), w_in[l].astype(BF16))
        ys5 = _s5_conv(proj, [op[l] for op in s5_ops])
        yhg = _hgrn(proj, lb_all[l], hg_out_norm[l])
        h = _mix_out(ys5, proj, yhg, h, s5_d[l], s5_glu[l].astype(BF16), s5_out_norm[l],
                     conv_w[l], conv_out_norm[l], w_out[l].astype(BF16))
        j = l // 2
        if l % 2 == 0:
            h = _ffn(h, ffn_norm[l], ffn_w1, ffn_w3, ffn_w2, j)
        else:
            h = _moe(h, ffn_norm[l], moe_router[j], moe_w1, moe_w3, moe_w2, j, final_norm, l == depth - 1)
    if depth % 2 == 1:
        h = _final_norm(h, final_norm)
    return h.reshape(bsz, seq, d)
```

```python
import functools
import math

import numpy as np
import jax
import jax.numpy as jnp
from jax import lax
from jax.experimental import pallas as pl
from jax.experimental.pallas import tpu as pltpu

F32 = jnp.float32
BF16 = jnp.bfloat16
I32 = jnp.int32

NORM_EPS = 1e-6
LB_FLOOR = 1e-30
TOP_K = 2

S5_WIDTH = 256
S5_GROUP = 16
S5_STATE = 64
CONV_WIDTH = 256
CONV_K = 3
HG_WIDTH = 512
HG_HEAD_DIM = 128
HG_HEADS = HG_WIDTH // HG_HEAD_DIM
LANES = 128
SUBLANES = 8
COL_CONV = S5_WIDTH
COL_HG = S5_WIDTH + 3 * CONV_WIDTH

VMEM_LIMIT = 56 * 1024 * 1024
S5_CHUNK = 32
HG_CHUNK = 128
HG_LEVELS = int(math.log2(HG_CHUNK))
HG_TBLOCK = 512
ROW_TILE = 512
FFN_ROW_TILE = 1024
FFN_COL_TILE = 512
SWIGLU_SUB = 256
MOE_TILE = 1024
MOE_COL_TILE = 512
GATHER_TILE = 256
GATHER_AHEAD = 6
GATHER_UNROLL = 4
GATHER_BUFS = GATHER_AHEAD + GATHER_UNROLL
TOK_WIN = 256
SLOT_ALIGN = 16
SLOT_WIN = TOK_WIN + SLOT_ALIGN
NARROW_WIN = 128
COMBINE_SETS = 3


def _cparams(*sem):
    return pltpu.CompilerParams(dimension_semantics=sem, vmem_limit_bytes=VMEM_LIMIT)


def _rms(x, g):
    ms = jnp.mean(x * x, axis=-1, keepdims=True)
    return x * lax.rsqrt(ms + NORM_EPS) * g


def _sigmoid(x):
    return 1.0 / (1.0 + jnp.exp(-x))


def _norm_inproj_kernel(h_ref, g_ref, w_ref, o_ref):
    xn = _rms(h_ref[...], g_ref[...]).astype(BF16)
    o_ref[...] = jnp.dot(xn, w_ref[...], preferred_element_type=F32)


def _norm_inproj(h, g, w):
    t, d = h.shape
    n = w.shape[1]
    tm = min(ROW_TILE, t)
    return pl.pallas_call(
        _norm_inproj_kernel,
        grid=(t // tm,),
        in_specs=[pl.BlockSpec((tm, d), lambda i: (i, 0)),
                  pl.BlockSpec((1, d), lambda i: (0, 0)),
                  pl.BlockSpec((d, n), lambda i: (0, 0))],
        out_specs=pl.BlockSpec((tm, n), lambda i: (i, 0)),
        out_shape=jax.ShapeDtypeStruct((t, n), F32),
        compiler_params=_cparams("parallel"),
        name="norm_inproj",
    )(h, g.reshape(1, d), w)


def _s5_operators(lam_re, lam_im, log_dt, b_re, b_im, c_re, c_im, n_scan):
    lc = S5_CHUNK
    hi = lax.Precision.HIGHEST
    lr, li = lam_re.astype(F32), lam_im.astype(F32)
    dt = jnp.exp(log_dt.astype(F32))[:, None]
    zr, zi = lr * dt, li * dt
    taus = jnp.arange(lc + 1, dtype=F32)[:, None, None]
    mag = jnp.exp(zr[None] * taus)
    pwr, pwi = mag * jnp.cos(zi[None] * taus), mag * jnp.sin(zi[None] * taus)
    nr, ni = pwr[1] - 1.0, pwi[1]
    den = lr * lr + li * li
    qr, qi = (nr * lr + ni * li) / den, (ni * lr - nr * li) / den
    br, bi = b_re.astype(F32), b_im.astype(F32)
    bbr = qr[..., None] * br - qi[..., None] * bi
    bbi = qr[..., None] * bi + qi[..., None] * br
    cr, ci = c_re.astype(F32), c_im.astype(F32)
    g_, p_ = lr.shape

    def c_times_pw(lo):
        wr, wi = pwr[lo:lo + lc, :, None, :], pwi[lo:lo + lc, :, None, :]
        return cr[None] * wr - ci[None] * wi, cr[None] * wi + ci[None] * wr

    cpr, cpi = c_times_pw(0)
    cp = jnp.concatenate([cpr, cpi], axis=-1).transpose(1, 0, 2, 3).reshape(g_, lc * S5_GROUP, 2 * p_)
    kt = jnp.matmul(cp, jnp.concatenate([bbr, -bbi], axis=1), precision=hi)
    kflat = kt.transpose(0, 2, 1)

    wr, wi = pwr[lc - 1::-1][:, :, :, None], pwi[lc - 1::-1][:, :, :, None]
    msr = (wr * bbr[None] - wi * bbi[None]).transpose(1, 0, 3, 2).reshape(g_, lc * S5_GROUP, p_)
    msi = (wr * bbi[None] + wi * bbr[None]).transpose(1, 0, 3, 2).reshape(g_, lc * S5_GROUP, p_)
    m_state = jnp.concatenate([msr, msi], axis=-1)

    c1r, c1i = c_times_pw(1)
    c1r = c1r.transpose(1, 3, 0, 2).reshape(g_, p_, lc * S5_GROUP)
    c1i = c1i.transpose(1, 3, 0, 2).reshape(g_, p_, lc * S5_GROUP)
    m_carry = jnp.concatenate([c1r, -c1i], axis=1)

    akr, aki = [pwr[lc]], [pwi[lc]]
    for _ in range(n_scan - 1):
        r, i = akr[-1], aki[-1]
        akr.append(r * r - i * i)
        aki.append(2.0 * r * i)
    akr, aki = jnp.stack(akr, axis=1), jnp.stack(aki, axis=1)
    ar = jnp.concatenate([akr, akr], axis=-1)
    ai = jnp.concatenate([-aki, aki], axis=-1)
    kpad = -(-n_scan // 8) * 8
    ar = jnp.pad(ar, ((0, 0), (0, kpad - n_scan), (0, 0)))
    ai = jnp.pad(ai, ((0, 0), (0, kpad - n_scan), (0, 0)))
    return kflat, m_state.astype(BF16), m_carry.astype(BF16), ar, ai


S5_PER_TILE = LANES // S5_GROUP


def _s5_kernel(proj_hbm, kf_ref, ms_ref, mc_ref, ar_ref, ai_ref, y_hbm, us_ref, ys_ref, mi_ref, sem_in, sem_out,
               *, n_scan):
    lc, gw, per = S5_CHUNK, S5_GROUP, S5_PER_TILE
    nch = us_ref.shape[1]
    cols = pl.ds(pl.multiple_of(pl.program_id(0) * LANES, LANES), LANES)

    def in_copy(s):
        return pltpu.make_async_copy(proj_hbm.at[:, s, cols], us_ref.at[s], sem_in.at[s])

    def out_copy(s):
        return pltpu.make_async_copy(ys_ref.at[s], y_hbm.at[:, s, cols], sem_out.at[s])

    for s in range(lc):
        in_copy(s).start()
    ys_ref[...] = jnp.zeros_like(ys_ref)
    for s in range(lc):
        in_copy(s).wait()
        if s % per:
            us_ref[s] = pltpu.roll(us_ref[s], (s % per) * gw, axis=1)

    lane_grp = lax.broadcasted_iota(I32, (1, LANES), 1) // gw
    kf_lane = lax.broadcasted_iota(I32, (gw, lc * gw), 1)

    def group(gl, carry):
        rel_grp = jnp.bitwise_and(lane_grp - gl, per - 1)
        tiles = []
        for j in range(lc // per):
            merged = us_ref[j * per]
            for k in range(1, per):
                merged = jnp.where(rel_grp == k, us_ref[j * per + k], merged)
            tiles.append(pltpu.roll(merged, jnp.bitwise_and(-gl * gw, LANES - 1), axis=1))
        u = jnp.concatenate(tiles, axis=1).astype(BF16)

        kf = kf_ref[gl]
        for s in range(lc):
            blk = kf if s == 0 else jnp.where(kf_lane >= s * gw, pltpu.roll(kf, s * gw, axis=1), 0.0)
            mi_ref[s * gw:(s + 1) * gw, :] = blk.astype(BF16)

        x = jnp.dot(u, ms_ref[gl], preferred_element_type=F32)
        row = lax.broadcasted_iota(I32, x.shape, 0)
        half = x.shape[1] // 2
        for k in range(n_scan):
            d = 1 << k
            s = jnp.where(row >= d, pltpu.roll(x, d, axis=0), 0.0)
            x = x + ar_ref[gl, k:k + 1, :] * s + ai_ref[gl, k:k + 1, :] * pltpu.roll(s, half, axis=1)
        xe = jnp.where(row >= 1, pltpu.roll(x, 1, axis=0), 0.0)
        y = jnp.dot(u, mi_ref[...], preferred_element_type=F32)
        y = y + jnp.dot(xe.astype(BF16), mc_ref[gl], preferred_element_type=F32)

        for j in range(lc // per):
            moved = pltpu.roll(y[:, j * LANES:(j + 1) * LANES], jnp.bitwise_and(gl * gw, LANES - 1), axis=1)
            for k in range(per):
                ys_ref[j * per + k] = jnp.where(rel_grp == k, moved, ys_ref[j * per + k])
        return carry

    lax.fori_loop(0, per, group, 0)
    for s in range(lc):
        if s % per:
            ys_ref[s] = pltpu.roll(ys_ref[s], LANES - (s % per) * gw, axis=1)
        out_copy(s).start()
    for s in range(lc):
        out_copy(s).wait()


def _s5_conv(proj, ops):
    kflat, m_state, m_carry, ar, ai = ops
    t, n = proj.shape
    lc = S5_CHUNK
    nch = t // lc
    n_scan = int(math.log2(nch))
    assert (1 << n_scan) == nch
    w = lc * S5_GROUP
    p2 = m_state.shape[-1]
    per = S5_PER_TILE

    def grp(shape):
        return pl.BlockSpec((per,) + shape, lambda hh: (hh, 0, 0))

    hbm = pl.BlockSpec(memory_space=pl.ANY)
    y = pl.pallas_call(
        functools.partial(_s5_kernel, n_scan=n_scan),
        grid=(S5_WIDTH // LANES,),
        in_specs=[hbm, grp((S5_GROUP, w)), grp((w, p2)), grp((p2, w)),
                  grp((ar.shape[1], p2)), grp((ai.shape[1], p2))],
        out_specs=hbm,
        out_shape=jax.ShapeDtypeStruct((nch, lc, S5_WIDTH), F32),
        scratch_shapes=[pltpu.VMEM((lc, nch, LANES), F32), pltpu.VMEM((lc, nch, LANES), F32),
                        pltpu.VMEM((w, w), BF16), pltpu.SemaphoreType.DMA((lc,)), pltpu.SemaphoreType.DMA((lc,))],
        compiler_params=_cparams("arbitrary"),
        name="s5_conv",
    )(proj.reshape(nch, lc, n), kflat, m_state, m_carry, ar, ai)
    return y.reshape(t, S5_WIDTH)


LOG2E = float(np.log2(np.e))


def _hg_level_table():
    idx = np.arange(HG_CHUNK)
    t, s = idx[:, None], idx[None, :]
    top_bit = np.floor(np.log2(np.maximum(t ^ s, 1))).astype(np.int32)
    return np.where(s < t, HG_LEVELS - 1 - top_bit, -1).astype(np.int32)


def _hg_midpoint(bc, m):
    c = bc.shape[0]
    h = m // 2
    if h >= SUBLANES:
        return jnp.concatenate([jnp.broadcast_to(bc[j * m + h - 1:j * m + h, :], (m, LANES))
                                for j in range(c // m)], axis=0)
    x3 = bc.reshape(c // SUBLANES, SUBLANES, LANES)
    sub = lax.broadcasted_iota(I32, x3.shape, 1)
    beta = None
    for j in reversed(range(SUBLANES // m)):
        row = jnp.broadcast_to(x3[:, j * m + h - 1:j * m + h, :], x3.shape)
        beta = row if beta is None else jnp.where(sub < (j + 1) * m, row, beta)
    return beta.reshape(c, LANES)


def _bf16_terms(x, n):
    terms = []
    for _ in range(n - 1):
        t = x.astype(BF16)
        terms.append(t)
        x = x - t.astype(F32)
    return terms + [x.astype(BF16)]


def _hg_chunk(fp, q, v, gt, lb, lbf, ng, lvl, st):
    c = HG_CHUNK
    prow = lax.broadcasted_iota(I32, (c, LANES), 0)
    nt = (((1,), (1,)), ((), ()))
    tn = (((0,), (0,)), ((), ()))
    en = jnp.exp(-jnp.abs(fp))
    rc = 1.0 / (1.0 + en)
    pos_f = fp >= 0.0
    sig_p = jnp.where(pos_f, rc, en * rc)
    sig_n = jnp.where(pos_f, en * rc, rc)
    f = lbf + (1.0 - lb) * sig_p
    lf = jnp.log(f)
    kc = (1.0 - lb) * sig_n
    qc = q * _sigmoid(q)
    bc = lf
    for k in range(HG_LEVELS):
        d = 1 << k
        bc = bc + jnp.where(prow >= d, pltpu.roll(bc, d, axis=0), 0.0)

    a = jnp.zeros((c, c), F32)
    for lev in range(HG_LEVELS):
        m = c >> lev
        upper = jnp.bitwise_and(prow, m - 1) >= m // 2
        z = jnp.where(upper, qc, kc)
        if m == 2:
            zw = jnp.where(upper, z * f, z)
        else:
            dlt = bc - _hg_midpoint(bc, m)
            zw = z * jnp.exp2(dlt * jnp.where(upper, LOG2E, -LOG2E))
        zw = zw.astype(BF16)
        s = lax.dot_general(zw, zw, nt, preferred_element_type=F32)
        a = jnp.where(lvl == lev, s, a)
    vb = v.astype(BF16)
    o = jnp.dot(a.astype(BF16), vb, preferred_element_type=F32)
    o = o + jnp.sum(qc * kc, axis=-1, keepdims=True) * v
    o = o + lax.dot_general((qc * jnp.exp(bc)).astype(BF16), st.astype(BF16), nt,
                            preferred_element_type=F32)
    bl = bc[c - 1:c, :]
    khat = (kc * jnp.exp(bl - bc)).astype(BF16)
    st = st * jnp.exp(bl) + lax.dot_general(vb, khat, tn, preferred_element_type=F32)
    return _rms(o, ng) * (gt * _sigmoid(gt)), st


def _hgrn_mix_kernel(q_ref, f_ref, i_ref, gt_ref, lb_ref, lbf_ref, ng_ref, lvl_ref,
                     ys_ref, u_ref, cb_ref, cc_ref, cv_ref, h_ref, d_ref, glu_ref, sn_ref, cw_ref, cn_ref, wo_ref,
                     o_ref, st_ref, hg_ref, carry_ref):
    c = HG_CHUNK

    @pl.when(pl.program_id(0) == 0)
    def _():
        st_ref[...] = jnp.zeros_like(st_ref)
        carry_ref[...] = jnp.zeros_like(carry_ref)

    def chunk(n, carry):
        rows = pl.ds(pl.multiple_of(n * c, c), c)
        for hd in range(HG_HEADS):
            cols = slice(hd * LANES, (hd + 1) * LANES)
            o, st = _hg_chunk(f_ref[rows, cols], q_ref[rows, cols], i_ref[rows, cols], gt_ref[rows, cols],
                              lb_ref[:, cols], lbf_ref[:, cols], ng_ref[:, cols], lvl_ref[...], st_ref[hd])
            hg_ref[rows, cols] = o.astype(BF16)
            st_ref[hd] = st
        return carry

    lax.fori_loop(0, q_ref.shape[0] // c, chunk, 0)
    _mix_epilogue(ys_ref, u_ref, cb_ref, cc_ref, cv_ref, hg_ref, h_ref, d_ref, glu_ref, sn_ref, cw_ref, cn_ref,
                  wo_ref, o_ref, carry_ref)


def _mix_epilogue(ys_ref, u_ref, cb_ref, cc_ref, cv_ref, hg_ref, h_ref,
                  d_ref, glu_ref, sn_ref, cw_ref, cn_ref, wo_ref, o_ref, carry_ref):
    y = ys_ref[...] + d_ref[...] * u_ref[...]
    y = jax.nn.gelu(y)
    y = y * _sigmoid(jnp.dot(y.astype(BF16), glu_ref[...], preferred_element_type=F32))
    y_s5 = _rms(y, sn_ref[...])

    z = cc_ref[...] * cv_ref[...]
    tm = z.shape[0]
    row = lax.broadcasted_iota(I32, z.shape, 0)
    p1 = carry_ref[7:8, :]
    p2 = carry_ref[6:7, :]
    z1 = jnp.where(row == 0, p1, pltpu.roll(z, 1, axis=0))
    z2 = jnp.where(row == 0, p2, jnp.where(row == 1, p1, pltpu.roll(z, 2, axis=0)))
    carry_ref[...] = z[tm - 8:tm, :]
    yc = cb_ref[...] * (z2 * cw_ref[0:1, :] + z1 * cw_ref[1:2, :] + z * cw_ref[2:3, :])
    y_cv = _rms(yc, cn_ref[...])

    acc = h_ref[...]
    hg_row = S5_WIDTH + CONV_WIDTH
    acc = acc + jnp.dot(y_s5.astype(BF16), wo_ref[0:S5_WIDTH, :], preferred_element_type=F32)
    acc = acc + jnp.dot(y_cv.astype(BF16), wo_ref[S5_WIDTH:hg_row, :], preferred_element_type=F32)
    acc = acc + jnp.dot(hg_ref[...], wo_ref[hg_row:, :], preferred_element_type=F32)
    o_ref[...] = acc


def _hgrn_mix(ys5, proj, h, lb, hg_norm, s5_d, s5_glu, s5_norm, conv_w, conv_norm, w_out):
    t, d = h.shape
    tb = min(HG_TBLOCK, t)
    lbh = jnp.clip(lb.astype(F32), 0.0, 1.0 - 1e-6).reshape(1, HG_WIDTH)
    lbf = jnp.maximum(lbh, LB_FLOOR)
    ng = hg_norm.astype(F32).reshape(1, HG_WIDTH)
    lvl = jnp.asarray(_hg_level_table())
    cw = jnp.pad(conv_w.astype(F32), ((0, SUBLANES - CONV_K), (0, 0)))
    wq = S5_WIDTH
    hg_col = COL_HG // HG_WIDTH

    def rowblk(width, colblk):
        return pl.BlockSpec((tb, width), lambda i, c=colblk: (i, c))

    def full(shape):
        return pl.BlockSpec(shape, lambda i: (0,) * len(shape))

    return pl.pallas_call(
        _hgrn_mix_kernel,
        grid=(t // tb,),
        in_specs=[rowblk(HG_WIDTH, hg_col), rowblk(HG_WIDTH, hg_col + 1), rowblk(HG_WIDTH, hg_col + 2),
                  rowblk(HG_WIDTH, hg_col + 3), full((1, HG_WIDTH)), full((1, HG_WIDTH)), full((1, HG_WIDTH)),
                  full(lvl.shape),
                  rowblk(wq, 0), rowblk(wq, 0), rowblk(wq, 1), rowblk(wq, 2), rowblk(wq, 3), rowblk(d, 0),
                  full((1, wq)), full((wq, wq)), full((1, wq)), full((SUBLANES, wq)), full((1, wq)),
                  full(w_out.shape)],
        out_specs=rowblk(d, 0),
        out_shape=jax.ShapeDtypeStruct((t, d), F32),
        scratch_shapes=[pltpu.VMEM((HG_HEADS, LANES, LANES), F32), pltpu.VMEM((tb, HG_WIDTH), BF16),
                        pltpu.VMEM((SUBLANES, CONV_WIDTH), F32)],
        compiler_params=_cparams("arbitrary"),
        name="hgrn2_mix_out",
    )(proj, proj, proj, proj, lbh, lbf, ng, lvl,
      ys5, proj, proj, proj, proj, h,
      s5_d.astype(F32).reshape(1, wq), s5_glu, s5_norm.astype(F32).reshape(1, wq), cw,
      conv_norm.astype(F32).reshape(1, wq), w_out)


def _swiglu_slice(x, w1_ref, w3_ref, w2_ref):
    out = None
    for c0 in range(0, w1_ref.shape[-1], SWIGLU_SUB):
        cs = slice(c0, c0 + SWIGLU_SUB)
        a = jnp.dot(x, w1_ref[:, cs].astype(BF16), preferred_element_type=F32)
        b = jnp.dot(x, w3_ref[:, cs].astype(BF16), preferred_element_type=F32)
        gact = (a * _sigmoid(a) * b).astype(BF16)
        part = jnp.dot(gact, w2_ref[cs, :].astype(BF16), preferred_element_type=F32)
        out = part if out is None else out + part
    return out


def _ffn_kernel(h_ref, g_ref, w1_ref, w3_ref, w2_ref, o_ref, hn_ref, acc_ref):
    j = pl.program_id(1)

    @pl.when(j == 0)
    def _():
        x = h_ref[...]
        hn_ref[...] = _rms(x, g_ref[...]).astype(BF16)
        acc_ref[...] = x

    acc_ref[...] += _swiglu_slice(hn_ref[...], w1_ref, w3_ref, w2_ref)

    @pl.when(j == pl.num_programs(1) - 1)
    def _():
        o_ref[...] = acc_ref[...]


def _ffn(h, g, w1, w3, w2, layer):
    t, d = h.shape
    f = w1.shape[2]
    tm = min(FFN_ROW_TILE, t)
    tf = FFN_COL_TILE
    return pl.pallas_call(
        _ffn_kernel,
        grid=(t // tm, f // tf),
        in_specs=[pl.BlockSpec((tm, d), lambda i, j: (i, 0)),
                  pl.BlockSpec((1, d), lambda i, j: (0, 0)),
                  pl.BlockSpec((None, d, tf), lambda i, j: (layer, 0, j)),
                  pl.BlockSpec((None, d, tf), lambda i, j: (layer, 0, j)),
                  pl.BlockSpec((None, tf, d), lambda i, j: (layer, j, 0))],
        out_specs=pl.BlockSpec((tm, d), lambda i, j: (i, 0)),
        out_shape=jax.ShapeDtypeStruct((t, d), F32),
        scratch_shapes=[pltpu.VMEM((tm, d), BF16), pltpu.VMEM((tm, d), F32)],
        compiler_params=_cparams("parallel", "arbitrary"),
        name="ffn_swiglu",
    )(h, g.astype(F32).reshape(1, d), w1, w3, w2)


def _router_kernel(h_ref, g_ref, r_ref, hn_ref, info_ref, info_t_ref, before_ref, total_ref, cnt_ref, *, n_exp):
    @pl.when(pl.program_id(0) == 0)
    def _():
        cnt_ref[...] = jnp.zeros_like(cnt_ref)

    xn = _rms(h_ref[...], g_ref[...])
    x_hi, x_lo = _bf16_terms(xn, 2)
    hn_ref[...] = x_hi
    r_hi, r_lo = _bf16_terms(r_ref[...], 2)
    logits = (jnp.dot(x_hi, r_hi, preferred_element_type=F32) + jnp.dot(x_lo, r_hi, preferred_element_type=F32)
              + jnp.dot(x_hi, r_lo, preferred_element_type=F32))
    tm = logits.shape[0]
    lane = lax.broadcasted_iota(I32, logits.shape, 1)
    lanef = lane.astype(F32)
    neg = jnp.float32(-jnp.inf)
    lg = jnp.where(lane < n_exp, logits, neg)
    m1 = jnp.max(lg, axis=-1, keepdims=True)
    i1 = jnp.min(jnp.where(lg == m1, lanef, float(LANES)), axis=-1, keepdims=True)
    oh1 = lanef == i1
    lg2 = jnp.where(oh1, neg, lg)
    m2 = jnp.max(lg2, axis=-1, keepdims=True)
    i2 = jnp.min(jnp.where(lg2 == m2, lanef, float(LANES)), axis=-1, keepdims=True)
    oh2 = lanef == i2
    ex = jnp.exp(m2 - m1)
    g1 = 1.0 / (1.0 + ex)
    g2 = ex * g1
    chosen = jnp.where(oh1, 1.0, jnp.where(oh2, 1.0, 0.0))
    ri = lax.broadcasted_iota(I32, (tm, tm), 0)
    ci = lax.broadcasted_iota(I32, (tm, tm), 1)
    tri = jnp.where(ri > ci, 1.0, 0.0).astype(BF16)
    before = cnt_ref[...]
    cexcl = jnp.dot(tri, chosen.astype(BF16), preferred_element_type=F32) + before
    rank1 = jnp.sum(jnp.where(oh1, cexcl, 0.0), axis=-1, keepdims=True)
    rank2 = jnp.sum(jnp.where(oh2, cexcl, 0.0), axis=-1, keepdims=True)
    info = jnp.where(lane == 0, i1, jnp.where(lane == 1, i2, jnp.where(lane == 2, g1, jnp.where(
        lane == 3, g2, jnp.where(lane == 4, rank1, jnp.where(lane == 5, rank2, 0.0))))))
    info_ref[...] = info
    info_t_ref[...] = info.T[0:SUBLANES, :]
    before_ref[0] = jnp.broadcast_to(before, before_ref.shape[1:])
    total = before + jnp.sum(chosen, axis=0, keepdims=True)
    cnt_ref[...] = total
    total_ref[...] = jnp.broadcast_to(total, total_ref.shape)


def _router(h, g, router):
    t, d = h.shape
    n_exp = router.shape[1]
    tm = min(TOK_WIN, t)
    ntw = t // tm
    rp = jnp.pad(router.astype(F32), ((0, 0), (0, LANES - n_exp)))
    return pl.pallas_call(
        functools.partial(_router_kernel, n_exp=n_exp),
        grid=(ntw,),
        in_specs=[pl.BlockSpec((tm, d), lambda i: (i, 0)),
                  pl.BlockSpec((1, d), lambda i: (0, 0)),
                  pl.BlockSpec((d, LANES), lambda i: (0, 0))],
        out_specs=[pl.BlockSpec((tm, d), lambda i: (i, 0)),
                   pl.BlockSpec((tm, LANES), lambda i: (i, 0)),
                   pl.BlockSpec((SUBLANES, tm), lambda i: (0, i)),
                   pl.BlockSpec((1, 8, LANES), lambda i: (i, 0, 0)),
                   pl.BlockSpec((8, LANES), lambda i: (0, 0))],
        out_shape=[jax.ShapeDtypeStruct((t, d), BF16),
                   jax.ShapeDtypeStruct((t, LANES), F32),
                   jax.ShapeDtypeStruct((SUBLANES, t), F32),
                   jax.ShapeDtypeStruct((ntw, 8, LANES), F32),
                   jax.ShapeDtypeStruct((8, LANES), F32)],
        scratch_shapes=[pltpu.VMEM((1, LANES), F32)],
        compiler_params=_cparams("arbitrary"),
        name="moe_router",
    )(h, g.astype(F32).reshape(1, d), rp)


def _gather_kernel(off_ref, nw_ref, wl_ref, tot_ref, dest_ref, hn_hbm, xs_ref, buf_ref, sem, acc_ref):
    b = pl.program_id(0)
    rows = acc_ref.shape[0]
    n_buf, win = buf_ref.shape[0], buf_ref.shape[1]
    n = nw_ref[b]
    q0 = off_ref[b]
    total = tot_ref[0]

    def copy(q):
        w = wl_ref[q]
        s = lax.rem(q, n_buf)
        return pltpu.make_async_copy(hn_hbm.at[pl.ds(pl.multiple_of(w * win, win), win), :], buf_ref.at[s],
                                     sem.at[s])

    @pl.when(b == 0)
    def _():
        for q in range(GATHER_AHEAD):
            @pl.when(q < total)
            def _(q=q):
                copy(q).start()

    acc_ref[...] = jnp.zeros_like(acc_ref)
    slot = b * rows + lax.broadcasted_iota(I32, (rows, win), 0)

    def gathered(q):
        @pl.when(q + GATHER_AHEAD < total)
        def _():
            copy(q + GATHER_AHEAD).start()

        col = pl.ds(pl.multiple_of(wl_ref[q] * win, win), win)
        d1 = dest_ref[0:1, col]
        d2 = dest_ref[1:2, col]
        hit = jnp.where(d1 == slot, 1.0, jnp.where(d2 == slot, 1.0, 0.0)).astype(BF16)
        copy(q).wait()
        return jnp.dot(hit, buf_ref[lax.rem(q, n_buf)], preferred_element_type=F32)

    def several(j, carry):
        q = q0 + GATHER_UNROLL * j
        acc_ref[...] += sum(gathered(q + r) for r in range(GATHER_UNROLL))
        return carry

    n_full = n // GATHER_UNROLL
    lax.fori_loop(0, n_full, several, 0)

    def single(j, carry):
        acc_ref[...] += gathered(q0 + n_full * GATHER_UNROLL + j)
        return carry

    lax.fori_loop(0, n - n_full * GATHER_UNROLL, single, 0)
    xs_ref[...] = acc_ref[...].astype(BF16)


def _gather(hn, dest, off, nw, wlist, total, n_slots):
    t, d = hn.shape
    return pl.pallas_call(
        _gather_kernel,
        grid_spec=pltpu.PrefetchScalarGridSpec(
            num_scalar_prefetch=4,
            grid=(n_slots // GATHER_TILE,),
            in_specs=[pl.BlockSpec(dest.shape, lambda b, *_: (0, 0)),
                      pl.BlockSpec(memory_space=pl.ANY)],
            out_specs=pl.BlockSpec((GATHER_TILE, d), lambda b, *_: (b, 0)),
            scratch_shapes=[pltpu.VMEM((GATHER_BUFS, TOK_WIN, d), BF16),
                            pltpu.SemaphoreType.DMA((GATHER_BUFS,)),
                            pltpu.VMEM((GATHER_TILE, d), F32)]),
        out_shape=jax.ShapeDtypeStruct((n_slots, d), BF16),
        compiler_params=_cparams("arbitrary"),
        name="moe_gather",
    )(off, nw, wlist, total, dest, hn)


def _expert_kernel(te_ref, tv_ref, x_ref, w1_ref, w3_ref, w2_ref, y_ref, acc_ref):
    b = pl.program_id(0)
    j = pl.program_id(1)

    @pl.when(j == 0)
    def _():
        acc_ref[...] = jnp.zeros_like(acc_ref)

    @pl.when(tv_ref[b] == 1)
    def _():
        acc_ref[...] += _swiglu_slice(x_ref[...], w1_ref, w3_ref, w2_ref)

    @pl.when(j == pl.num_programs(1) - 1)
    def _():
        y_ref[...] = acc_ref[...].astype(BF16)


def _experts(xs, tile_e, tile_v, w1, w3, w2, layer):
    ns, d = xs.shape
    f = w1.shape[3]
    tf = MOE_COL_TILE
    nj = f // tf
    n_tiles = ns // MOE_TILE

    def col(b, j, tv):
        return j * tv[b] + (nj - 1) * (1 - tv[b])

    return pl.pallas_call(
        _expert_kernel,
        grid_spec=pltpu.PrefetchScalarGridSpec(
            num_scalar_prefetch=2,
            grid=(n_tiles, nj),
            in_specs=[pl.BlockSpec((MOE_TILE, d), lambda b, j, te, tv: (b, 0)),
                      pl.BlockSpec((None, None, d, tf), lambda b, j, te, tv: (layer, te[b], 0, col(b, j, tv))),
                      pl.BlockSpec((None, None, d, tf), lambda b, j, te, tv: (layer, te[b], 0, col(b, j, tv))),
                      pl.BlockSpec((None, None, tf, d), lambda b, j, te, tv: (layer, te[b], col(b, j, tv), 0))],
            out_specs=pl.BlockSpec((MOE_TILE, d), lambda b, j, te, tv: (b, 0)),
            scratch_shapes=[pltpu.VMEM((MOE_TILE, d), F32)]),
        out_shape=jax.ShapeDtypeStruct((ns, d), BF16),
        compiler_params=_cparams("arbitrary", "arbitrary"),
        name="moe_experts",
    )(tile_e, tile_v, xs, w1, w3, w2)


def _combine_kernel(ws_ref, po_ref, ex_ref, nar_ref, info_ref, h_ref, fin_ref, yb_hbm, o_ref,
                    buf_ref, nbuf_ref, sem, *, n_exp, final_norm):
    i = pl.program_id(0)
    n_steps = pl.num_programs(0)
    n_set = COMBINE_SETS
    cur = lax.rem(i, n_set)

    def wide_copy(step, e, s):
        start = pl.multiple_of(ws_ref[step * n_exp + e], SLOT_ALIGN)
        return pltpu.make_async_copy(yb_hbm.at[pl.ds(start, SLOT_WIN), :], buf_ref.at[s, e], sem.at[s, e])

    def narrow_copy(step, e, s):
        start = pl.multiple_of(ws_ref[step * n_exp + e], SLOT_ALIGN)
        return pltpu.make_async_copy(yb_hbm.at[pl.ds(start, NARROW_WIN), :],
                                     nbuf_ref.at[s, e * NARROW_WIN:(e + 1) * NARROW_WIN, :], sem.at[s, e])

    def for_step(step, s, act):
        @pl.when(nar_ref[step] == 1)
        def _():
            for e in range(n_exp):
                act(narrow_copy(step, e, s))

        @pl.when(nar_ref[step] == 0)
        def _():
            for e in range(n_exp):
                act(wide_copy(step, e, s))

    @pl.when(i == 0)
    def _():
        for s in range(n_set - 1):
            @pl.when(s < n_steps)
            def _(s=s):
                for_step(s, s, lambda c: c.start())

    ahead = i + n_set - 1

    @pl.when(ahead < n_steps)
    def _():
        for_step(ahead, lax.rem(ahead, n_set), lambda c: c.start())

    info = info_ref[...]
    tm = info.shape[0]
    e1, e2 = info[:, 0:1], info[:, 1:2]
    g1, g2 = info[:, 2:3], info[:, 3:4]
    r1, r2 = info[:, 4:5], info[:, 5:6]

    def rank_offset(e, shift):
        return (ws_ref[i * n_exp + e] - po_ref[e] + shift).astype(F32)

    for_step(i, cur, lambda c: c.wait())

    @pl.when(nar_ref[i] == 1)
    def _():
        k1 = jnp.full_like(r1, -1.0)
        k2 = jnp.full_like(r2, -1.0)
        for e in range(n_exp):
            k1 = jnp.where(e1 == float(e), r1 - rank_offset(e, -e * NARROW_WIN), k1)
            k2 = jnp.where(e2 == float(e), r2 - rank_offset(e, -e * NARROW_WIN), k2)
        width = n_exp * NARROW_WIN
        lanef = lax.broadcasted_iota(I32, (tm, width), 1).astype(F32)
        pt = jnp.where(k1 == lanef, jnp.broadcast_to(g1, (tm, width)),
                       jnp.where(k2 == lanef, jnp.broadcast_to(g2, (tm, width)), 0.0)).astype(BF16)
        o_ref[...] = h_ref[...] + jnp.dot(pt, nbuf_ref[cur], preferred_element_type=F32)

    @pl.when(nar_ref[i] == 0)
    def _():
        lanef = lax.broadcasted_iota(I32, (tm, TOK_WIN), 1).astype(F32)
        g1b = jnp.broadcast_to(g1, (tm, TOK_WIN))
        g2b = jnp.broadcast_to(g2, (tm, TOK_WIN))

        def weights(e, shift, lo_lane):
            off = rank_offset(e, shift)
            k1 = jnp.where(e1 == float(e), r1 - off, -1.0)
            k2 = jnp.where(e2 == float(e), r2 - off, -1.0)
            k1 = jnp.where(k1 >= float(lo_lane), k1, -1.0)
            k2 = jnp.where(k2 >= float(lo_lane), k2, -1.0)
            return jnp.where(k1 == lanef, g1b, jnp.where(k2 == lanef, g2b, 0.0)).astype(BF16)

        acc = h_ref[...]
        for e in range(n_exp):
            acc = acc + jnp.dot(weights(e, 0, 0), buf_ref[cur, e, 0:TOK_WIN, :], preferred_element_type=F32)
        o_ref[...] = acc

        tail = SLOT_WIN - TOK_WIN
        for e in range(n_exp):
            @pl.when(ex_ref[i * n_exp + e] == 1)
            def _(e=e):
                pt = weights(e, tail, TOK_WIN - tail)
                o_ref[...] += jnp.dot(pt, buf_ref[cur, e, tail:SLOT_WIN, :], preferred_element_type=F32)

    if final_norm:
        o_ref[...] = _rms(o_ref[...], fin_ref[...])


def _combine(info, h, yb, win_start, pstart, extra, narrow, fin_g, final_norm):
    t, d = h.shape
    n_exp = pstart.shape[0]
    tm = min(TOK_WIN, t)
    return pl.pallas_call(
        functools.partial(_combine_kernel, n_exp=n_exp, final_norm=final_norm),
        grid_spec=pltpu.PrefetchScalarGridSpec(
            num_scalar_prefetch=4,
            grid=(t // tm,),
            in_specs=[pl.BlockSpec((tm, LANES), lambda i, *_: (i, 0)),
                      pl.BlockSpec((tm, d), lambda i, *_: (i, 0)),
                      pl.BlockSpec((1, d), lambda i, *_: (0, 0)),
                      pl.BlockSpec(memory_space=pl.ANY)],
            out_specs=pl.BlockSpec((tm, d), lambda i, *_: (i, 0)),
            scratch_shapes=[pltpu.VMEM((COMBINE_SETS, n_exp, SLOT_WIN, d), BF16),
                            pltpu.VMEM((COMBINE_SETS, n_exp * NARROW_WIN, d), BF16),
                            pltpu.SemaphoreType.DMA((COMBINE_SETS, n_exp))]),
        out_shape=jax.ShapeDtypeStruct((t, d), F32),
        compiler_params=_cparams("arbitrary"),
        name="moe_combine",
    )(win_start, pstart, extra, narrow, info, h, fin_g.astype(F32).reshape(1, d), yb)


def _count_le(sorted_vals, x):
    return jnp.sum((sorted_vals[None, :] <= x[:, None]).astype(I32), axis=1)


def _moe(h, g, router, w1, w3, w2, layer, fin_g, final_norm):
    t, d = h.shape
    n_exp = router.shape[1]
    tw = min(TOK_WIN, t)
    ntw = t // tw
    hn, info, info_t, before, total = _router(h, g, router)

    counts = total[0, :n_exp].astype(I32)
    padded = (counts + MOE_TILE - 1) // MOE_TILE * MOE_TILE
    pend = jnp.cumsum(padded)
    pstart = pend - padded
    n_tiles = (t * TOP_K) // MOE_TILE + n_exp
    n_slots = n_tiles * MOE_TILE
    tile0 = jnp.arange(n_tiles, dtype=I32) * MOE_TILE
    tile_e = jnp.minimum(_count_le(pend, tile0), n_exp - 1)
    tile_v = (tile0 < pend[-1]).astype(I32)
    cum = jnp.concatenate([before[:, 0, :n_exp], total[0:1, :n_exp]], axis=0).astype(I32)

    n_gb = n_slots // GATHER_TILE
    gb0 = jnp.arange(n_gb, dtype=I32) * GATHER_TILE
    gb_e = jnp.minimum(_count_le(pend, gb0), n_exp - 1)
    r0 = gb0 - pstart[gb_e]
    cum_b = cum[:, gb_e]
    ilo = jnp.sum((cum_b[1:] <= r0[None, :]).astype(I32), axis=0)
    ihi = jnp.sum((cum_b[:-1] < (r0 + GATHER_TILE)[None, :]).astype(I32), axis=0) - 1
    n_work = jnp.where(gb0 < pend[-1], jnp.maximum(ihi - ilo + 1, 0), 0).astype(I32)
    ends = jnp.cumsum(n_work)
    offs = (ends - n_work).astype(I32)
    entry = jnp.arange(n_exp * ntw + n_gb, dtype=I32)
    entry_blk = jnp.minimum(_count_le(ends, entry), n_gb - 1)
    wlist = jnp.clip(ilo[entry_blk] + entry - offs[entry_blk], 0, ntw - 1).astype(I32)

    e_t = info_t[0:2].astype(I32)
    slot_t = jnp.sum(jnp.where(e_t[None] == jnp.arange(n_exp, dtype=I32)[:, None, None], pstart[:, None, None], 0),
                     axis=0) + info_t[4:6].astype(I32)
    dest = jnp.concatenate([slot_t, jnp.full((SUBLANES - TOP_K, t), -1, I32)], axis=0)

    xs = _gather(hn, dest, offs, n_work, wlist, ends[-1:].astype(I32), n_slots)
    yb = _experts(xs, tile_e, tile_v, w1, w3, w2, layer)

    lo = pstart[None, :] + cum[:-1]
    win_start = jnp.minimum(lo // SLOT_ALIGN * SLOT_ALIGN, n_slots - SLOT_WIN).astype(I32)
    reach = lo - win_start + cum[1:] - cum[:-1]
    extra = (reach > TOK_WIN).astype(I32)
    narrow = jnp.all(reach <= NARROW_WIN, axis=1).astype(I32)
    return _combine(info, h, yb, win_start.reshape(-1), pstart.astype(I32), extra.reshape(-1), narrow, fin_g,
                    final_norm)


def _final_norm_kernel(h_ref, g_ref, o_ref):
    o_ref[...] = _rms(h_ref[...], g_ref[...])


def _final_norm(h, g):
    t, d = h.shape
    tm = min(ROW_TILE, t)
    return pl.pallas_call(
        _final_norm_kernel,
        grid=(t // tm,),
        in_specs=[pl.BlockSpec((tm, d), lambda i: (i, 0)), pl.BlockSpec((1, d), lambda i: (0, 0))],
        out_specs=pl.BlockSpec((tm, d), lambda i: (i, 0)),
        out_shape=jax.ShapeDtypeStruct((t, d), F32),
        compiler_params=_cparams("parallel"),
        name="final_norm",
    )(h, g.astype(F32).reshape(1, d))


def kernel(x, attn_norm, ffn_norm, final_norm, w_in, w_out, s5_lambda_re, s5_lambda_im, s5_log_dt, s5_b_re, s5_b_im, s5_c_re, s5_c_im, s5_d, s5_glu, s5_out_norm, conv_w, conv_out_norm, hg_lower_bounds, hg_out_norm, ffn_w1, ffn_w3, ffn_w2, moe_router, moe_w1, moe_w3, moe_w2):
    bsz, seq, d = x.shape
    depth = w_in.shape[0]
    t = bsz * seq
    assert bsz == 1, "token mixers are written for a single sequence"
    lb_soft = jax.nn.softmax(hg_lower_bounds.astype(F32), axis=0)
    lb_all = jnp.cumsum(lb_soft, axis=0) - lb_soft[0]
    n_scan = int(math.log2(t // S5_CHUNK))
    s5_ops = jax.vmap(functools.partial(_s5_operators, n_scan=n_scan))(
        s5_lambda_re, s5_lambda_im, s5_log_dt, s5_b_re, s5_b_im, s5_c_re, s5_c_im)
    h = x.reshape(t, d).astype(F32)
    for l in range(depth):
        proj = _norm_inproj(h, attn_norm[l].astype(F32), w_in[l].astype(BF16))
        ys5 = _s5_conv(proj, [op[l] for op in s5_ops])
        h = _hgrn_mix(ys5, proj, h, lb_all[l], hg_out_norm[l], s5_d[l], s5_glu[l].astype(BF16), s5_out_norm[l],
                      conv_w[l], conv_out_norm[l], w_out[l].astype(BF16))
        j = l // 2
        if l % 2 == 0:
            h = _ffn(h, ffn_norm[l], ffn_w1, ffn_w3, ffn_w2, j)
        else:
            h = _moe(h, ffn_norm[l], moe_router[j], moe_w1, moe_w3, moe_w2, j, final_norm, l == depth - 1)
    if depth % 2 == 1:
        h = _final_norm(h, final_norm)
    return h.reshape(bsz, seq, d)
```

```python
import functools
import math

import numpy as np
import jax
import jax.numpy as jnp
from jax import lax
from jax.experimental import pallas as pl
from jax.experimental.pallas import tpu as pltpu

F32 = jnp.float32
BF16 = jnp.bfloat16
I32 = jnp.int32

NORM_EPS = 1e-6
LB_FLOOR = 1e-30
TOP_K = 2

S5_WIDTH = 256
S5_GROUP = 16
S5_STATE = 64
CONV_WIDTH = 256
CONV_K = 3
HG_WIDTH = 512
HG_HEAD_DIM = 128
HG_HEADS = HG_WIDTH // HG_HEAD_DIM
LANES = 128
SUBLANES = 8
COL_CONV = S5_WIDTH
COL_HG = S5_WIDTH + 3 * CONV_WIDTH

VMEM_LIMIT = 56 * 1024 * 1024
S5_CHUNK = 32
HG_CHUNK = 128
HG_LEVELS = int(math.log2(HG_CHUNK))
HG_TBLOCK = 512
ROW_TILE = 512
FFN_ROW_TILE = 1024
FFN_COL_TILE = 512
SWIGLU_SUB = 256
MOE_TILE = 1024
MOE_COL_TILE = 512
GATHER_TILE = 256
GATHER_AHEAD = 6
GATHER_UNROLL = 4
GATHER_BUFS = GATHER_AHEAD + GATHER_UNROLL
TOK_WIN = 256
SLOT_ALIGN = 16
SLOT_WIN = TOK_WIN + SLOT_ALIGN
NARROW_WIN = 128
COMBINE_SETS = 3


def _cparams(*sem):
    return pltpu.CompilerParams(dimension_semantics=sem, vmem_limit_bytes=VMEM_LIMIT)


def _rms(x, g):
    ms = jnp.mean(x * x, axis=-1, keepdims=True)
    return x * lax.rsqrt(ms + NORM_EPS) * g


def _sigmoid(x):
    return 1.0 / (1.0 + jnp.exp(-x))


def _norm_inproj_kernel(h_ref, g_ref, w_ref, o_ref):
    xn = _rms(h_ref[...], g_ref[...]).astype(BF16)
    o_ref[...] = jnp.dot(xn, w_ref[...], preferred_element_type=F32)


def _norm_inproj(h, g, w):
    t, d = h.shape
    n = w.shape[1]
    tm = min(ROW_TILE, t)
    return pl.pallas_call(
        _norm_inproj_kernel,
        grid=(t // tm,),
        in_specs=[pl.BlockSpec((tm, d), lambda i: (i, 0)),
                  pl.BlockSpec((1, d), lambda i: (0, 0)),
                  pl.BlockSpec((d, n), lambda i: (0, 0))],
        out_specs=pl.BlockSpec((tm, n), lambda i: (i, 0)),
        out_shape=jax.ShapeDtypeStruct((t, n), F32),
        compiler_params=_cparams("parallel"),
        name="norm_inproj",
    )(h, g.reshape(1, d), w)


def _s5_operators(lam_re, lam_im, log_dt, b_re, b_im, c_re, c_im, n_scan):
    lc = S5_CHUNK
    hi = lax.Precision.HIGHEST
    lr, li = lam_re.astype(F32), lam_im.astype(F32)
    dt = jnp.exp(log_dt.astype(F32))[:, None]
    zr, zi = lr * dt, li * dt
    taus = jnp.arange(lc + 1, dtype=F32)[:, None, None]
    mag = jnp.exp(zr[None] * taus)
    pwr, pwi = mag * jnp.cos(zi[None] * taus), mag * jnp.sin(zi[None] * taus)
    nr, ni = pwr[1] - 1.0, pwi[1]
    den = lr * lr + li * li
    qr, qi = (nr * lr + ni * li) / den, (ni * lr - nr * li) / den
    br, bi = b_re.astype(F32), b_im.astype(F32)
    bbr = qr[..., None] * br - qi[..., None] * bi
    bbi = qr[..., None] * bi + qi[..., None] * br
    cr, ci = c_re.astype(F32), c_im.astype(F32)
    g_, p_ = lr.shape

    def c_times_pw(lo):
        wr, wi = pwr[lo:lo + lc, :, None, :], pwi[lo:lo + lc, :, None, :]
        return cr[None] * wr - ci[None] * wi, cr[None] * wi + ci[None] * wr

    cpr, cpi = c_times_pw(0)
    cp = jnp.concatenate([cpr, cpi], axis=-1).transpose(1, 0, 2, 3).reshape(g_, lc * S5_GROUP, 2 * p_)
    kt = jnp.matmul(cp, jnp.concatenate([bbr, -bbi], axis=1), precision=hi)
    kflat = kt.transpose(0, 2, 1)

    wr, wi = pwr[lc - 1::-1][:, :, :, None], pwi[lc - 1::-1][:, :, :, None]
    msr = (wr * bbr[None] - wi * bbi[None]).transpose(1, 0, 3, 2).reshape(g_, lc * S5_GROUP, p_)
    msi = (wr * bbi[None] + wi * bbr[None]).transpose(1, 0, 3, 2).reshape(g_, lc * S5_GROUP, p_)
    m_state = jnp.concatenate([msr, msi], axis=-1)

    c1r, c1i = c_times_pw(1)
    c1r = c1r.transpose(1, 3, 0, 2).reshape(g_, p_, lc * S5_GROUP)
    c1i = c1i.transpose(1, 3, 0, 2).reshape(g_, p_, lc * S5_GROUP)
    m_carry = jnp.concatenate([c1r, -c1i], axis=1)

    akr, aki = [pwr[lc]], [pwi[lc]]
    for _ in range(n_scan - 1):
        r, i = akr[-1], aki[-1]
        akr.append(r * r - i * i)
        aki.append(2.0 * r * i)
    akr, aki = jnp.stack(akr, axis=1), jnp.stack(aki, axis=1)
    ar = jnp.concatenate([akr, akr], axis=-1)
    ai = jnp.concatenate([-aki, aki], axis=-1)
    kpad = -(-n_scan // 8) * 8
    ar = jnp.pad(ar, ((0, 0), (0, kpad - n_scan), (0, 0)))
    ai = jnp.pad(ai, ((0, 0), (0, kpad - n_scan), (0, 0)))
    return kflat, m_state.astype(BF16), m_carry.astype(BF16), ar, ai


S5_PER_TILE = LANES // S5_GROUP


def _s5_kernel(proj_hbm, kf_ref, ms_ref, mc_ref, ar_ref, ai_ref, y_hbm, us_ref, ys_ref, mi_ref, sem_in, sem_out,
               *, n_scan):
    lc, gw, per = S5_CHUNK, S5_GROUP, S5_PER_TILE
    nch = us_ref.shape[1]
    cols = pl.ds(pl.multiple_of(pl.program_id(0) * LANES, LANES), LANES)

    def in_copy(s):
        return pltpu.make_async_copy(proj_hbm.at[:, s, cols], us_ref.at[s], sem_in.at[s])

    def out_copy(s):
        return pltpu.make_async_copy(ys_ref.at[s], y_hbm.at[:, s, cols], sem_out.at[s])

    for s in range(lc):
        in_copy(s).start()
    ys_ref[...] = jnp.zeros_like(ys_ref)
    for s in range(lc):
        in_copy(s).wait()
        if s % per:
            us_ref[s] = pltpu.roll(us_ref[s], (s % per) * gw, axis=1)

    lane_grp = lax.broadcasted_iota(I32, (1, LANES), 1) // gw
    kf_lane = lax.broadcasted_iota(I32, (gw, lc * gw), 1)

    def group(gl, carry):
        rel_grp = jnp.bitwise_and(lane_grp - gl, per - 1)
        tiles = []
        for j in range(lc // per):
            merged = us_ref[j * per]
            for k in range(1, per):
                merged = jnp.where(rel_grp == k, us_ref[j * per + k], merged)
            tiles.append(pltpu.roll(merged, jnp.bitwise_and(-gl * gw, LANES - 1), axis=1))
        u = jnp.concatenate(tiles, axis=1).astype(BF16)

        kf = kf_ref[gl]
        for s in range(lc):
            blk = kf if s == 0 else jnp.where(kf_lane >= s * gw, pltpu.roll(kf, s * gw, axis=1), 0.0)
            mi_ref[s * gw:(s + 1) * gw, :] = blk.astype(BF16)

        x = jnp.dot(u, ms_ref[gl], preferred_element_type=F32)
        row = lax.broadcasted_iota(I32, x.shape, 0)
        half = x.shape[1] // 2
        for k in range(n_scan):
            d = 1 << k
            s = jnp.where(row >= d, pltpu.roll(x, d, axis=0), 0.0)
            x = x + ar_ref[gl, k:k + 1, :] * s + ai_ref[gl, k:k + 1, :] * pltpu.roll(s, half, axis=1)
        xe = jnp.where(row >= 1, pltpu.roll(x, 1, axis=0), 0.0)
        y = jnp.dot(u, mi_ref[...], preferred_element_type=F32)
        y = y + jnp.dot(xe.astype(BF16), mc_ref[gl], preferred_element_type=F32)

        for j in range(lc // per):
            moved = pltpu.roll(y[:, j * LANES:(j + 1) * LANES], jnp.bitwise_and(gl * gw, LANES - 1), axis=1)
            for k in range(per):
                ys_ref[j * per + k] = jnp.where(rel_grp == k, moved, ys_ref[j * per + k])
        return carry

    lax.fori_loop(0, per, group, 0)
    for s in range(lc):
        if s % per:
            ys_ref[s] = pltpu.roll(ys_ref[s], LANES - (s % per) * gw, axis=1)
        out_copy(s).start()
    for s in range(lc):
        out_copy(s).wait()


def _s5_conv(proj, ops):
    kflat, m_state, m_carry, ar, ai = ops
    t, n = proj.shape
    lc = S5_CHUNK
    nch = t // lc
    n_scan = int(math.log2(nch))
    assert (1 << n_scan) == nch
    w = lc * S5_GROUP
    p2 = m_state.shape[-1]
    per = S5_PER_TILE

    def grp(shape):
        return pl.BlockSpec((per,) + shape, lambda hh: (hh, 0, 0))

    hbm = pl.BlockSpec(memory_space=pl.ANY)
    y = pl.pallas_call(
        functools.partial(_s5_kernel, n_scan=n_scan),
        grid=(S5_WIDTH // LANES,),
        in_specs=[hbm, grp((S5_GROUP, w)), grp((w, p2)), grp((p2, w)),
                  grp((ar.shape[1], p2)), grp((ai.shape[1], p2))],
        out_specs=hbm,
        out_shape=jax.ShapeDtypeStruct((nch, lc, S5_WIDTH), F32),
        scratch_shapes=[pltpu.VMEM((lc, nch, LANES), F32), pltpu.VMEM((lc, nch, LANES), F32),
                        pltpu.VMEM((w, w), BF16), pltpu.SemaphoreType.DMA((lc,)), pltpu.SemaphoreType.DMA((lc,))],
        compiler_params=_cparams("arbitrary"),
        name="s5_conv",
    )(proj.reshape(nch, lc, n), kflat, m_state, m_carry, ar, ai)
    return y.reshape(t, S5_WIDTH)


LOG2E = float(np.log2(np.e))


def _hg_level_table():
    idx = np.arange(HG_CHUNK)
    t, s = idx[:, None], idx[None, :]
    top_bit = np.floor(np.log2(np.maximum(t ^ s, 1))).astype(np.int32)
    return np.where(s < t, HG_LEVELS - 1 - top_bit, -1).astype(np.int32)


def _hg_midpoint(bc, m):
    c = bc.shape[0]
    h = m // 2
    if h >= SUBLANES:
        return jnp.concatenate([jnp.broadcast_to(bc[j * m + h - 1:j * m + h, :], (m, LANES))
                                for j in range(c // m)], axis=0)
    x3 = bc.reshape(c // SUBLANES, SUBLANES, LANES)
    sub = lax.broadcasted_iota(I32, x3.shape, 1)
    beta = None
    for j in reversed(range(SUBLANES // m)):
        row = jnp.broadcast_to(x3[:, j * m + h - 1:j * m + h, :], x3.shape)
        beta = row if beta is None else jnp.where(sub < (j + 1) * m, row, beta)
    return beta.reshape(c, LANES)


def _bf16_terms(x, n):
    terms = []
    for _ in range(n - 1):
        t = x.astype(BF16)
        terms.append(t)
        x = x - t.astype(F32)
    return terms + [x.astype(BF16)]


def _hg_chunk(fp, q, v, gt, lb, lbf, ng, lvl, st):
    c = HG_CHUNK
    prow = lax.broadcasted_iota(I32, (c, LANES), 0)
    nt = (((1,), (1,)), ((), ()))
    tn = (((0,), (0,)), ((), ()))
    en = jnp.exp(-jnp.abs(fp))
    rc = 1.0 / (1.0 + en)
    pos_f = fp >= 0.0
    sig_p = jnp.where(pos_f, rc, en * rc)
    sig_n = jnp.where(pos_f, en * rc, rc)
    f = lbf + (1.0 - lb) * sig_p
    lf = jnp.log(f)
    kc = (1.0 - lb) * sig_n
    qc = q * _sigmoid(q)
    bc = lf
    for k in range(HG_LEVELS):
        d = 1 << k
        bc = bc + jnp.where(prow >= d, pltpu.roll(bc, d, axis=0), 0.0)

    a = jnp.zeros((c, c), F32)
    for lev in range(HG_LEVELS):
        m = c >> lev
        upper = jnp.bitwise_and(prow, m - 1) >= m // 2
        z = jnp.where(upper, qc, kc)
        if m == 2:
            zw = jnp.where(upper, z * f, z)
        else:
            dlt = bc - _hg_midpoint(bc, m)
            zw = z * jnp.exp2(dlt * jnp.where(upper, LOG2E, -LOG2E))
        zw = zw.astype(BF16)
        s = lax.dot_general(zw, zw, nt, preferred_element_type=F32)
        a = jnp.where(lvl == lev, s, a)
    vb = v.astype(BF16)
    o = jnp.dot(a.astype(BF16), vb, preferred_element_type=F32)
    o = o + jnp.sum(qc * kc, axis=-1, keepdims=True) * v
    o = o + lax.dot_general((qc * jnp.exp(bc)).astype(BF16), st.astype(BF16), nt,
                            preferred_element_type=F32)
    bl = bc[c - 1:c, :]
    khat = (kc * jnp.exp(bl - bc)).astype(BF16)
    st = st * jnp.exp(bl) + lax.dot_general(vb, khat, tn, preferred_element_type=F32)
    return _rms(o, ng) * (gt * _sigmoid(gt)), st


def _hgrn_mix_kernel(q_ref, f_ref, i_ref, gt_ref, lb_ref, lbf_ref, ng_ref, lvl_ref,
                     ys_ref, u_ref, cb_ref, cc_ref, cv_ref, h_ref, d_ref, glu_ref, sn_ref, cw_ref, cn_ref, wo_ref,
                     o_ref, st_ref, hg_ref, carry_ref):
    c = HG_CHUNK

    @pl.when(pl.program_id(0) == 0)
    def _():
        st_ref[...] = jnp.zeros_like(st_ref)
        carry_ref[...] = jnp.zeros_like(carry_ref)

    def chunk(n, carry):
        rows = pl.ds(pl.multiple_of(n * c, c), c)
        for hd in range(HG_HEADS):
            cols = slice(hd * LANES, (hd + 1) * LANES)
            o, st = _hg_chunk(f_ref[rows, cols], q_ref[rows, cols], i_ref[rows, cols], gt_ref[rows, cols],
                              lb_ref[:, cols], lbf_ref[:, cols], ng_ref[:, cols], lvl_ref[...], st_ref[hd])
            hg_ref[rows, cols] = o.astype(BF16)
            st_ref[hd] = st
        return carry

    lax.fori_loop(0, q_ref.shape[0] // c, chunk, 0)
    _mix_epilogue(ys_ref, u_ref, cb_ref, cc_ref, cv_ref, hg_ref, h_ref, d_ref, glu_ref, sn_ref, cw_ref, cn_ref,
                  wo_ref, o_ref, carry_ref)


def _mix_epilogue(ys_ref, u_ref, cb_ref, cc_ref, cv_ref, hg_ref, h_ref,
                  d_ref, glu_ref, sn_ref, cw_ref, cn_ref, wo_ref, o_ref, carry_ref):
    y = ys_ref[...] + d_ref[...] * u_ref[...]
    y = jax.nn.gelu(y)
    y = y * _sigmoid(jnp.dot(y.astype(BF16), glu_ref[...], preferred_element_type=F32))
    y_s5 = _rms(y, sn_ref[...])

    z = cc_ref[...] * cv_ref[...]
    tm = z.shape[0]
    row = lax.broadcasted_iota(I32, z.shape, 0)
    p1 = carry_ref[7:8, :]
    p2 = carry_ref[6:7, :]
    z1 = jnp.where(row == 0, p1, pltpu.roll(z, 1, axis=0))
    z2 = jnp.where(row == 0, p2, jnp.where(row == 1, p1, pltpu.roll(z, 2, axis=0)))
    carry_ref[...] = z[tm - 8:tm, :]
    yc = cb_ref[...] * (z2 * cw_ref[0:1, :] + z1 * cw_ref[1:2, :] + z * cw_ref[2:3, :])
    y_cv = _rms(yc, cn_ref[...])

    acc = h_ref[...]
    hg_row = S5_WIDTH + CONV_WIDTH
    acc = acc + jnp.dot(y_s5.astype(BF16), wo_ref[0:S5_WIDTH, :], preferred_element_type=F32)
    acc = acc + jnp.dot(y_cv.astype(BF16), wo_ref[S5_WIDTH:hg_row, :], preferred_element_type=F32)
    acc = acc + jnp.dot(hg_ref[...], wo_ref[hg_row:, :], preferred_element_type=F32)
    o_ref[...] = acc


def _hgrn_mix(ys5, proj, h, lb, hg_norm, s5_d, s5_glu, s5_norm, conv_w, conv_norm, w_out):
    t, d = h.shape
    tb = min(HG_TBLOCK, t)
    lbh = jnp.clip(lb.astype(F32), 0.0, 1.0 - 1e-6).reshape(1, HG_WIDTH)
    lbf = jnp.maximum(lbh, LB_FLOOR)
    ng = hg_norm.astype(F32).reshape(1, HG_WIDTH)
    lvl = jnp.asarray(_hg_level_table())
    cw = jnp.pad(conv_w.astype(F32), ((0, SUBLANES - CONV_K), (0, 0)))
    wq = S5_WIDTH
    hg_col = COL_HG // HG_WIDTH

    def rowblk(width, colblk):
        return pl.BlockSpec((tb, width), lambda i, c=colblk: (i, c))

    def full(shape):
        return pl.BlockSpec(shape, lambda i: (0,) * len(shape))

    return pl.pallas_call(
        _hgrn_mix_kernel,
        grid=(t // tb,),
        in_specs=[rowblk(HG_WIDTH, hg_col), rowblk(HG_WIDTH, hg_col + 1), rowblk(HG_WIDTH, hg_col + 2),
                  rowblk(HG_WIDTH, hg_col + 3), full((1, HG_WIDTH)), full((1, HG_WIDTH)), full((1, HG_WIDTH)),
                  full(lvl.shape),
                  rowblk(wq, 0), rowblk(wq, 0), rowblk(wq, 1), rowblk(wq, 2), rowblk(wq, 3), rowblk(d, 0),
                  full((1, wq)), full((wq, wq)), full((1, wq)), full((SUBLANES, wq)), full((1, wq)),
                  full(w_out.shape)],
        out_specs=rowblk(d, 0),
        out_shape=jax.ShapeDtypeStruct((t, d), F32),
        scratch_shapes=[pltpu.VMEM((HG_HEADS, LANES, LANES), F32), pltpu.VMEM((tb, HG_WIDTH), BF16),
                        pltpu.VMEM((SUBLANES, CONV_WIDTH), F32)],
        compiler_params=_cparams("arbitrary"),
        name="hgrn2_mix_out",
    )(proj, proj, proj, proj, lbh, lbf, ng, lvl,
      ys5, proj, proj, proj, proj, h,
      s5_d.astype(F32).reshape(1, wq), s5_glu, s5_norm.astype(F32).reshape(1, wq), cw,
      conv_norm.astype(F32).reshape(1, wq), w_out)


def _swiglu_slice(x, w1_ref, w3_ref, w2_ref):
    out = None
    for c0 in range(0, w1_ref.shape[-1], SWIGLU_SUB):
        cs = slice(c0, c0 + SWIGLU_SUB)
        a = jnp.dot(x, w1_ref[:, cs].astype(BF16), preferred_element_type=F32)
        b = jnp.dot(x, w3_ref[:, cs].astype(BF16), preferred_element_type=F32)
        gact = (a * _sigmoid(a) * b).astype(BF16)
        part = jnp.dot(gact, w2_ref[cs, :].astype(BF16), preferred_element_type=F32)
        out = part if out is None else out + part
    return out


def _ffn_kernel(h_ref, g_ref, w1_ref, w3_ref, w2_ref, o_ref, hn_ref, acc_ref):
    j = pl.program_id(1)

    @pl.when(j == 0)
    def _():
        x = h_ref[...]
        hn_ref[...] = _rms(x, g_ref[...]).astype(BF16)
        acc_ref[...] = x

    acc_ref[...] += _swiglu_slice(hn_ref[...], w1_ref, w3_ref, w2_ref)

    @pl.when(j == pl.num_programs(1) - 1)
    def _():
        o_ref[...] = acc_ref[...]


def _ffn(h, g, w1, w3, w2, layer):
    t, d = h.shape
    f = w1.shape[2]
    tm = min(FFN_ROW_TILE, t)
    tf = FFN_COL_TILE
    return pl.pallas_call(
        _ffn_kernel,
        grid=(t // tm, f // tf),
        in_specs=[pl.BlockSpec((tm, d), lambda i, j: (i, 0)),
                  pl.BlockSpec((1, d), lambda i, j: (0, 0)),
                  pl.BlockSpec((None, d, tf), lambda i, j: (layer, 0, j)),
                  pl.BlockSpec((None, d, tf), lambda i, j: (layer, 0, j)),
                  pl.BlockSpec((None, tf, d), lambda i, j: (layer, j, 0))],
        out_specs=pl.BlockSpec((tm, d), lambda i, j: (i, 0)),
        out_shape=jax.ShapeDtypeStruct((t, d), F32),
        scratch_shapes=[pltpu.VMEM((tm, d), BF16), pltpu.VMEM((tm, d), F32)],
        compiler_params=_cparams("parallel", "arbitrary"),
        name="ffn_swiglu",
    )(h, g.astype(F32).reshape(1, d), w1, w3, w2)


def _router_kernel(h_ref, g_ref, r_ref, hn_ref, info_ref, info_t_ref, before_ref, total_ref, cnt_ref, *, n_exp):
    @pl.when(pl.program_id(0) == 0)
    def _():
        cnt_ref[...] = jnp.zeros_like(cnt_ref)

    xn = _rms(h_ref[...], g_ref[...])
    x_hi, x_lo = _bf16_terms(xn, 2)
    hn_ref[...] = x_hi
    r_hi, r_lo = _bf16_terms(r_ref[...], 2)
    logits = (jnp.dot(x_hi, r_hi, preferred_element_type=F32) + jnp.dot(x_lo, r_hi, preferred_element_type=F32)
              + jnp.dot(x_hi, r_lo, preferred_element_type=F32))
    tm = logits.shape[0]
    lane = lax.broadcasted_iota(I32, logits.shape, 1)
    lanef = lane.astype(F32)
    neg = jnp.float32(-jnp.inf)
    lg = jnp.where(lane < n_exp, logits, neg)
    m1 = jnp.max(lg, axis=-1, keepdims=True)
    i1 = jnp.min(jnp.where(lg == m1, lanef, float(LANES)), axis=-1, keepdims=True)
    oh1 = lanef == i1
    lg2 = jnp.where(oh1, neg, lg)
    m2 = jnp.max(lg2, axis=-1, keepdims=True)
    i2 = jnp.min(jnp.where(lg2 == m2, lanef, float(LANES)), axis=-1, keepdims=True)
    oh2 = lanef == i2
    ex = jnp.exp(m2 - m1)
    g1 = 1.0 / (1.0 + ex)
    g2 = ex * g1
    chosen = jnp.where(oh1, 1.0, jnp.where(oh2, 1.0, 0.0))
    ri = lax.broadcasted_iota(I32, (tm, tm), 0)
    ci = lax.broadcasted_iota(I32, (tm, tm), 1)
    tri = jnp.where(ri > ci, 1.0, 0.0).astype(BF16)
    before = cnt_ref[...]
    cexcl = jnp.dot(tri, chosen.astype(BF16), preferred_element_type=F32) + before
    rank1 = jnp.sum(jnp.where(oh1, cexcl, 0.0), axis=-1, keepdims=True)
    rank2 = jnp.sum(jnp.where(oh2, cexcl, 0.0), axis=-1, keepdims=True)
    info = jnp.where(lane == 0, i1, jnp.where(lane == 1, i2, jnp.where(lane == 2, g1, jnp.where(
        lane == 3, g2, jnp.where(lane == 4, rank1, jnp.where(lane == 5, rank2, 0.0))))))
    info_ref[...] = info
    info_t_ref[...] = info.T[0:SUBLANES, :]
    before_ref[0] = jnp.broadcast_to(before, before_ref.shape[1:])
    total = before + jnp.sum(chosen, axis=0, keepdims=True)
    cnt_ref[...] = total
    total_ref[...] = jnp.broadcast_to(total, total_ref.shape)


def _router(h, g, router):
    t, d = h.shape
    n_exp = router.shape[1]
    tm = min(TOK_WIN, t)
    ntw = t // tm
    rp = jnp.pad(router.astype(F32), ((0, 0), (0, LANES - n_exp)))
    return pl.pallas_call(
        functools.partial(_router_kernel, n_exp=n_exp),
        grid=(ntw,),
        in_specs=[pl.BlockSpec((tm, d), lambda i: (i, 0)),
                  pl.BlockSpec((1, d), lambda i: (0, 0)),
                  pl.BlockSpec((d, LANES), lambda i: (0, 0))],
        out_specs=[pl.BlockSpec((tm, d), lambda i: (i, 0)),
                   pl.BlockSpec((tm, LANES), lambda i: (i, 0)),
                   pl.BlockSpec((SUBLANES, tm), lambda i: (0, i)),
                   pl.BlockSpec((1, 8, LANES), lambda i: (i, 0, 0)),
                   pl.BlockSpec((8, LANES), lambda i: (0, 0))],
        out_shape=[jax.ShapeDtypeStruct((t, d), BF16),
                   jax.ShapeDtypeStruct((t, LANES), F32),
                   jax.ShapeDtypeStruct((SUBLANES, t), F32),
                   jax.ShapeDtypeStruct((ntw, 8, LANES), F32),
                   jax.ShapeDtypeStruct((8, LANES), F32)],
        scratch_shapes=[pltpu.VMEM((1, LANES), F32)],
        compiler_params=_cparams("arbitrary"),
        name="moe_router",
    )(h, g.astype(F32).reshape(1, d), rp)


def _gather_kernel(off_ref, nw_ref, wl_ref, tot_ref, dest_ref, hn_hbm, xs_ref, buf_ref, sem, acc_ref):
    b = pl.program_id(0)
    rows = acc_ref.shape[0]
    n_buf, win = buf_ref.shape[0], buf_ref.shape[1]
    n = nw_ref[b]
    q0 = off_ref[b]
    total = tot_ref[0]

    def copy(q):
        w = wl_ref[q]
        s = lax.rem(q, n_buf)
        return pltpu.make_async_copy(hn_hbm.at[pl.ds(pl.multiple_of(w * win, win), win), :], buf_ref.at[s],
                                     sem.at[s])

    @pl.when(b == 0)
    def _():
        for q in range(GATHER_AHEAD):
            @pl.when(q < total)
            def _(q=q):
                copy(q).start()

    acc_ref[...] = jnp.zeros_like(acc_ref)
    slot = b * rows + lax.broadcasted_iota(I32, (rows, win), 0)

    def gathered(q):
        @pl.when(q + GATHER_AHEAD < total)
        def _():
            copy(q + GATHER_AHEAD).start()

        col = pl.ds(pl.multiple_of(wl_ref[q] * win, win), win)
        d1 = dest_ref[0:1, col]
        d2 = dest_ref[1:2, col]
        hit = jnp.where(d1 == slot, 1.0, jnp.where(d2 == slot, 1.0, 0.0)).astype(BF16)
        copy(q).wait()
        return jnp.dot(hit, buf_ref[lax.rem(q, n_buf)], preferred_element_type=F32)

    def several(j, carry):
        q = q0 + GATHER_UNROLL * j
        acc_ref[...] += sum(gathered(q + r) for r in range(GATHER_UNROLL))
        return carry

    n_full = n // GATHER_UNROLL
    lax.fori_loop(0, n_full, several, 0)

    def single(j, carry):
        acc_ref[...] += gathered(q0 + n_full * GATHER_UNROLL + j)
        return carry

    lax.fori_loop(0, n - n_full * GATHER_UNROLL, single, 0)
    xs_ref[...] = acc_ref[...].astype(BF16)


def _gather(hn, dest, off, nw, wlist, total, n_slots):
    t, d = hn.shape
    return pl.pallas_call(
        _gather_kernel,
        grid_spec=pltpu.PrefetchScalarGridSpec(
            num_scalar_prefetch=4,
            grid=(n_slots // GATHER_TILE,),
            in_specs=[pl.BlockSpec(dest.shape, lambda b, *_: (0, 0)),
                      pl.BlockSpec(memory_space=pl.ANY)],
            out_specs=pl.BlockSpec((GATHER_TILE, d), lambda b, *_: (b, 0)),
            scratch_shapes=[pltpu.VMEM((GATHER_BUFS, TOK_WIN, d), BF16),
                            pltpu.SemaphoreType.DMA((GATHER_BUFS,)),
                            pltpu.VMEM((GATHER_TILE, d), F32)]),
        out_shape=jax.ShapeDtypeStruct((n_slots, d), BF16),
        compiler_params=_cparams("arbitrary"),
        name="moe_gather",
    )(off, nw, wlist, total, dest, hn)


def _expert_kernel(te_ref, tv_ref, x_ref, w1_ref, w3_ref, w2_ref, y_ref, acc_ref):
    b = pl.program_id(0)
    j = pl.program_id(1)

    @pl.when(j == 0)
    def _():
        acc_ref[...] = jnp.zeros_like(acc_ref)

    fill = tv_ref[b]
    half = x_ref.shape[0] // 2

    @pl.when(fill == 2)
    def _():
        acc_ref[...] += _swiglu_slice(x_ref[...], w1_ref, w3_ref, w2_ref)

    @pl.when(fill == 1)
    def _():
        acc_ref[0:half, :] += _swiglu_slice(x_ref[0:half, :], w1_ref, w3_ref, w2_ref)

    @pl.when(j == pl.num_programs(1) - 1)
    def _():
        y_ref[...] = acc_ref[...].astype(BF16)


def _experts(xs, tile_e, tile_v, w1, w3, w2, layer):
    ns, d = xs.shape
    f = w1.shape[3]
    tf = MOE_COL_TILE
    nj = f // tf
    n_tiles = ns // MOE_TILE

    def col(b, j, tv):
        used = jnp.minimum(tv[b], 1)
        return j * used + (nj - 1) * (1 - used)

    return pl.pallas_call(
        _expert_kernel,
        grid_spec=pltpu.PrefetchScalarGridSpec(
            num_scalar_prefetch=2,
            grid=(n_tiles, nj),
            in_specs=[pl.BlockSpec((MOE_TILE, d), lambda b, j, te, tv: (b, 0)),
                      pl.BlockSpec((None, None, d, tf), lambda b, j, te, tv: (layer, te[b], 0, col(b, j, tv))),
                      pl.BlockSpec((None, None, d, tf), lambda b, j, te, tv: (layer, te[b], 0, col(b, j, tv))),
                      pl.BlockSpec((None, None, tf, d), lambda b, j, te, tv: (layer, te[b], col(b, j, tv), 0))],
            out_specs=pl.BlockSpec((MOE_TILE, d), lambda b, j, te, tv: (b, 0)),
            scratch_shapes=[pltpu.VMEM((MOE_TILE, d), F32)]),
        out_shape=jax.ShapeDtypeStruct((ns, d), BF16),
        compiler_params=_cparams("arbitrary", "arbitrary"),
        name="moe_experts",
    )(tile_e, tile_v, xs, w1, w3, w2)


def _combine_kernel(ws_ref, po_ref, ex_ref, nar_ref, info_ref, h_ref, fin_ref, yb_hbm, o_ref,
                    buf_ref, nbuf_ref, sem, *, n_exp, final_norm):
    i = pl.program_id(0)
    n_steps = pl.num_programs(0)
    n_set = COMBINE_SETS
    cur = lax.rem(i, n_set)

    def wide_copy(step, e, s):
        start = pl.multiple_of(ws_ref[step * n_exp + e], SLOT_ALIGN)
        return pltpu.make_async_copy(yb_hbm.at[pl.ds(start, SLOT_WIN), :], buf_ref.at[s, e], sem.at[s, e])

    def narrow_copy(step, e, s):
        start = pl.multiple_of(ws_ref[step * n_exp + e], SLOT_ALIGN)
        return pltpu.make_async_copy(yb_hbm.at[pl.ds(start, NARROW_WIN), :],
                                     nbuf_ref.at[s, e * NARROW_WIN:(e + 1) * NARROW_WIN, :], sem.at[s, e])

    def for_step(step, s, act):
        @pl.when(nar_ref[step] == 1)
        def _():
            for e in range(n_exp):
                act(narrow_copy(step, e, s))

        @pl.when(nar_ref[step] == 0)
        def _():
            for e in range(n_exp):
                act(wide_copy(step, e, s))

    @pl.when(i == 0)
    def _():
        for s in range(n_set - 1):
            @pl.when(s < n_steps)
            def _(s=s):
                for_step(s, s, lambda c: c.start())

    ahead = i + n_set - 1

    @pl.when(ahead < n_steps)
    def _():
        for_step(ahead, lax.rem(ahead, n_set), lambda c: c.start())

    info = info_ref[...]
    tm = info.shape[0]
    e1, e2 = info[:, 0:1], info[:, 1:2]
    g1, g2 = info[:, 2:3], info[:, 3:4]
    r1, r2 = info[:, 4:5], info[:, 5:6]

    def rank_offset(e, shift):
        return (ws_ref[i * n_exp + e] - po_ref[e] + shift).astype(F32)

    for_step(i, cur, lambda c: c.wait())

    @pl.when(nar_ref[i] == 1)
    def _():
        k1 = jnp.full_like(r1, -1.0)
        k2 = jnp.full_like(r2, -1.0)
        for e in range(n_exp):
            k1 = jnp.where(e1 == float(e), r1 - rank_offset(e, -e * NARROW_WIN), k1)
            k2 = jnp.where(e2 == float(e), r2 - rank_offset(e, -e * NARROW_WIN), k2)
        width = n_exp * NARROW_WIN
        lanef = lax.broadcasted_iota(I32, (tm, width), 1).astype(F32)
        pt = jnp.where(k1 == lanef, jnp.broadcast_to(g1, (tm, width)),
                       jnp.where(k2 == lanef, jnp.broadcast_to(g2, (tm, width)), 0.0)).astype(BF16)
        o_ref[...] = h_ref[...] + jnp.dot(pt, nbuf_ref[cur], preferred_element_type=F32)

    @pl.when(nar_ref[i] == 0)
    def _():
        lanef = lax.broadcasted_iota(I32, (tm, TOK_WIN), 1).astype(F32)
        g1b = jnp.broadcast_to(g1, (tm, TOK_WIN))
        g2b = jnp.broadcast_to(g2, (tm, TOK_WIN))

        def weights(e, shift, lo_lane):
            off = rank_offset(e, shift)
            k1 = jnp.where(e1 == float(e), r1 - off, -1.0)
            k2 = jnp.where(e2 == float(e), r2 - off, -1.0)
            k1 = jnp.where(k1 >= float(lo_lane), k1, -1.0)
            k2 = jnp.where(k2 >= float(lo_lane), k2, -1.0)
            return jnp.where(k1 == lanef, g1b, jnp.where(k2 == lanef, g2b, 0.0)).astype(BF16)

        acc = h_ref[...]
        for e in range(n_exp):
            acc = acc + jnp.dot(weights(e, 0, 0), buf_ref[cur, e, 0:TOK_WIN, :], preferred_element_type=F32)
        o_ref[...] = acc

        tail = SLOT_WIN - TOK_WIN
        for e in range(n_exp):
            @pl.when(ex_ref[i * n_exp + e] == 1)
            def _(e=e):
                pt = weights(e, tail, TOK_WIN - tail)
                o_ref[...] += jnp.dot(pt, buf_ref[cur, e, tail:SLOT_WIN, :], preferred_element_type=F32)

    if final_norm:
        o_ref[...] = _rms(o_ref[...], fin_ref[...])


def _combine(info, h, yb, win_start, pstart, extra, narrow, fin_g, final_norm):
    t, d = h.shape
    n_exp = pstart.shape[0]
    tm = min(TOK_WIN, t)
    return pl.pallas_call(
        functools.partial(_combine_kernel, n_exp=n_exp, final_norm=final_norm),
        grid_spec=pltpu.PrefetchScalarGridSpec(
            num_scalar_prefetch=4,
            grid=(t // tm,),
            in_specs=[pl.BlockSpec((tm, LANES), lambda i, *_: (i, 0)),
                      pl.BlockSpec((tm, d), lambda i, *_: (i, 0)),
                      pl.BlockSpec((1, d), lambda i, *_: (0, 0)),
                      pl.BlockSpec(memory_space=pl.ANY)],
            out_specs=pl.BlockSpec((tm, d), lambda i, *_: (i, 0)),
            scratch_shapes=[pltpu.VMEM((COMBINE_SETS, n_exp, SLOT_WIN, d), BF16),
                            pltpu.VMEM((COMBINE_SETS, n_exp * NARROW_WIN, d), BF16),
                            pltpu.SemaphoreType.DMA((COMBINE_SETS, n_exp))]),
        out_shape=jax.ShapeDtypeStruct((t, d), F32),
        compiler_params=_cparams("arbitrary"),
        name="moe_combine",
    )(win_start, pstart, extra, narrow, info, h, fin_g.astype(F32).reshape(1, d), yb)


def _count_le(sorted_vals, x):
    return jnp.sum((sorted_vals[None, :] <= x[:, None]).astype(I32), axis=1)


def _moe(h, g, router, w1, w3, w2, layer, fin_g, final_norm):
    t, d = h.shape
    n_exp = router.shape[1]
    tw = min(TOK_WIN, t)
    ntw = t // tw
    hn, info, info_t, before, total = _router(h, g, router)

    counts = total[0, :n_exp].astype(I32)
    padded = (counts + MOE_TILE - 1) // MOE_TILE * MOE_TILE
    pend = jnp.cumsum(padded)
    pstart = pend - padded
    n_tiles = (t * TOP_K) // MOE_TILE + n_exp
    n_slots = n_tiles * MOE_TILE
    tile0 = jnp.arange(n_tiles, dtype=I32) * MOE_TILE
    tile_e = jnp.minimum(_count_le(pend, tile0), n_exp - 1)
    half = MOE_TILE // 2
    tile_rows = jnp.clip(counts[tile_e] - (tile0 - pstart[tile_e]), 0, MOE_TILE)
    tile_v = jnp.where(tile0 < pend[-1], (tile_rows + half - 1) // half, 0).astype(I32)
    cum = jnp.concatenate([before[:, 0, :n_exp], total[0:1, :n_exp]], axis=0).astype(I32)

    n_gb = n_slots // GATHER_TILE
    gb0 = jnp.arange(n_gb, dtype=I32) * GATHER_TILE
    gb_e = jnp.minimum(_count_le(pend, gb0), n_exp - 1)
    r0 = gb0 - pstart[gb_e]
    cum_b = cum[:, gb_e]
    ilo = jnp.sum((cum_b[1:] <= r0[None, :]).astype(I32), axis=0)
    ihi = jnp.sum((cum_b[:-1] < (r0 + GATHER_TILE)[None, :]).astype(I32), axis=0) - 1
    n_work = jnp.where(gb0 < pend[-1], jnp.maximum(ihi - ilo + 1, 0), 0).astype(I32)
    ends = jnp.cumsum(n_work)
    offs = (ends - n_work).astype(I32)
    entry = jnp.arange(n_exp * ntw + n_gb, dtype=I32)
    entry_blk = jnp.minimum(_count_le(ends, entry), n_gb - 1)
    wlist = jnp.clip(ilo[entry_blk] + entry - offs[entry_blk], 0, ntw - 1).astype(I32)

    e_t = info_t[0:2].astype(I32)
    slot_t = jnp.sum(jnp.where(e_t[None] == jnp.arange(n_exp, dtype=I32)[:, None, None], pstart[:, None, None], 0),
                     axis=0) + info_t[4:6].astype(I32)
    dest = jnp.concatenate([slot_t, jnp.full((SUBLANES - TOP_K, t), -1, I32)], axis=0)

    xs = _gather(hn, dest, offs, n_work, wlist, ends[-1:].astype(I32), n_slots)
    yb = _experts(xs, tile_e, tile_v, w1, w3, w2, layer)

    lo = pstart[None, :] + cum[:-1]
    win_start = jnp.minimum(lo // SLOT_ALIGN * SLOT_ALIGN, n_slots - SLOT_WIN).astype(I32)
    reach = lo - win_start + cum[1:] - cum[:-1]
    extra = (reach > TOK_WIN).astype(I32)
    narrow = jnp.all(reach <= NARROW_WIN, axis=1).astype(I32)
    return _combine(info, h, yb, win_start.reshape(-1), pstart.astype(I32), extra.reshape(-1), narrow, fin_g,
                    final_norm)


def _final_norm_kernel(h_ref, g_ref, o_ref):
    o_ref[...] = _rms(h_ref[...], g_ref[...])


def _final_norm(h, g):
    t, d = h.shape
    tm = min(ROW_TILE, t)
    return pl.pallas_call(
        _final_norm_kernel,
        grid=(t // tm,),
        in_specs=[pl.BlockSpec((tm, d), lambda i: (i, 0)), pl.BlockSpec((1, d), lambda i: (0, 0))],
        out_specs=pl.BlockSpec((tm, d), lambda i: (i, 0)),
        out_shape=jax.ShapeDtypeStruct((t, d), F32),
        compiler_params=_cparams("parallel"),
        name="final_norm",
    )(h, g.astype(F32).reshape(1, d))


def kernel(x, attn_norm, ffn_norm, final_norm, w_in, w_out, s5_lambda_re, s5_lambda_im, s5_log_dt, s5_b_re, s5_b_im, s5_c_re, s5_c_im, s5_d, s5_glu, s5_out_norm, conv_w, conv_out_norm, hg_lower_bounds, hg_out_norm, ffn_w1, ffn_w3, ffn_w2, moe_router, moe_w1, moe_w3, moe_w2):
    bsz, seq, d = x.shape
    depth = w_in.shape[0]
    t = bsz * seq
    assert bsz == 1, "token mixers are written for a single sequence"
    lb_soft = jax.nn.softmax(hg_lower_bounds.astype(F32), axis=0)
    lb_all = jnp.cumsum(lb_soft, axis=0) - lb_soft[0]
    n_scan = int(math.log2(t // S5_CHUNK))
    s5_ops = jax.vmap(functools.partial(_s5_operators, n_scan=n_scan))(
        s5_lambda_re, s5_lambda_im, s5_log_dt, s5_b_re, s5_b_im, s5_c_re, s5_c_im)
    h = x.reshape(t, d).astype(F32)
    for l in range(depth):
        proj = _norm_inproj(h, attn_norm[l].astype(F32), w_in[l].astype(BF16))
        ys5 = _s5_conv(proj, [op[l] for op in s5_ops])
        h = _hgrn_mix(ys5, proj, h, lb_all[l], hg_out_norm[l], s5_d[l], s5_glu[l].astype(BF16), s5_out_norm[l],
                      conv_w[l], conv_out_norm[l], w_out[l].astype(BF16))
        j = l // 2
        if l % 2 == 0:
            h = _ffn(h, ffn_norm[l], ffn_w1, ffn_w3, ffn_w2, j)
        else:
            h = _moe(h, ffn_norm[l], moe_router[j], moe_w1, moe_w3, moe_w2, j, final_norm, l == depth - 1)
    if depth % 2 == 1:
        h = _final_norm(h, final_norm)
    return h.reshape(bsz, seq, d)
```

```python
import functools
import math

import numpy as np
import jax
import jax.numpy as jnp
from jax import lax
from jax.experimental import pallas as pl
from jax.experimental.pallas import tpu as pltpu

F32 = jnp.float32
BF16 = jnp.bfloat16
I32 = jnp.int32

NORM_EPS = 1e-6
LB_FLOOR = 1e-30
TOP_K = 2

S5_WIDTH = 256
S5_GROUP = 16
S5_STATE = 64
CONV_WIDTH = 256
CONV_K = 3
HG_WIDTH = 512
HG_HEAD_DIM = 128
HG_HEADS = HG_WIDTH // HG_HEAD_DIM
LANES = 128
SUBLANES = 8
COL_CONV = S5_WIDTH
COL_HG = S5_WIDTH + 3 * CONV_WIDTH

VMEM_LIMIT = 56 * 1024 * 1024
S5_CHUNK = 32
HG_CHUNK = 128
HG_LEVELS = int(math.log2(HG_CHUNK))
HG_TBLOCK = 512
ROW_TILE = 512
FFN_ROW_TILE = 1024
FFN_COL_TILE = 512
SWIGLU_SUB = 256
MOE_TILE = 1024
MOE_COL_TILE = 512
GATHER_TILE = 256
GATHER_AHEAD = 6
GATHER_UNROLL = 4
GATHER_BUFS = GATHER_AHEAD + GATHER_UNROLL
TOK_WIN = 256
SLOT_ALIGN = 16
SLOT_WIN = TOK_WIN + SLOT_ALIGN
NARROW_WIN = 128
COMBINE_SETS = 3


def _cparams(*sem):
    return pltpu.CompilerParams(dimension_semantics=sem, vmem_limit_bytes=VMEM_LIMIT)


def _rms(x, g):
    ms = jnp.mean(x * x, axis=-1, keepdims=True)
    return x * lax.rsqrt(ms + NORM_EPS) * g


def _sigmoid(x):
    return 1.0 / (1.0 + jnp.exp(-x))


def _norm_inproj_kernel(h_ref, g_ref, w_ref, o_ref, u_ref):
    xn = _rms(h_ref[...], g_ref[...]).astype(BF16)
    p = jnp.dot(xn, w_ref[...], preferred_element_type=F32)
    o_ref[...] = p.astype(BF16)
    u_ref[...] = p[:, :S5_WIDTH]


def _norm_inproj(h, g, w):
    t, d = h.shape
    n = w.shape[1]
    tm = min(ROW_TILE, t)
    return pl.pallas_call(
        _norm_inproj_kernel,
        grid=(t // tm,),
        in_specs=[pl.BlockSpec((tm, d), lambda i: (i, 0)),
                  pl.BlockSpec((1, d), lambda i: (0, 0)),
                  pl.BlockSpec((d, n), lambda i: (0, 0))],
        out_specs=[pl.BlockSpec((tm, n), lambda i: (i, 0)), pl.BlockSpec((tm, S5_WIDTH), lambda i: (i, 0))],
        out_shape=[jax.ShapeDtypeStruct((t, n), BF16), jax.ShapeDtypeStruct((t, S5_WIDTH), F32)],
        compiler_params=_cparams("parallel"),
        name="norm_inproj",
    )(h, g.reshape(1, d), w)


def _s5_operators(lam_re, lam_im, log_dt, b_re, b_im, c_re, c_im, n_scan):
    lc = S5_CHUNK
    hi = lax.Precision.HIGHEST
    lr, li = lam_re.astype(F32), lam_im.astype(F32)
    dt = jnp.exp(log_dt.astype(F32))[:, None]
    zr, zi = lr * dt, li * dt
    taus = jnp.arange(lc + 1, dtype=F32)[:, None, None]
    mag = jnp.exp(zr[None] * taus)
    pwr, pwi = mag * jnp.cos(zi[None] * taus), mag * jnp.sin(zi[None] * taus)
    nr, ni = pwr[1] - 1.0, pwi[1]
    den = lr * lr + li * li
    qr, qi = (nr * lr + ni * li) / den, (ni * lr - nr * li) / den
    br, bi = b_re.astype(F32), b_im.astype(F32)
    bbr = qr[..., None] * br - qi[..., None] * bi
    bbi = qr[..., None] * bi + qi[..., None] * br
    cr, ci = c_re.astype(F32), c_im.astype(F32)
    g_, p_ = lr.shape

    def c_times_pw(lo):
        wr, wi = pwr[lo:lo + lc, :, None, :], pwi[lo:lo + lc, :, None, :]
        return cr[None] * wr - ci[None] * wi, cr[None] * wi + ci[None] * wr

    cpr, cpi = c_times_pw(0)
    cp = jnp.concatenate([cpr, cpi], axis=-1).transpose(1, 0, 2, 3).reshape(g_, lc * S5_GROUP, 2 * p_)
    kt = jnp.matmul(cp, jnp.concatenate([bbr, -bbi], axis=1), precision=hi)
    kflat = kt.transpose(0, 2, 1)

    wr, wi = pwr[lc - 1::-1][:, :, :, None], pwi[lc - 1::-1][:, :, :, None]
    msr = (wr * bbr[None] - wi * bbi[None]).transpose(1, 0, 3, 2).reshape(g_, lc * S5_GROUP, p_)
    msi = (wr * bbi[None] + wi * bbr[None]).transpose(1, 0, 3, 2).reshape(g_, lc * S5_GROUP, p_)
    m_state = jnp.concatenate([msr, msi], axis=-1)

    c1r, c1i = c_times_pw(1)
    c1r = c1r.transpose(1, 3, 0, 2).reshape(g_, p_, lc * S5_GROUP)
    c1i = c1i.transpose(1, 3, 0, 2).reshape(g_, p_, lc * S5_GROUP)
    m_carry = jnp.concatenate([c1r, -c1i], axis=1)

    akr, aki = [pwr[lc]], [pwi[lc]]
    for _ in range(n_scan - 1):
        r, i = akr[-1], aki[-1]
        akr.append(r * r - i * i)
        aki.append(2.0 * r * i)
    akr, aki = jnp.stack(akr, axis=1), jnp.stack(aki, axis=1)
    ar = jnp.concatenate([akr, akr], axis=-1)
    ai = jnp.concatenate([-aki, aki], axis=-1)
    kpad = -(-n_scan // 8) * 8
    ar = jnp.pad(ar, ((0, 0), (0, kpad - n_scan), (0, 0)))
    ai = jnp.pad(ai, ((0, 0), (0, kpad - n_scan), (0, 0)))
    return kflat, m_state.astype(BF16), m_carry.astype(BF16), ar, ai


S5_PER_TILE = LANES // S5_GROUP


def _s5_kernel(u_hbm, kf_ref, ms_ref, mc_ref, ar_ref, ai_ref, y_hbm, us_ref, ys_ref, mi_ref, sem_in, sem_out,
               *, n_scan):
    lc, gw, per = S5_CHUNK, S5_GROUP, S5_PER_TILE
    nch = us_ref.shape[1]
    cols = pl.ds(pl.multiple_of(pl.program_id(0) * LANES, LANES), LANES)

    def in_copy(s):
        return pltpu.make_async_copy(u_hbm.at[:, s, cols], us_ref.at[s], sem_in.at[s])

    def out_copy(s):
        return pltpu.make_async_copy(ys_ref.at[s], y_hbm.at[:, s, cols], sem_out.at[s])

    for s in range(lc):
        in_copy(s).start()
    ys_ref[...] = jnp.zeros_like(ys_ref)
    for s in range(lc):
        in_copy(s).wait()
        if s % per:
            us_ref[s] = pltpu.roll(us_ref[s], (s % per) * gw, axis=1)

    lane_grp = lax.broadcasted_iota(I32, (1, LANES), 1) // gw
    kf_lane = lax.broadcasted_iota(I32, (gw, lc * gw), 1)

    def group(gl, carry):
        rel_grp = jnp.bitwise_and(lane_grp - gl, per - 1)
        tiles = []
        for j in range(lc // per):
            merged = us_ref[j * per]
            for k in range(1, per):
                merged = jnp.where(rel_grp == k, us_ref[j * per + k], merged)
            tiles.append(pltpu.roll(merged, jnp.bitwise_and(-gl * gw, LANES - 1), axis=1))
        u = jnp.concatenate(tiles, axis=1).astype(BF16)

        kf = kf_ref[gl]
        for s in range(lc):
            blk = kf if s == 0 else jnp.where(kf_lane >= s * gw, pltpu.roll(kf, s * gw, axis=1), 0.0)
            mi_ref[s * gw:(s + 1) * gw, :] = blk.astype(BF16)

        x = jnp.dot(u, ms_ref[gl], preferred_element_type=F32)
        row = lax.broadcasted_iota(I32, x.shape, 0)
        half = x.shape[1] // 2
        for k in range(n_scan):
            d = 1 << k
            s = jnp.where(row >= d, pltpu.roll(x, d, axis=0), 0.0)
            x = x + ar_ref[gl, k:k + 1, :] * s + ai_ref[gl, k:k + 1, :] * pltpu.roll(s, half, axis=1)
        xe = jnp.where(row >= 1, pltpu.roll(x, 1, axis=0), 0.0)
        y = jnp.dot(u, mi_ref[...], preferred_element_type=F32)
        y = y + jnp.dot(xe.astype(BF16), mc_ref[gl], preferred_element_type=F32)

        for j in range(lc // per):
            moved = pltpu.roll(y[:, j * LANES:(j + 1) * LANES], jnp.bitwise_and(gl * gw, LANES - 1), axis=1)
            for k in range(per):
                ys_ref[j * per + k] = jnp.where(rel_grp == k, moved, ys_ref[j * per + k])
        return carry

    lax.fori_loop(0, per, group, 0)
    for s in range(lc):
        if s % per:
            ys_ref[s] = pltpu.roll(ys_ref[s], LANES - (s % per) * gw, axis=1)
        out_copy(s).start()
    for s in range(lc):
        out_copy(s).wait()


def _s5_conv(u, ops):
    kflat, m_state, m_carry, ar, ai = ops
    t, n = u.shape
    lc = S5_CHUNK
    nch = t // lc
    n_scan = int(math.log2(nch))
    assert (1 << n_scan) == nch
    w = lc * S5_GROUP
    p2 = m_state.shape[-1]
    per = S5_PER_TILE

    def grp(shape):
        return pl.BlockSpec((per,) + shape, lambda hh: (hh, 0, 0))

    hbm = pl.BlockSpec(memory_space=pl.ANY)
    y = pl.pallas_call(
        functools.partial(_s5_kernel, n_scan=n_scan),
        grid=(S5_WIDTH // LANES,),
        in_specs=[hbm, grp((S5_GROUP, w)), grp((w, p2)), grp((p2, w)),
                  grp((ar.shape[1], p2)), grp((ai.shape[1], p2))],
        out_specs=hbm,
        out_shape=jax.ShapeDtypeStruct((nch, lc, S5_WIDTH), F32),
        scratch_shapes=[pltpu.VMEM((lc, nch, LANES), F32), pltpu.VMEM((lc, nch, LANES), F32),
                        pltpu.VMEM((w, w), BF16), pltpu.SemaphoreType.DMA((lc,)), pltpu.SemaphoreType.DMA((lc,))],
        compiler_params=_cparams("arbitrary"),
        name="s5_conv",
    )(u.reshape(nch, lc, n), kflat, m_state, m_carry, ar, ai)
    return y.reshape(t, S5_WIDTH)


LOG2E = float(np.log2(np.e))


def _hg_level_table():
    idx = np.arange(HG_CHUNK)
    t, s = idx[:, None], idx[None, :]
    top_bit = np.floor(np.log2(np.maximum(t ^ s, 1))).astype(np.int32)
    return np.where(s < t, HG_LEVELS - 1 - top_bit, -1).astype(np.int32)


def _hg_midpoint(bc, m):
    c = bc.shape[0]
    h = m // 2
    if h >= SUBLANES:
        return jnp.concatenate([jnp.broadcast_to(bc[j * m + h - 1:j * m + h, :], (m, LANES))
                                for j in range(c // m)], axis=0)
    x3 = bc.reshape(c // SUBLANES, SUBLANES, LANES)
    sub = lax.broadcasted_iota(I32, x3.shape, 1)
    beta = None
    for j in reversed(range(SUBLANES // m)):
        row = jnp.broadcast_to(x3[:, j * m + h - 1:j * m + h, :], x3.shape)
        beta = row if beta is None else jnp.where(sub < (j + 1) * m, row, beta)
    return beta.reshape(c, LANES)


def _bf16_terms(x, n):
    terms = []
    for _ in range(n - 1):
        t = x.astype(BF16)
        terms.append(t)
        x = x - t.astype(F32)
    return terms + [x.astype(BF16)]


def _hg_chunk(fp, q, v, gt, lb, lbf, ng, lvl, st):
    c = HG_CHUNK
    prow = lax.broadcasted_iota(I32, (c, LANES), 0)
    nt = (((1,), (1,)), ((), ()))
    tn = (((0,), (0,)), ((), ()))
    en = jnp.exp(-jnp.abs(fp))
    rc = 1.0 / (1.0 + en)
    pos_f = fp >= 0.0
    sig_p = jnp.where(pos_f, rc, en * rc)
    sig_n = jnp.where(pos_f, en * rc, rc)
    f = lbf + (1.0 - lb) * sig_p
    lf = jnp.log(f)
    kc = (1.0 - lb) * sig_n
    qc = q * _sigmoid(q)
    bc = lf
    for k in range(HG_LEVELS):
        d = 1 << k
        bc = bc + jnp.where(prow >= d, pltpu.roll(bc, d, axis=0), 0.0)

    a = jnp.zeros((c, c), F32)
    for lev in range(HG_LEVELS):
        m = c >> lev
        upper = jnp.bitwise_and(prow, m - 1) >= m // 2
        z = jnp.where(upper, qc, kc)
        if m == 2:
            zw = jnp.where(upper, z * f, z)
        else:
            dlt = bc - _hg_midpoint(bc, m)
            zw = z * jnp.exp2(dlt * jnp.where(upper, LOG2E, -LOG2E))
        zw = zw.astype(BF16)
        s = lax.dot_general(zw, zw, nt, preferred_element_type=F32)
        a = jnp.where(lvl == lev, s, a)
    vb = v.astype(BF16)
    o = jnp.dot(a.astype(BF16), vb, preferred_element_type=F32)
    o = o + jnp.sum(qc * kc, axis=-1, keepdims=True) * v
    o = o + lax.dot_general((qc * jnp.exp(bc)).astype(BF16), st.astype(BF16), nt,
                            preferred_element_type=F32)
    bl = bc[c - 1:c, :]
    khat = (kc * jnp.exp(bl - bc)).astype(BF16)
    st = st * jnp.exp(bl) + lax.dot_general(vb, khat, tn, preferred_element_type=F32)
    return _rms(o, ng) * (gt * _sigmoid(gt)), st


def _hgrn_mix_kernel(q_ref, f_ref, i_ref, gt_ref, lb_ref, lbf_ref, ng_ref, lvl_ref,
                     ys_ref, u_ref, cb_ref, cc_ref, cv_ref, h_ref, d_ref, glu_ref, sn_ref, cw_ref, cn_ref, wo_ref,
                     o_ref, st_ref, hg_ref, carry_ref):
    c = HG_CHUNK

    @pl.when(pl.program_id(0) == 0)
    def _():
        st_ref[...] = jnp.zeros_like(st_ref)
        carry_ref[...] = jnp.zeros_like(carry_ref)

    def chunk(n, carry):
        rows = pl.ds(pl.multiple_of(n * c, c), c)
        for hd in range(HG_HEADS):
            cols = slice(hd * LANES, (hd + 1) * LANES)
            fp, q, v, gt = (r[rows, cols].astype(F32) for r in (f_ref, q_ref, i_ref, gt_ref))
            o, st = _hg_chunk(fp, q, v, gt, lb_ref[:, cols], lbf_ref[:, cols], ng_ref[:, cols], lvl_ref[...],
                              st_ref[hd])
            hg_ref[rows, cols] = o.astype(BF16)
            st_ref[hd] = st
        return carry

    lax.fori_loop(0, q_ref.shape[0] // c, chunk, 0)
    _mix_epilogue(ys_ref, u_ref, cb_ref, cc_ref, cv_ref, hg_ref, h_ref, d_ref, glu_ref, sn_ref, cw_ref, cn_ref,
                  wo_ref, o_ref, carry_ref)


def _mix_epilogue(ys_ref, u_ref, cb_ref, cc_ref, cv_ref, hg_ref, h_ref,
                  d_ref, glu_ref, sn_ref, cw_ref, cn_ref, wo_ref, o_ref, carry_ref):
    y = ys_ref[...] + d_ref[...] * u_ref[...]
    y = jax.nn.gelu(y)
    y = y * _sigmoid(jnp.dot(y.astype(BF16), glu_ref[...], preferred_element_type=F32))
    y_s5 = _rms(y, sn_ref[...])

    z = cc_ref[...].astype(F32) * cv_ref[...].astype(F32)
    tm = z.shape[0]
    row = lax.broadcasted_iota(I32, z.shape, 0)
    p1 = carry_ref[7:8, :]
    p2 = carry_ref[6:7, :]
    z1 = jnp.where(row == 0, p1, pltpu.roll(z, 1, axis=0))
    z2 = jnp.where(row == 0, p2, jnp.where(row == 1, p1, pltpu.roll(z, 2, axis=0)))
    carry_ref[...] = z[tm - 8:tm, :]
    yc = cb_ref[...].astype(F32) * (z2 * cw_ref[0:1, :] + z1 * cw_ref[1:2, :] + z * cw_ref[2:3, :])
    y_cv = _rms(yc, cn_ref[...])

    acc = h_ref[...]
    hg_row = S5_WIDTH + CONV_WIDTH
    acc = acc + jnp.dot(y_s5.astype(BF16), wo_ref[0:S5_WIDTH, :], preferred_element_type=F32)
    acc = acc + jnp.dot(y_cv.astype(BF16), wo_ref[S5_WIDTH:hg_row, :], preferred_element_type=F32)
    acc = acc + jnp.dot(hg_ref[...], wo_ref[hg_row:, :], preferred_element_type=F32)
    o_ref[...] = acc


def _hgrn_mix(ys5, u, proj, h, lb, hg_norm, s5_d, s5_glu, s5_norm, conv_w, conv_norm, w_out):
    t, d = h.shape
    tb = min(HG_TBLOCK, t)
    lbh = jnp.clip(lb.astype(F32), 0.0, 1.0 - 1e-6).reshape(1, HG_WIDTH)
    lbf = jnp.maximum(lbh, LB_FLOOR)
    ng = hg_norm.astype(F32).reshape(1, HG_WIDTH)
    lvl = jnp.asarray(_hg_level_table())
    cw = jnp.pad(conv_w.astype(F32), ((0, SUBLANES - CONV_K), (0, 0)))
    wq = S5_WIDTH
    hg_col = COL_HG // HG_WIDTH

    def rowblk(width, colblk):
        return pl.BlockSpec((tb, width), lambda i, c=colblk: (i, c))

    def full(shape):
        return pl.BlockSpec(shape, lambda i: (0,) * len(shape))

    return pl.pallas_call(
        _hgrn_mix_kernel,
        grid=(t // tb,),
        in_specs=[rowblk(HG_WIDTH, hg_col), rowblk(HG_WIDTH, hg_col + 1), rowblk(HG_WIDTH, hg_col + 2),
                  rowblk(HG_WIDTH, hg_col + 3), full((1, HG_WIDTH)), full((1, HG_WIDTH)), full((1, HG_WIDTH)),
                  full(lvl.shape),
                  rowblk(wq, 0), rowblk(wq, 0), rowblk(wq, 1), rowblk(wq, 2), rowblk(wq, 3), rowblk(d, 0),
                  full((1, wq)), full((wq, wq)), full((1, wq)), full((SUBLANES, wq)), full((1, wq)),
                  full(w_out.shape)],
        out_specs=rowblk(d, 0),
        out_shape=jax.ShapeDtypeStruct((t, d), F32),
        scratch_shapes=[pltpu.VMEM((HG_HEADS, LANES, LANES), F32), pltpu.VMEM((tb, HG_WIDTH), BF16),
                        pltpu.VMEM((SUBLANES, CONV_WIDTH), F32)],
        compiler_params=_cparams("arbitrary"),
        name="hgrn2_mix_out",
    )(proj, proj, proj, proj, lbh, lbf, ng, lvl,
      ys5, u, proj, proj, proj, h,
      s5_d.astype(F32).reshape(1, wq), s5_glu, s5_norm.astype(F32).reshape(1, wq), cw,
      conv_norm.astype(F32).reshape(1, wq), w_out)


def _swiglu_slice(x, w1_ref, w3_ref, w2_ref):
    out = None
    for c0 in range(0, w1_ref.shape[-1], SWIGLU_SUB):
        cs = slice(c0, c0 + SWIGLU_SUB)
        a = jnp.dot(x, w1_ref[:, cs].astype(BF16), preferred_element_type=F32)
        b = jnp.dot(x, w3_ref[:, cs].astype(BF16), preferred_element_type=F32)
        gact = (a * _sigmoid(a) * b).astype(BF16)
        part = jnp.dot(gact, w2_ref[cs, :].astype(BF16), preferred_element_type=F32)
        out = part if out is None else out + part
    return out


def _ffn_kernel(h_ref, g_ref, w1_ref, w3_ref, w2_ref, o_ref, hn_ref, acc_ref):
    j = pl.program_id(1)

    @pl.when(j == 0)
    def _():
        x = h_ref[...]
        hn_ref[...] = _rms(x, g_ref[...]).astype(BF16)
        acc_ref[...] = x

    acc_ref[...] += _swiglu_slice(hn_ref[...], w1_ref, w3_ref, w2_ref)

    @pl.when(j == pl.num_programs(1) - 1)
    def _():
        o_ref[...] = acc_ref[...]


def _ffn(h, g, w1, w3, w2, layer):
    t, d = h.shape
    f = w1.shape[2]
    tm = min(FFN_ROW_TILE, t)
    tf = FFN_COL_TILE
    return pl.pallas_call(
        _ffn_kernel,
        grid=(t // tm, f // tf),
        in_specs=[pl.BlockSpec((tm, d), lambda i, j: (i, 0)),
                  pl.BlockSpec((1, d), lambda i, j: (0, 0)),
                  pl.BlockSpec((None, d, tf), lambda i, j: (layer, 0, j)),
                  pl.BlockSpec((None, d, tf), lambda i, j: (layer, 0, j)),
                  pl.BlockSpec((None, tf, d), lambda i, j: (layer, j, 0))],
        out_specs=pl.BlockSpec((tm, d), lambda i, j: (i, 0)),
        out_shape=jax.ShapeDtypeStruct((t, d), F32),
        scratch_shapes=[pltpu.VMEM((tm, d), BF16), pltpu.VMEM((tm, d), F32)],
        compiler_params=_cparams("parallel", "arbitrary"),
        name="ffn_swiglu",
    )(h, g.astype(F32).reshape(1, d), w1, w3, w2)


def _router_kernel(h_ref, g_ref, r_ref, hn_ref, info_ref, info_t_ref, before_ref, total_ref, cnt_ref, *, n_exp):
    @pl.when(pl.program_id(0) == 0)
    def _():
        cnt_ref[...] = jnp.zeros_like(cnt_ref)

    xn = _rms(h_ref[...], g_ref[...])
    x_hi, x_lo = _bf16_terms(xn, 2)
    hn_ref[...] = x_hi
    r_hi, r_lo = _bf16_terms(r_ref[...], 2)
    logits = (jnp.dot(x_hi, r_hi, preferred_element_type=F32) + jnp.dot(x_lo, r_hi, preferred_element_type=F32)
              + jnp.dot(x_hi, r_lo, preferred_element_type=F32))
    tm = logits.shape[0]
    lane = lax.broadcasted_iota(I32, logits.shape, 1)
    lanef = lane.astype(F32)
    neg = jnp.float32(-jnp.inf)
    lg = jnp.where(lane < n_exp, logits, neg)
    m1 = jnp.max(lg, axis=-1, keepdims=True)
    i1 = jnp.min(jnp.where(lg == m1, lanef, float(LANES)), axis=-1, keepdims=True)
    oh1 = lanef == i1
    lg2 = jnp.where(oh1, neg, lg)
    m2 = jnp.max(lg2, axis=-1, keepdims=True)
    i2 = jnp.min(jnp.where(lg2 == m2, lanef, float(LANES)), axis=-1, keepdims=True)
    oh2 = lanef == i2
    ex = jnp.exp(m2 - m1)
    g1 = 1.0 / (1.0 + ex)
    g2 = ex * g1
    chosen = jnp.where(oh1, 1.0, jnp.where(oh2, 1.0, 0.0))
    ri = lax.broadcasted_iota(I32, (tm, tm), 0)
    ci = lax.broadcasted_iota(I32, (tm, tm), 1)
    tri = jnp.where(ri > ci, 1.0, 0.0).astype(BF16)
    before = cnt_ref[...]
    cexcl = jnp.dot(tri, chosen.astype(BF16), preferred_element_type=F32) + before
    rank1 = jnp.sum(jnp.where(oh1, cexcl, 0.0), axis=-1, keepdims=True)
    rank2 = jnp.sum(jnp.where(oh2, cexcl, 0.0), axis=-1, keepdims=True)
    info = jnp.where(lane == 0, i1, jnp.where(lane == 1, i2, jnp.where(lane == 2, g1, jnp.where(
        lane == 3, g2, jnp.where(lane == 4, rank1, jnp.where(lane == 5, rank2, 0.0))))))
    info_ref[...] = info
    info_t_ref[...] = info.T[0:SUBLANES, :]
    before_ref[0] = jnp.broadcast_to(before, before_ref.shape[1:])
    total = before + jnp.sum(chosen, axis=0, keepdims=True)
    cnt_ref[...] = total
    total_ref[...] = jnp.broadcast_to(total, total_ref.shape)


def _router(h, g, router):
    t, d = h.shape
    n_exp = router.shape[1]
    tm = min(TOK_WIN, t)
    ntw = t // tm
    rp = jnp.pad(router.astype(F32), ((0, 0), (0, LANES - n_exp)))
    return pl.pallas_call(
        functools.partial(_router_kernel, n_exp=n_exp),
        grid=(ntw,),
        in_specs=[pl.BlockSpec((tm, d), lambda i: (i, 0)),
                  pl.BlockSpec((1, d), lambda i: (0, 0)),
                  pl.BlockSpec((d, LANES), lambda i: (0, 0))],
        out_specs=[pl.BlockSpec((tm, d), lambda i: (i, 0)),
                   pl.BlockSpec((tm, LANES), lambda i: (i, 0)),
                   pl.BlockSpec((SUBLANES, tm), lambda i: (0, i)),
                   pl.BlockSpec((1, 8, LANES), lambda i: (i, 0, 0)),
                   pl.BlockSpec((8, LANES), lambda i: (0, 0))],
        out_shape=[jax.ShapeDtypeStruct((t, d), BF16),
                   jax.ShapeDtypeStruct((t, LANES), F32),
                   jax.ShapeDtypeStruct((SUBLANES, t), F32),
                   jax.ShapeDtypeStruct((ntw, 8, LANES), F32),
                   jax.ShapeDtypeStruct((8, LANES), F32)],
        scratch_shapes=[pltpu.VMEM((1, LANES), F32)],
        compiler_params=_cparams("arbitrary"),
        name="moe_router",
    )(h, g.astype(F32).reshape(1, d), rp)


def _gather_kernel(off_ref, nw_ref, wl_ref, tot_ref, dest_ref, hn_hbm, xs_ref, buf_ref, sem, acc_ref):
    b = pl.program_id(0)
    rows = acc_ref.shape[0]
    n_buf, win = buf_ref.shape[0], buf_ref.shape[1]
    n = nw_ref[b]
    q0 = off_ref[b]
    total = tot_ref[0]

    def copy(q):
        w = wl_ref[q]
        s = lax.rem(q, n_buf)
        return pltpu.make_async_copy(hn_hbm.at[pl.ds(pl.multiple_of(w * win, win), win), :], buf_ref.at[s],
                                     sem.at[s])

    @pl.when(b == 0)
    def _():
        for q in range(GATHER_AHEAD):
            @pl.when(q < total)
            def _(q=q):
                copy(q).start()

    acc_ref[...] = jnp.zeros_like(acc_ref)
    slot = b * rows + lax.broadcasted_iota(I32, (rows, win), 0)

    def gathered(q):
        @pl.when(q + GATHER_AHEAD < total)
        def _():
            copy(q + GATHER_AHEAD).start()

        col = pl.ds(pl.multiple_of(wl_ref[q] * win, win), win)
        d1 = dest_ref[0:1, col]
        d2 = dest_ref[1:2, col]
        hit = jnp.where(d1 == slot, 1.0, jnp.where(d2 == slot, 1.0, 0.0)).astype(BF16)
        copy(q).wait()
        return jnp.dot(hit, buf_ref[lax.rem(q, n_buf)], preferred_element_type=F32)

    def several(j, carry):
        q = q0 + GATHER_UNROLL * j
        acc_ref[...] += sum(gathered(q + r) for r in range(GATHER_UNROLL))
        return carry

    n_full = n // GATHER_UNROLL
    lax.fori_loop(0, n_full, several, 0)

    def single(j, carry):
        acc_ref[...] += gathered(q0 + n_full * GATHER_UNROLL + j)
        return carry

    lax.fori_loop(0, n - n_full * GATHER_UNROLL, single, 0)
    xs_ref[...] = acc_ref[...].astype(BF16)


def _gather(hn, dest, off, nw, wlist, total, n_slots):
    t, d = hn.shape
    return pl.pallas_call(
        _gather_kernel,
        grid_spec=pltpu.PrefetchScalarGridSpec(
            num_scalar_prefetch=4,
            grid=(n_slots // GATHER_TILE,),
            in_specs=[pl.BlockSpec(dest.shape, lambda b, *_: (0, 0)),
                      pl.BlockSpec(memory_space=pl.ANY)],
            out_specs=pl.BlockSpec((GATHER_TILE, d), lambda b, *_: (b, 0)),
            scratch_shapes=[pltpu.VMEM((GATHER_BUFS, TOK_WIN, d), BF16),
                            pltpu.SemaphoreType.DMA((GATHER_BUFS,)),
                            pltpu.VMEM((GATHER_TILE, d), F32)]),
        out_shape=jax.ShapeDtypeStruct((n_slots, d), BF16),
        compiler_params=_cparams("arbitrary"),
        name="moe_gather",
    )(off, nw, wlist, total, dest, hn)


def _expert_kernel(te_ref, tv_ref, x_ref, w1_ref, w3_ref, w2_ref, y_ref, acc_ref):
    b = pl.program_id(0)
    j = pl.program_id(1)

    @pl.when(j == 0)
    def _():
        acc_ref[...] = jnp.zeros_like(acc_ref)

    fill = tv_ref[b]
    half = x_ref.shape[0] // 2

    @pl.when(fill == 2)
    def _():
        acc_ref[...] += _swiglu_slice(x_ref[...], w1_ref, w3_ref, w2_ref)

    @pl.when(fill == 1)
    def _():
        acc_ref[0:half, :] += _swiglu_slice(x_ref[0:half, :], w1_ref, w3_ref, w2_ref)

    @pl.when(j == pl.num_programs(1) - 1)
    def _():
        y_ref[...] = acc_ref[...].astype(BF16)


def _experts(xs, tile_e, tile_v, w1, w3, w2, layer):
    ns, d = xs.shape
    f = w1.shape[3]
    tf = MOE_COL_TILE
    nj = f // tf
    n_tiles = ns // MOE_TILE

    def col(b, j, tv):
        used = jnp.minimum(tv[b], 1)
        return j * used + (nj - 1) * (1 - used)

    return pl.pallas_call(
        _expert_kernel,
        grid_spec=pltpu.PrefetchScalarGridSpec(
            num_scalar_prefetch=2,
            grid=(n_tiles, nj),
            in_specs=[pl.BlockSpec((MOE_TILE, d), lambda b, j, te, tv: (b, 0)),
                      pl.BlockSpec((None, None, d, tf), lambda b, j, te, tv: (layer, te[b], 0, col(b, j, tv))),
                      pl.BlockSpec((None, None, d, tf), lambda b, j, te, tv: (layer, te[b], 0, col(b, j, tv))),
                      pl.BlockSpec((None, None, tf, d), lambda b, j, te, tv: (layer, te[b], col(b, j, tv), 0))],
            out_specs=pl.BlockSpec((MOE_TILE, d), lambda b, j, te, tv: (b, 0)),
            scratch_shapes=[pltpu.VMEM((MOE_TILE, d), F32)]),
        out_shape=jax.ShapeDtypeStruct((ns, d), BF16),
        compiler_params=_cparams("arbitrary", "arbitrary"),
        name="moe_experts",
    )(tile_e, tile_v, xs, w1, w3, w2)


def _combine_kernel(ws_ref, po_ref, ex_ref, nar_ref, info_ref, h_ref, fin_ref, yb_hbm, o_ref,
                    buf_ref, nbuf_ref, sem, *, n_exp, final_norm):
    i = pl.program_id(0)
    n_steps = pl.num_programs(0)
    n_set = COMBINE_SETS
    cur = lax.rem(i, n_set)

    def wide_copy(step, e, s):
        start = pl.multiple_of(ws_ref[step * n_exp + e], SLOT_ALIGN)
        return pltpu.make_async_copy(yb_hbm.at[pl.ds(start, SLOT_WIN), :], buf_ref.at[s, e], sem.at[s, e])

    def narrow_copy(step, e, s):
        start = pl.multiple_of(ws_ref[step * n_exp + e], SLOT_ALIGN)
        return pltpu.make_async_copy(yb_hbm.at[pl.ds(start, NARROW_WIN), :],
                                     nbuf_ref.at[s, e * NARROW_WIN:(e + 1) * NARROW_WIN, :], sem.at[s, e])

    def for_step(step, s, act):
        @pl.when(nar_ref[step] == 1)
        def _():
            for e in range(n_exp):
                act(narrow_copy(step, e, s))

        @pl.when(nar_ref[step] == 0)
        def _():
            for e in range(n_exp):
                act(wide_copy(step, e, s))

    @pl.when(i == 0)
    def _():
        for s in range(n_set - 1):
            @pl.when(s < n_steps)
            def _(s=s):
                for_step(s, s, lambda c: c.start())

    ahead = i + n_set - 1

    @pl.when(ahead < n_steps)
    def _():
        for_step(ahead, lax.rem(ahead, n_set), lambda c: c.start())

    info = info_ref[...]
    tm = info.shape[0]
    e1, e2 = info[:, 0:1], info[:, 1:2]
    g1, g2 = info[:, 2:3], info[:, 3:4]
    r1, r2 = info[:, 4:5], info[:, 5:6]

    def rank_offset(e, shift):
        return (ws_ref[i * n_exp + e] - po_ref[e] + shift).astype(F32)

    for_step(i, cur, lambda c: c.wait())

    @pl.when(nar_ref[i] == 1)
    def _():
        k1 = jnp.full_like(r1, -1.0)
        k2 = jnp.full_like(r2, -1.0)
        for e in range(n_exp):
            k1 = jnp.where(e1 == float(e), r1 - rank_offset(e, -e * NARROW_WIN), k1)
            k2 = jnp.where(e2 == float(e), r2 - rank_offset(e, -e * NARROW_WIN), k2)
        width = n_exp * NARROW_WIN
        lanef = lax.broadcasted_iota(I32, (tm, width), 1).astype(F32)
        pt = jnp.where(k1 == lanef, jnp.broadcast_to(g1, (tm, width)),
                       jnp.where(k2 == lanef, jnp.broadcast_to(g2, (tm, width)), 0.0)).astype(BF16)
        o_ref[...] = h_ref[...] + jnp.dot(pt, nbuf_ref[cur], preferred_element_type=F32)

    @pl.when(nar_ref[i] == 0)
    def _():
        lanef = lax.broadcasted_iota(I32, (tm, TOK_WIN), 1).astype(F32)
        g1b = jnp.broadcast_to(g1, (tm, TOK_WIN))
        g2b = jnp.broadcast_to(g2, (tm, TOK_WIN))

        def weights(e, shift, lo_lane):
            off = rank_offset(e, shift)
            k1 = jnp.where(e1 == float(e), r1 - off, -1.0)
            k2 = jnp.where(e2 == float(e), r2 - off, -1.0)
            k1 = jnp.where(k1 >= float(lo_lane), k1, -1.0)
            k2 = jnp.where(k2 >= float(lo_lane), k2, -1.0)
            return jnp.where(k1 == lanef, g1b, jnp.where(k2 == lanef, g2b, 0.0)).astype(BF16)

        acc = h_ref[...]
        for e in range(n_exp):
            acc = acc + jnp.dot(weights(e, 0, 0), buf_ref[cur, e, 0:TOK_WIN, :], preferred_element_type=F32)
        o_ref[...] = acc

        tail = SLOT_WIN - TOK_WIN
        for e in range(n_exp):
            @pl.when(ex_ref[i * n_exp + e] == 1)
            def _(e=e):
                pt = weights(e, tail, TOK_WIN - tail)
                o_ref[...] += jnp.dot(pt, buf_ref[cur, e, tail:SLOT_WIN, :], preferred_element_type=F32)

    if final_norm:
        o_ref[...] = _rms(o_ref[...], fin_ref[...])


def _combine(info, h, yb, win_start, pstart, extra, narrow, fin_g, final_norm):
    t, d = h.shape
    n_exp = pstart.shape[0]
    tm = min(TOK_WIN, t)
    return pl.pallas_call(
        functools.partial(_combine_kernel, n_exp=n_exp, final_norm=final_norm),
        grid_spec=pltpu.PrefetchScalarGridSpec(
            num_scalar_prefetch=4,
            grid=(t // tm,),
            in_specs=[pl.BlockSpec((tm, LANES), lambda i, *_: (i, 0)),
                      pl.BlockSpec((tm, d), lambda i, *_: (i, 0)),
                      pl.BlockSpec((1, d), lambda i, *_: (0, 0)),
                      pl.BlockSpec(memory_space=pl.ANY)],
            out_specs=pl.BlockSpec((tm, d), lambda i, *_: (i, 0)),
            scratch_shapes=[pltpu.VMEM((COMBINE_SETS, n_exp, SLOT_WIN, d), BF16),
                            pltpu.VMEM((COMBINE_SETS, n_exp * NARROW_WIN, d), BF16),
                            pltpu.SemaphoreType.DMA((COMBINE_SETS, n_exp))]),
        out_shape=jax.ShapeDtypeStruct((t, d), F32),
        compiler_params=_cparams("arbitrary"),
        name="moe_combine",
    )(win_start, pstart, extra, narrow, info, h, fin_g.astype(F32).reshape(1, d), yb)


def _count_le(sorted_vals, x):
    return jnp.sum((sorted_vals[None, :] <= x[:, None]).astype(I32), axis=1)


def _moe(h, g, router, w1, w3, w2, layer, fin_g, final_norm):
    t, d = h.shape
    n_exp = router.shape[1]
    tw = min(TOK_WIN, t)
    ntw = t // tw
    hn, info, info_t, before, total = _router(h, g, router)

    counts = total[0, :n_exp].astype(I32)
    padded = (counts + MOE_TILE - 1) // MOE_TILE * MOE_TILE
    pend = jnp.cumsum(padded)
    pstart = pend - padded
    n_tiles = (t * TOP_K) // MOE_TILE + n_exp
    n_slots = n_tiles * MOE_TILE
    tile0 = jnp.arange(n_tiles, dtype=I32) * MOE_TILE
    tile_e = jnp.minimum(_count_le(pend, tile0), n_exp - 1)
    half = MOE_TILE // 2
    tile_rows = jnp.clip(counts[tile_e] - (tile0 - pstart[tile_e]), 0, MOE_TILE)
    tile_v = jnp.where(tile0 < pend[-1], (tile_rows + half - 1) // half, 0).astype(I32)
    cum = jnp.concatenate([before[:, 0, :n_exp], total[0:1, :n_exp]], axis=0).astype(I32)

    n_gb = n_slots // GATHER_TILE
    gb0 = jnp.arange(n_gb, dtype=I32) * GATHER_TILE
    gb_e = jnp.minimum(_count_le(pend, gb0), n_exp - 1)
    r0 = gb0 - pstart[gb_e]
    cum_b = cum[:, gb_e]
    ilo = jnp.sum((cum_b[1:] <= r0[None, :]).astype(I32), axis=0)
    ihi = jnp.sum((cum_b[:-1] < (r0 + GATHER_TILE)[None, :]).astype(I32), axis=0) - 1
    n_work = jnp.where(gb0 < pend[-1], jnp.maximum(ihi - ilo + 1, 0), 0).astype(I32)
    ends = jnp.cumsum(n_work)
    offs = (ends - n_work).astype(I32)
    entry = jnp.arange(n_exp * ntw + n_gb, dtype=I32)
    entry_blk = jnp.minimum(_count_le(ends, entry), n_gb - 1)
    wlist = jnp.clip(ilo[entry_blk] + entry - offs[entry_blk], 0, ntw - 1).astype(I32)

    e_t = info_t[0:2].astype(I32)
    slot_t = jnp.sum(jnp.where(e_t[None] == jnp.arange(n_exp, dtype=I32)[:, None, None], pstart[:, None, None], 0),
                     axis=0) + info_t[4:6].astype(I32)
    dest = jnp.concatenate([slot_t, jnp.full((SUBLANES - TOP_K, t), -1, I32)], axis=0)

    xs = _gather(hn, dest, offs, n_work, wlist, ends[-1:].astype(I32), n_slots)
    yb = _experts(xs, tile_e, tile_v, w1, w3, w2, layer)

    lo = pstart[None, :] + cum[:-1]
    win_start = jnp.minimum(lo // SLOT_ALIGN * SLOT_ALIGN, n_slots - SLOT_WIN).astype(I32)
    reach = lo - win_start + cum[1:] - cum[:-1]
    extra = (reach > TOK_WIN).astype(I32)
    narrow = jnp.all(reach <= NARROW_WIN, axis=1).astype(I32)
    return _combine(info, h, yb, win_start.reshape(-1), pstart.astype(I32), extra.reshape(-1), narrow, fin_g,
                    final_norm)


def _final_norm_kernel(h_ref, g_ref, o_ref):
    o_ref[...] = _rms(h_ref[...], g_ref[...])


def _final_norm(h, g):
    t, d = h.shape
    tm = min(ROW_TILE, t)
    return pl.pallas_call(
        _final_norm_kernel,
        grid=(t // tm,),
        in_specs=[pl.BlockSpec((tm, d), lambda i: (i, 0)), pl.BlockSpec((1, d), lambda i: (0, 0))],
        out_specs=pl.BlockSpec((tm, d), lambda i: (i, 0)),
        out_shape=jax.ShapeDtypeStruct((t, d), F32),
        compiler_params=_cparams("parallel"),
        name="final_norm",
    )(h, g.astype(F32).reshape(1, d))


def kernel(x, attn_norm, ffn_norm, final_norm, w_in, w_out, s5_lambda_re, s5_lambda_im, s5_log_dt, s5_b_re, s5_b_im, s5_c_re, s5_c_im, s5_d, s5_glu, s5_out_norm, conv_w, conv_out_norm, hg_lower_bounds, hg_out_norm, ffn_w1, ffn_w3, ffn_w2, moe_router, moe_w1, moe_w3, moe_w2):
    bsz, seq, d = x.shape
    depth = w_in.shape[0]
    t = bsz * seq
    assert bsz == 1, "token mixers are written for a single sequence"
    lb_soft = jax.nn.softmax(hg_lower_bounds.astype(F32), axis=0)
    lb_all = jnp.cumsum(lb_soft, axis=0) - lb_soft[0]
    n_scan = int(math.log2(t // S5_CHUNK))
    s5_ops = jax.vmap(functools.partial(_s5_operators, n_scan=n_scan))(
        s5_lambda_re, s5_lambda_im, s5_log_dt, s5_b_re, s5_b_im, s5_c_re, s5_c_im)
    h = x.reshape(t, d).astype(F32)
    for l in range(depth):
        proj, u = _norm_inproj(h, attn_norm[l].astype(F32), w_in[l].astype(BF16))
        ys5 = _s5_conv(u, [op[l] for op in s5_ops])
        h = _hgrn_mix(ys5, u, proj, h, lb_all[l], hg_out_norm[l], s5_d[l], s5_glu[l].astype(BF16),
                      s5_out_norm[l], conv_w[l], conv_out_norm[l], w_out[l].astype(BF16))
        j = l // 2
        if l % 2 == 0:
            h = _ffn(h, ffn_norm[l], ffn_w1, ffn_w3, ffn_w2, j)
        else:
            h = _moe(h, ffn_norm[l], moe_router[j], moe_w1, moe_w3, moe_w2, j, final_norm, l == depth - 1)
    if depth % 2 == 1:
        h = _final_norm(h, final_norm)
    return h.reshape(bsz, seq, d)
```

```python
import functools
import math

import numpy as np
import jax
import jax.numpy as jnp
from jax import lax
from jax.experimental import pallas as pl
from jax.experimental.pallas import tpu as pltpu

F32 = jnp.float32
BF16 = jnp.bfloat16
I32 = jnp.int32

NORM_EPS = 1e-6
LB_FLOOR = 1e-30
TOP_K = 2

S5_WIDTH = 256
S5_GROUP = 16
S5_STATE = 64
CONV_WIDTH = 256
CONV_K = 3
HG_WIDTH = 512
HG_HEAD_DIM = 128
HG_HEADS = HG_WIDTH // HG_HEAD_DIM
LANES = 128
SUBLANES = 8
COL_CONV = S5_WIDTH
COL_HG = S5_WIDTH + 3 * CONV_WIDTH

VMEM_LIMIT = 56 * 1024 * 1024
S5_CHUNK = 32
HG_CHUNK = 128
HG_LEVELS = int(math.log2(HG_CHUNK))
HG_TBLOCK = 512
ROW_TILE = 512
FFN_ROW_TILE = 1024
FFN_COL_TILE = 512
SWIGLU_SUB = 256
MOE_TILE = 1024
MOE_COL_TILE = 512
GATHER_TILE = 256
GATHER_AHEAD = 6
GATHER_UNROLL = 4
GATHER_BUFS = GATHER_AHEAD + GATHER_UNROLL
TOK_WIN = 256
SLOT_ALIGN = 16
SLOT_WIN = TOK_WIN + SLOT_ALIGN
NARROW_WIN = 128
COMBINE_SETS = 3


def _cparams(*sem):
    return pltpu.CompilerParams(dimension_semantics=sem, vmem_limit_bytes=VMEM_LIMIT)


def _rms(x, g):
    ms = jnp.mean(x * x, axis=-1, keepdims=True)
    return x * lax.rsqrt(ms + NORM_EPS) * g


def _sigmoid(x):
    return 1.0 / (1.0 + jnp.exp(-x))


def _norm_inproj_kernel(h_ref, g_ref, w_ref, o_ref):
    xn = _rms(h_ref[...], g_ref[...]).astype(BF16)
    o_ref[...] = jnp.dot(xn, w_ref[...], preferred_element_type=F32)


def _norm_inproj(h, g, w):
    t, d = h.shape
    n = w.shape[1]
    tm = min(ROW_TILE, t)
    return pl.pallas_call(
        _norm_inproj_kernel,
        grid=(t // tm,),
        in_specs=[pl.BlockSpec((tm, d), lambda i: (i, 0)),
                  pl.BlockSpec((1, d), lambda i: (0, 0)),
                  pl.BlockSpec((d, n), lambda i: (0, 0))],
        out_specs=pl.BlockSpec((tm, n), lambda i: (i, 0)),
        out_shape=jax.ShapeDtypeStruct((t, n), F32),
        compiler_params=_cparams("parallel"),
        name="norm_inproj",
    )(h, g.reshape(1, d), w)


def _s5_operators(lam_re, lam_im, log_dt, b_re, b_im, c_re, c_im, n_scan):
    lc = S5_CHUNK
    hi = lax.Precision.HIGHEST
    lr, li = lam_re.astype(F32), lam_im.astype(F32)
    dt = jnp.exp(log_dt.astype(F32))[:, None]
    zr, zi = lr * dt, li * dt
    taus = jnp.arange(lc + 1, dtype=F32)[:, None, None]
    mag = jnp.exp(zr[None] * taus)
    pwr, pwi = mag * jnp.cos(zi[None] * taus), mag * jnp.sin(zi[None] * taus)
    nr, ni = pwr[1] - 1.0, pwi[1]
    den = lr * lr + li * li
    qr, qi = (nr * lr + ni * li) / den, (ni * lr - nr * li) / den
    br, bi = b_re.astype(F32), b_im.astype(F32)
    bbr = qr[..., None] * br - qi[..., None] * bi
    bbi = qr[..., None] * bi + qi[..., None] * br
    cr, ci = c_re.astype(F32), c_im.astype(F32)
    g_, p_ = lr.shape

    def c_times_pw(lo):
        wr, wi = pwr[lo:lo + lc, :, None, :], pwi[lo:lo + lc, :, None, :]
        return cr[None] * wr - ci[None] * wi, cr[None] * wi + ci[None] * wr

    cpr, cpi = c_times_pw(0)
    cp = jnp.concatenate([cpr, cpi], axis=-1).transpose(1, 0, 2, 3).reshape(g_, lc * S5_GROUP, 2 * p_)
    kt = jnp.matmul(cp, jnp.concatenate([bbr, -bbi], axis=1), precision=hi)
    kflat = kt.transpose(0, 2, 1)

    wr, wi = pwr[lc - 1::-1][:, :, :, None], pwi[lc - 1::-1][:, :, :, None]
    msr = (wr * bbr[None] - wi * bbi[None]).transpose(1, 0, 3, 2).reshape(g_, lc * S5_GROUP, p_)
    msi = (wr * bbi[None] + wi * bbr[None]).transpose(1, 0, 3, 2).reshape(g_, lc * S5_GROUP, p_)
    m_state = jnp.concatenate([msr, msi], axis=-1)

    c1r, c1i = c_times_pw(1)
    c1r = c1r.transpose(1, 3, 0, 2).reshape(g_, p_, lc * S5_GROUP)
    c1i = c1i.transpose(1, 3, 0, 2).reshape(g_, p_, lc * S5_GROUP)
    m_carry = jnp.concatenate([c1r, -c1i], axis=1)

    akr, aki = [pwr[lc]], [pwi[lc]]
    for _ in range(n_scan - 1):
        r, i = akr[-1], aki[-1]
        akr.append(r * r - i * i)
        aki.append(2.0 * r * i)
    akr, aki = jnp.stack(akr, axis=1), jnp.stack(aki, axis=1)
    ar = jnp.concatenate([akr, akr], axis=-1)
    ai = jnp.concatenate([-aki, aki], axis=-1)
    kpad = -(-n_scan // 8) * 8
    ar = jnp.pad(ar, ((0, 0), (0, kpad - n_scan), (0, 0)))
    ai = jnp.pad(ai, ((0, 0), (0, kpad - n_scan), (0, 0)))
    return kflat, m_state.astype(BF16), m_carry.astype(BF16), ar, ai


S5_PER_TILE = LANES // S5_GROUP


def _s5_kernel(proj_hbm, kf_ref, ms_ref, mc_ref, ar_ref, ai_ref, y_hbm, us_ref, ys_ref, mi_ref, sem_in, sem_out,
               *, n_scan):
    lc, gw, per = S5_CHUNK, S5_GROUP, S5_PER_TILE
    nch = us_ref.shape[1]
    cols = pl.ds(pl.multiple_of(pl.program_id(0) * LANES, LANES), LANES)

    def in_copy(s):
        return pltpu.make_async_copy(proj_hbm.at[:, s, cols], us_ref.at[s], sem_in.at[s])

    def out_copy(s):
        return pltpu.make_async_copy(ys_ref.at[s], y_hbm.at[:, s, cols], sem_out.at[s])

    for s in range(lc):
        in_copy(s).start()
    ys_ref[...] = jnp.zeros_like(ys_ref)
    for s in range(lc):
        in_copy(s).wait()
        if s % per:
            us_ref[s] = pltpu.roll(us_ref[s], (s % per) * gw, axis=1)

    lane_grp = lax.broadcasted_iota(I32, (1, LANES), 1) // gw
    kf_lane = lax.broadcasted_iota(I32, (gw, lc * gw), 1)

    def group(gl, carry):
        rel_grp = jnp.bitwise_and(lane_grp - gl, per - 1)
        tiles = []
        for j in range(lc // per):
            merged = us_ref[j * per]
            for k in range(1, per):
                merged = jnp.where(rel_grp == k, us_ref[j * per + k], merged)
            tiles.append(pltpu.roll(merged, jnp.bitwise_and(-gl * gw, LANES - 1), axis=1))
        u = jnp.concatenate(tiles, axis=1).astype(BF16)

        kf = kf_ref[gl]
        for s in range(lc):
            blk = kf if s == 0 else jnp.where(kf_lane >= s * gw, pltpu.roll(kf, s * gw, axis=1), 0.0)
            mi_ref[s * gw:(s + 1) * gw, :] = blk.astype(BF16)

        x = jnp.dot(u, ms_ref[gl], preferred_element_type=F32)
        row = lax.broadcasted_iota(I32, x.shape, 0)
        half = x.shape[1] // 2
        for k in range(n_scan):
            d = 1 << k
            s = jnp.where(row >= d, pltpu.roll(x, d, axis=0), 0.0)
            x = x + ar_ref[gl, k:k + 1, :] * s + ai_ref[gl, k:k + 1, :] * pltpu.roll(s, half, axis=1)
        xe = jnp.where(row >= 1, pltpu.roll(x, 1, axis=0), 0.0)
        y = jnp.dot(u, mi_ref[...], preferred_element_type=F32)
        y = y + jnp.dot(xe.astype(BF16), mc_ref[gl], preferred_element_type=F32)

        for j in range(lc // per):
            moved = pltpu.roll(y[:, j * LANES:(j + 1) * LANES], jnp.bitwise_and(gl * gw, LANES - 1), axis=1)
            for k in range(per):
                ys_ref[j * per + k] = jnp.where(rel_grp == k, moved, ys_ref[j * per + k])
        return carry

    lax.fori_loop(0, per, group, 0)
    for s in range(lc):
        if s % per:
            ys_ref[s] = pltpu.roll(ys_ref[s], LANES - (s % per) * gw, axis=1)
        out_copy(s).start()
    for s in range(lc):
        out_copy(s).wait()


def _s5_conv(proj, ops):
    kflat, m_state, m_carry, ar, ai = ops
    t, n = proj.shape
    lc = S5_CHUNK
    nch = t // lc
    n_scan = int(math.log2(nch))
    assert (1 << n_scan) == nch
    w = lc * S5_GROUP
    p2 = m_state.shape[-1]
    per = S5_PER_TILE

    def grp(shape):
        return pl.BlockSpec((per,) + shape, lambda hh: (hh, 0, 0))

    hbm = pl.BlockSpec(memory_space=pl.ANY)
    y = pl.pallas_call(
        functools.partial(_s5_kernel, n_scan=n_scan),
        grid=(S5_WIDTH // LANES,),
        in_specs=[hbm, grp((S5_GROUP, w)), grp((w, p2)), grp((p2, w)),
                  grp((ar.shape[1], p2)), grp((ai.shape[1], p2))],
        out_specs=hbm,
        out_shape=jax.ShapeDtypeStruct((nch, lc, S5_WIDTH), F32),
        scratch_shapes=[pltpu.VMEM((lc, nch, LANES), F32), pltpu.VMEM((lc, nch, LANES), F32),
                        pltpu.VMEM((w, w), BF16), pltpu.SemaphoreType.DMA((lc,)), pltpu.SemaphoreType.DMA((lc,))],
        compiler_params=_cparams("arbitrary"),
        name="s5_conv",
    )(proj.reshape(nch, lc, n), kflat, m_state, m_carry, ar, ai)
    return y.reshape(t, S5_WIDTH)


LOG2E = float(np.log2(np.e))


def _hg_level_table():
    idx = np.arange(HG_CHUNK)
    t, s = idx[:, None], idx[None, :]
    top_bit = np.floor(np.log2(np.maximum(t ^ s, 1))).astype(np.int32)
    return np.where(s < t, HG_LEVELS - 1 - top_bit, -1).astype(np.int32)


def _hg_midpoint(bc, m):
    c = bc.shape[0]
    h = m // 2
    if h >= SUBLANES:
        return jnp.concatenate([jnp.broadcast_to(bc[j * m + h - 1:j * m + h, :], (m, LANES))
                                for j in range(c // m)], axis=0)
    x3 = bc.reshape(c // SUBLANES, SUBLANES, LANES)
    sub = lax.broadcasted_iota(I32, x3.shape, 1)
    beta = None
    for j in reversed(range(SUBLANES // m)):
        row = jnp.broadcast_to(x3[:, j * m + h - 1:j * m + h, :], x3.shape)
        beta = row if beta is None else jnp.where(sub < (j + 1) * m, row, beta)
    return beta.reshape(c, LANES)


def _bf16_terms(x, n):
    terms = []
    for _ in range(n - 1):
        t = x.astype(BF16)
        terms.append(t)
        x = x - t.astype(F32)
    return terms + [x.astype(BF16)]


def _hg_chunk(fp, q, v, gt, lb, lbf, ng, lvl, st, t_ref):
    c = HG_CHUNK
    prow = lax.broadcasted_iota(I32, (c, LANES), 0)
    tn = (((0,), (0,)), ((), ()))
    en = jnp.exp(-jnp.abs(fp))
    rc = 1.0 / (1.0 + en)
    pos_f = fp >= 0.0
    sig_p = jnp.where(pos_f, rc, en * rc)
    sig_n = jnp.where(pos_f, en * rc, rc)
    f = lbf + (1.0 - lb) * sig_p
    lf = jnp.log(f)
    kc = (1.0 - lb) * sig_n
    qc = q * _sigmoid(q)
    bc = lf
    for k in range(HG_LEVELS):
        d = 1 << k
        bc = bc + jnp.where(prow >= d, pltpu.roll(bc, d, axis=0), 0.0)

    a = jnp.zeros((c, c), F32)
    for lev in range(HG_LEVELS):
        m = c >> lev
        upper = jnp.bitwise_and(prow, m - 1) >= m // 2
        z = jnp.where(upper, qc, kc)
        if m == 2:
            zw = jnp.where(upper, z * f, z)
        else:
            dlt = bc - _hg_midpoint(bc, m)
            zw = z * jnp.exp2(dlt * jnp.where(upper, LOG2E, -LOG2E))
        zw = zw.astype(BF16)
        t_ref[lev] = zw.T
        s = jnp.dot(zw, t_ref[lev], preferred_element_type=F32)
        a = jnp.where(lvl == lev, s, a)
    vb = v.astype(BF16)
    o = jnp.dot(a.astype(BF16), vb, preferred_element_type=F32)
    o = o + jnp.sum(qc * kc, axis=-1, keepdims=True) * v
    t_ref[HG_LEVELS] = st.astype(BF16).T
    o = o + jnp.dot((qc * jnp.exp(bc)).astype(BF16), t_ref[HG_LEVELS], preferred_element_type=F32)
    bl = bc[c - 1:c, :]
    khat = (kc * jnp.exp(bl - bc)).astype(BF16)
    st = st * jnp.exp(bl) + lax.dot_general(vb, khat, tn, preferred_element_type=F32)
    return _rms(o, ng) * (gt * _sigmoid(gt)), st


def _hgrn_mix_kernel(q_ref, f_ref, i_ref, gt_ref, lb_ref, lbf_ref, ng_ref, lvl_ref,
                     ys_ref, u_ref, cb_ref, cc_ref, cv_ref, h_ref, d_ref, glu_ref, sn_ref, cw_ref, cn_ref, wo_ref,
                     o_ref, st_ref, hg_ref, carry_ref, t_ref):
    c = HG_CHUNK

    @pl.when(pl.program_id(0) == 0)
    def _():
        st_ref[...] = jnp.zeros_like(st_ref)
        carry_ref[...] = jnp.zeros_like(carry_ref)

    def chunk(n, carry):
        rows = pl.ds(pl.multiple_of(n * c, c), c)
        for hd in range(HG_HEADS):
            cols = slice(hd * LANES, (hd + 1) * LANES)
            o, st = _hg_chunk(f_ref[rows, cols], q_ref[rows, cols], i_ref[rows, cols], gt_ref[rows, cols],
                              lb_ref[:, cols], lbf_ref[:, cols], ng_ref[:, cols], lvl_ref[...], st_ref[hd],
                              t_ref.at[hd])
            hg_ref[rows, cols] = o.astype(BF16)
            st_ref[hd] = st
        return carry

    lax.fori_loop(0, q_ref.shape[0] // c, chunk, 0)
    _mix_epilogue(ys_ref, u_ref, cb_ref, cc_ref, cv_ref, hg_ref, h_ref, d_ref, glu_ref, sn_ref, cw_ref, cn_ref,
                  wo_ref, o_ref, carry_ref)


def _mix_epilogue(ys_ref, u_ref, cb_ref, cc_ref, cv_ref, hg_ref, h_ref,
                  d_ref, glu_ref, sn_ref, cw_ref, cn_ref, wo_ref, o_ref, carry_ref):
    y = ys_ref[...] + d_ref[...] * u_ref[...]
    y = jax.nn.gelu(y)
    y = y * _sigmoid(jnp.dot(y.astype(BF16), glu_ref[...], preferred_element_type=F32))
    y_s5 = _rms(y, sn_ref[...])

    z = cc_ref[...] * cv_ref[...]
    tm = z.shape[0]
    row = lax.broadcasted_iota(I32, z.shape, 0)
    p1 = carry_ref[7:8, :]
    p2 = carry_ref[6:7, :]
    z1 = jnp.where(row == 0, p1, pltpu.roll(z, 1, axis=0))
    z2 = jnp.where(row == 0, p2, jnp.where(row == 1, p1, pltpu.roll(z, 2, axis=0)))
    carry_ref[...] = z[tm - 8:tm, :]
    yc = cb_ref[...] * (z2 * cw_ref[0:1, :] + z1 * cw_ref[1:2, :] + z * cw_ref[2:3, :])
    y_cv = _rms(yc, cn_ref[...])

    acc = h_ref[...]
    hg_row = S5_WIDTH + CONV_WIDTH
    acc = acc + jnp.dot(y_s5.astype(BF16), wo_ref[0:S5_WIDTH, :], preferred_element_type=F32)
    acc = acc + jnp.dot(y_cv.astype(BF16), wo_ref[S5_WIDTH:hg_row, :], preferred_element_type=F32)
    acc = acc + jnp.dot(hg_ref[...], wo_ref[hg_row:, :], preferred_element_type=F32)
    o_ref[...] = acc


def _hgrn_mix(ys5, proj, h, lb, hg_norm, s5_d, s5_glu, s5_norm, conv_w, conv_norm, w_out):
    t, d = h.shape
    tb = min(HG_TBLOCK, t)
    lbh = jnp.clip(lb.astype(F32), 0.0, 1.0 - 1e-6).reshape(1, HG_WIDTH)
    lbf = jnp.maximum(lbh, LB_FLOOR)
    ng = hg_norm.astype(F32).reshape(1, HG_WIDTH)
    lvl = jnp.asarray(_hg_level_table())
    cw = jnp.pad(conv_w.astype(F32), ((0, SUBLANES - CONV_K), (0, 0)))
    wq = S5_WIDTH
    hg_col = COL_HG // HG_WIDTH

    def rowblk(width, colblk):
        return pl.BlockSpec((tb, width), lambda i, c=colblk: (i, c))

    def full(shape):
        return pl.BlockSpec(shape, lambda i: (0,) * len(shape))

    return pl.pallas_call(
        _hgrn_mix_kernel,
        grid=(t // tb,),
        in_specs=[rowblk(HG_WIDTH, hg_col), rowblk(HG_WIDTH, hg_col + 1), rowblk(HG_WIDTH, hg_col + 2),
                  rowblk(HG_WIDTH, hg_col + 3), full((1, HG_WIDTH)), full((1, HG_WIDTH)), full((1, HG_WIDTH)),
                  full(lvl.shape),
                  rowblk(wq, 0), rowblk(wq, 0), rowblk(wq, 1), rowblk(wq, 2), rowblk(wq, 3), rowblk(d, 0),
                  full((1, wq)), full((wq, wq)), full((1, wq)), full((SUBLANES, wq)), full((1, wq)),
                  full(w_out.shape)],
        out_specs=rowblk(d, 0),
        out_shape=jax.ShapeDtypeStruct((t, d), F32),
        scratch_shapes=[pltpu.VMEM((HG_HEADS, LANES, LANES), F32), pltpu.VMEM((tb, HG_WIDTH), BF16),
                        pltpu.VMEM((SUBLANES, CONV_WIDTH), F32),
                        pltpu.VMEM((HG_HEADS, HG_LEVELS + 1, HG_CHUNK, LANES), BF16)],
        compiler_params=_cparams("arbitrary"),
        name="hgrn2_mix_out",
    )(proj, proj, proj, proj, lbh, lbf, ng, lvl,
      ys5, proj, proj, proj, proj, h,
      s5_d.astype(F32).reshape(1, wq), s5_glu, s5_norm.astype(F32).reshape(1, wq), cw,
      conv_norm.astype(F32).reshape(1, wq), w_out)


def _swiglu_slice(x, w1_ref, w3_ref, w2_ref):
    out = None
    for c0 in range(0, w1_ref.shape[-1], SWIGLU_SUB):
        cs = slice(c0, c0 + SWIGLU_SUB)
        a = jnp.dot(x, w1_ref[:, cs].astype(BF16), preferred_element_type=F32)
        b = jnp.dot(x, w3_ref[:, cs].astype(BF16), preferred_element_type=F32)
        gact = (a * _sigmoid(a) * b).astype(BF16)
        part = jnp.dot(gact, w2_ref[cs, :].astype(BF16), preferred_element_type=F32)
        out = part if out is None else out + part
    return out


def _ffn_kernel(h_ref, g_ref, w1_ref, w3_ref, w2_ref, o_ref, hn_ref, acc_ref):
    j = pl.program_id(1)

    @pl.when(j == 0)
    def _():
        x = h_ref[...]
        hn_ref[...] = _rms(x, g_ref[...]).astype(BF16)
        acc_ref[...] = x

    acc_ref[...] += _swiglu_slice(hn_ref[...], w1_ref, w3_ref, w2_ref)

    @pl.when(j == pl.num_programs(1) - 1)
    def _():
        o_ref[...] = acc_ref[...]


def _ffn(h, g, w1, w3, w2, layer):
    t, d = h.shape
    f = w1.shape[2]
    tm = min(FFN_ROW_TILE, t)
    tf = FFN_COL_TILE
    return pl.pallas_call(
        _ffn_kernel,
        grid=(t // tm, f // tf),
        in_specs=[pl.BlockSpec((tm, d), lambda i, j: (i, 0)),
                  pl.BlockSpec((1, d), lambda i, j: (0, 0)),
                  pl.BlockSpec((None, d, tf), lambda i, j: (layer, 0, j)),
                  pl.BlockSpec((None, d, tf), lambda i, j: (layer, 0, j)),
                  pl.BlockSpec((None, tf, d), lambda i, j: (layer, j, 0))],
        out_specs=pl.BlockSpec((tm, d), lambda i, j: (i, 0)),
        out_shape=jax.ShapeDtypeStruct((t, d), F32),
        scratch_shapes=[pltpu.VMEM((tm, d), BF16), pltpu.VMEM((tm, d), F32)],
        compiler_params=_cparams("parallel", "arbitrary"),
        name="ffn_swiglu",
    )(h, g.astype(F32).reshape(1, d), w1, w3, w2)


def _router_kernel(h_ref, g_ref, r_ref, hn_ref, info_ref, info_t_ref, before_ref, total_ref, cnt_ref, *, n_exp):
    @pl.when(pl.program_id(0) == 0)
    def _():
        cnt_ref[...] = jnp.zeros_like(cnt_ref)

    xn = _rms(h_ref[...], g_ref[...])
    x_hi, x_lo = _bf16_terms(xn, 2)
    hn_ref[...] = x_hi
    r_hi, r_lo = _bf16_terms(r_ref[...], 2)
    logits = (jnp.dot(x_hi, r_hi, preferred_element_type=F32) + jnp.dot(x_lo, r_hi, preferred_element_type=F32)
              + jnp.dot(x_hi, r_lo, preferred_element_type=F32))
    tm = logits.shape[0]
    lane = lax.broadcasted_iota(I32, logits.shape, 1)
    lanef = lane.astype(F32)
    neg = jnp.float32(-jnp.inf)
    lg = jnp.where(lane < n_exp, logits, neg)
    m1 = jnp.max(lg, axis=-1, keepdims=True)
    i1 = jnp.min(jnp.where(lg == m1, lanef, float(LANES)), axis=-1, keepdims=True)
    oh1 = lanef == i1
    lg2 = jnp.where(oh1, neg, lg)
    m2 = jnp.max(lg2, axis=-1, keepdims=True)
    i2 = jnp.min(jnp.where(lg2 == m2, lanef, float(LANES)), axis=-1, keepdims=True)
    oh2 = lanef == i2
    ex = jnp.exp(m2 - m1)
    g1 = 1.0 / (1.0 + ex)
    g2 = ex * g1
    chosen = jnp.where(oh1, 1.0, jnp.where(oh2, 1.0, 0.0))
    ri = lax.broadcasted_iota(I32, (tm, tm), 0)
    ci = lax.broadcasted_iota(I32, (tm, tm), 1)
    tri = jnp.where(ri > ci, 1.0, 0.0).astype(BF16)
    before = cnt_ref[...]
    cexcl = jnp.dot(tri, chosen.astype(BF16), preferred_element_type=F32) + before
    rank1 = jnp.sum(jnp.where(oh1, cexcl, 0.0), axis=-1, keepdims=True)
    rank2 = jnp.sum(jnp.where(oh2, cexcl, 0.0), axis=-1, keepdims=True)
    info = jnp.where(lane == 0, i1, jnp.where(lane == 1, i2, jnp.where(lane == 2, g1, jnp.where(
        lane == 3, g2, jnp.where(lane == 4, rank1, jnp.where(lane == 5, rank2, 0.0))))))
    info_ref[...] = info
    info_t_ref[...] = info.T[0:SUBLANES, :]
    before_ref[0] = jnp.broadcast_to(before, before_ref.shape[1:])
    total = before + jnp.sum(chosen, axis=0, keepdims=True)
    cnt_ref[...] = total
    total_ref[...] = jnp.broadcast_to(total, total_ref.shape)


def _router(h, g, router):
    t, d = h.shape
    n_exp = router.shape[1]
    tm = min(TOK_WIN, t)
    ntw = t // tm
    rp = jnp.pad(router.astype(F32), ((0, 0), (0, LANES - n_exp)))
    return pl.pallas_call(
        functools.partial(_router_kernel, n_exp=n_exp),
        grid=(ntw,),
        in_specs=[pl.BlockSpec((tm, d), lambda i: (i, 0)),
                  pl.BlockSpec((1, d), lambda i: (0, 0)),
                  pl.BlockSpec((d, LANES), lambda i: (0, 0))],
        out_specs=[pl.BlockSpec((tm, d), lambda i: (i, 0)),
                   pl.BlockSpec((tm, LANES), lambda i: (i, 0)),
                   pl.BlockSpec((SUBLANES, tm), lambda i: (0, i)),
                   pl.BlockSpec((1, 8, LANES), lambda i: (i, 0, 0)),
                   pl.BlockSpec((8, LANES), lambda i: (0, 0))],
        out_shape=[jax.ShapeDtypeStruct((t, d), BF16),
                   jax.ShapeDtypeStruct((t, LANES), F32),
                   jax.ShapeDtypeStruct((SUBLANES, t), F32),
                   jax.ShapeDtypeStruct((ntw, 8, LANES), F32),
                   jax.ShapeDtypeStruct((8, LANES), F32)],
        scratch_shapes=[pltpu.VMEM((1, LANES), F32)],
        compiler_params=_cparams("arbitrary"),
        name="moe_router",
    )(h, g.astype(F32).reshape(1, d), rp)


def _gather_kernel(off_ref, nw_ref, wl_ref, tot_ref, dest_ref, hn_hbm, xs_ref, buf_ref, sem, acc_ref):
    b = pl.program_id(0)
    rows = acc_ref.shape[0]
    n_buf, win = buf_ref.shape[0], buf_ref.shape[1]
    n = nw_ref[b]
    q0 = off_ref[b]
    total = tot_ref[0]

    def copy(q):
        w = wl_ref[q]
        s = lax.rem(q, n_buf)
        return pltpu.make_async_copy(hn_hbm.at[pl.ds(pl.multiple_of(w * win, win), win), :], buf_ref.at[s],
                                     sem.at[s])

    @pl.when(b == 0)
    def _():
        for q in range(GATHER_AHEAD):
            @pl.when(q < total)
            def _(q=q):
                copy(q).start()

    acc_ref[...] = jnp.zeros_like(acc_ref)
    slot = b * rows + lax.broadcasted_iota(I32, (rows, win), 0)

    def gathered(q):
        @pl.when(q + GATHER_AHEAD < total)
        def _():
            copy(q + GATHER_AHEAD).start()

        col = pl.ds(pl.multiple_of(wl_ref[q] * win, win), win)
        d1 = dest_ref[0:1, col]
        d2 = dest_ref[1:2, col]
        hit = jnp.where(d1 == slot, 1.0, jnp.where(d2 == slot, 1.0, 0.0)).astype(BF16)
        copy(q).wait()
        return jnp.dot(hit, buf_ref[lax.rem(q, n_buf)], preferred_element_type=F32)

    def several(j, carry):
        q = q0 + GATHER_UNROLL * j
        acc_ref[...] += sum(gathered(q + r) for r in range(GATHER_UNROLL))
        return carry

    n_full = n // GATHER_UNROLL
    lax.fori_loop(0, n_full, several, 0)

    def single(j, carry):
        acc_ref[...] += gathered(q0 + n_full * GATHER_UNROLL + j)
        return carry

    lax.fori_loop(0, n - n_full * GATHER_UNROLL, single, 0)
    xs_ref[...] = acc_ref[...].astype(BF16)


def _gather(hn, dest, off, nw, wlist, total, n_slots):
    t, d = hn.shape
    return pl.pallas_call(
        _gather_kernel,
        grid_spec=pltpu.PrefetchScalarGridSpec(
            num_scalar_prefetch=4,
            grid=(n_slots // GATHER_TILE,),
            in_specs=[pl.BlockSpec(dest.shape, lambda b, *_: (0, 0)),
                      pl.BlockSpec(memory_space=pl.ANY)],
            out_specs=pl.BlockSpec((GATHER_TILE, d), lambda b, *_: (b, 0)),
            scratch_shapes=[pltpu.VMEM((GATHER_BUFS, TOK_WIN, d), BF16),
                            pltpu.SemaphoreType.DMA((GATHER_BUFS,)),
                            pltpu.VMEM((GATHER_TILE, d), F32)]),
        out_shape=jax.ShapeDtypeStruct((n_slots, d), BF16),
        compiler_params=_cparams("arbitrary"),
        name="moe_gather",
    )(off, nw, wlist, total, dest, hn)


def _expert_kernel(te_ref, tv_ref, x_ref, w1_ref, w3_ref, w2_ref, y_ref, acc_ref):
    b = pl.program_id(0)
    j = pl.program_id(1)

    @pl.when(j == 0)
    def _():
        acc_ref[...] = jnp.zeros_like(acc_ref)

    fill = tv_ref[b]
    half = x_ref.shape[0] // 2

    @pl.when(fill == 2)
    def _():
        acc_ref[...] += _swiglu_slice(x_ref[...], w1_ref, w3_ref, w2_ref)

    @pl.when(fill == 1)
    def _():
        acc_ref[0:half, :] += _swiglu_slice(x_ref[0:half, :], w1_ref, w3_ref, w2_ref)

    @pl.when(j == pl.num_programs(1) - 1)
    def _():
        y_ref[...] = acc_ref[...].astype(BF16)


def _experts(xs, tile_e, tile_v, w1, w3, w2, layer):
    ns, d = xs.shape
    f = w1.shape[3]
    tf = MOE_COL_TILE
    nj = f // tf
    n_tiles = ns // MOE_TILE

    def col(b, j, tv):
        used = jnp.minimum(tv[b], 1)
        return j * used + (nj - 1) * (1 - used)

    return pl.pallas_call(
        _expert_kernel,
        grid_spec=pltpu.PrefetchScalarGridSpec(
            num_scalar_prefetch=2,
            grid=(n_tiles, nj),
            in_specs=[pl.BlockSpec((MOE_TILE, d), lambda b, j, te, tv: (b, 0)),
                      pl.BlockSpec((None, None, d, tf), lambda b, j, te, tv: (layer, te[b], 0, col(b, j, tv))),
                      pl.BlockSpec((None, None, d, tf), lambda b, j, te, tv: (layer, te[b], 0, col(b, j, tv))),
                      pl.BlockSpec((None, None, tf, d), lambda b, j, te, tv: (layer, te[b], col(b, j, tv), 0))],
            out_specs=pl.BlockSpec((MOE_TILE, d), lambda b, j, te, tv: (b, 0)),
            scratch_shapes=[pltpu.VMEM((MOE_TILE, d), F32)]),
        out_shape=jax.ShapeDtypeStruct((ns, d), BF16),
        compiler_params=_cparams("arbitrary", "arbitrary"),
        name="moe_experts",
    )(tile_e, tile_v, xs, w1, w3, w2)


def _combine_kernel(ws_ref, po_ref, ex_ref, nar_ref, info_ref, h_ref, fin_ref, yb_hbm, o_ref,
                    buf_ref, nbuf_ref, sem, *, n_exp, final_norm):
    i = pl.program_id(0)
    n_steps = pl.num_programs(0)
    n_set = COMBINE_SETS
    cur = lax.rem(i, n_set)

    def wide_copy(step, e, s):
        start = pl.multiple_of(ws_ref[step * n_exp + e], SLOT_ALIGN)
        return pltpu.make_async_copy(yb_hbm.at[pl.ds(start, SLOT_WIN), :], buf_ref.at[s, e], sem.at[s, e])

    def narrow_copy(step, e, s):
        start = pl.multiple_of(ws_ref[step * n_exp + e], SLOT_ALIGN)
        return pltpu.make_async_copy(yb_hbm.at[pl.ds(start, NARROW_WIN), :],
                                     nbuf_ref.at[s, e * NARROW_WIN:(e + 1) * NARROW_WIN, :], sem.at[s, e])

    def for_step(step, s, act):
        @pl.when(nar_ref[step] == 1)
        def _():
            for e in range(n_exp):
                act(narrow_copy(step, e, s))

        @pl.when(nar_ref[step] == 0)
        def _():
            for e in range(n_exp):
                act(wide_copy(step, e, s))

    @pl.when(i == 0)
    def _():
        for s in range(n_set - 1):
            @pl.when(s < n_steps)
            def _(s=s):
                for_step(s, s, lambda c: c.start())

    ahead = i + n_set - 1

    @pl.when(ahead < n_steps)
    def _():
        for_step(ahead, lax.rem(ahead, n_set), lambda c: c.start())

    info = info_ref[...]
    tm = info.shape[0]
    e1, e2 = info[:, 0:1], info[:, 1:2]
    g1, g2 = info[:, 2:3], info[:, 3:4]
    r1, r2 = info[:, 4:5], info[:, 5:6]

    def rank_offset(e, shift):
        return (ws_ref[i * n_exp + e] - po_ref[e] + shift).astype(F32)

    for_step(i, cur, lambda c: c.wait())

    @pl.when(nar_ref[i] == 1)
    def _():
        k1 = jnp.full_like(r1, -1.0)
        k2 = jnp.full_like(r2, -1.0)
        for e in range(n_exp):
            k1 = jnp.where(e1 == float(e), r1 - rank_offset(e, -e * NARROW_WIN), k1)
            k2 = jnp.where(e2 == float(e), r2 - rank_offset(e, -e * NARROW_WIN), k2)
        width = n_exp * NARROW_WIN
        lanef = lax.broadcasted_iota(I32, (tm, width), 1).astype(F32)
        pt = jnp.where(k1 == lanef, jnp.broadcast_to(g1, (tm, width)),
                       jnp.where(k2 == lanef, jnp.broadcast_to(g2, (tm, width)), 0.0)).astype(BF16)
        o_ref[...] = h_ref[...] + jnp.dot(pt, nbuf_ref[cur], preferred_element_type=F32)

    @pl.when(nar_ref[i] == 0)
    def _():
        lanef = lax.broadcasted_iota(I32, (tm, TOK_WIN), 1).astype(F32)
        g1b = jnp.broadcast_to(g1, (tm, TOK_WIN))
        g2b = jnp.broadcast_to(g2, (tm, TOK_WIN))

        def weights(e, shift, lo_lane):
            off = rank_offset(e, shift)
            k1 = jnp.where(e1 == float(e), r1 - off, -1.0)
            k2 = jnp.where(e2 == float(e), r2 - off, -1.0)
            k1 = jnp.where(k1 >= float(lo_lane), k1, -1.0)
            k2 = jnp.where(k2 >= float(lo_lane), k2, -1.0)
            return jnp.where(k1 == lanef, g1b, jnp.where(k2 == lanef, g2b, 0.0)).astype(BF16)

        acc = h_ref[...]
        for e in range(n_exp):
            acc = acc + jnp.dot(weights(e, 0, 0), buf_ref[cur, e, 0:TOK_WIN, :], preferred_element_type=F32)
        o_ref[...] = acc

        tail = SLOT_WIN - TOK_WIN
        for e in range(n_exp):
            @pl.when(ex_ref[i * n_exp + e] == 1)
            def _(e=e):
                pt = weights(e, tail, TOK_WIN - tail)
                o_ref[...] += jnp.dot(pt, buf_ref[cur, e, tail:SLOT_WIN, :], preferred_element_type=F32)

    if final_norm:
        o_ref[...] = _rms(o_ref[...], fin_ref[...])


def _combine(info, h, yb, win_start, pstart, extra, narrow, fin_g, final_norm):
    t, d = h.shape
    n_exp = pstart.shape[0]
    tm = min(TOK_WIN, t)
    return pl.pallas_call(
        functools.partial(_combine_kernel, n_exp=n_exp, final_norm=final_norm),
        grid_spec=pltpu.PrefetchScalarGridSpec(
            num_scalar_prefetch=4,
            grid=(t // tm,),
            in_specs=[pl.BlockSpec((tm, LANES), lambda i, *_: (i, 0)),
                      pl.BlockSpec((tm, d), lambda i, *_: (i, 0)),
                      pl.BlockSpec((1, d), lambda i, *_: (0, 0)),
                      pl.BlockSpec(memory_space=pl.ANY)],
            out_specs=pl.BlockSpec((tm, d), lambda i, *_: (i, 0)),
            scratch_shapes=[pltpu.VMEM((COMBINE_SETS, n_exp, SLOT_WIN, d), BF16),
                            pltpu.VMEM((COMBINE_SETS, n_exp * NARROW_WIN, d), BF16),
                            pltpu.SemaphoreType.DMA((COMBINE_SETS, n_exp))]),
        out_shape=jax.ShapeDtypeStruct((t, d), F32),
        compiler_params=_cparams("arbitrary"),
        name="moe_combine",
    )(win_start, pstart, extra, narrow, info, h, fin_g.astype(F32).reshape(1, d), yb)


def _count_le(sorted_vals, x):
    return jnp.sum((sorted_vals[None, :] <= x[:, None]).astype(I32), axis=1)


def _moe(h, g, router, w1, w3, w2, layer, fin_g, final_norm):
    t, d = h.shape
    n_exp = router.shape[1]
    tw = min(TOK_WIN, t)
    ntw = t // tw
    hn, info, info_t, before, total = _router(h, g, router)

    counts = total[0, :n_exp].astype(I32)
    padded = (counts + MOE_TILE - 1) // MOE_TILE * MOE_TILE
    pend = jnp.cumsum(padded)
    pstart = pend - padded
    n_tiles = (t * TOP_K) // MOE_TILE + n_exp
    n_slots = n_tiles * MOE_TILE
    tile0 = jnp.arange(n_tiles, dtype=I32) * MOE_TILE
    tile_e = jnp.minimum(_count_le(pend, tile0), n_exp - 1)
    half = MOE_TILE // 2
    tile_rows = jnp.clip(counts[tile_e] - (tile0 - pstart[tile_e]), 0, MOE_TILE)
    tile_v = jnp.where(tile0 < pend[-1], (tile_rows + half - 1) // half, 0).astype(I32)
    cum = jnp.concatenate([before[:, 0, :n_exp], total[0:1, :n_exp]], axis=0).astype(I32)

    n_gb = n_slots // GATHER_TILE
    gb0 = jnp.arange(n_gb, dtype=I32) * GATHER_TILE
    gb_e = jnp.minimum(_count_le(pend, gb0), n_exp - 1)
    r0 = gb0 - pstart[gb_e]
    cum_b = cum[:, gb_e]
    ilo = jnp.sum((cum_b[1:] <= r0[None, :]).astype(I32), axis=0)
    ihi = jnp.sum((cum_b[:-1] < (r0 + GATHER_TILE)[None, :]).astype(I32), axis=0) - 1
    n_work = jnp.where(gb0 < pend[-1], jnp.maximum(ihi - ilo + 1, 0), 0).astype(I32)
    ends = jnp.cumsum(n_work)
    offs = (ends - n_work).astype(I32)
    entry = jnp.arange(n_exp * ntw + n_gb, dtype=I32)
    entry_blk = jnp.minimum(_count_le(ends, entry), n_gb - 1)
    wlist = jnp.clip(ilo[entry_blk] + entry - offs[entry_blk], 0, ntw - 1).astype(I32)

    e_t = info_t[0:2].astype(I32)
    slot_t = jnp.sum(jnp.where(e_t[None] == jnp.arange(n_exp, dtype=I32)[:, None, None], pstart[:, None, None], 0),
                     axis=0) + info_t[4:6].astype(I32)
    dest = jnp.concatenate([slot_t, jnp.full((SUBLANES - TOP_K, t), -1, I32)], axis=0)

    xs = _gather(hn, dest, offs, n_work, wlist, ends[-1:].astype(I32), n_slots)
    yb = _experts(xs, tile_e, tile_v, w1, w3, w2, layer)

    lo = pstart[None, :] + cum[:-1]
    win_start = jnp.minimum(lo // SLOT_ALIGN * SLOT_ALIGN, n_slots - SLOT_WIN).astype(I32)
    reach = lo - win_start + cum[1:] - cum[:-1]
    extra = (reach > TOK_WIN).astype(I32)
    narrow = jnp.all(reach <= NARROW_WIN, axis=1).astype(I32)
    return _combine(info, h, yb, win_start.reshape(-1), pstart.astype(I32), extra.reshape(-1), narrow, fin_g,
                    final_norm)


def _final_norm_kernel(h_ref, g_ref, o_ref):
    o_ref[...] = _rms(h_ref[...], g_ref[...])


def _final_norm(h, g):
    t, d = h.shape
    tm = min(ROW_TILE, t)
    return pl.pallas_call(
        _final_norm_kernel,
        grid=(t // tm,),
        in_specs=[pl.BlockSpec((tm, d), lambda i: (i, 0)), pl.BlockSpec((1, d), lambda i: (0, 0))],
        out_specs=pl.BlockSpec((tm, d), lambda i: (i, 0)),
        out_shape=jax.ShapeDtypeStruct((t, d), F32),
        compiler_params=_cparams("parallel"),
        name="final_norm",
    )(h, g.astype(F32).reshape(1, d))


def kernel(x, attn_norm, ffn_norm, final_norm, w_in, w_out, s5_lambda_re, s5_lambda_im, s5_log_dt, s5_b_re, s5_b_im, s5_c_re, s5_c_im, s5_d, s5_glu, s5_out_norm, conv_w, conv_out_norm, hg_lower_bounds, hg_out_norm, ffn_w1, ffn_w3, ffn_w2, moe_router, moe_w1, moe_w3, moe_w2):
    bsz, seq, d = x.shape
    depth = w_in.shape[0]
    t = bsz * seq
    assert bsz == 1, "token mixers are written for a single sequence"
    lb_soft = jax.nn.softmax(hg_lower_bounds.astype(F32), axis=0)
    lb_all = jnp.cumsum(lb_soft, axis=0) - lb_soft[0]
    n_scan = int(math.log2(t // S5_CHUNK))
    s5_ops = jax.vmap(functools.partial(_s5_operators, n_scan=n_scan))(
        s5_lambda_re, s5_lambda_im, s5_log_dt, s5_b_re, s5_b_im, s5_c_re, s5_c_im)
    h = x.reshape(t, d).astype(F32)
    for l in range(depth):
        proj = _norm_inproj(h, attn_norm[l].astype(F32), w_in[l].astype(BF16))
        ys5 = _s5_conv(proj, [op[l] for op in s5_ops])
        h = _hgrn_mix(ys5, proj, h, lb_all[l], hg_out_norm[l], s5_d[l], s5_glu[l].astype(BF16), s5_out_norm[l],
                      conv_w[l], conv_out_norm[l], w_out[l].astype(BF16))
        j = l // 2
        if l % 2 == 0:
            h = _ffn(h, ffn_norm[l], ffn_w1, ffn_w3, ffn_w2, j)
        else:
            h = _moe(h, ffn_norm[l], moe_router[j], moe_w1, moe_w3, moe_w2, j, final_norm, l == depth - 1)
    if depth % 2 == 1:
        h = _final_norm(h, final_norm)
    return h.reshape(bsz, seq, d)
```

```python
import functools
import math

import numpy as np
import jax
import jax.numpy as jnp
from jax import lax
from jax.experimental import pallas as pl
from jax.experimental.pallas import tpu as pltpu

F32 = jnp.float32
BF16 = jnp.bfloat16
I32 = jnp.int32

NORM_EPS = 1e-6
LB_FLOOR = 1e-30
TOP_K = 2

S5_WIDTH = 256
S5_GROUP = 16
CONV_WIDTH = 256
CONV_K = 3
HG_WIDTH = 512
HG_HEAD_DIM = 128
HG_HEADS = HG_WIDTH // HG_HEAD_DIM
LANES = 128
SUBLANES = 8
COL_HG = S5_WIDTH + 3 * CONV_WIDTH

VMEM_LIMIT = 56 * 1024 * 1024
S5_CHUNK = 32
HG_CHUNK = 128
HG_LEVELS = int(math.log2(HG_CHUNK))
HG_TBLOCK = 512
ROW_TILE = 512
FFN_ROW_TILE = 1024
FFN_COL_TILE = 512
SWIGLU_SUB = 256
MOE_TILE = 1024
MOE_COL_TILE = 512
GATHER_TILE = 256
GATHER_AHEAD = 6
GATHER_UNROLL = 4
GATHER_BUFS = GATHER_AHEAD + GATHER_UNROLL
TOK_WIN = 256
SLOT_ALIGN = 16
SLOT_WIN = TOK_WIN + SLOT_ALIGN
NARROW_WIN = 128
COMBINE_SETS = 3


def _cparams(*sem):
    return pltpu.CompilerParams(dimension_semantics=sem, vmem_limit_bytes=VMEM_LIMIT)


def _rms(x, g):
    ms = jnp.mean(x * x, axis=-1, keepdims=True)
    return x * lax.rsqrt(ms + NORM_EPS) * g


def _sigmoid(x):
    return 1.0 / (1.0 + jnp.exp(-x))


def _norm_inproj_kernel(h_ref, g_ref, w_ref, o_ref):
    xn = _rms(h_ref[...], g_ref[...]).astype(BF16)
    o_ref[...] = jnp.dot(xn, w_ref[...], preferred_element_type=F32)


def _norm_inproj(h, g, w):
    t, d = h.shape
    n = w.shape[1]
    tm = min(ROW_TILE, t)
    return pl.pallas_call(
        _norm_inproj_kernel,
        grid=(t // tm,),
        in_specs=[pl.BlockSpec((tm, d), lambda i: (i, 0)),
                  pl.BlockSpec((1, d), lambda i: (0, 0)),
                  pl.BlockSpec((d, n), lambda i: (0, 0))],
        out_specs=pl.BlockSpec((tm, n), lambda i: (i, 0)),
        out_shape=jax.ShapeDtypeStruct((t, n), F32),
        compiler_params=_cparams("parallel"),
        name="norm_inproj",
    )(h, g.reshape(1, d), w)


def _s5_operators(lam_re, lam_im, log_dt, b_re, b_im, c_re, c_im, n_scan):
    lc = S5_CHUNK
    hi = lax.Precision.HIGHEST
    lr, li = lam_re.astype(F32), lam_im.astype(F32)
    dt = jnp.exp(log_dt.astype(F32))[:, None]
    zr, zi = lr * dt, li * dt
    taus = jnp.arange(lc + 1, dtype=F32)[:, None, None]
    mag = jnp.exp(zr[None] * taus)
    pwr, pwi = mag * jnp.cos(zi[None] * taus), mag * jnp.sin(zi[None] * taus)
    nr, ni = pwr[1] - 1.0, pwi[1]
    den = lr * lr + li * li
    qr, qi = (nr * lr + ni * li) / den, (ni * lr - nr * li) / den
    br, bi = b_re.astype(F32), b_im.astype(F32)
    bbr = qr[..., None] * br - qi[..., None] * bi
    bbi = qr[..., None] * bi + qi[..., None] * br
    cr, ci = c_re.astype(F32), c_im.astype(F32)
    g_, p_ = lr.shape

    def c_times_pw(lo):
        wr, wi = pwr[lo:lo + lc, :, None, :], pwi[lo:lo + lc, :, None, :]
        return cr[None] * wr - ci[None] * wi, cr[None] * wi + ci[None] * wr

    cpr, cpi = c_times_pw(0)
    cp = jnp.concatenate([cpr, cpi], axis=-1).transpose(1, 0, 2, 3).reshape(g_, lc * S5_GROUP, 2 * p_)
    kt = jnp.matmul(cp, jnp.concatenate([bbr, -bbi], axis=1), precision=hi)
    kflat = kt.transpose(0, 2, 1)

    wr, wi = pwr[lc - 1::-1][:, :, :, None], pwi[lc - 1::-1][:, :, :, None]
    msr = (wr * bbr[None] - wi * bbi[None]).transpose(1, 0, 3, 2).reshape(g_, lc * S5_GROUP, p_)
    msi = (wr * bbi[None] + wi * bbr[None]).transpose(1, 0, 3, 2).reshape(g_, lc * S5_GROUP, p_)
    m_state = jnp.concatenate([msr, msi], axis=-1)

    c1r, c1i = c_times_pw(1)
    c1r = c1r.transpose(1, 3, 0, 2).reshape(g_, p_, lc * S5_GROUP)
    c1i = c1i.transpose(1, 3, 0, 2).reshape(g_, p_, lc * S5_GROUP)
    m_carry = jnp.concatenate([c1r, -c1i], axis=1)

    akr, aki = [pwr[lc]], [pwi[lc]]
    for _ in range(n_scan - 1):
        r, i = akr[-1], aki[-1]
        akr.append(r * r - i * i)
        aki.append(2.0 * r * i)
    akr, aki = jnp.stack(akr, axis=1), jnp.stack(aki, axis=1)
    ar = jnp.concatenate([akr, akr], axis=-1)
    ai = jnp.concatenate([-aki, aki], axis=-1)
    kpad = -(-n_scan // 8) * 8
    ar = jnp.pad(ar, ((0, 0), (0, kpad - n_scan), (0, 0)))
    ai = jnp.pad(ai, ((0, 0), (0, kpad - n_scan), (0, 0)))
    return kflat, m_state.astype(BF16), m_carry.astype(BF16), ar, ai


S5_PER_TILE = LANES // S5_GROUP


def _s5_kernel(proj_hbm, kf_ref, ms_ref, mc_ref, ar_ref, ai_ref, y_hbm, us_ref, ys_ref, mi_ref, sem_in, sem_out,
               *, n_scan):
    lc, gw, per = S5_CHUNK, S5_GROUP, S5_PER_TILE
    nch = us_ref.shape[1]
    cols = pl.ds(pl.multiple_of(pl.program_id(0) * LANES, LANES), LANES)

    def in_copy(s):
        return pltpu.make_async_copy(proj_hbm.at[:, s, cols], us_ref.at[s], sem_in.at[s])

    def out_copy(s):
        return pltpu.make_async_copy(ys_ref.at[s], y_hbm.at[:, s, cols], sem_out.at[s])

    for s in range(lc):
        in_copy(s).start()
    ys_ref[...] = jnp.zeros_like(ys_ref)
    for s in range(lc):
        in_copy(s).wait()
        if s % per:
            us_ref[s] = pltpu.roll(us_ref[s], (s % per) * gw, axis=1)

    lane_grp = lax.broadcasted_iota(I32, (1, LANES), 1) // gw
    kf_lane = lax.broadcasted_iota(I32, (gw, lc * gw), 1)

    def group(gl, carry):
        rel_grp = jnp.bitwise_and(lane_grp - gl, per - 1)
        tiles = []
        for j in range(lc // per):
            merged = us_ref[j * per]
            for k in range(1, per):
                merged = jnp.where(rel_grp == k, us_ref[j * per + k], merged)
            tiles.append(pltpu.roll(merged, jnp.bitwise_and(-gl * gw, LANES - 1), axis=1))
        u = jnp.concatenate(tiles, axis=1).astype(BF16)

        kf = kf_ref[gl]
        for s in range(lc):
            blk = kf if s == 0 else jnp.where(kf_lane >= s * gw, pltpu.roll(kf, s * gw, axis=1), 0.0)
            mi_ref[s * gw:(s + 1) * gw, :] = blk.astype(BF16)

        x = jnp.dot(u, ms_ref[gl], preferred_element_type=F32)
        row = lax.broadcasted_iota(I32, x.shape, 0)
        half = x.shape[1] // 2
        for k in range(n_scan):
            d = 1 << k
            s = jnp.where(row >= d, pltpu.roll(x, d, axis=0), 0.0)
            x = x + ar_ref[gl, k:k + 1, :] * s + ai_ref[gl, k:k + 1, :] * pltpu.roll(s, half, axis=1)
        xe = jnp.where(row >= 1, pltpu.roll(x, 1, axis=0), 0.0)
        y = jnp.dot(u, mi_ref[...], preferred_element_type=F32)
        y = y + jnp.dot(xe.astype(BF16), mc_ref[gl], preferred_element_type=F32)

        for j in range(lc // per):
            moved = pltpu.roll(y[:, j * LANES:(j + 1) * LANES], jnp.bitwise_and(gl * gw, LANES - 1), axis=1)
            for k in range(per):
                ys_ref[j * per + k] = jnp.where(rel_grp == k, moved, ys_ref[j * per + k])
        return carry

    lax.fori_loop(0, per, group, 0)
    for s in range(lc):
        if s % per:
            ys_ref[s] = pltpu.roll(ys_ref[s], LANES - (s % per) * gw, axis=1)
        out_copy(s).start()
    for s in range(lc):
        out_copy(s).wait()


def _s5_conv(proj, ops):
    kflat, m_state, m_carry, ar, ai = ops
    t, n = proj.shape
    lc = S5_CHUNK
    nch = t // lc
    n_scan = int(math.log2(nch))
    assert (1 << n_scan) == nch
    w = lc * S5_GROUP
    p2 = m_state.shape[-1]
    per = S5_PER_TILE

    def grp(shape):
        return pl.BlockSpec((per,) + shape, lambda hh: (hh, 0, 0))

    hbm = pl.BlockSpec(memory_space=pl.ANY)
    y = pl.pallas_call(
        functools.partial(_s5_kernel, n_scan=n_scan),
        grid=(S5_WIDTH // LANES,),
        in_specs=[hbm, grp((S5_GROUP, w)), grp((w, p2)), grp((p2, w)),
                  grp((ar.shape[1], p2)), grp((ai.shape[1], p2))],
        out_specs=hbm,
        out_shape=jax.ShapeDtypeStruct((nch, lc, S5_WIDTH), F32),
        scratch_shapes=[pltpu.VMEM((lc, nch, LANES), F32), pltpu.VMEM((lc, nch, LANES), F32),
                        pltpu.VMEM((w, w), BF16), pltpu.SemaphoreType.DMA((lc,)), pltpu.SemaphoreType.DMA((lc,))],
        compiler_params=_cparams("arbitrary"),
        name="s5_conv",
    )(proj.reshape(nch, lc, n), kflat, m_state, m_carry, ar, ai)
    return y.reshape(t, S5_WIDTH)


LOG2E = float(np.log2(np.e))


def _hg_level_table():
    idx = np.arange(HG_CHUNK)
    t, s = idx[:, None], idx[None, :]
    top_bit = np.floor(np.log2(np.maximum(t ^ s, 1))).astype(np.int32)
    return np.where(s < t, HG_LEVELS - 1 - top_bit, -1).astype(np.int32)


def _hg_midpoint(bc, m):
    c = bc.shape[0]
    h = m // 2
    if h >= SUBLANES:
        return jnp.concatenate([jnp.broadcast_to(bc[j * m + h - 1:j * m + h, :], (m, LANES))
                                for j in range(c // m)], axis=0)
    x3 = bc.reshape(c // SUBLANES, SUBLANES, LANES)
    sub = lax.broadcasted_iota(I32, x3.shape, 1)
    beta = None
    for j in reversed(range(SUBLANES // m)):
        row = jnp.broadcast_to(x3[:, j * m + h - 1:j * m + h, :], x3.shape)
        beta = row if beta is None else jnp.where(sub < (j + 1) * m, row, beta)
    return beta.reshape(c, LANES)


def _bf16_terms(x, n):
    terms = []
    for _ in range(n - 1):
        t = x.astype(BF16)
        terms.append(t)
        x = x - t.astype(F32)
    return terms + [x.astype(BF16)]


def _hg_chunk(fp, q, v, gt, lb, lbf, ng, lvl, st):
    c = HG_CHUNK
    prow = lax.broadcasted_iota(I32, (c, LANES), 0)
    nt = (((1,), (1,)), ((), ()))
    tn = (((0,), (0,)), ((), ()))
    en = jnp.exp(-jnp.abs(fp))
    rc = 1.0 / (1.0 + en)
    pos_f = fp >= 0.0
    sig_p = jnp.where(pos_f, rc, en * rc)
    sig_n = jnp.where(pos_f, en * rc, rc)
    f = lbf + (1.0 - lb) * sig_p
    lf = jnp.log(f)
    kc = (1.0 - lb) * sig_n
    qc = q * _sigmoid(q)
    bc = lf
    for k in range(HG_LEVELS):
        d = 1 << k
        bc = bc + jnp.where(prow >= d, pltpu.roll(bc, d, axis=0), 0.0)

    a = jnp.zeros((c, c), F32)
    for lev in range(HG_LEVELS):
        m = c >> lev
        upper = jnp.bitwise_and(prow, m - 1) >= m // 2
        z = jnp.where(upper, qc, kc)
        if m == 2:
            zw = jnp.where(upper, z * f, z)
        else:
            dlt = bc - _hg_midpoint(bc, m)
            zw = z * jnp.exp2(dlt * jnp.where(upper, LOG2E, -LOG2E))
        zw = zw.astype(BF16)
        s = lax.dot_general(zw, zw, nt, preferred_element_type=F32)
        a = jnp.where(lvl == lev, s, a)
    vb = v.astype(BF16)
    o = jnp.dot(a.astype(BF16), vb, preferred_element_type=F32)
    o = o + jnp.sum(qc * kc, axis=-1, keepdims=True) * v
    o = o + lax.dot_general((qc * jnp.exp(bc)).astype(BF16), st.astype(BF16), nt,
                            preferred_element_type=F32)
    bl = bc[c - 1:c, :]
    khat = (kc * jnp.exp(bl - bc)).astype(BF16)
    st = st * jnp.exp(bl) + lax.dot_general(vb, khat, tn, preferred_element_type=F32)
    return _rms(o, ng) * (gt * _sigmoid(gt)), st


def _hgrn_mix_kernel(q_ref, f_ref, i_ref, gt_ref, lb_ref, lbf_ref, ng_ref, lvl_ref,
                     ys_ref, u_ref, cb_ref, cc_ref, cv_ref, h_ref, d_ref, glu_ref, sn_ref, cw_ref, cn_ref, wo_ref,
                     o_ref, st_ref, hg_ref, carry_ref):
    c = HG_CHUNK

    @pl.when(pl.program_id(0) == 0)
    def _():
        st_ref[...] = jnp.zeros_like(st_ref)
        carry_ref[...] = jnp.zeros_like(carry_ref)

    def chunk(n, carry):
        rows = pl.ds(pl.multiple_of(n * c, c), c)
        for hd in range(HG_HEADS):
            cols = slice(hd * LANES, (hd + 1) * LANES)
            o, st = _hg_chunk(f_ref[rows, cols], q_ref[rows, cols], i_ref[rows, cols], gt_ref[rows, cols],
                              lb_ref[:, cols], lbf_ref[:, cols], ng_ref[:, cols], lvl_ref[...], st_ref[hd])
            hg_ref[rows, cols] = o.astype(BF16)
            st_ref[hd] = st
        return carry

    lax.fori_loop(0, q_ref.shape[0] // c, chunk, 0)
    _mix_epilogue(ys_ref, u_ref, cb_ref, cc_ref, cv_ref, hg_ref, h_ref, d_ref, glu_ref, sn_ref, cw_ref, cn_ref,
                  wo_ref, o_ref, carry_ref)


def _mix_epilogue(ys_ref, u_ref, cb_ref, cc_ref, cv_ref, hg_ref, h_ref,
                  d_ref, glu_ref, sn_ref, cw_ref, cn_ref, wo_ref, o_ref, carry_ref):
    y = ys_ref[...] + d_ref[...] * u_ref[...]
    y = jax.nn.gelu(y)
    y = y * _sigmoid(jnp.dot(y.astype(BF16), glu_ref[...], preferred_element_type=F32))
    y_s5 = _rms(y, sn_ref[...])

    z = cc_ref[...] * cv_ref[...]
    tm = z.shape[0]
    row = lax.broadcasted_iota(I32, z.shape, 0)
    p1 = carry_ref[7:8, :]
    p2 = carry_ref[6:7, :]
    z1 = jnp.where(row == 0, p1, pltpu.roll(z, 1, axis=0))
    z2 = jnp.where(row == 0, p2, jnp.where(row == 1, p1, pltpu.roll(z, 2, axis=0)))
    carry_ref[...] = z[tm - 8:tm, :]
    yc = cb_ref[...] * (z2 * cw_ref[0:1, :] + z1 * cw_ref[1:2, :] + z * cw_ref[2:3, :])
    y_cv = _rms(yc, cn_ref[...])

    acc = h_ref[...]
    hg_row = S5_WIDTH + CONV_WIDTH
    acc = acc + jnp.dot(y_s5.astype(BF16), wo_ref[0:S5_WIDTH, :], preferred_element_type=F32)
    acc = acc + jnp.dot(y_cv.astype(BF16), wo_ref[S5_WIDTH:hg_row, :], preferred_element_type=F32)
    acc = acc + jnp.dot(hg_ref[...], wo_ref[hg_row:, :], preferred_element_type=F32)
    o_ref[...] = acc


def _hgrn_mix(ys5, proj, h, lb, hg_norm, s5_d, s5_glu, s5_norm, conv_w, conv_norm, w_out):
    t, d = h.shape
    tb = min(HG_TBLOCK, t)
    lbh = jnp.clip(lb.astype(F32), 0.0, 1.0 - 1e-6).reshape(1, HG_WIDTH)
    lbf = jnp.maximum(lbh, LB_FLOOR)
    ng = hg_norm.astype(F32).reshape(1, HG_WIDTH)
    lvl = jnp.asarray(_hg_level_table())
    cw = jnp.pad(conv_w.astype(F32), ((0, SUBLANES - CONV_K), (0, 0)))
    wq = S5_WIDTH
    hg_col = COL_HG // HG_WIDTH

    def rowblk(width, colblk):
        return pl.BlockSpec((tb, width), lambda i, c=colblk: (i, c))

    def full(shape):
        return pl.BlockSpec(shape, lambda i: (0,) * len(shape))

    return pl.pallas_call(
        _hgrn_mix_kernel,
        grid=(t // tb,),
        in_specs=[rowblk(HG_WIDTH, hg_col), rowblk(HG_WIDTH, hg_col + 1), rowblk(HG_WIDTH, hg_col + 2),
                  rowblk(HG_WIDTH, hg_col + 3), full((1, HG_WIDTH)), full((1, HG_WIDTH)), full((1, HG_WIDTH)),
                  full(lvl.shape),
                  rowblk(wq, 0), rowblk(wq, 0), rowblk(wq, 1), rowblk(wq, 2), rowblk(wq, 3), rowblk(d, 0),
                  full((1, wq)), full((wq, wq)), full((1, wq)), full((SUBLANES, wq)), full((1, wq)),
                  full(w_out.shape)],
        out_specs=rowblk(d, 0),
        out_shape=jax.ShapeDtypeStruct((t, d), F32),
        scratch_shapes=[pltpu.VMEM((HG_HEADS, LANES, LANES), F32), pltpu.VMEM((tb, HG_WIDTH), BF16),
                        pltpu.VMEM((SUBLANES, CONV_WIDTH), F32)],
        compiler_params=_cparams("arbitrary"),
        name="hgrn2_mix_out",
    )(proj, proj, proj, proj, lbh, lbf, ng, lvl,
      ys5, proj, proj, proj, proj, h,
      s5_d.astype(F32).reshape(1, wq), s5_glu, s5_norm.astype(F32).reshape(1, wq), cw,
      conv_norm.astype(F32).reshape(1, wq), w_out)


def _swiglu_slice(x, w1_ref, w3_ref, w2_ref):
    out = None
    for c0 in range(0, w1_ref.shape[-1], SWIGLU_SUB):
        cs = slice(c0, c0 + SWIGLU_SUB)
        a = jnp.dot(x, w1_ref[:, cs].astype(BF16), preferred_element_type=F32)
        b = jnp.dot(x, w3_ref[:, cs].astype(BF16), preferred_element_type=F32)
        gact = (a * _sigmoid(a) * b).astype(BF16)
        part = jnp.dot(gact, w2_ref[cs, :].astype(BF16), preferred_element_type=F32)
        out = part if out is None else out + part
    return out


def _ffn_kernel(h_ref, g_ref, w1_ref, w3_ref, w2_ref, o_ref, hn_ref, acc_ref):
    j = pl.program_id(1)

    @pl.when(j == 0)
    def _():
        x = h_ref[...]
        hn_ref[...] = _rms(x, g_ref[...]).astype(BF16)
        acc_ref[...] = x

    acc_ref[...] += _swiglu_slice(hn_ref[...], w1_ref, w3_ref, w2_ref)

    @pl.when(j == pl.num_programs(1) - 1)
    def _():
        o_ref[...] = acc_ref[...]


def _ffn(h, g, w1, w3, w2, layer):
    t, d = h.shape
    f = w1.shape[2]
    tm = min(FFN_ROW_TILE, t)
    tf = FFN_COL_TILE
    return pl.pallas_call(
        _ffn_kernel,
        grid=(t // tm, f // tf),
        in_specs=[pl.BlockSpec((tm, d), lambda i, j: (i, 0)),
                  pl.BlockSpec((1, d), lambda i, j: (0, 0)),
                  pl.BlockSpec((None, d, tf), lambda i, j: (layer, 0, j)),
                  pl.BlockSpec((None, d, tf), lambda i, j: (layer, 0, j)),
                  pl.BlockSpec((None, tf, d), lambda i, j: (layer, j, 0))],
        out_specs=pl.BlockSpec((tm, d), lambda i, j: (i, 0)),
        out_shape=jax.ShapeDtypeStruct((t, d), F32),
        scratch_shapes=[pltpu.VMEM((tm, d), BF16), pltpu.VMEM((tm, d), F32)],
        compiler_params=_cparams("parallel", "arbitrary"),
        name="ffn_swiglu",
    )(h, g.astype(F32).reshape(1, d), w1, w3, w2)


def _router_kernel(h_ref, g_ref, r_ref, hn_ref, info_ref, info_t_ref, before_ref, total_ref, cnt_ref, *, n_exp):
    @pl.when(pl.program_id(0) == 0)
    def _():
        cnt_ref[...] = jnp.zeros_like(cnt_ref)

    xn = _rms(h_ref[...], g_ref[...])
    x_hi, x_lo = _bf16_terms(xn, 2)
    hn_ref[...] = x_hi
    r_hi, r_lo = _bf16_terms(r_ref[...], 2)
    logits = (jnp.dot(x_hi, r_hi, preferred_element_type=F32) + jnp.dot(x_lo, r_hi, preferred_element_type=F32)
              + jnp.dot(x_hi, r_lo, preferred_element_type=F32))
    tm = logits.shape[0]
    lane = lax.broadcasted_iota(I32, logits.shape, 1)
    lanef = lane.astype(F32)
    neg = jnp.float32(-jnp.inf)
    lg = jnp.where(lane < n_exp, logits, neg)
    m1 = jnp.max(lg, axis=-1, keepdims=True)
    i1 = jnp.min(jnp.where(lg == m1, lanef, float(LANES)), axis=-1, keepdims=True)
    oh1 = lanef == i1
    lg2 = jnp.where(oh1, neg, lg)
    m2 = jnp.max(lg2, axis=-1, keepdims=True)
    i2 = jnp.min(jnp.where(lg2 == m2, lanef, float(LANES)), axis=-1, keepdims=True)
    oh2 = lanef == i2
    ex = jnp.exp(m2 - m1)
    g1 = 1.0 / (1.0 + ex)
    g2 = ex * g1
    chosen = jnp.where(oh1, 1.0, jnp.where(oh2, 1.0, 0.0))
    ri = lax.broadcasted_iota(I32, (tm, tm), 0)
    ci = lax.broadcasted_iota(I32, (tm, tm), 1)
    tri = jnp.where(ri > ci, 1.0, 0.0).astype(BF16)
    before = cnt_ref[...]
    cexcl = jnp.dot(tri, chosen.astype(BF16), preferred_element_type=F32) + before
    rank1 = jnp.sum(jnp.where(oh1, cexcl, 0.0), axis=-1, keepdims=True)
    rank2 = jnp.sum(jnp.where(oh2, cexcl, 0.0), axis=-1, keepdims=True)
    info = jnp.where(lane == 0, i1, jnp.where(lane == 1, i2, jnp.where(lane == 2, g1, jnp.where(
        lane == 3, g2, jnp.where(lane == 4, rank1, jnp.where(lane == 5, rank2, 0.0))))))
    info_ref[...] = info
    info_t_ref[...] = info.T[0:SUBLANES, :]
    before_ref[0] = jnp.broadcast_to(before, before_ref.shape[1:])
    total = before + jnp.sum(chosen, axis=0, keepdims=True)
    cnt_ref[...] = total
    total_ref[...] = jnp.broadcast_to(total, total_ref.shape)


def _router(h, g, router):
    t, d = h.shape
    n_exp = router.shape[1]
    tm = min(TOK_WIN, t)
    ntw = t // tm
    rp = jnp.pad(router.astype(F32), ((0, 0), (0, LANES - n_exp)))
    return pl.pallas_call(
        functools.partial(_router_kernel, n_exp=n_exp),
        grid=(ntw,),
        in_specs=[pl.BlockSpec((tm, d), lambda i: (i, 0)),
                  pl.BlockSpec((1, d), lambda i: (0, 0)),
                  pl.BlockSpec((d, LANES), lambda i: (0, 0))],
        out_specs=[pl.BlockSpec((tm, d), lambda i: (i, 0)),
                   pl.BlockSpec((tm, LANES), lambda i: (i, 0)),
                   pl.BlockSpec((SUBLANES, tm), lambda i: (0, i)),
                   pl.BlockSpec((1, 8, LANES), lambda i: (i, 0, 0)),
                   pl.BlockSpec((8, LANES), lambda i: (0, 0))],
        out_shape=[jax.ShapeDtypeStruct((t, d), BF16),
                   jax.ShapeDtypeStruct((t, LANES), F32),
                   jax.ShapeDtypeStruct((SUBLANES, t), F32),
                   jax.ShapeDtypeStruct((ntw, 8, LANES), F32),
                   jax.ShapeDtypeStruct((8, LANES), F32)],
        scratch_shapes=[pltpu.VMEM((1, LANES), F32)],
        compiler_params=_cparams("arbitrary"),
        name="moe_router",
    )(h, g.astype(F32).reshape(1, d), rp)


def _gather_kernel(off_ref, nw_ref, wl_ref, tot_ref, dest_ref, hn_hbm, xs_ref, buf_ref, sem, acc_ref):
    b = pl.program_id(0)
    rows = acc_ref.shape[0]
    n_buf, win = buf_ref.shape[0], buf_ref.shape[1]
    n = nw_ref[b]
    q0 = off_ref[b]
    total = tot_ref[0]

    def copy(q):
        w = wl_ref[q]
        s = lax.rem(q, n_buf)
        return pltpu.make_async_copy(hn_hbm.at[pl.ds(pl.multiple_of(w * win, win), win), :], buf_ref.at[s],
                                     sem.at[s])

    @pl.when(b == 0)
    def _():
        for q in range(GATHER_AHEAD):
            @pl.when(q < total)
            def _(q=q):
                copy(q).start()

    acc_ref[...] = jnp.zeros_like(acc_ref)
    slot = b * rows + lax.broadcasted_iota(I32, (rows, win), 0)

    def gathered(q):
        @pl.when(q + GATHER_AHEAD < total)
        def _():
            copy(q + GATHER_AHEAD).start()

        col = pl.ds(pl.multiple_of(wl_ref[q] * win, win), win)
        d1 = dest_ref[0:1, col]
        d2 = dest_ref[1:2, col]
        hit = jnp.where(d1 == slot, 1.0, jnp.where(d2 == slot, 1.0, 0.0)).astype(BF16)
        copy(q).wait()
        return jnp.dot(hit, buf_ref[lax.rem(q, n_buf)], preferred_element_type=F32)

    def several(j, carry):
        q = q0 + GATHER_UNROLL * j
        acc_ref[...] += sum(gathered(q + r) for r in range(GATHER_UNROLL))
        return carry

    n_full = n // GATHER_UNROLL
    lax.fori_loop(0, n_full, several, 0)

    def single(j, carry):
        acc_ref[...] += gathered(q0 + n_full * GATHER_UNROLL + j)
        return carry

    lax.fori_loop(0, n - n_full * GATHER_UNROLL, single, 0)
    xs_ref[...] = acc_ref[...].astype(BF16)


def _gather(hn, dest, off, nw, wlist, total, n_slots):
    t, d = hn.shape
    return pl.pallas_call(
        _gather_kernel,
        grid_spec=pltpu.PrefetchScalarGridSpec(
            num_scalar_prefetch=4,
            grid=(n_slots // GATHER_TILE,),
            in_specs=[pl.BlockSpec(dest.shape, lambda b, *_: (0, 0)),
                      pl.BlockSpec(memory_space=pl.ANY)],
            out_specs=pl.BlockSpec((GATHER_TILE, d), lambda b, *_: (b, 0)),
            scratch_shapes=[pltpu.VMEM((GATHER_BUFS, TOK_WIN, d), BF16),
                            pltpu.SemaphoreType.DMA((GATHER_BUFS,)),
                            pltpu.VMEM((GATHER_TILE, d), F32)]),
        out_shape=jax.ShapeDtypeStruct((n_slots, d), BF16),
        compiler_params=_cparams("arbitrary"),
        name="moe_gather",
    )(off, nw, wlist, total, dest, hn)


def _expert_kernel(te_ref, tv_ref, x_ref, w1_ref, w3_ref, w2_ref, y_ref, acc_ref):
    b = pl.program_id(0)
    j = pl.program_id(1)

    @pl.when(j == 0)
    def _():
        acc_ref[...] = jnp.zeros_like(acc_ref)

    fill = tv_ref[b]
    half = x_ref.shape[0] // 2

    @pl.when(fill == 2)
    def _():
        acc_ref[...] += _swiglu_slice(x_ref[...], w1_ref, w3_ref, w2_ref)

    @pl.when(fill == 1)
    def _():
        acc_ref[0:half, :] += _swiglu_slice(x_ref[0:half, :], w1_ref, w3_ref, w2_ref)

    @pl.when(j == pl.num_programs(1) - 1)
    def _():
        y_ref[...] = acc_ref[...].astype(BF16)


def _experts(xs, tile_e, tile_v, w1, w3, w2, layer):
    ns, d = xs.shape
    f = w1.shape[3]
    tf = MOE_COL_TILE
    nj = f // tf
    n_tiles = ns // MOE_TILE

    def col(b, j, tv):
        used = jnp.minimum(tv[b], 1)
        return j * used + (nj - 1) * (1 - used)

    return pl.pallas_call(
        _expert_kernel,
        grid_spec=pltpu.PrefetchScalarGridSpec(
            num_scalar_prefetch=2,
            grid=(n_tiles, nj),
            in_specs=[pl.BlockSpec((MOE_TILE, d), lambda b, j, te, tv: (b, 0)),
                      pl.BlockSpec((None, None, d, tf), lambda b, j, te, tv: (layer, te[b], 0, col(b, j, tv))),
                      pl.BlockSpec((None, None, d, tf), lambda b, j, te, tv: (layer, te[b], 0, col(b, j, tv))),
                      pl.BlockSpec((None, None, tf, d), lambda b, j, te, tv: (layer, te[b], col(b, j, tv), 0))],
            out_specs=pl.BlockSpec((MOE_TILE, d), lambda b, j, te, tv: (b, 0)),
            scratch_shapes=[pltpu.VMEM((MOE_TILE, d), F32)]),
        out_shape=jax.ShapeDtypeStruct((ns, d), BF16),
        compiler_params=_cparams("arbitrary", "arbitrary"),
        name="moe_experts",
    )(tile_e, tile_v, xs, w1, w3, w2)


def _combine_kernel(ws_ref, po_ref, ex_ref, nar_ref, info_ref, h_ref, fin_ref, yb_hbm, o_ref,
                    buf_ref, nbuf_ref, sem, *, n_exp, final_norm):
    i = pl.program_id(0)
    n_steps = pl.num_programs(0)
    n_set = COMBINE_SETS
    cur = lax.rem(i, n_set)

    def wide_copy(step, e, s):
        start = pl.multiple_of(ws_ref[step * n_exp + e], SLOT_ALIGN)
        return pltpu.make_async_copy(yb_hbm.at[pl.ds(start, SLOT_WIN), :], buf_ref.at[s, e], sem.at[s, e])

    def narrow_copy(step, e, s):
        start = pl.multiple_of(ws_ref[step * n_exp + e], SLOT_ALIGN)
        return pltpu.make_async_copy(yb_hbm.at[pl.ds(start, NARROW_WIN), :],
                                     nbuf_ref.at[s, e * NARROW_WIN:(e + 1) * NARROW_WIN, :], sem.at[s, e])

    def for_step(step, s, act):
        @pl.when(nar_ref[step] == 1)
        def _():
            for e in range(n_exp):
                act(narrow_copy(step, e, s))

        @pl.when(nar_ref[step] == 0)
        def _():
            for e in range(n_exp):
                act(wide_copy(step, e, s))

    @pl.when(i == 0)
    def _():
        for s in range(n_set - 1):
            @pl.when(s < n_steps)
            def _(s=s):
                for_step(s, s, lambda c: c.start())

    ahead = i + n_set - 1

    @pl.when(ahead < n_steps)
    def _():
        for_step(ahead, lax.rem(ahead, n_set), lambda c: c.start())

    info = info_ref[...]
    tm = info.shape[0]
    e1, e2 = info[:, 0:1], info[:, 1:2]
    g1, g2 = info[:, 2:3], info[:, 3:4]
    r1, r2 = info[:, 4:5], info[:, 5:6]

    def rank_offset(e, shift):
        return (ws_ref[i * n_exp + e] - po_ref[e] + shift).astype(F32)

    for_step(i, cur, lambda c: c.wait())

    @pl.when(nar_ref[i] == 1)
    def _():
        k1 = jnp.full_like(r1, -1.0)
        k2 = jnp.full_like(r2, -1.0)
        for e in range(n_exp):
            k1 = jnp.where(e1 == float(e), r1 - rank_offset(e, -e * NARROW_WIN), k1)
            k2 = jnp.where(e2 == float(e), r2 - rank_offset(e, -e * NARROW_WIN), k2)
        width = n_exp * NARROW_WIN
        lanef = lax.broadcasted_iota(I32, (tm, width), 1).astype(F32)
        pt = jnp.where(k1 == lanef, jnp.broadcast_to(g1, (tm, width)),
                       jnp.where(k2 == lanef, jnp.broadcast_to(g2, (tm, width)), 0.0)).astype(BF16)
        o_ref[...] = h_ref[...] + jnp.dot(pt, nbuf_ref[cur], preferred_element_type=F32)

    @pl.when(nar_ref[i] == 0)
    def _():
        lanef = lax.broadcasted_iota(I32, (tm, TOK_WIN), 1).astype(F32)
        g1b = jnp.broadcast_to(g1, (tm, TOK_WIN))
        g2b = jnp.broadcast_to(g2, (tm, TOK_WIN))

        def weights(e, shift, lo_lane):
            off = rank_offset(e, shift)
            k1 = jnp.where(e1 == float(e), r1 - off, -1.0)
            k2 = jnp.where(e2 == float(e), r2 - off, -1.0)
            k1 = jnp.where(k1 >= float(lo_lane), k1, -1.0)
            k2 = jnp.where(k2 >= float(lo_lane), k2, -1.0)
            return jnp.where(k1 == lanef, g1b, jnp.where(k2 == lanef, g2b, 0.0)).astype(BF16)

        acc = h_ref[...]
        for e in range(n_exp):
            acc = acc + jnp.dot(weights(e, 0, 0), buf_ref[cur, e, 0:TOK_WIN, :], preferred_element_type=F32)
        o_ref[...] = acc

        tail = SLOT_WIN - TOK_WIN
        for e in range(n_exp):
            @pl.when(ex_ref[i * n_exp + e] == 1)
            def _(e=e):
                pt = weights(e, tail, TOK_WIN - tail)
                o_ref[...] += jnp.dot(pt, buf_ref[cur, e, tail:SLOT_WIN, :], preferred_element_type=F32)

    if final_norm:
        o_ref[...] = _rms(o_ref[...], fin_ref[...])


def _combine(info, h, yb, win_start, pstart, extra, narrow, fin_g, final_norm):
    t, d = h.shape
    n_exp = pstart.shape[0]
    tm = min(TOK_WIN, t)
    return pl.pallas_call(
        functools.partial(_combine_kernel, n_exp=n_exp, final_norm=final_norm),
        grid_spec=pltpu.PrefetchScalarGridSpec(
            num_scalar_prefetch=4,
            grid=(t // tm,),
            in_specs=[pl.BlockSpec((tm, LANES), lambda i, *_: (i, 0)),
                      pl.BlockSpec((tm, d), lambda i, *_: (i, 0)),
                      pl.BlockSpec((1, d), lambda i, *_: (0, 0)),
                      pl.BlockSpec(memory_space=pl.ANY)],
            out_specs=pl.BlockSpec((tm, d), lambda i, *_: (i, 0)),
            scratch_shapes=[pltpu.VMEM((COMBINE_SETS, n_exp, SLOT_WIN, d), BF16),
                            pltpu.VMEM((COMBINE_SETS, n_exp * NARROW_WIN, d), BF16),
                            pltpu.SemaphoreType.DMA((COMBINE_SETS, n_exp))]),
        out_shape=jax.ShapeDtypeStruct((t, d), F32),
        compiler_params=_cparams("arbitrary"),
        name="moe_combine",
    )(win_start, pstart, extra, narrow, info, h, fin_g.astype(F32).reshape(1, d), yb)


def _count_le(sorted_vals, x):
    return jnp.sum((sorted_vals[None, :] <= x[:, None]).astype(I32), axis=1)


def _moe(h, g, router, w1, w3, w2, layer, fin_g, final_norm):
    t, d = h.shape
    n_exp = router.shape[1]
    tw = min(TOK_WIN, t)
    ntw = t // tw
    hn, info, info_t, before, total = _router(h, g, router)

    counts = total[0, :n_exp].astype(I32)
    padded = (counts + MOE_TILE - 1) // MOE_TILE * MOE_TILE
    pend = jnp.cumsum(padded)
    pstart = pend - padded
    n_tiles = (t * TOP_K) // MOE_TILE + n_exp
    n_slots = n_tiles * MOE_TILE
    tile0 = jnp.arange(n_tiles, dtype=I32) * MOE_TILE
    tile_e = jnp.minimum(_count_le(pend, tile0), n_exp - 1)
    half = MOE_TILE // 2
    tile_rows = jnp.clip(counts[tile_e] - (tile0 - pstart[tile_e]), 0, MOE_TILE)
    tile_v = jnp.where(tile0 < pend[-1], (tile_rows + half - 1) // half, 0).astype(I32)
    cum = jnp.concatenate([before[:, 0, :n_exp], total[0:1, :n_exp]], axis=0).astype(I32)

    n_gb = n_slots // GATHER_TILE
    gb0 = jnp.arange(n_gb, dtype=I32) * GATHER_TILE
    gb_e = jnp.minimum(_count_le(pend, gb0), n_exp - 1)
    r0 = gb0 - pstart[gb_e]
    cum_b = cum[:, gb_e]
    ilo = jnp.sum((cum_b[1:] <= r0[None, :]).astype(I32), axis=0)
    ihi = jnp.sum((cum_b[:-1] < (r0 + GATHER_TILE)[None, :]).astype(I32), axis=0) - 1
    n_work = jnp.where(gb0 < pend[-1], jnp.maximum(ihi - ilo + 1, 0), 0).astype(I32)
    ends = jnp.cumsum(n_work)
    offs = (ends - n_work).astype(I32)
    entry = jnp.arange(n_exp * ntw + n_gb, dtype=I32)
    entry_blk = jnp.minimum(_count_le(ends, entry), n_gb - 1)
    wlist = jnp.clip(ilo[entry_blk] + entry - offs[entry_blk], 0, ntw - 1).astype(I32)

    e_t = info_t[0:2].astype(I32)
    slot_t = jnp.sum(jnp.where(e_t[None] == jnp.arange(n_exp, dtype=I32)[:, None, None], pstart[:, None, None], 0),
                     axis=0) + info_t[4:6].astype(I32)
    dest = jnp.concatenate([slot_t, jnp.full((SUBLANES - TOP_K, t), -1, I32)], axis=0)

    xs = _gather(hn, dest, offs, n_work, wlist, ends[-1:].astype(I32), n_slots)
    yb = _experts(xs, tile_e, tile_v, w1, w3, w2, layer)

    lo = pstart[None, :] + cum[:-1]
    win_start = jnp.minimum(lo // SLOT_ALIGN * SLOT_ALIGN, n_slots - SLOT_WIN).astype(I32)
    reach = lo - win_start + cum[1:] - cum[:-1]
    extra = (reach > TOK_WIN).astype(I32)
    narrow = jnp.all(reach <= NARROW_WIN, axis=1).astype(I32)
    return _combine(info, h, yb, win_start.reshape(-1), pstart.astype(I32), extra.reshape(-1), narrow, fin_g,
                    final_norm)


def _final_norm_kernel(h_ref, g_ref, o_ref):
    o_ref[...] = _rms(h_ref[...], g_ref[...])


def _final_norm(h, g):
    t, d = h.shape
    tm = min(ROW_TILE, t)
    return pl.pallas_call(
        _final_norm_kernel,
        grid=(t // tm,),
        in_specs=[pl.BlockSpec((tm, d), lambda i: (i, 0)), pl.BlockSpec((1, d), lambda i: (0, 0))],
        out_specs=pl.BlockSpec((tm, d), lambda i: (i, 0)),
        out_shape=jax.ShapeDtypeStruct((t, d), F32),
        compiler_params=_cparams("parallel"),
        name="final_norm",
    )(h, g.astype(F32).reshape(1, d))


def kernel(x, attn_norm, ffn_norm, final_norm, w_in, w_out, s5_lambda_re, s5_lambda_im, s5_log_dt, s5_b_re, s5_b_im, s5_c_re, s5_c_im, s5_d, s5_glu, s5_out_norm, conv_w, conv_out_norm, hg_lower_bounds, hg_out_norm, ffn_w1, ffn_w3, ffn_w2, moe_router, moe_w1, moe_w3, moe_w2):
    bsz, seq, d = x.shape
    depth = w_in.shape[0]
    t = bsz * seq
    assert bsz == 1, "token mixers are written for a single sequence"
    lb_soft = jax.nn.softmax(hg_lower_bounds.astype(F32), axis=0)
    lb_all = jnp.cumsum(lb_soft, axis=0) - lb_soft[0]
    n_scan = int(math.log2(t // S5_CHUNK))
    s5_ops = jax.vmap(functools.partial(_s5_operators, n_scan=n_scan))(
        s5_lambda_re, s5_lambda_im, s5_log_dt, s5_b_re, s5_b_im, s5_c_re, s5_c_im)
    h = x.reshape(t, d).astype(F32)
    for l in range(depth):
        proj = _norm_inproj(h, attn_norm[l].astype(F32), w_in[l].astype(BF16))
        ys5 = _s5_conv(proj, [op[l] for op in s5_ops])
        h = _hgrn_mix(ys5, proj, h, lb_all[l], hg_out_norm[l], s5_d[l], s5_glu[l].astype(BF16), s5_out_norm[l],
                      conv_w[l], conv_out_norm[l], w_out[l].astype(BF16))
        j = l // 2
        if l % 2 == 0:
            h = _ffn(h, ffn_norm[l], ffn_w1, ffn_w3, ffn_w2, j)
        else:
            h = _moe(h, ffn_norm[l], moe_router[j], moe_w1, moe_w3, moe_w2, j, final_norm, l == depth - 1)
    if depth % 2 == 1:
        h = _final_norm(h, final_norm)
    return h.reshape(bsz, seq, d)
```

```python
import functools
import math

import numpy as np
import jax
import jax.numpy as jnp
from jax import lax
from jax.experimental import pallas as pl
from jax.experimental.pallas import tpu as pltpu

F32 = jnp.float32
BF16 = jnp.bfloat16
I32 = jnp.int32

NORM_EPS = 1e-6
LB_FLOOR = 1e-30
TOP_K = 2

S5_WIDTH = 256
S5_GROUP = 16
CONV_WIDTH = 256
CONV_K = 3
HG_WIDTH = 512
HG_HEAD_DIM = 128
HG_HEADS = HG_WIDTH // HG_HEAD_DIM
LANES = 128
SUBLANES = 8
COL_HG = S5_WIDTH + 3 * CONV_WIDTH

VMEM_LIMIT = 56 * 1024 * 1024
S5_CHUNK = 32
HG_CHUNK = 128
HG_LEVELS = int(math.log2(HG_CHUNK))
HG_TBLOCK = 512
ROW_TILE = 512
HG_RING = 3
FFN_ROW_TILE = 1024
FFN_COL_TILE = 512
SWIGLU_SUB = 256
MOE_TILE = 1024
MOE_COL_TILE = 512
GATHER_TILE = 256
GATHER_AHEAD = 6
GATHER_UNROLL = 4
GATHER_BUFS = GATHER_AHEAD + GATHER_UNROLL
TOK_WIN = 256
SLOT_ALIGN = 16
SLOT_WIN = TOK_WIN + SLOT_ALIGN
NARROW_WIN = 128
COMBINE_SETS = 3


def _cparams(*sem):
    return pltpu.CompilerParams(dimension_semantics=sem, vmem_limit_bytes=VMEM_LIMIT)


def _rms(x, g):
    ms = jnp.mean(x * x, axis=-1, keepdims=True)
    return x * lax.rsqrt(ms + NORM_EPS) * g


def _sigmoid(x):
    return 1.0 / (1.0 + jnp.exp(-x))


def _norm_inproj_kernel(h_ref, g_ref, w_ref, o_ref):
    xn = _rms(h_ref[...], g_ref[...]).astype(BF16)
    o_ref[...] = jnp.dot(xn, w_ref[...], preferred_element_type=F32)


def _norm_inproj(h, g, w):
    t, d = h.shape
    n = w.shape[1]
    tm = min(ROW_TILE, t)
    return pl.pallas_call(
        _norm_inproj_kernel,
        grid=(t // tm,),
        in_specs=[pl.BlockSpec((tm, d), lambda i: (i, 0)),
                  pl.BlockSpec((1, d), lambda i: (0, 0)),
                  pl.BlockSpec((d, n), lambda i: (0, 0))],
        out_specs=pl.BlockSpec((tm, n), lambda i: (i, 0)),
        out_shape=jax.ShapeDtypeStruct((t, n), F32),
        compiler_params=_cparams("parallel"),
        name="norm_inproj",
    )(h, g.reshape(1, d), w)


def _s5_operators(lam_re, lam_im, log_dt, b_re, b_im, c_re, c_im, n_scan):
    lc = S5_CHUNK
    hi = lax.Precision.HIGHEST
    lr, li = lam_re.astype(F32), lam_im.astype(F32)
    dt = jnp.exp(log_dt.astype(F32))[:, None]
    zr, zi = lr * dt, li * dt
    taus = jnp.arange(lc + 1, dtype=F32)[:, None, None]
    mag = jnp.exp(zr[None] * taus)
    pwr, pwi = mag * jnp.cos(zi[None] * taus), mag * jnp.sin(zi[None] * taus)
    nr, ni = pwr[1] - 1.0, pwi[1]
    den = lr * lr + li * li
    qr, qi = (nr * lr + ni * li) / den, (ni * lr - nr * li) / den
    br, bi = b_re.astype(F32), b_im.astype(F32)
    bbr = qr[..., None] * br - qi[..., None] * bi
    bbi = qr[..., None] * bi + qi[..., None] * br
    cr, ci = c_re.astype(F32), c_im.astype(F32)
    g_, p_ = lr.shape

    def c_times_pw(lo):
        wr, wi = pwr[lo:lo + lc, :, None, :], pwi[lo:lo + lc, :, None, :]
        return cr[None] * wr - ci[None] * wi, cr[None] * wi + ci[None] * wr

    cpr, cpi = c_times_pw(0)
    cp = jnp.concatenate([cpr, cpi], axis=-1).transpose(1, 0, 2, 3).reshape(g_, lc * S5_GROUP, 2 * p_)
    kt = jnp.matmul(cp, jnp.concatenate([bbr, -bbi], axis=1), precision=hi)
    kflat = kt.transpose(0, 2, 1)

    wr, wi = pwr[lc - 1::-1][:, :, :, None], pwi[lc - 1::-1][:, :, :, None]
    msr = (wr * bbr[None] - wi * bbi[None]).transpose(1, 0, 3, 2).reshape(g_, lc * S5_GROUP, p_)
    msi = (wr * bbi[None] + wi * bbr[None]).transpose(1, 0, 3, 2).reshape(g_, lc * S5_GROUP, p_)
    m_state = jnp.concatenate([msr, msi], axis=-1)

    c1r, c1i = c_times_pw(1)
    c1r = c1r.transpose(1, 3, 0, 2).reshape(g_, p_, lc * S5_GROUP)
    c1i = c1i.transpose(1, 3, 0, 2).reshape(g_, p_, lc * S5_GROUP)
    m_carry = jnp.concatenate([c1r, -c1i], axis=1)

    akr, aki = [pwr[lc]], [pwi[lc]]
    for _ in range(n_scan - 1):
        r, i = akr[-1], aki[-1]
        akr.append(r * r - i * i)
        aki.append(2.0 * r * i)
    akr, aki = jnp.stack(akr, axis=1), jnp.stack(aki, axis=1)
    ar = jnp.concatenate([akr, akr], axis=-1)
    ai = jnp.concatenate([-aki, aki], axis=-1)
    kpad = -(-n_scan // 8) * 8
    ar = jnp.pad(ar, ((0, 0), (0, kpad - n_scan), (0, 0)))
    ai = jnp.pad(ai, ((0, 0), (0, kpad - n_scan), (0, 0)))
    return kflat, m_state.astype(BF16), m_carry.astype(BF16), ar, ai


S5_PER_TILE = LANES // S5_GROUP


def _s5_kernel(proj_hbm, kf_ref, ms_ref, mc_ref, ar_ref, ai_ref, y_hbm, us_ref, ys_ref, mi_ref, sem_in, sem_out,
               *, n_scan):
    lc, gw, per = S5_CHUNK, S5_GROUP, S5_PER_TILE
    nch = us_ref.shape[1]
    cols = pl.ds(pl.multiple_of(pl.program_id(0) * LANES, LANES), LANES)

    def in_copy(s):
        return pltpu.make_async_copy(proj_hbm.at[:, s, cols], us_ref.at[s], sem_in.at[s])

    def out_copy(s):
        return pltpu.make_async_copy(ys_ref.at[s], y_hbm.at[:, s, cols], sem_out.at[s])

    for s in range(lc):
        in_copy(s).start()
    ys_ref[...] = jnp.zeros_like(ys_ref)
    for s in range(lc):
        in_copy(s).wait()
        if s % per:
            us_ref[s] = pltpu.roll(us_ref[s], (s % per) * gw, axis=1)

    lane_grp = lax.broadcasted_iota(I32, (1, LANES), 1) // gw
    kf_lane = lax.broadcasted_iota(I32, (gw, lc * gw), 1)

    def group(gl, carry):
        rel_grp = jnp.bitwise_and(lane_grp - gl, per - 1)
        tiles = []
        for j in range(lc // per):
            merged = us_ref[j * per]
            for k in range(1, per):
                merged = jnp.where(rel_grp == k, us_ref[j * per + k], merged)
            tiles.append(pltpu.roll(merged, jnp.bitwise_and(-gl * gw, LANES - 1), axis=1))
        u = jnp.concatenate(tiles, axis=1).astype(BF16)

        kf = kf_ref[gl]
        for s in range(lc):
            blk = kf if s == 0 else jnp.where(kf_lane >= s * gw, pltpu.roll(kf, s * gw, axis=1), 0.0)
            mi_ref[s * gw:(s + 1) * gw, :] = blk.astype(BF16)

        x = jnp.dot(u, ms_ref[gl], preferred_element_type=F32)
        row = lax.broadcasted_iota(I32, x.shape, 0)
        half = x.shape[1] // 2
        for k in range(n_scan):
            d = 1 << k
            s = jnp.where(row >= d, pltpu.roll(x, d, axis=0), 0.0)
            x = x + ar_ref[gl, k:k + 1, :] * s + ai_ref[gl, k:k + 1, :] * pltpu.roll(s, half, axis=1)
        xe = jnp.where(row >= 1, pltpu.roll(x, 1, axis=0), 0.0)
        y = jnp.dot(u, mi_ref[...], preferred_element_type=F32)
        y = y + jnp.dot(xe.astype(BF16), mc_ref[gl], preferred_element_type=F32)

        for j in range(lc // per):
            moved = pltpu.roll(y[:, j * LANES:(j + 1) * LANES], jnp.bitwise_and(gl * gw, LANES - 1), axis=1)
            for k in range(per):
                ys_ref[j * per + k] = jnp.where(rel_grp == k, moved, ys_ref[j * per + k])
        return carry

    lax.fori_loop(0, per, group, 0)
    for s in range(lc):
        if s % per:
            ys_ref[s] = pltpu.roll(ys_ref[s], LANES - (s % per) * gw, axis=1)
        out_copy(s).start()
    for s in range(lc):
        out_copy(s).wait()


def _s5_conv(proj, ops):
    kflat, m_state, m_carry, ar, ai = ops
    t, n = proj.shape
    lc = S5_CHUNK
    nch = t // lc
    n_scan = int(math.log2(nch))
    assert (1 << n_scan) == nch
    w = lc * S5_GROUP
    p2 = m_state.shape[-1]
    per = S5_PER_TILE

    def grp(shape):
        return pl.BlockSpec((per,) + shape, lambda hh: (hh, 0, 0))

    hbm = pl.BlockSpec(memory_space=pl.ANY)
    y = pl.pallas_call(
        functools.partial(_s5_kernel, n_scan=n_scan),
        grid=(S5_WIDTH // LANES,),
        in_specs=[hbm, grp((S5_GROUP, w)), grp((w, p2)), grp((p2, w)),
                  grp((ar.shape[1], p2)), grp((ai.shape[1], p2))],
        out_specs=hbm,
        out_shape=jax.ShapeDtypeStruct((nch, lc, S5_WIDTH), F32),
        scratch_shapes=[pltpu.VMEM((lc, nch, LANES), F32), pltpu.VMEM((lc, nch, LANES), F32),
                        pltpu.VMEM((w, w), BF16), pltpu.SemaphoreType.DMA((lc,)), pltpu.SemaphoreType.DMA((lc,))],
        compiler_params=_cparams("arbitrary"),
        name="s5_conv",
    )(proj.reshape(nch, lc, n), kflat, m_state, m_carry, ar, ai)
    return y.reshape(t, S5_WIDTH)


LOG2E = float(np.log2(np.e))


def _hg_level_table():
    idx = np.arange(HG_CHUNK)
    t, s = idx[:, None], idx[None, :]
    top_bit = np.floor(np.log2(np.maximum(t ^ s, 1))).astype(np.int32)
    return np.where(s < t, HG_LEVELS - 1 - top_bit, -1).astype(np.int32)


def _hg_midpoint(bc, m):
    c = bc.shape[0]
    h = m // 2
    if h >= SUBLANES:
        return jnp.concatenate([jnp.broadcast_to(bc[j * m + h - 1:j * m + h, :], (m, LANES))
                                for j in range(c // m)], axis=0)
    x3 = bc.reshape(c // SUBLANES, SUBLANES, LANES)
    sub = lax.broadcasted_iota(I32, x3.shape, 1)
    beta = None
    for j in reversed(range(SUBLANES // m)):
        row = jnp.broadcast_to(x3[:, j * m + h - 1:j * m + h, :], x3.shape)
        beta = row if beta is None else jnp.where(sub < (j + 1) * m, row, beta)
    return beta.reshape(c, LANES)


def _bf16_terms(x, n):
    terms = []
    for _ in range(n - 1):
        t = x.astype(BF16)
        terms.append(t)
        x = x - t.astype(F32)
    return terms + [x.astype(BF16)]


def _hg_chunk(fp, q, v, gt, lb, lbf, ng, lvl, st):
    c = HG_CHUNK
    prow = lax.broadcasted_iota(I32, (c, LANES), 0)
    nt = (((1,), (1,)), ((), ()))
    tn = (((0,), (0,)), ((), ()))
    en = jnp.exp(-jnp.abs(fp))
    rc = 1.0 / (1.0 + en)
    pos_f = fp >= 0.0
    sig_p = jnp.where(pos_f, rc, en * rc)
    sig_n = jnp.where(pos_f, en * rc, rc)
    f = lbf + (1.0 - lb) * sig_p
    lf = jnp.log(f)
    kc = (1.0 - lb) * sig_n
    qc = q * _sigmoid(q)
    bc = lf
    for k in range(HG_LEVELS):
        d = 1 << k
        bc = bc + jnp.where(prow >= d, pltpu.roll(bc, d, axis=0), 0.0)

    a = jnp.zeros((c, c), F32)
    for lev in range(HG_LEVELS):
        m = c >> lev
        upper = jnp.bitwise_and(prow, m - 1) >= m // 2
        z = jnp.where(upper, qc, kc)
        if m == 2:
            zw = jnp.where(upper, z * f, z)
        else:
            dlt = bc - _hg_midpoint(bc, m)
            zw = z * jnp.exp2(dlt * jnp.where(upper, LOG2E, -LOG2E))
        zw = zw.astype(BF16)
        s = lax.dot_general(zw, zw, nt, preferred_element_type=F32)
        a = jnp.where(lvl == lev, s, a)
    vb = v.astype(BF16)
    o = jnp.dot(a.astype(BF16), vb, preferred_element_type=F32)
    o = o + jnp.sum(qc * kc, axis=-1, keepdims=True) * v
    o = o + lax.dot_general((qc * jnp.exp(bc)).astype(BF16), st.astype(BF16), nt,
                            preferred_element_type=F32)
    bl = bc[c - 1:c, :]
    khat = (kc * jnp.exp(bl - bc)).astype(BF16)
    st = st * jnp.exp(bl) + lax.dot_general(vb, khat, tn, preferred_element_type=F32)
    return _rms(o, ng) * (gt * _sigmoid(gt)), st


def _hgrn_mix_kernel(proj_hbm, lb_ref, lbf_ref, ng_ref, lvl_ref,
                     ys_ref, u_ref, cb_ref, cc_ref, cv_ref, h_ref, d_ref, glu_ref, sn_ref, cw_ref, cn_ref, wo_ref,
                     o_ref, st_ref, hg_ref, carry_ref, ring_ref, sem):
    c = HG_CHUNK
    i = pl.program_id(0)
    n_steps = pl.num_programs(0)
    tb = ring_ref.shape[1]

    def copy(step):
        slot = lax.rem(step, HG_RING)
        src = proj_hbm.at[pl.ds(pl.multiple_of(step * tb, tb), tb), pl.ds(COL_HG, 4 * HG_WIDTH)]
        return pltpu.make_async_copy(src, ring_ref.at[slot], sem.at[slot])

    @pl.when(i == 0)
    def _():
        st_ref[...] = jnp.zeros_like(st_ref)
        carry_ref[...] = jnp.zeros_like(carry_ref)
        for s in range(HG_RING - 1):
            @pl.when(s < n_steps)
            def _(s=s):
                copy(s).start()

    @pl.when(i + HG_RING - 1 < n_steps)
    def _():
        copy(i + HG_RING - 1).start()

    copy(i).wait()
    cur = lax.rem(i, HG_RING)

    def chunk(n, carry):
        rows = pl.ds(pl.multiple_of(n * c, c), c)
        for hd in range(HG_HEADS):
            cols = slice(hd * LANES, (hd + 1) * LANES)
            q, fp, v, gt = (ring_ref[cur, rows, k * HG_WIDTH + hd * LANES:k * HG_WIDTH + (hd + 1) * LANES]
                            for k in range(4))
            o, st = _hg_chunk(fp, q, v, gt, lb_ref[:, cols], lbf_ref[:, cols], ng_ref[:, cols], lvl_ref[...],
                              st_ref[hd])
            hg_ref[rows, cols] = o.astype(BF16)
            st_ref[hd] = st
        return carry

    lax.fori_loop(0, tb // c, chunk, 0)
    _mix_epilogue(ys_ref, u_ref, cb_ref, cc_ref, cv_ref, hg_ref, h_ref, d_ref, glu_ref, sn_ref, cw_ref, cn_ref,
                  wo_ref, o_ref, carry_ref)


def _mix_epilogue(ys_ref, u_ref, cb_ref, cc_ref, cv_ref, hg_ref, h_ref,
                  d_ref, glu_ref, sn_ref, cw_ref, cn_ref, wo_ref, o_ref, carry_ref):
    y = ys_ref[...] + d_ref[...] * u_ref[...]
    y = jax.nn.gelu(y)
    y = y * _sigmoid(jnp.dot(y.astype(BF16), glu_ref[...], preferred_element_type=F32))
    y_s5 = _rms(y, sn_ref[...])

    z = cc_ref[...] * cv_ref[...]
    tm = z.shape[0]
    row = lax.broadcasted_iota(I32, z.shape, 0)
    p1 = carry_ref[7:8, :]
    p2 = carry_ref[6:7, :]
    z1 = jnp.where(row == 0, p1, pltpu.roll(z, 1, axis=0))
    z2 = jnp.where(row == 0, p2, jnp.where(row == 1, p1, pltpu.roll(z, 2, axis=0)))
    carry_ref[...] = z[tm - 8:tm, :]
    yc = cb_ref[...] * (z2 * cw_ref[0:1, :] + z1 * cw_ref[1:2, :] + z * cw_ref[2:3, :])
    y_cv = _rms(yc, cn_ref[...])

    acc = h_ref[...]
    hg_row = S5_WIDTH + CONV_WIDTH
    acc = acc + jnp.dot(y_s5.astype(BF16), wo_ref[0:S5_WIDTH, :], preferred_element_type=F32)
    acc = acc + jnp.dot(y_cv.astype(BF16), wo_ref[S5_WIDTH:hg_row, :], preferred_element_type=F32)
    acc = acc + jnp.dot(hg_ref[...], wo_ref[hg_row:, :], preferred_element_type=F32)
    o_ref[...] = acc


def _hgrn_mix(ys5, proj, h, lb, hg_norm, s5_d, s5_glu, s5_norm, conv_w, conv_norm, w_out):
    t, d = h.shape
    tb = min(HG_TBLOCK, t)
    lbh = jnp.clip(lb.astype(F32), 0.0, 1.0 - 1e-6).reshape(1, HG_WIDTH)
    lbf = jnp.maximum(lbh, LB_FLOOR)
    ng = hg_norm.astype(F32).reshape(1, HG_WIDTH)
    lvl = jnp.asarray(_hg_level_table())
    cw = jnp.pad(conv_w.astype(F32), ((0, SUBLANES - CONV_K), (0, 0)))
    wq = S5_WIDTH
    hg_col = COL_HG // HG_WIDTH

    def rowblk(width, colblk):
        return pl.BlockSpec((tb, width), lambda i, c=colblk: (i, c))

    def full(shape):
        return pl.BlockSpec(shape, lambda i: (0,) * len(shape))

    return pl.pallas_call(
        _hgrn_mix_kernel,
        grid=(t // tb,),
        in_specs=[pl.BlockSpec(memory_space=pl.ANY), full((1, HG_WIDTH)), full((1, HG_WIDTH)), full((1, HG_WIDTH)),
                  full(lvl.shape),
                  rowblk(wq, 0), rowblk(wq, 0), rowblk(wq, 1), rowblk(wq, 2), rowblk(wq, 3), rowblk(d, 0),
                  full((1, wq)), full((wq, wq)), full((1, wq)), full((SUBLANES, wq)), full((1, wq)),
                  full(w_out.shape)],
        out_specs=rowblk(d, 0),
        out_shape=jax.ShapeDtypeStruct((t, d), F32),
        scratch_shapes=[pltpu.VMEM((HG_HEADS, LANES, LANES), F32), pltpu.VMEM((tb, HG_WIDTH), BF16),
                        pltpu.VMEM((SUBLANES, CONV_WIDTH), F32),
                        pltpu.VMEM((HG_RING, tb, 4 * HG_WIDTH), F32), pltpu.SemaphoreType.DMA((HG_RING,))],
        compiler_params=_cparams("arbitrary"),
        name="hgrn2_mix_out",
    )(proj, lbh, lbf, ng, lvl,
      ys5, proj, proj, proj, proj, h,
      s5_d.astype(F32).reshape(1, wq), s5_glu, s5_norm.astype(F32).reshape(1, wq), cw,
      conv_norm.astype(F32).reshape(1, wq), w_out)


def _swiglu_slice(x, w1_ref, w3_ref, w2_ref):
    out = None
    for c0 in range(0, w1_ref.shape[-1], SWIGLU_SUB):
        cs = slice(c0, c0 + SWIGLU_SUB)
        a = jnp.dot(x, w1_ref[:, cs].astype(BF16), preferred_element_type=F32)
        b = jnp.dot(x, w3_ref[:, cs].astype(BF16), preferred_element_type=F32)
        gact = (a * _sigmoid(a) * b).astype(BF16)
        part = jnp.dot(gact, w2_ref[cs, :].astype(BF16), preferred_element_type=F32)
        out = part if out is None else out + part
    return out


def _ffn_kernel(h_ref, g_ref, w1_ref, w3_ref, w2_ref, o_ref, hn_ref, acc_ref):
    j = pl.program_id(1)

    @pl.when(j == 0)
    def _():
        x = h_ref[...]
        hn_ref[...] = _rms(x, g_ref[...]).astype(BF16)
        acc_ref[...] = x

    acc_ref[...] += _swiglu_slice(hn_ref[...], w1_ref, w3_ref, w2_ref)

    @pl.when(j == pl.num_programs(1) - 1)
    def _():
        o_ref[...] = acc_ref[...]


def _ffn(h, g, w1, w3, w2, layer):
    t, d = h.shape
    f = w1.shape[2]
    tm = min(FFN_ROW_TILE, t)
    tf = FFN_COL_TILE
    return pl.pallas_call(
        _ffn_kernel,
        grid=(t // tm, f // tf),
        in_specs=[pl.BlockSpec((tm, d), lambda i, j: (i, 0)),
                  pl.BlockSpec((1, d), lambda i, j: (0, 0)),
                  pl.BlockSpec((None, d, tf), lambda i, j: (layer, 0, j)),
                  pl.BlockSpec((None, d, tf), lambda i, j: (layer, 0, j)),
                  pl.BlockSpec((None, tf, d), lambda i, j: (layer, j, 0))],
        out_specs=pl.BlockSpec((tm, d), lambda i, j: (i, 0)),
        out_shape=jax.ShapeDtypeStruct((t, d), F32),
        scratch_shapes=[pltpu.VMEM((tm, d), BF16), pltpu.VMEM((tm, d), F32)],
        compiler_params=_cparams("parallel", "arbitrary"),
        name="ffn_swiglu",
    )(h, g.astype(F32).reshape(1, d), w1, w3, w2)


def _router_kernel(h_ref, g_ref, r_ref, hn_ref, info_ref, info_t_ref, before_ref, total_ref, cnt_ref, *, n_exp):
    @pl.when(pl.program_id(0) == 0)
    def _():
        cnt_ref[...] = jnp.zeros_like(cnt_ref)

    xn = _rms(h_ref[...], g_ref[...])
    x_hi, x_lo = _bf16_terms(xn, 2)
    hn_ref[...] = x_hi
    r_hi, r_lo = _bf16_terms(r_ref[...], 2)
    logits = (jnp.dot(x_hi, r_hi, preferred_element_type=F32) + jnp.dot(x_lo, r_hi, preferred_element_type=F32)
              + jnp.dot(x_hi, r_lo, preferred_element_type=F32))
    tm = logits.shape[0]
    lane = lax.broadcasted_iota(I32, logits.shape, 1)
    lanef = lane.astype(F32)
    neg = jnp.float32(-jnp.inf)
    lg = jnp.where(lane < n_exp, logits, neg)
    m1 = jnp.max(lg, axis=-1, keepdims=True)
    i1 = jnp.min(jnp.where(lg == m1, lanef, float(LANES)), axis=-1, keepdims=True)
    oh1 = lanef == i1
    lg2 = jnp.where(oh1, neg, lg)
    m2 = jnp.max(lg2, axis=-1, keepdims=True)
    i2 = jnp.min(jnp.where(lg2 == m2, lanef, float(LANES)), axis=-1, keepdims=True)
    oh2 = lanef == i2
    ex = jnp.exp(m2 - m1)
    g1 = 1.0 / (1.0 + ex)
    g2 = ex * g1
    chosen = jnp.where(oh1, 1.0, jnp.where(oh2, 1.0, 0.0))
    ri = lax.broadcasted_iota(I32, (tm, tm), 0)
    ci = lax.broadcasted_iota(I32, (tm, tm), 1)
    tri = jnp.where(ri > ci, 1.0, 0.0).astype(BF16)
    before = cnt_ref[...]
    cexcl = jnp.dot(tri, chosen.astype(BF16), preferred_element_type=F32) + before
    rank1 = jnp.sum(jnp.where(oh1, cexcl, 0.0), axis=-1, keepdims=True)
    rank2 = jnp.sum(jnp.where(oh2, cexcl, 0.0), axis=-1, keepdims=True)
    info = jnp.where(lane == 0, i1, jnp.where(lane == 1, i2, jnp.where(lane == 2, g1, jnp.where(
        lane == 3, g2, jnp.where(lane == 4, rank1, jnp.where(lane == 5, rank2, 0.0))))))
    info_ref[...] = info
    info_t_ref[...] = info.T[0:SUBLANES, :]
    before_ref[0] = jnp.broadcast_to(before, before_ref.shape[1:])
    total = before + jnp.sum(chosen, axis=0, keepdims=True)
    cnt_ref[...] = total
    total_ref[...] = jnp.broadcast_to(total, total_ref.shape)


def _router(h, g, router):
    t, d = h.shape
    n_exp = router.shape[1]
    tm = min(TOK_WIN, t)
    ntw = t // tm
    rp = jnp.pad(router.astype(F32), ((0, 0), (0, LANES - n_exp)))
    return pl.pallas_call(
        functools.partial(_router_kernel, n_exp=n_exp),
        grid=(ntw,),
        in_specs=[pl.BlockSpec((tm, d), lambda i: (i, 0)),
                  pl.BlockSpec((1, d), lambda i: (0, 0)),
                  pl.BlockSpec((d, LANES), lambda i: (0, 0))],
        out_specs=[pl.BlockSpec((tm, d), lambda i: (i, 0)),
                   pl.BlockSpec((tm, LANES), lambda i: (i, 0)),
                   pl.BlockSpec((SUBLANES, tm), lambda i: (0, i)),
                   pl.BlockSpec((1, 8, LANES), lambda i: (i, 0, 0)),
                   pl.BlockSpec((8, LANES), lambda i: (0, 0))],
        out_shape=[jax.ShapeDtypeStruct((t, d), BF16),
                   jax.ShapeDtypeStruct((t, LANES), F32),
                   jax.ShapeDtypeStruct((SUBLANES, t), F32),
                   jax.ShapeDtypeStruct((ntw, 8, LANES), F32),
                   jax.ShapeDtypeStruct((8, LANES), F32)],
        scratch_shapes=[pltpu.VMEM((1, LANES), F32)],
        compiler_params=_cparams("arbitrary"),
        name="moe_router",
    )(h, g.astype(F32).reshape(1, d), rp)


def _gather_kernel(off_ref, nw_ref, wl_ref, tot_ref, dest_ref, hn_hbm, xs_ref, buf_ref, sem, acc_ref):
    b = pl.program_id(0)
    rows = acc_ref.shape[0]
    n_buf, win = buf_ref.shape[0], buf_ref.shape[1]
    n = nw_ref[b]
    q0 = off_ref[b]
    total = tot_ref[0]

    def copy(q):
        w = wl_ref[q]
        s = lax.rem(q, n_buf)
        return pltpu.make_async_copy(hn_hbm.at[pl.ds(pl.multiple_of(w * win, win), win), :], buf_ref.at[s],
                                     sem.at[s])

    @pl.when(b == 0)
    def _():
        for q in range(GATHER_AHEAD):
            @pl.when(q < total)
            def _(q=q):
                copy(q).start()

    acc_ref[...] = jnp.zeros_like(acc_ref)
    slot = b * rows + lax.broadcasted_iota(I32, (rows, win), 0)

    def gathered(q):
        @pl.when(q + GATHER_AHEAD < total)
        def _():
            copy(q + GATHER_AHEAD).start()

        col = pl.ds(pl.multiple_of(wl_ref[q] * win, win), win)
        d1 = dest_ref[0:1, col]
        d2 = dest_ref[1:2, col]
        hit = jnp.where(d1 == slot, 1.0, jnp.where(d2 == slot, 1.0, 0.0)).astype(BF16)
        copy(q).wait()
        return jnp.dot(hit, buf_ref[lax.rem(q, n_buf)], preferred_element_type=F32)

    def several(j, carry):
        q = q0 + GATHER_UNROLL * j
        acc_ref[...] += sum(gathered(q + r) for r in range(GATHER_UNROLL))
        return carry

    n_full = n // GATHER_UNROLL
    lax.fori_loop(0, n_full, several, 0)

    def single(j, carry):
        acc_ref[...] += gathered(q0 + n_full * GATHER_UNROLL + j)
        return carry

    lax.fori_loop(0, n - n_full * GATHER_UNROLL, single, 0)
    xs_ref[...] = acc_ref[...].astype(BF16)


def _gather(hn, dest, off, nw, wlist, total, n_slots):
    t, d = hn.shape
    return pl.pallas_call(
        _gather_kernel,
        grid_spec=pltpu.PrefetchScalarGridSpec(
            num_scalar_prefetch=4,
            grid=(n_slots // GATHER_TILE,),
            in_specs=[pl.BlockSpec(dest.shape, lambda b, *_: (0, 0)),
                      pl.BlockSpec(memory_space=pl.ANY)],
            out_specs=pl.BlockSpec((GATHER_TILE, d), lambda b, *_: (b, 0)),
            scratch_shapes=[pltpu.VMEM((GATHER_BUFS, TOK_WIN, d), BF16),
                            pltpu.SemaphoreType.DMA((GATHER_BUFS,)),
                            pltpu.VMEM((GATHER_TILE, d), F32)]),
        out_shape=jax.ShapeDtypeStruct((n_slots, d), BF16),
        compiler_params=_cparams("arbitrary"),
        name="moe_gather",
    )(off, nw, wlist, total, dest, hn)


def _expert_kernel(te_ref, tv_ref, x_ref, w1_ref, w3_ref, w2_ref, y_ref, acc_ref):
    b = pl.program_id(0)
    j = pl.program_id(1)

    @pl.when(j == 0)
    def _():
        acc_ref[...] = jnp.zeros_like(acc_ref)

    fill = tv_ref[b]
    half = x_ref.shape[0] // 2

    @pl.when(fill == 2)
    def _():
        acc_ref[...] += _swiglu_slice(x_ref[...], w1_ref, w3_ref, w2_ref)

    @pl.when(fill == 1)
    def _():
        acc_ref[0:half, :] += _swiglu_slice(x_ref[0:half, :], w1_ref, w3_ref, w2_ref)

    @pl.when(j == pl.num_programs(1) - 1)
    def _():
        y_ref[...] = acc_ref[...].astype(BF16)


def _experts(xs, tile_e, tile_v, w1, w3, w2, layer):
    ns, d = xs.shape
    f = w1.shape[3]
    tf = MOE_COL_TILE
    nj = f // tf
    n_tiles = ns // MOE_TILE

    def col(b, j, tv):
        used = jnp.minimum(tv[b], 1)
        return j * used + (nj - 1) * (1 - used)

    return pl.pallas_call(
        _expert_kernel,
        grid_spec=pltpu.PrefetchScalarGridSpec(
            num_scalar_prefetch=2,
            grid=(n_tiles, nj),
            in_specs=[pl.BlockSpec((MOE_TILE, d), lambda b, j, te, tv: (b, 0)),
                      pl.BlockSpec((None, None, d, tf), lambda b, j, te, tv: (layer, te[b], 0, col(b, j, tv))),
                      pl.BlockSpec((None, None, d, tf), lambda b, j, te, tv: (layer, te[b], 0, col(b, j, tv))),
                      pl.BlockSpec((None, None, tf, d), lambda b, j, te, tv: (layer, te[b], col(b, j, tv), 0))],
            out_specs=pl.BlockSpec((MOE_TILE, d), lambda b, j, te, tv: (b, 0)),
            scratch_shapes=[pltpu.VMEM((MOE_TILE, d), F32)]),
        out_shape=jax.ShapeDtypeStruct((ns, d), BF16),
        compiler_params=_cparams("arbitrary", "arbitrary"),
        name="moe_experts",
    )(tile_e, tile_v, xs, w1, w3, w2)


def _combine_kernel(ws_ref, po_ref, ex_ref, nar_ref, info_ref, h_ref, fin_ref, yb_hbm, o_ref,
                    buf_ref, nbuf_ref, sem, *, n_exp, final_norm):
    i = pl.program_id(0)
    n_steps = pl.num_programs(0)
    n_set = COMBINE_SETS
    cur = lax.rem(i, n_set)

    def wide_copy(step, e, s):
        start = pl.multiple_of(ws_ref[step * n_exp + e], SLOT_ALIGN)
        return pltpu.make_async_copy(yb_hbm.at[pl.ds(start, SLOT_WIN), :], buf_ref.at[s, e], sem.at[s, e])

    def narrow_copy(step, e, s):
        start = pl.multiple_of(ws_ref[step * n_exp + e], SLOT_ALIGN)
        return pltpu.make_async_copy(yb_hbm.at[pl.ds(start, NARROW_WIN), :],
                                     nbuf_ref.at[s, e * NARROW_WIN:(e + 1) * NARROW_WIN, :], sem.at[s, e])

    def for_step(step, s, act):
        @pl.when(nar_ref[step] == 1)
        def _():
            for e in range(n_exp):
                act(narrow_copy(step, e, s))

        @pl.when(nar_ref[step] == 0)
        def _():
            for e in range(n_exp):
                act(wide_copy(step, e, s))

    @pl.when(i == 0)
    def _():
        for s in range(n_set - 1):
            @pl.when(s < n_steps)
            def _(s=s):
                for_step(s, s, lambda c: c.start())

    ahead = i + n_set - 1

    @pl.when(ahead < n_steps)
    def _():
        for_step(ahead, lax.rem(ahead, n_set), lambda c: c.start())

    info = info_ref[...]
    tm = info.shape[0]
    e1, e2 = info[:, 0:1], info[:, 1:2]
    g1, g2 = info[:, 2:3], info[:, 3:4]
    r1, r2 = info[:, 4:5], info[:, 5:6]

    def rank_offset(e, shift):
        return (ws_ref[i * n_exp + e] - po_ref[e] + shift).astype(F32)

    for_step(i, cur, lambda c: c.wait())

    @pl.when(nar_ref[i] == 1)
    def _():
        k1 = jnp.full_like(r1, -1.0)
        k2 = jnp.full_like(r2, -1.0)
        for e in range(n_exp):
            k1 = jnp.where(e1 == float(e), r1 - rank_offset(e, -e * NARROW_WIN), k1)
            k2 = jnp.where(e2 == float(e), r2 - rank_offset(e, -e * NARROW_WIN), k2)
        width = n_exp * NARROW_WIN
        lanef = lax.broadcasted_iota(I32, (tm, width), 1).astype(F32)
        pt = jnp.where(k1 == lanef, jnp.broadcast_to(g1, (tm, width)),
                       jnp.where(k2 == lanef, jnp.broadcast_to(g2, (tm, width)), 0.0)).astype(BF16)
        o_ref[...] = h_ref[...] + jnp.dot(pt, nbuf_ref[cur], preferred_element_type=F32)

    @pl.when(nar_ref[i] == 0)
    def _():
        lanef = lax.broadcasted_iota(I32, (tm, TOK_WIN), 1).astype(F32)
        g1b = jnp.broadcast_to(g1, (tm, TOK_WIN))
        g2b = jnp.broadcast_to(g2, (tm, TOK_WIN))

        def weights(e, shift, lo_lane):
            off = rank_offset(e, shift)
            k1 = jnp.where(e1 == float(e), r1 - off, -1.0)
            k2 = jnp.where(e2 == float(e), r2 - off, -1.0)
            k1 = jnp.where(k1 >= float(lo_lane), k1, -1.0)
            k2 = jnp.where(k2 >= float(lo_lane), k2, -1.0)
            return jnp.where(k1 == lanef, g1b, jnp.where(k2 == lanef, g2b, 0.0)).astype(BF16)

        acc = h_ref[...]
        for e in range(n_exp):
            acc = acc + jnp.dot(weights(e, 0, 0), buf_ref[cur, e, 0:TOK_WIN, :], preferred_element_type=F32)
        o_ref[...] = acc

        tail = SLOT_WIN - TOK_WIN
        for e in range(n_exp):
            @pl.when(ex_ref[i * n_exp + e] == 1)
            def _(e=e):
                pt = weights(e, tail, TOK_WIN - tail)
                o_ref[...] += jnp.dot(pt, buf_ref[cur, e, tail:SLOT_WIN, :], preferred_element_type=F32)

    if final_norm:
        o_ref[...] = _rms(o_ref[...], fin_ref[...])


def _combine(info, h, yb, win_start, pstart, extra, narrow, fin_g, final_norm):
    t, d = h.shape
    n_exp = pstart.shape[0]
    tm = min(TOK_WIN, t)
    return pl.pallas_call(
        functools.partial(_combine_kernel, n_exp=n_exp, final_norm=final_norm),
        grid_spec=pltpu.PrefetchScalarGridSpec(
            num_scalar_prefetch=4,
            grid=(t // tm,),
            in_specs=[pl.BlockSpec((tm, LANES), lambda i, *_: (i, 0)),
                      pl.BlockSpec((tm, d), lambda i, *_: (i, 0)),
                      pl.BlockSpec((1, d), lambda i, *_: (0, 0)),
                      pl.BlockSpec(memory_space=pl.ANY)],
            out_specs=pl.BlockSpec((tm, d), lambda i, *_: (i, 0)),
            scratch_shapes=[pltpu.VMEM((COMBINE_SETS, n_exp, SLOT_WIN, d), BF16),
                            pltpu.VMEM((COMBINE_SETS, n_exp * NARROW_WIN, d), BF16),
                            pltpu.SemaphoreType.DMA((COMBINE_SETS, n_exp))]),
        out_shape=jax.ShapeDtypeStruct((t, d), F32),
        compiler_params=_cparams("arbitrary"),
        name="moe_combine",
    )(win_start, pstart, extra, narrow, info, h, fin_g.astype(F32).reshape(1, d), yb)


def _count_le(sorted_vals, x):
    return jnp.sum((sorted_vals[None, :] <= x[:, None]).astype(I32), axis=1)


def _moe(h, g, router, w1, w3, w2, layer, fin_g, final_norm):
    t, d = h.shape
    n_exp = router.shape[1]
    tw = min(TOK_WIN, t)
    ntw = t // tw
    hn, info, info_t, before, total = _router(h, g, router)

    counts = total[0, :n_exp].astype(I32)
    padded = (counts + MOE_TILE - 1) // MOE_TILE * MOE_TILE
    pend = jnp.cumsum(padded)
    pstart = pend - padded
    n_tiles = (t * TOP_K) // MOE_TILE + n_exp
    n_slots = n_tiles * MOE_TILE
    tile0 = jnp.arange(n_tiles, dtype=I32) * MOE_TILE
    tile_e = jnp.minimum(_count_le(pend, tile0), n_exp - 1)
    half = MOE_TILE // 2
    tile_rows = jnp.clip(counts[tile_e] - (tile0 - pstart[tile_e]), 0, MOE_TILE)
    tile_v = jnp.where(tile0 < pend[-1], (tile_rows + half - 1) // half, 0).astype(I32)
    cum = jnp.concatenate([before[:, 0, :n_exp], total[0:1, :n_exp]], axis=0).astype(I32)

    n_gb = n_slots // GATHER_TILE
    gb0 = jnp.arange(n_gb, dtype=I32) * GATHER_TILE
    gb_e = jnp.minimum(_count_le(pend, gb0), n_exp - 1)
    r0 = gb0 - pstart[gb_e]
    cum_b = cum[:, gb_e]
    ilo = jnp.sum((cum_b[1:] <= r0[None, :]).astype(I32), axis=0)
    ihi = jnp.sum((cum_b[:-1] < (r0 + GATHER_TILE)[None, :]).astype(I32), axis=0) - 1
    n_work = jnp.where(gb0 < pend[-1], jnp.maximum(ihi - ilo + 1, 0), 0).astype(I32)
    ends = jnp.cumsum(n_work)
    offs = (ends - n_work).astype(I32)
    entry = jnp.arange(n_exp * ntw + n_gb, dtype=I32)
    entry_blk = jnp.minimum(_count_le(ends, entry), n_gb - 1)
    wlist = jnp.clip(ilo[entry_blk] + entry - offs[entry_blk], 0, ntw - 1).astype(I32)

    e_t = info_t[0:2].astype(I32)
    slot_t = jnp.sum(jnp.where(e_t[None] == jnp.arange(n_exp, dtype=I32)[:, None, None], pstart[:, None, None], 0),
                     axis=0) + info_t[4:6].astype(I32)
    dest = jnp.concatenate([slot_t, jnp.full((SUBLANES - TOP_K, t), -1, I32)], axis=0)

    xs = _gather(hn, dest, offs, n_work, wlist, ends[-1:].astype(I32), n_slots)
    yb = _experts(xs, tile_e, tile_v, w1, w3, w2, layer)

    lo = pstart[None, :] + cum[:-1]
    win_start = jnp.minimum(lo // SLOT_ALIGN * SLOT_ALIGN, n_slots - SLOT_WIN).astype(I32)
    reach = lo - win_start + cum[1:] - cum[:-1]
    extra = (reach > TOK_WIN).astype(I32)
    narrow = jnp.all(reach <= NARROW_WIN, axis=1).astype(I32)
    return _combine(info, h, yb, win_start.reshape(-1), pstart.astype(I32), extra.reshape(-1), narrow, fin_g,
                    final_norm)


def _final_norm_kernel(h_ref, g_ref, o_ref):
    o_ref[...] = _rms(h_ref[...], g_ref[...])


def _final_norm(h, g):
    t, d = h.shape
    tm = min(ROW_TILE, t)
    return pl.pallas_call(
        _final_norm_kernel,
        grid=(t // tm,),
        in_specs=[pl.BlockSpec((tm, d), lambda i: (i, 0)), pl.BlockSpec((1, d), lambda i: (0, 0))],
        out_specs=pl.BlockSpec((tm, d), lambda i: (i, 0)),
        out_shape=jax.ShapeDtypeStruct((t, d), F32),
        compiler_params=_cparams("parallel"),
        name="final_norm",
    )(h, g.astype(F32).reshape(1, d))


def kernel(x, attn_norm, ffn_norm, final_norm, w_in, w_out, s5_lambda_re, s5_lambda_im, s5_log_dt, s5_b_re, s5_b_im, s5_c_re, s5_c_im, s5_d, s5_glu, s5_out_norm, conv_w, conv_out_norm, hg_lower_bounds, hg_out_norm, ffn_w1, ffn_w3, ffn_w2, moe_router, moe_w1, moe_w3, moe_w2):
    bsz, seq, d = x.shape
    depth = w_in.shape[0]
    t = bsz * seq
    assert bsz == 1, "token mixers are written for a single sequence"
    lb_soft = jax.nn.softmax(hg_lower_bounds.astype(F32), axis=0)
    lb_all = jnp.cumsum(lb_soft, axis=0) - lb_soft[0]
    n_scan = int(math.log2(t // S5_CHUNK))
    s5_ops = jax.vmap(functools.partial(_s5_operators, n_scan=n_scan))(
        s5_lambda_re, s5_lambda_im, s5_log_dt, s5_b_re, s5_b_im, s5_c_re, s5_c_im)
    h = x.reshape(t, d).astype(F32)
    for l in range(depth):
        proj = _norm_inproj(h, attn_norm[l].astype(F32), w_in[l].astype(BF16))
        ys5 = _s5_conv(proj, [op[l] for op in s5_ops])
        h = _hgrn_mix(ys5, proj, h, lb_all[l], hg_out_norm[l], s5_d[l], s5_glu[l].astype(BF16), s5_out_norm[l],
                      conv_w[l], conv_out_norm[l], w_out[l].astype(BF16))
        j = l // 2
        if l % 2 == 0:
            h = _ffn(h, ffn_norm[l], ffn_w1, ffn_w3, ffn_w2, j)
        else:
            h = _moe(h, ffn_norm[l], moe_router[j], moe_w1, moe_w3, moe_w2, j, final_norm, l == depth - 1)
    if depth % 2 == 1:
        h = _final_norm(h, final_norm)
    return h.reshape(bsz, seq, d)
```
